```python
import jax, jax.numpy as jnp
from jax import lax
import numpy as np

D_MODEL = 2048
BATCH = 4
SEQ = 2048
DEPTH = 1
DEC_BATCH = 128
DEC_SEQ = 8
PAST_LEN = 16384
PAGE_SIZE = 128

RET_HEADS = 8
RET_DK = D_MODEL // RET_HEADS
RET_DV = 2 * D_MODEL // RET_HEADS
RET_CHUNK = 128
ROPE_BASE = 10000.0
D_CONV = D_MODEL
CONV_WIDTH = 31
N_GROUPS = 4
EXPERTS_PER_GROUP = 8
N_EXPERTS = N_GROUPS * EXPERTS_PER_GROUP
TOP_K_FINE = 2
D_EXPERT = D_MODEL // 2
MOE_BLOCK = 128
EPS = 1e-6

W_Q = RET_HEADS * RET_DK
W_K = RET_HEADS * RET_DK
W_V = RET_HEADS * RET_DV
W_G = RET_HEADS * RET_DV
W_CA = D_CONV
W_CB = D_CONV
W_GATE_RET = D_MODEL
W_GATE_CONV = D_MODEL
IN_COLS = W_Q + W_K + W_V + W_G + W_CA + W_CB + W_GATE_RET + W_GATE_CONV
SPLIT_POINTS = (W_Q, W_Q + W_K, W_Q + W_K + W_V, W_Q + W_K + W_V + W_G,
                W_Q + W_K + W_V + W_G + W_CA, W_Q + W_K + W_V + W_G + W_CA + W_CB,
                W_Q + W_K + W_V + W_G + W_CA + W_CB + W_GATE_RET)

kernel_name = "retnet_conformer_hmoe_decode_step"


def rmsnorm(x, w):
    xf = x.astype(jnp.float32)
    y = xf * lax.rsqrt(jnp.mean(xf * xf, axis=-1, keepdims=True) + EPS) * w.astype(jnp.float32)
    return y.astype(x.dtype)


def rotary(x, pos):
    half = x.shape[-1] // 2
    inv = ROPE_BASE ** (-jnp.arange(half, dtype=jnp.float32) / half)
    ang = pos[:, None] * inv[None, :]
    cos = jnp.cos(ang)[None, :, None, :]
    sin = jnp.sin(ang)[None, :, None, :]
    x1, x2 = x[..., :half], x[..., half:]
    return jnp.concatenate([x1 * cos - x2 * sin, x1 * sin + x2 * cos], axis=-1)


def retention(q, k, v, s0):
    b, l, h, _ = q.shape
    dv = v.shape[-1]
    c = RET_CHUNK if l % RET_CHUNK == 0 else l
    n = l // c
    log_g = jnp.log1p(-jnp.exp2(-5.0 - jnp.arange(h, dtype=jnp.float32)))
    idx = jnp.arange(c, dtype=jnp.float32)
    diff = idx[:, None] - idx[None, :]
    decay = jnp.where(diff[None] >= 0.0,
                      jnp.exp(jnp.maximum(diff, 0.0)[None] * log_g[:, None, None]), 0.0)
    q_dec = jnp.exp((idx[:, None] + 1.0) * log_g[None, :])
    k_dec = jnp.exp((c - 1.0 - idx)[:, None] * log_g[None, :])
    c_dec = jnp.exp(c * log_g)

    def to_chunks(t):
        return t.reshape(b, n, c, h, t.shape[-1]).swapaxes(0, 1)

    def step(s, inp):
        qc, kc, vc = inp
        att = jnp.einsum('bihd,bjhd->bhij', qc, kc) * decay
        o = (jnp.einsum('bhij,bjhe->bihe', att, vc)
             + jnp.einsum('bihd,bhde->bihe', qc * q_dec[:, :, None], s))
        s = s * c_dec[:, None, None] + jnp.einsum('bjhd,bjhe->bhde', kc * k_dec[:, :, None], vc)
        return s, o

    s, o = lax.scan(step, s0, (to_chunks(q), to_chunks(k), to_chunks(v)))
    return o.swapaxes(0, 1).reshape(b, l, h, dv), s


def conv_module(a, bgate, buf, conv_w, conv_b, ln_w, ln_b, w_out):
    u = a * jax.nn.sigmoid(bgate)
    u_pad = jnp.concatenate([buf.astype(u.dtype), u], axis=1)
    new_buf = u_pad[:, -(CONV_WIDTH - 1):]
    c = lax.conv_general_dilated(u_pad, conv_w[:, None, :].astype(u.dtype), (1,), 'VALID',
                                 dimension_numbers=('NWC', 'WIO', 'NWC'),
                                 feature_group_count=D_CONV)
    cf = c.astype(jnp.float32) + conv_b.astype(jnp.float32)
    mu = jnp.mean(cf, axis=-1, keepdims=True)
    var = jnp.mean(jnp.square(cf - mu), axis=-1, keepdims=True)
    cn = (cf - mu) * lax.rsqrt(var + EPS) * ln_w.astype(jnp.float32) + ln_b.astype(jnp.float32)
    return jnp.matmul(jax.nn.silu(cn).astype(a.dtype), w_out), new_buf


def hier_moe(x, w_coarse, b_coarse, w_fine, b_fine, w_gate, w_up, w_down):
    n_tok, d = x.shape
    lc = jnp.matmul(x, w_coarse).astype(jnp.float32) + b_coarse.astype(jnp.float32)
    pc = jax.nn.softmax(lc, axis=-1)
    grp = jnp.argmax(lc, axis=-1).astype(jnp.int32)
    p_grp = jnp.take_along_axis(pc, grp[:, None], axis=-1)
    lf = (jnp.matmul(x, w_fine).astype(jnp.float32) + b_fine.astype(jnp.float32)
          ).reshape(n_tok, N_GROUPS, EXPERTS_PER_GROUP)
    lf_g = jnp.take_along_axis(lf, grp[:, None, None], axis=1)[:, 0]
    top_v, top_i = lax.top_k(lf_g, TOP_K_FINE)
    gate = p_grp * jax.nn.softmax(top_v, axis=-1)
    expert = grp[:, None] * EXPERTS_PER_GROUP + top_i.astype(jnp.int32)

    n_asg = n_tok * TOP_K_FINE
    e_flat = expert.reshape(-1)
    t_flat = jnp.repeat(jnp.arange(n_tok, dtype=jnp.int32), TOP_K_FINE)
    g_flat = gate.reshape(-1)
    order = jnp.argsort(e_flat)
    e_s, t_s, g_s = e_flat[order], t_flat[order], g_flat[order]
    counts = jax.ops.segment_sum(jnp.ones_like(e_flat), e_flat, num_segments=N_EXPERTS)
    start = jnp.cumsum(counts) - counts
    padded = (counts + MOE_BLOCK - 1) // MOE_BLOCK * MOE_BLOCK
    pend = jnp.cumsum(padded)
    pstart = pend - padded
    row = pstart[e_s] + jnp.arange(n_asg, dtype=jnp.int32) - start[e_s]
    n_blocks = -(-n_asg // MOE_BLOCK) + N_EXPERTS
    n_rows = n_blocks * MOE_BLOCK
    tok_rows = jnp.full((n_rows,), n_tok, jnp.int32).at[row].set(t_s)
    gate_rows = jnp.zeros((n_rows,), jnp.float32).at[row].set(g_s)
    blk_exp = jnp.minimum(
        jnp.searchsorted(pend, jnp.arange(n_blocks, dtype=jnp.int32) * MOE_BLOCK, side='right'),
        N_EXPERTS - 1).astype(jnp.int32)
    x_pad = jnp.concatenate([x, jnp.zeros((1, d), x.dtype)], axis=0)

    def expert_block(args):
        idx, e = args
        xb = x_pad[idx]
        hb = jax.nn.silu(jnp.matmul(xb, w_gate[e])) * jnp.matmul(xb, w_up[e])
        return jnp.matmul(hb, w_down[e])

    out = lax.map(expert_block, (tok_rows.reshape(n_blocks, MOE_BLOCK), blk_exp))
    out = out.reshape(n_rows, d) * gate_rows[:, None].astype(x.dtype)
    return jax.ops.segment_sum(out, tok_rows, num_segments=n_tok + 1)[:n_tok]


def trunk_layer(x, s_ret, s_conv, pos0, norm_mix, w_in, w_ret_o, conv_w, conv_b, conv_ln_w,
                conv_ln_b, w_conv_o, w_o, norm_ffn, w_coarse, b_coarse, w_fine, b_fine,
                w_gate, w_up, w_down):
    b, l, _ = x.shape
    h = rmsnorm(x, norm_mix)
    z = jnp.matmul(h, w_in)
    q, k, v, g, ca, cb, g_ret, g_conv = jnp.split(z, SPLIT_POINTS, axis=-1)
    pos = (pos0 + jnp.arange(l)).astype(jnp.float32)
    q = rotary(q.reshape(b, l, RET_HEADS, RET_DK).astype(jnp.float32), pos)
    k = rotary(k.reshape(b, l, RET_HEADS, RET_DK).astype(jnp.float32), pos) * (RET_DK ** -0.5)
    v = v.reshape(b, l, RET_HEADS, RET_DV).astype(jnp.float32)
    o, s_ret_new = retention(q, k, v, s_ret.astype(jnp.float32))
    o = o * lax.rsqrt(jnp.mean(o * o, axis=-1, keepdims=True) + EPS)
    o = (jax.nn.silu(g.reshape(b, l, RET_HEADS, RET_DV).astype(jnp.float32)) * o
         ).astype(x.dtype).reshape(b, l, RET_HEADS * RET_DV)
    ret_out = jnp.matmul(o, w_ret_o)
    conv_out, s_conv_new = conv_module(ca, cb, s_conv, conv_w, conv_b, conv_ln_w, conv_ln_b, w_conv_o)
    merged = jax.nn.sigmoid(g_ret) * ret_out + jax.nn.sigmoid(g_conv) * conv_out
    x = x + jnp.matmul(merged, w_o)
    h2 = rmsnorm(x, norm_ffn)
    x = x + hier_moe(h2.reshape(b * l, -1), w_coarse, b_coarse, w_fine, b_fine,
                     w_gate, w_up, w_down).reshape(b, l, -1)
    return x, s_ret_new.astype(s_ret.dtype), s_conv_new


def setup_inputs(seed: int = 0) -> dict:
    key = jax.random.key(seed)
    ks = jax.random.split(key, 24)

    def nrm(k, shape, scale):
        return jax.random.normal(k, shape, jnp.float32) * scale

    return {
        "x_prompt": nrm(ks[0], (BATCH, SEQ, D_MODEL), 1.0),
        "x_sample": nrm(ks[1], (DEC_BATCH, DEC_SEQ, D_MODEL), 1.0),
        "state_ret": nrm(ks[2], (DEPTH, DEC_BATCH, RET_HEADS, RET_DK, RET_DV), 0.1),
        "state_conv": nrm(ks[3], (DEPTH, DEC_BATCH, CONV_WIDTH - 1, D_CONV), 0.5),
        "norm_mix": 1.0 + nrm(ks[4], (DEPTH, D_MODEL), 0.01),
        "w_in": nrm(ks[5], (DEPTH, D_MODEL, IN_COLS), D_MODEL ** -0.5),
        "w_ret_o": nrm(ks[6], (DEPTH, RET_HEADS * RET_DV, D_MODEL), (RET_HEADS * RET_DV) ** -0.5),
        "conv_w": nrm(ks[7], (DEPTH, CONV_WIDTH, D_CONV), CONV_WIDTH ** -0.5),
        "conv_b": nrm(ks[8], (DEPTH, D_CONV), 0.01),
        "conv_ln_w": 1.0 + nrm(ks[9], (DEPTH, D_CONV), 0.01),
        "conv_ln_b": nrm(ks[10], (DEPTH, D_CONV), 0.01),
        "w_conv_o": nrm(ks[11], (DEPTH, D_CONV, D_MODEL), D_CONV ** -0.5),
        "w_o": nrm(ks[12], (DEPTH, D_MODEL, D_MODEL), D_MODEL ** -0.5),
        "norm_ffn": 1.0 + nrm(ks[13], (DEPTH, D_MODEL), 0.01),
        "w_coarse": nrm(ks[14], (DEPTH, D_MODEL, N_GROUPS), D_MODEL ** -0.5),
        "b_coarse": nrm(ks[15], (DEPTH, N_GROUPS), 0.01),
        "w_fine": nrm(ks[16], (DEPTH, D_MODEL, N_EXPERTS), D_MODEL ** -0.5),
        "b_fine": nrm(ks[17], (DEPTH, N_EXPERTS), 0.01),
        "w_gate": nrm(ks[18], (DEPTH, N_EXPERTS, D_MODEL, D_EXPERT), D_MODEL ** -0.5),
        "w_up": nrm(ks[19], (DEPTH, N_EXPERTS, D_MODEL, D_EXPERT), D_MODEL ** -0.5),
        "w_down": nrm(ks[20], (DEPTH, N_EXPERTS, D_EXPERT, D_MODEL), D_EXPERT ** -0.5),
        "norm_f": 1.0 + nrm(ks[21], (D_MODEL,), 0.01),
    }


def reference(x_prompt, x_sample, state_ret, state_conv, norm_mix, w_in, w_ret_o, conv_w, conv_b,
              conv_ln_w, conv_ln_b, w_conv_o, w_o, norm_ffn, w_coarse, b_coarse, w_fine, b_fine,
              w_gate, w_up, w_down, norm_f):
    yp, ys = x_prompt, x_sample
    bp = x_prompt.shape[0]
    ret_p, conv_p, ret_s, conv_s = [], [], [], []
    for l in range(DEPTH):
        lw = (norm_mix[l], w_in[l], w_ret_o[l], conv_w[l], conv_b[l], conv_ln_w[l], conv_ln_b[l],
              w_conv_o[l], w_o[l], norm_ffn[l], w_coarse[l], b_coarse[l], w_fine[l], b_fine[l],
              w_gate[l], w_up[l], w_down[l])
        zero_ret = jnp.zeros((bp, RET_HEADS, RET_DK, RET_DV), state_ret.dtype)
        zero_conv = jnp.zeros((bp, CONV_WIDTH - 1, D_CONV), x_prompt.dtype)
        yp, rp, cp = trunk_layer(yp, zero_ret, zero_conv, 0, *lw)
        ys, rs, cs = trunk_layer(ys, state_ret[l], state_conv[l], PAST_LEN, *lw)
        ret_p.append(rp)
        conv_p.append(cp)
        ret_s.append(rs)
        conv_s.append(cs)
    yp = rmsnorm(yp, norm_f)
    ys = rmsnorm(ys, norm_f)
    return (yp, ys, jnp.stack(ret_p, 0), jnp.stack(conv_p, 0), jnp.stack(ret_s, 0), jnp.stack(conv_s, 0))
```

```python
import functools

import jax
import jax.numpy as jnp
from jax import lax
from jax.experimental import pallas as pl
from jax.experimental.pallas import tpu as pltpu

F32 = jnp.float32
BF16 = jnp.bfloat16
I32 = jnp.int32

D_MODEL = 2048
PAST_LEN = 16384
RET_HEADS = 8
RET_DK = D_MODEL // RET_HEADS
RET_DV = 2 * D_MODEL // RET_HEADS
RET_HALF = RET_DK // 2
RET_CHUNK = 128
ROPE_BASE = 10000.0
CONV_WIDTH = 31
CONV_HIST = CONV_WIDTH - 1
N_GROUPS = 4
EXPERTS_PER_GROUP = 8
N_EXPERTS = N_GROUPS * EXPERTS_PER_GROUP
TOP_K_FINE = 2
D_EXPERT = D_MODEL // 2
EPS = 1e-6

COL_Q = 0
COL_K = COL_Q + RET_HEADS * RET_DK
COL_V = COL_K + RET_HEADS * RET_DK
COL_G = COL_V + RET_HEADS * RET_DV
COL_CA = COL_G + RET_HEADS * RET_DV
COL_CB = COL_CA + D_MODEL
COL_GRET = COL_CB + D_MODEL
COL_GCONV = COL_GRET + D_MODEL
IN_COLS = COL_GCONV + D_MODEL

LANES = 128
SUBLANES = 8
VMEM_LIMIT = 56 * 1024 * 1024
MOE_ROWS = 256
ROUTER_COLS = LANES
ROUTER_OFF = N_GROUPS
NEG = -3.0e38


def _cparams(sem):
    return pltpu.CompilerParams(dimension_semantics=sem, vmem_limit_bytes=VMEM_LIMIT)


def _rmsnorm_kernel(x_ref, w_ref, o_ref):
    x = x_ref[...]
    ms = jnp.mean(x * x, axis=-1, keepdims=True)
    o_ref[...] = (x * lax.rsqrt(ms + EPS) * w_ref[...]).astype(o_ref.dtype)


def _rmsnorm_cast(x, w, tm=512):
    m, d = x.shape
    tm = min(tm, m)
    return pl.pallas_call(
        _rmsnorm_kernel,
        grid=(m // tm,),
        in_specs=[pl.BlockSpec((tm, d), lambda i: (i, 0)), pl.BlockSpec((1, d), lambda i: (0, 0))],
        out_specs=pl.BlockSpec((tm, d), lambda i: (i, 0)),
        out_shape=jax.ShapeDtypeStruct((m, d), BF16),
        compiler_params=_cparams(("arbitrary",)),
        name="rmsnorm_cast",
    )(x, w.reshape(1, d))


def _inproj_kernel(a_ref, w_ref, o_ref, wb_ref):
    @pl.when(pl.program_id(1) == 0)
    def _():
        wb_ref[...] = w_ref[...].astype(BF16)

    o_ref[...] = jnp.dot(a_ref[...], wb_ref[...], preferred_element_type=F32)


def _inproj(a, w, tm=1024, tn=1024):
    m, k = a.shape
    n = w.shape[1]
    tm = min(tm, m)
    return pl.pallas_call(
        _inproj_kernel,
        grid=(n // tn, m // tm),
        in_specs=[pl.BlockSpec((tm, k), lambda j, i: (i, 0)), pl.BlockSpec((k, tn), lambda j, i: (0, j))],
        out_specs=pl.BlockSpec((tm, tn), lambda j, i: (i, j)),
        out_shape=jax.ShapeDtypeStruct((m, n), F32),
        scratch_shapes=[pltpu.VMEM((k, tn), BF16)],
        compiler_params=_cparams(("arbitrary", "arbitrary")),
        name="inproj",
    )(a, w)


def _rotary_pair(x1, x2, cos, sin):
    return jnp.concatenate([x1 * cos - x2 * sin, x1 * sin + x2 * cos], axis=1)


def _retention_tables(c):
    h = RET_HEADS
    log_g = jnp.log1p(-jnp.exp2(-5.0 - jnp.arange(h, dtype=F32)))
    idx = jnp.arange(c, dtype=F32)
    diff = idx[:, None] - idx[None, :]
    decay = jnp.where(diff[None] >= 0.0, jnp.exp(jnp.maximum(diff, 0.0)[None] * log_g[:, None, None]), 0.0)
    q_dec = jnp.exp((idx[:, None] + 1.0) * log_g[None, :])
    k_dec = jnp.exp((c - 1.0 - idx)[:, None] * log_g[None, :])
    c_dec = jnp.exp(c * log_g)
    return decay, q_dec, k_dec, c_dec


def _rope_tables(pos):
    inv = ROPE_BASE ** (-jnp.arange(RET_HALF, dtype=F32) / RET_HALF)
    ang = pos.astype(F32)[:, None] * inv[None, :]
    return jnp.cos(ang), jnp.sin(ang)


def _lane_bcast(t):
    return jnp.broadcast_to(t.T[:, :, None], (t.shape[1], t.shape[0], LANES))


def _group_norm_gate(o, g):
    ms = jnp.mean(o * o, axis=-1, keepdims=True)
    return jax.nn.silu(g) * (o * lax.rsqrt(ms + EPS))


def _ret_prompt_kernel(cdec_ref, q_ref, k_ref, v_ref, g_ref, cos_ref, sin_ref, dec_ref, qd_ref, kd_ref,
                       o_ref, s_ref):
    @pl.when(pl.program_id(1) == 0)
    def _():
        s_ref[...] = jnp.zeros_like(s_ref)

    cos = cos_ref[...]
    sin = sin_ref[...]
    for h in range(RET_HEADS):
        c0 = h * RET_DK
        qr = _rotary_pair(q_ref[:, c0:c0 + RET_HALF], q_ref[:, c0 + RET_HALF:c0 + RET_DK], cos, sin)
        kr = _rotary_pair(k_ref[:, c0:c0 + RET_HALF], k_ref[:, c0 + RET_HALF:c0 + RET_DK], cos, sin)
        kr = kr * (RET_DK ** -0.5)
        qb = qr.astype(BF16)
        kb = kr.astype(BF16)
        vb = v_ref[:, h * RET_DV:(h + 1) * RET_DV].astype(BF16)
        att = lax.dot_general(qb, kb, (((1,), (1,)), ((), ())), preferred_element_type=F32) * dec_ref[h]
        s = s_ref[0, h]
        o = (jnp.dot(att.astype(BF16), vb, preferred_element_type=F32)
             + qd_ref[h][:, :1] * jnp.dot(qb, s.astype(BF16), preferred_element_type=F32))
        kk = (kr * kd_ref[h][:, :1]).astype(BF16)
        s_ref[0, h] = s * cdec_ref[h] + lax.dot_general(kk, vb, (((0,), (0,)), ((), ())),
                                                         preferred_element_type=F32)
        og = _group_norm_gate(o, g_ref[:, h * RET_DV:(h + 1) * RET_DV])
        o_ref[:, h * RET_DV:(h + 1) * RET_DV] = og.astype(o_ref.dtype)


def _retention_prompt(z, b, l):
    c = RET_CHUNK if l % RET_CHUNK == 0 else l
    nc = l // c
    decay, q_dec, k_dec, c_dec = _retention_tables(c)
    cos, sin = _rope_tables(jnp.arange(l))
    wq = RET_HEADS * RET_DK
    wv = RET_HEADS * RET_DV
    full3 = lambda bi, ci: (0, 0, 0)
    return pl.pallas_call(
        _ret_prompt_kernel,
        grid=(b, nc),
        in_specs=[
            pl.BlockSpec(memory_space=pltpu.SMEM),
            pl.BlockSpec((c, wq), lambda bi, ci: (bi * nc + ci, COL_Q // wq)),
            pl.BlockSpec((c, wq), lambda bi, ci: (bi * nc + ci, COL_K // wq)),
            pl.BlockSpec((c, wv), lambda bi, ci: (bi * nc + ci, COL_V // wv)),
            pl.BlockSpec((c, wv), lambda bi, ci: (bi * nc + ci, COL_G // wv)),
            pl.BlockSpec((c, RET_HALF), lambda bi, ci: (ci, 0)),
            pl.BlockSpec((c, RET_HALF), lambda bi, ci: (ci, 0)),
            pl.BlockSpec((RET_HEADS, c, c), full3),
            pl.BlockSpec((RET_HEADS, c, LANES), full3),
            pl.BlockSpec((RET_HEADS, c, LANES), full3),
        ],
        out_specs=[
            pl.BlockSpec((c, wv), lambda bi, ci: (bi * nc + ci, 0)),
            pl.BlockSpec((1, RET_HEADS, RET_DK, RET_DV), lambda bi, ci: (bi, 0, 0, 0)),
        ],
        out_shape=[
            jax.ShapeDtypeStruct((b * l, wv), BF16),
            jax.ShapeDtypeStruct((b, RET_HEADS, RET_DK, RET_DV), F32),
        ],
        compiler_params=_cparams(("arbitrary", "arbitrary")),
        name="retention_prompt",
    )(c_dec, z, z, z, z, cos, sin, decay, _lane_bcast(q_dec), _lane_bcast(k_dec))


SAMPLE_PAD = 16


def _pad_rows(x, rows):
    return jnp.concatenate([x, jnp.zeros((rows - x.shape[0], x.shape[1]), x.dtype)], axis=0)


def _ret_sample_kernel(cdec_ref, q_ref, k_ref, v_ref, g_ref, cos_ref, sin_ref, dec_ref, qd_ref, kd_ref,
                       s0_ref, o_ref, s_ref, *, nb, l):
    cos = cos_ref[...]
    sin = sin_ref[...]
    for bt in range(nb):
        r0 = bt * l
        for h in range(RET_HEADS):
            c0 = h * RET_DK
            qr = _rotary_pair(q_ref[r0:r0 + l, c0:c0 + RET_HALF], q_ref[r0:r0 + l, c0 + RET_HALF:c0 + RET_DK],
                              cos, sin)
            kr = _rotary_pair(k_ref[r0:r0 + l, c0:c0 + RET_HALF], k_ref[r0:r0 + l, c0 + RET_HALF:c0 + RET_DK],
                              cos, sin)
            kr = kr * (RET_DK ** -0.5)
            qb = _pad_rows(qr, SAMPLE_PAD).astype(BF16)
            kb = _pad_rows(kr, SAMPLE_PAD).astype(BF16)
            vb = _pad_rows(v_ref[r0:r0 + l, h * RET_DV:(h + 1) * RET_DV], SAMPLE_PAD).astype(BF16)
            att = lax.dot_general(qb, kb, (((1,), (1,)), ((), ())), preferred_element_type=F32) * dec_ref[h]
            s = s0_ref[bt, h]
            o = (jnp.dot(att.astype(BF16), vb, preferred_element_type=F32)
                 + qd_ref[h][:, :1] * jnp.dot(qb, s.astype(BF16), preferred_element_type=F32))
            kk = _pad_rows(kr * kd_ref[h][:l, :1], SAMPLE_PAD).astype(BF16)
            s_ref[bt, h] = s * cdec_ref[h] + lax.dot_general(kk, vb, (((0,), (0,)), ((), ())),
                                                              preferred_element_type=F32)
            og = _group_norm_gate(o[:l], g_ref[r0:r0 + l, h * RET_DV:(h + 1) * RET_DV])
            o_ref[r0:r0 + l, h * RET_DV:(h + 1) * RET_DV] = og.astype(o_ref.dtype)


def _retention_sample(z, s0, b, l, pos0, nb=2):
    assert l % RET_CHUNK != 0 and l <= SAMPLE_PAD and b % nb == 0, (b, l)
    decay, q_dec, k_dec, c_dec = _retention_tables(l)
    p = SAMPLE_PAD - l
    decay = jnp.pad(decay, ((0, 0), (0, p), (0, p)))
    q_dec = jnp.pad(q_dec, ((0, p), (0, 0)))
    k_dec = jnp.pad(k_dec, ((0, p), (0, 0)))
    cos, sin = _rope_tables(pos0 + jnp.arange(l))
    wq = RET_HEADS * RET_DK
    wv = RET_HEADS * RET_DV
    rows = nb * l
    full2 = lambda bi: (0, 0)
    full3 = lambda bi: (0, 0, 0)
    state_spec = pl.BlockSpec((nb, RET_HEADS, RET_DK, RET_DV), lambda bi: (bi, 0, 0, 0))
    return pl.pallas_call(
        functools.partial(_ret_sample_kernel, nb=nb, l=l),
        grid=(b // nb,),
        in_specs=[
            pl.BlockSpec(memory_space=pltpu.SMEM),
            pl.BlockSpec((rows, wq), lambda bi: (bi, COL_Q // wq)),
            pl.BlockSpec((rows, wq), lambda bi: (bi, COL_K // wq)),
            pl.BlockSpec((rows, wv), lambda bi: (bi, COL_V // wv)),
            pl.BlockSpec((rows, wv), lambda bi: (bi, COL_G // wv)),
            pl.BlockSpec((l, RET_HALF), full2),
            pl.BlockSpec((l, RET_HALF), full2),
            pl.BlockSpec((RET_HEADS, SAMPLE_PAD, SAMPLE_PAD), full3),
            pl.BlockSpec((RET_HEADS, SAMPLE_PAD, LANES), full3),
            pl.BlockSpec((RET_HEADS, SAMPLE_PAD, LANES), full3),
            state_spec,
        ],
        out_specs=[pl.BlockSpec((rows, wv), lambda bi: (bi, 0)), state_spec],
        out_shape=[
            jax.ShapeDtypeStruct((b * l, wv), BF16),
            jax.ShapeDtypeStruct((b, RET_HEADS, RET_DK, RET_DV), F32),
        ],
        compiler_params=_cparams(("arbitrary",)),
        name="retention_sample",
    )(c_dec, z, z, z, z, cos, sin, decay, _lane_bcast(q_dec), _lane_bcast(k_dec), s0)


CONV_PAD = 32
CONV_COLS = 512


def _conv_kernel(ca_ref, cb_ref, buf_ref, cw_ref, bias_ref, lnw_ref, lnb_ref, sw_ref, nbuf_ref,
                 upad_ref, c_ref, *, tl, nb, nl):
    li = pl.program_id(1)
    lead = CONV_PAD - CONV_HIST
    d = ca_ref.shape[1]
    for bt in range(nb):
        @pl.when(li == 0)
        def _():
            upad_ref[bt, 0:lead, :] = jnp.zeros((lead, d), F32)
            upad_ref[bt, lead:CONV_PAD, :] = buf_ref[bt]

        r0 = bt * tl
        upad_ref[bt, CONV_PAD:CONV_PAD + tl, :] = ca_ref[r0:r0 + tl, :] * jax.nn.sigmoid(cb_ref[r0:r0 + tl, :])
        for rb in range(tl // SUBLANES):
            for cc in range(d // CONV_COLS):
                cs = slice(cc * CONV_COLS, (cc + 1) * CONV_COLS)
                acc = jnp.zeros((SUBLANES, CONV_COLS), F32)
                for w in range(CONV_WIDTH):
                    top = lead + w + rb * SUBLANES
                    acc = acc + upad_ref[bt, top:top + SUBLANES, cs] * cw_ref[w:w + 1, cs]
                c_ref[rb * SUBLANES:(rb + 1) * SUBLANES, cs] = acc
        cf = c_ref[...] + bias_ref[...]
        mu = jnp.mean(cf, axis=-1, keepdims=True)
        var = jnp.mean(jnp.square(cf - mu), axis=-1, keepdims=True)
        cn = (cf - mu) * lax.rsqrt(var + EPS) * lnw_ref[...] + lnb_ref[...]
        sw_ref[r0:r0 + tl, :] = jax.nn.silu(cn).astype(sw_ref.dtype)

        @pl.when(li == nl - 1)
        def _():
            nbuf_ref[bt] = upad_ref[bt, tl + lead:tl + CONV_PAD, :]

        if nl > 1:
            upad_ref[bt, 0:CONV_PAD, :] = upad_ref[bt, tl:tl + CONV_PAD, :]


def _conv_branch(z, buf, conv_w, conv_b, ln_w, ln_b, b, l, tl, nb):
    d = D_MODEL
    nl = l // tl
    assert nb == 1 or nl == 1
    rows = nb * tl
    vec = lambda a: a.reshape(1, d)
    const2 = lambda bi, li: (0, 0)
    return pl.pallas_call(
        functools.partial(_conv_kernel, tl=tl, nb=nb, nl=nl),
        grid=(b // nb, nl),
        in_specs=[
            pl.BlockSpec((rows, d), lambda bi, li: (bi * nl + li, COL_CA // d)),
            pl.BlockSpec((rows, d), lambda bi, li: (bi * nl + li, COL_CB // d)),
            pl.BlockSpec((nb, CONV_HIST, d), lambda bi, li: (bi, 0, 0)),
            pl.BlockSpec((CONV_WIDTH, d), const2),
            pl.BlockSpec((1, d), const2),
            pl.BlockSpec((1, d), const2),
            pl.BlockSpec((1, d), const2),
        ],
        out_specs=[
            pl.BlockSpec((rows, d), lambda bi, li: (bi * nl + li, 0)),
            pl.BlockSpec((nb, CONV_HIST, d), lambda bi, li: (bi, 0, 0)),
        ],
        out_shape=[
            jax.ShapeDtypeStruct((b * l, d), BF16),
            jax.ShapeDtypeStruct((b, CONV_HIST, d), F32),
        ],
        scratch_shapes=[pltpu.VMEM((nb, CONV_PAD + tl, d), F32), pltpu.VMEM((tl, d), F32)],
        compiler_params=_cparams(("arbitrary", "arbitrary")),
        name="conv_branch",
    )(z, z, buf, conv_w, vec(conv_b), vec(ln_w), vec(ln_b))


def _merge_kernel(og_ref, sw_ref, wr_ref, wc_ref, gr_ref, gc_ref, o_ref, wrb_ref, wcb_ref):
    @pl.when(pl.program_id(1) == 0)
    def _():
        wrb_ref[...] = wr_ref[...].astype(BF16)
        wcb_ref[...] = wc_ref[...].astype(BF16)

    ret_out = jnp.dot(og_ref[...], wrb_ref[...], preferred_element_type=F32)
    conv_out = jnp.dot(sw_ref[...], wcb_ref[...], preferred_element_type=F32)
    merged = jax.nn.sigmoid(gr_ref[...]) * ret_out + jax.nn.sigmoid(gc_ref[...]) * conv_out
    o_ref[...] = merged.astype(o_ref.dtype)


def _merge(og, sw, w_ret_o, w_conv_o, z, tm=512, tn=512):
    m = og.shape[0]
    n = w_ret_o.shape[1]
    tm = min(tm, m)
    kr = og.shape[1]
    kc = sw.shape[1]
    return pl.pallas_call(
        _merge_kernel,
        grid=(n // tn, m // tm),
        in_specs=[
            pl.BlockSpec((tm, kr), lambda j, i: (i, 0)),
            pl.BlockSpec((tm, kc), lambda j, i: (i, 0)),
            pl.BlockSpec((kr, tn), lambda j, i: (0, j)),
            pl.BlockSpec((kc, tn), lambda j, i: (0, j)),
            pl.BlockSpec((tm, tn), lambda j, i: (i, COL_GRET // tn + j)),
            pl.BlockSpec((tm, tn), lambda j, i: (i, COL_GCONV // tn + j)),
        ],
        out_specs=pl.BlockSpec((tm, tn), lambda j, i: (i, j)),
        out_shape=jax.ShapeDtypeStruct((m, n), BF16),
        scratch_shapes=[pltpu.VMEM((kr, tn), BF16), pltpu.VMEM((kc, tn), BF16)],
        compiler_params=_cparams(("arbitrary", "arbitrary")),
        name="merge",
    )(og, sw, w_ret_o, w_conv_o, z, z)


def _outproj_kernel(a_ref, w_ref, x_ref, *rest):
    o_ref, wb_ref = rest[-2], rest[-1]

    @pl.when(pl.program_id(1) == 0)
    def _():
        wb_ref[...] = w_ref[...].astype(BF16)

    o_ref[...] = x_ref[...] + jnp.dot(a_ref[...], wb_ref[...], preferred_element_type=F32)


def _outproj_residual(a, w, x, total_rows, row0, prev=None, tm=512, tn=512):
    m, k = a.shape
    n = w.shape[1]
    tm = min(tm, m)
    assert row0 % tm == 0
    off = row0 // tm
    in_specs = [
        pl.BlockSpec((tm, k), lambda j, i: (i, 0)),
        pl.BlockSpec((k, tn), lambda j, i: (0, j)),
        pl.BlockSpec((tm, tn), lambda j, i: (i, j)),
    ]
    args = [a, w, x]
    aliases = {}
    if prev is not None:
        in_specs.append(pl.BlockSpec(memory_space=pl.ANY))
        args.append(prev)
        aliases = {3: 0}
    return pl.pallas_call(
        _outproj_kernel,
        grid=(n // tn, m // tm),
        in_specs=in_specs,
        out_specs=pl.BlockSpec((tm, tn), lambda j, i: (i + off, j)),
        out_shape=jax.ShapeDtypeStruct((total_rows, n), F32),
        scratch_shapes=[pltpu.VMEM((k, tn), BF16)],
        input_output_aliases=aliases,
        compiler_params=_cparams(("arbitrary", "arbitrary")),
        name="outproj_residual",
    )(*args)


def _router_kernel(x_ref, nw_ref, w_ref, b_ref, h_ref, meta_ref, cnt_ref, base_ref):
    tm = x_ref.shape[0]

    @pl.when(pl.program_id(0) == 0)
    def _():
        base_ref[...] = jnp.zeros_like(base_ref)

    x = x_ref[...]
    ms = jnp.mean(x * x, axis=-1, keepdims=True)
    h = x * lax.rsqrt(ms + EPS) * nw_ref[...]
    h_ref[...] = h
    logits = jnp.dot(h, w_ref[...], preferred_element_type=F32, precision=lax.Precision.HIGHEST) + b_ref[...]
    lane = lax.broadcasted_iota(I32, (tm, ROUTER_COLS), 1)

    def first_argmax(vals):
        top = jnp.max(vals, axis=1, keepdims=True)
        return top, jnp.min(jnp.where(vals == top, lane, ROUTER_COLS), axis=1, keepdims=True)

    cmask = lane < N_GROUPS
    lc = jnp.where(cmask, logits, NEG)
    mc, grp = first_argmax(lc)
    p_grp = 1.0 / jnp.sum(jnp.where(cmask, jnp.exp(lc - mc), 0.0), axis=1, keepdims=True)
    lo = ROUTER_OFF + grp * EXPERTS_PER_GROUP
    lf = jnp.where((lane >= lo) & (lane < lo + EXPERTS_PER_GROUP), logits, NEG)
    v1, i1 = first_argmax(lf)
    v2, i2 = first_argmax(jnp.where(lane == i1, NEG, lf))
    e2 = jnp.exp(v2 - v1)
    den = 1.0 + e2
    g1 = p_grp * (1.0 / den)
    g2 = p_grp * (e2 / den)

    oh1 = lane == i1
    oh2 = lane == i2
    both = jnp.where(oh1 | oh2, 1.0, 0.0)
    row = lax.broadcasted_iota(I32, (tm, tm), 0)
    col = lax.broadcasted_iota(I32, (tm, tm), 1)
    earlier = jnp.where(col < row, 1.0, 0.0).astype(BF16)
    before = jnp.dot(earlier, both.astype(BF16), preferred_element_type=F32) + base_ref[...]
    r1 = jnp.sum(jnp.where(oh1, before, 0.0), axis=1, keepdims=True)
    r2 = jnp.sum(jnp.where(oh2, before, 0.0), axis=1, keepdims=True)
    base_ref[...] = base_ref[...] + jnp.sum(both, axis=0, keepdims=True)
    cnt_ref[...] = base_ref[...]

    fields = [(i1 - ROUTER_OFF).astype(F32), (i2 - ROUTER_OFF).astype(F32), g1, g2, r1, r2]
    meta = jnp.zeros((tm, ROUTER_COLS), F32)
    for k, f in enumerate(fields):
        meta = jnp.where(lane == k, f, meta)
    meta_ref[...] = meta


def _router(x, norm_w, w_router, b_router, tm=512):
    n, d = x.shape
    tm = min(tm, n)
    const2 = lambda i: (0, 0)
    return pl.pallas_call(
        _router_kernel,
        grid=(n // tm,),
        in_specs=[
            pl.BlockSpec((tm, d), lambda i: (i, 0)),
            pl.BlockSpec((1, d), const2),
            pl.BlockSpec((d, ROUTER_COLS), const2),
            pl.BlockSpec((1, ROUTER_COLS), const2),
        ],
        out_specs=[
            pl.BlockSpec((tm, d), lambda i: (i, 0)),
            pl.BlockSpec((tm, ROUTER_COLS), lambda i: (i, 0)),
            pl.BlockSpec((1, ROUTER_COLS), const2),
        ],
        out_shape=[
            jax.ShapeDtypeStruct((n, d), F32),
            jax.ShapeDtypeStruct((n, ROUTER_COLS), F32),
            jax.ShapeDtypeStruct((1, ROUTER_COLS), F32),
        ],
        scratch_shapes=[pltpu.VMEM((1, ROUTER_COLS), F32)],
        compiler_params=_cparams(("arbitrary",)),
        name="router",
    )(x, norm_w.reshape(1, d), w_router, b_router)


DISPATCH_CHUNK = 256


def _dispatch_kernel(rows_ref, h_hbm, xs_in_hbm, xs_hbm, sem, *, n_tok):
    del xs_in_hbm
    n_chunks = n_tok // DISPATCH_CHUNK

    def chunk_wait():
        for _ in range(TOP_K_FINE):
            pltpu.make_async_copy(h_hbm.at[pl.ds(0, DISPATCH_CHUNK)], xs_hbm.at[pl.ds(0, DISPATCH_CHUNK)],
                                  sem).wait()

    def chunk(ci, carry):
        def token(t, c2):
            n = ci * DISPATCH_CHUNK + t
            for k in range(TOP_K_FINE):
                r = rows_ref[n * TOP_K_FINE + k]
                pltpu.make_async_copy(h_hbm.at[pl.ds(n, 1)], xs_hbm.at[pl.ds(r, 1)], sem).start()
            return c2

        lax.fori_loop(0, DISPATCH_CHUNK, token, 0)

        @pl.when(ci > 0)
        def _():
            chunk_wait()

        return carry

    lax.fori_loop(0, n_chunks, chunk, 0)
    chunk_wait()


def _dispatch(rows_flat, h, n_rows):
    n_tok, d = h.shape
    assert n_tok % DISPATCH_CHUNK == 0
    xs0 = jnp.zeros((n_rows, d), F32)
    return pl.pallas_call(
        functools.partial(_dispatch_kernel, n_tok=n_tok),
        grid_spec=pltpu.PrefetchScalarGridSpec(
            num_scalar_prefetch=1,
            grid=(1,),
            in_specs=[pl.BlockSpec(memory_space=pl.ANY), pl.BlockSpec(memory_space=pl.ANY)],
            out_specs=pl.BlockSpec(memory_space=pl.ANY),
            scratch_shapes=[pltpu.SemaphoreType.DMA(())],
        ),
        out_shape=jax.ShapeDtypeStruct((n_rows, d), F32),
        input_output_aliases={2: 0},
        compiler_params=_cparams(("arbitrary",)),
        name="moe_dispatch",
    )(rows_flat, h, xs0)


def _expert_kernel(be_ref, nu_ref, x_ref, wg_ref, wu_ref, wd_ref, o_ref):
    del be_ref

    @pl.when(pl.program_id(0) < nu_ref[0])
    def _():
        xb = x_ref[...].astype(BF16)
        a = jnp.dot(xb, wg_ref[0], preferred_element_type=F32)
        b = jnp.dot(xb, wu_ref[0], preferred_element_type=F32)
        hb = (jax.nn.silu(a) * b).astype(BF16)
        o_ref[...] = jnp.dot(hb, wd_ref[0], preferred_element_type=F32)


def _experts(blk_exp, n_used, xs, wg, wu, wd):
    n_rows, d = xs.shape
    n_blocks = n_rows // MOE_ROWS
    de = wg.shape[2]
    row_map = lambda i, be, nu: (jnp.minimum(i, nu[0] - 1), 0)
    return pl.pallas_call(
        _expert_kernel,
        grid_spec=pltpu.PrefetchScalarGridSpec(
            num_scalar_prefetch=2,
            grid=(n_blocks,),
            in_specs=[
                pl.BlockSpec((MOE_ROWS, d), row_map),
                pl.BlockSpec((1, d, de), lambda i, be, nu: (be[i], 0, 0)),
                pl.BlockSpec((1, d, de), lambda i, be, nu: (be[i], 0, 0)),
                pl.BlockSpec((1, de, d), lambda i, be, nu: (be[i], 0, 0)),
            ],
            out_specs=pl.BlockSpec((MOE_ROWS, d), row_map),
        ),
        out_shape=jax.ShapeDtypeStruct((n_rows, d), F32),
        compiler_params=_cparams(("arbitrary",)),
        name="moe_experts",
    )(blk_exp, n_used, xs, wg, wu, wd)


def _combine_kernel(rows_ref, ys_hbm, x_ref, meta_ref, nf_ref, o_ref, buf_ref, sem):
    tm = x_ref.shape[0]
    i = pl.program_id(0)

    def token(t, carry):
        n = i * tm + t
        for k in range(TOP_K_FINE):
            r = rows_ref[n * TOP_K_FINE + k]
            pltpu.make_async_copy(ys_hbm.at[pl.ds(r, 1)], buf_ref.at[k, pl.ds(t, 1)], sem).start()
        return carry

    lax.fori_loop(0, tm, token, 0)
    for k in range(TOP_K_FINE):
        pltpu.make_async_copy(ys_hbm.at[pl.ds(0, tm)], buf_ref.at[k], sem).wait()
    meta = meta_ref[...]
    y = x_ref[...] + (buf_ref[0] * meta[:, 2:3] + buf_ref[1] * meta[:, 3:4])
    ms = jnp.mean(y * y, axis=-1, keepdims=True)
    o_ref[...] = y * lax.rsqrt(ms + EPS) * nf_ref[...]


def _combine(rows_flat, ys, x, meta, norm_f, tm=256):
    n, d = x.shape
    tm = min(tm, n)
    return pl.pallas_call(
        _combine_kernel,
        grid_spec=pltpu.PrefetchScalarGridSpec(
            num_scalar_prefetch=1,
            grid=(n // tm,),
            in_specs=[
                pl.BlockSpec(memory_space=pl.ANY),
                pl.BlockSpec((tm, d), lambda i, rows: (i, 0)),
                pl.BlockSpec((tm, ROUTER_COLS), lambda i, rows: (i, 0)),
                pl.BlockSpec((1, d), lambda i, rows: (0, 0)),
            ],
            out_specs=pl.BlockSpec((tm, d), lambda i, rows: (i, 0)),
            scratch_shapes=[pltpu.VMEM((TOP_K_FINE, tm, d), F32), pltpu.SemaphoreType.DMA(())],
        ),
        out_shape=jax.ShapeDtypeStruct((n, d), F32),
        compiler_params=_cparams(("arbitrary",)),
        name="moe_combine",
    )(rows_flat, ys, x, meta, norm_f.reshape(1, d))


def _moe_and_final_norm(x, norm_ffn, w_coarse, b_coarse, w_fine, b_fine, w_gate, w_up, w_down, norm_f):
    n_tok, d = x.shape
    pad = ROUTER_COLS - N_GROUPS - N_EXPERTS
    w_router = jnp.concatenate([w_coarse, w_fine, jnp.zeros((d, pad), F32)], axis=1)
    b_router = jnp.concatenate([b_coarse, b_fine, jnp.zeros((pad,), F32)]).reshape(1, ROUTER_COLS)
    h, meta, counts = _router(x, norm_ffn, w_router, b_router)

    n_asg = n_tok * TOP_K_FINE
    n_blocks = -(-n_asg // MOE_ROWS) + N_EXPERTS
    n_rows = n_blocks * MOE_ROWS
    cnt = counts[0, ROUTER_OFF:ROUTER_OFF + N_EXPERTS].astype(I32)
    padded = (cnt + MOE_ROWS - 1) // MOE_ROWS * MOE_ROWS
    pend = jnp.cumsum(padded)
    pstart = pend - padded
    expert = meta[:, 0:TOP_K_FINE].astype(I32)
    rank = meta[:, 4:4 + TOP_K_FINE].astype(I32)
    rows_flat = (pstart[expert] + rank).reshape(-1)
    n_used = (pend[-1] // MOE_ROWS).astype(I32)
    blk = jnp.arange(n_blocks, dtype=I32)
    blk_exp = jnp.searchsorted(pend, jnp.minimum(blk, n_used - 1) * MOE_ROWS, side='right').astype(I32)
    blk_exp = jnp.minimum(blk_exp, N_EXPERTS - 1)

    xs = _dispatch(rows_flat, h, n_rows)
    ys = _experts(blk_exp, n_used.reshape(1), xs, w_gate.astype(BF16), w_up.astype(BF16), w_down.astype(BF16))
    return _combine(rows_flat, ys, x, meta, norm_f)


def _mixer(x, s_ret, s_conv, pos0, norm_mix, w_in, w_ret_o, conv_w, conv_b, conv_ln_w, conv_ln_b,
           w_conv_o, w_o, total_rows, row0, prev):
    b, l, d = x.shape
    x2 = x.reshape(b * l, d)
    h = _rmsnorm_cast(x2, norm_mix)
    z = _inproj(h, w_in)
    if s_ret is None:
        og, s_ret_new = _retention_prompt(z, b, l)
        buf = jnp.zeros((b, CONV_HIST, d), F32)
        sw, s_conv_new = _conv_branch(z, buf, conv_w, conv_b, conv_ln_w, conv_ln_b, b, l, tl=min(64, l), nb=1)
    else:
        og, s_ret_new = _retention_sample(z, s_ret, b, l, pos0)
        sw, s_conv_new = _conv_branch(z, s_conv, conv_w, conv_b, conv_ln_w, conv_ln_b, b, l, tl=l, nb=2)
    merged = _merge(og, sw, w_ret_o, w_conv_o, z)
    x1 = _outproj_residual(merged, w_o, x2, total_rows, row0, prev)
    return x1, s_ret_new, s_conv_new


def kernel(x_prompt, x_sample, state_ret, state_conv, norm_mix, w_in, w_ret_o, conv_w, conv_b, conv_ln_w,
           conv_ln_b, w_conv_o, w_o, norm_ffn, w_coarse, b_coarse, w_fine, b_fine, w_gate, w_up, w_down, norm_f):
    depth = w_in.shape[0]
    assert depth == 1, "single-layer trunk"
    bp, lp, d = x_prompt.shape
    bs, ls, _ = x_sample.shape
    n_p = bp * lp
    n_s = bs * ls
    mix_w = (norm_mix[0], w_in[0], w_ret_o[0], conv_w[0], conv_b[0], conv_ln_w[0], conv_ln_b[0], w_conv_o[0],
             w_o[0])
    x1, ret_p, conv_p = _mixer(x_prompt, None, None, 0, *mix_w, n_p + n_s, 0, None)
    x1, ret_s, conv_s = _mixer(x_sample, state_ret[0], state_conv[0], PAST_LEN, *mix_w, n_p + n_s, n_p, x1)
    y = _moe_and_final_norm(x1, norm_ffn[0], w_coarse[0], b_coarse[0], w_fine[0], b_fine[0],
                            w_gate[0], w_up[0], w_down[0], norm_f)
    return (y[:n_p].reshape(bp, lp, d), y[n_p:].reshape(bs, ls, d),
            ret_p[None], conv_p[None], ret_s[None], conv_s[None])
```

```python
import functools

import jax
import jax.numpy as jnp
from jax import lax
from jax.experimental import pallas as pl
from jax.experimental.pallas import tpu as pltpu

F32 = jnp.float32
BF16 = jnp.bfloat16
I32 = jnp.int32

D_MODEL = 2048
PAST_LEN = 16384
RET_HEADS = 8
RET_DK = D_MODEL // RET_HEADS
RET_DV = 2 * D_MODEL // RET_HEADS
RET_HALF = RET_DK // 2
RET_CHUNK = 128
ROPE_BASE = 10000.0
CONV_WIDTH = 31
CONV_HIST = CONV_WIDTH - 1
N_GROUPS = 4
EXPERTS_PER_GROUP = 8
N_EXPERTS = N_GROUPS * EXPERTS_PER_GROUP
TOP_K_FINE = 2
D_EXPERT = D_MODEL // 2
EPS = 1e-6

COL_Q = 0
COL_K = COL_Q + RET_HEADS * RET_DK
COL_V = COL_K + RET_HEADS * RET_DK
COL_G = COL_V + RET_HEADS * RET_DV
COL_CA = COL_G + RET_HEADS * RET_DV
COL_CB = COL_CA + D_MODEL
COL_GRET = COL_CB + D_MODEL
COL_GCONV = COL_GRET + D_MODEL
IN_COLS = COL_GCONV + D_MODEL

LANES = 128
SUBLANES = 8
VMEM_LIMIT = 56 * 1024 * 1024
MOE_ROWS = 256
ROUTER_COLS = LANES
ROUTER_OFF = N_GROUPS
NEG = -3.0e38


def _cparams(sem):
    return pltpu.CompilerParams(dimension_semantics=sem, vmem_limit_bytes=VMEM_LIMIT)


def _rmsnorm_kernel(x_ref, w_ref, o_ref):
    x = x_ref[...]
    ms = jnp.mean(x * x, axis=-1, keepdims=True)
    o_ref[...] = (x * lax.rsqrt(ms + EPS) * w_ref[...]).astype(o_ref.dtype)


def _rmsnorm_cast(x, w, tm=512):
    m, d = x.shape
    tm = min(tm, m)
    return pl.pallas_call(
        _rmsnorm_kernel,
        grid=(m // tm,),
        in_specs=[pl.BlockSpec((tm, d), lambda i: (i, 0)), pl.BlockSpec((1, d), lambda i: (0, 0))],
        out_specs=pl.BlockSpec((tm, d), lambda i: (i, 0)),
        out_shape=jax.ShapeDtypeStruct((m, d), BF16),
        compiler_params=_cparams(("arbitrary",)),
        name="rmsnorm_cast",
    )(x, w.reshape(1, d))


def _inproj_kernel(a_ref, w_ref, o_ref, wb_ref):
    @pl.when(pl.program_id(1) == 0)
    def _():
        wb_ref[...] = w_ref[...].astype(BF16)

    o_ref[...] = jnp.dot(a_ref[...], wb_ref[...], preferred_element_type=F32)


def _inproj(a, w, tm=1024, tn=1024):
    m, k = a.shape
    n = w.shape[1]
    tm = min(tm, m)
    return pl.pallas_call(
        _inproj_kernel,
        grid=(n // tn, m // tm),
        in_specs=[pl.BlockSpec((tm, k), lambda j, i: (i, 0)), pl.BlockSpec((k, tn), lambda j, i: (0, j))],
        out_specs=pl.BlockSpec((tm, tn), lambda j, i: (i, j)),
        out_shape=jax.ShapeDtypeStruct((m, n), F32),
        scratch_shapes=[pltpu.VMEM((k, tn), BF16)],
        compiler_params=_cparams(("arbitrary", "arbitrary")),
        name="inproj",
    )(a, w)


def _rotary_pair(x1, x2, cos, sin):
    return jnp.concatenate([x1 * cos - x2 * sin, x1 * sin + x2 * cos], axis=1)


def _retention_tables(c):
    h = RET_HEADS
    log_g = jnp.log1p(-jnp.exp2(-5.0 - jnp.arange(h, dtype=F32)))
    idx = jnp.arange(c, dtype=F32)
    diff = idx[:, None] - idx[None, :]
    decay = jnp.where(diff[None] >= 0.0, jnp.exp(jnp.maximum(diff, 0.0)[None] * log_g[:, None, None]), 0.0)
    q_dec = jnp.exp((idx[:, None] + 1.0) * log_g[None, :])
    k_dec = jnp.exp((c - 1.0 - idx)[:, None] * log_g[None, :])
    c_dec = jnp.exp(c * log_g)
    return decay, q_dec, k_dec, c_dec


def _rope_tables(pos):
    inv = ROPE_BASE ** (-jnp.arange(RET_HALF, dtype=F32) / RET_HALF)
    ang = pos.astype(F32)[:, None] * inv[None, :]
    return jnp.cos(ang), jnp.sin(ang)


def _lane_bcast(t):
    return jnp.broadcast_to(t.T[:, :, None], (t.shape[1], t.shape[0], LANES))


def _group_norm_gate(o, g):
    ms = jnp.mean(o * o, axis=-1, keepdims=True)
    return jax.nn.silu(g) * (o * lax.rsqrt(ms + EPS))


def _ret_prompt_kernel(cdec_ref, q_ref, k_ref, v_ref, g_ref, cos_ref, sin_ref, dec_ref, qd_ref, kd_ref,
                       o_ref, s_ref):
    @pl.when(pl.program_id(1) == 0)
    def _():
        s_ref[...] = jnp.zeros_like(s_ref)

    cos = cos_ref[...]
    sin = sin_ref[...]
    for h in range(RET_HEADS):
        c0 = h * RET_DK
        qr = _rotary_pair(q_ref[:, c0:c0 + RET_HALF], q_ref[:, c0 + RET_HALF:c0 + RET_DK], cos, sin)
        kr = _rotary_pair(k_ref[:, c0:c0 + RET_HALF], k_ref[:, c0 + RET_HALF:c0 + RET_DK], cos, sin)
        kr = kr * (RET_DK ** -0.5)
        qb = qr.astype(BF16)
        kb = kr.astype(BF16)
        vb = v_ref[:, h * RET_DV:(h + 1) * RET_DV].astype(BF16)
        att = lax.dot_general(qb, kb, (((1,), (1,)), ((), ())), preferred_element_type=F32) * dec_ref[h]
        s = s_ref[0, h]
        o = (jnp.dot(att.astype(BF16), vb, preferred_element_type=F32)
             + qd_ref[h][:, :1] * jnp.dot(qb, s.astype(BF16), preferred_element_type=F32))
        kk = (kr * kd_ref[h][:, :1]).astype(BF16)
        s_ref[0, h] = s * cdec_ref[h] + lax.dot_general(kk, vb, (((0,), (0,)), ((), ())),
                                                         preferred_element_type=F32)
        og = _group_norm_gate(o, g_ref[:, h * RET_DV:(h + 1) * RET_DV])
        o_ref[:, h * RET_DV:(h + 1) * RET_DV] = og.astype(o_ref.dtype)


def _retention_prompt(z, b, l):
    c = RET_CHUNK if l % RET_CHUNK == 0 else l
    nc = l // c
    decay, q_dec, k_dec, c_dec = _retention_tables(c)
    cos, sin = _rope_tables(jnp.arange(l))
    wq = RET_HEADS * RET_DK
    wv = RET_HEADS * RET_DV
    full3 = lambda bi, ci: (0, 0, 0)
    return pl.pallas_call(
        _ret_prompt_kernel,
        grid=(b, nc),
        in_specs=[
            pl.BlockSpec(memory_space=pltpu.SMEM),
            pl.BlockSpec((c, wq), lambda bi, ci: (bi * nc + ci, COL_Q // wq)),
            pl.BlockSpec((c, wq), lambda bi, ci: (bi * nc + ci, COL_K // wq)),
            pl.BlockSpec((c, wv), lambda bi, ci: (bi * nc + ci, COL_V // wv)),
            pl.BlockSpec((c, wv), lambda bi, ci: (bi * nc + ci, COL_G // wv)),
            pl.BlockSpec((c, RET_HALF), lambda bi, ci: (ci, 0)),
            pl.BlockSpec((c, RET_HALF), lambda bi, ci: (ci, 0)),
            pl.BlockSpec((RET_HEADS, c, c), full3),
            pl.BlockSpec((RET_HEADS, c, LANES), full3),
            pl.BlockSpec((RET_HEADS, c, LANES), full3),
        ],
        out_specs=[
            pl.BlockSpec((c, wv), lambda bi, ci: (bi * nc + ci, 0)),
            pl.BlockSpec((1, RET_HEADS, RET_DK, RET_DV), lambda bi, ci: (bi, 0, 0, 0)),
        ],
        out_shape=[
            jax.ShapeDtypeStruct((b * l, wv), BF16),
            jax.ShapeDtypeStruct((b, RET_HEADS, RET_DK, RET_DV), F32),
        ],
        compiler_params=_cparams(("arbitrary", "arbitrary")),
        name="retention_prompt",
    )(c_dec, z, z, z, z, cos, sin, decay, _lane_bcast(q_dec), _lane_bcast(k_dec))


SAMPLE_PAD = 16


def _pad_rows(x, rows):
    return jnp.concatenate([x, jnp.zeros((rows - x.shape[0], x.shape[1]), x.dtype)], axis=0)


def _ret_sample_kernel(cdec_ref, q_ref, k_ref, v_ref, g_ref, cos_ref, sin_ref, dec_ref, qd_ref, kd_ref,
                       s0_ref, o_ref, s_ref, *, nb, l):
    cos = cos_ref[...]
    sin = sin_ref[...]
    for bt in range(nb):
        r0 = bt * l
        for h in range(RET_HEADS):
            c0 = h * RET_DK
            qr = _rotary_pair(q_ref[r0:r0 + l, c0:c0 + RET_HALF], q_ref[r0:r0 + l, c0 + RET_HALF:c0 + RET_DK],
                              cos, sin)
            kr = _rotary_pair(k_ref[r0:r0 + l, c0:c0 + RET_HALF], k_ref[r0:r0 + l, c0 + RET_HALF:c0 + RET_DK],
                              cos, sin)
            kr = kr * (RET_DK ** -0.5)
            qb = _pad_rows(qr, SAMPLE_PAD).astype(BF16)
            kb = _pad_rows(kr, SAMPLE_PAD).astype(BF16)
            vb = _pad_rows(v_ref[r0:r0 + l, h * RET_DV:(h + 1) * RET_DV], SAMPLE_PAD).astype(BF16)
            att = lax.dot_general(qb, kb, (((1,), (1,)), ((), ())), preferred_element_type=F32) * dec_ref[h]
            s = s0_ref[bt, h]
            o = (jnp.dot(att.astype(BF16), vb, preferred_element_type=F32)
                 + qd_ref[h][:, :1] * jnp.dot(qb, s.astype(BF16), preferred_element_type=F32))
            kk = _pad_rows(kr * kd_ref[h][:l, :1], SAMPLE_PAD).astype(BF16)
            s_ref[bt, h] = s * cdec_ref[h] + lax.dot_general(kk, vb, (((0,), (0,)), ((), ())),
                                                              preferred_element_type=F32)
            og = _group_norm_gate(o[:l], g_ref[r0:r0 + l, h * RET_DV:(h + 1) * RET_DV])
            o_ref[r0:r0 + l, h * RET_DV:(h + 1) * RET_DV] = og.astype(o_ref.dtype)


def _retention_sample(z, s0, b, l, pos0, nb=2):
    assert l % RET_CHUNK != 0 and l <= SAMPLE_PAD and b % nb == 0, (b, l)
    decay, q_dec, k_dec, c_dec = _retention_tables(l)
    p = SAMPLE_PAD - l
    decay = jnp.pad(decay, ((0, 0), (0, p), (0, p)))
    q_dec = jnp.pad(q_dec, ((0, p), (0, 0)))
    k_dec = jnp.pad(k_dec, ((0, p), (0, 0)))
    cos, sin = _rope_tables(pos0 + jnp.arange(l))
    wq = RET_HEADS * RET_DK
    wv = RET_HEADS * RET_DV
    rows = nb * l
    full2 = lambda bi: (0, 0)
    full3 = lambda bi: (0, 0, 0)
    state_spec = pl.BlockSpec((nb, RET_HEADS, RET_DK, RET_DV), lambda bi: (bi, 0, 0, 0))
    return pl.pallas_call(
        functools.partial(_ret_sample_kernel, nb=nb, l=l),
        grid=(b // nb,),
        in_specs=[
            pl.BlockSpec(memory_space=pltpu.SMEM),
            pl.BlockSpec((rows, wq), lambda bi: (bi, COL_Q // wq)),
            pl.BlockSpec((rows, wq), lambda bi: (bi, COL_K // wq)),
            pl.BlockSpec((rows, wv), lambda bi: (bi, COL_V // wv)),
            pl.BlockSpec((rows, wv), lambda bi: (bi, COL_G // wv)),
            pl.BlockSpec((l, RET_HALF), full2),
            pl.BlockSpec((l, RET_HALF), full2),
            pl.BlockSpec((RET_HEADS, SAMPLE_PAD, SAMPLE_PAD), full3),
            pl.BlockSpec((RET_HEADS, SAMPLE_PAD, LANES), full3),
            pl.BlockSpec((RET_HEADS, SAMPLE_PAD, LANES), full3),
            state_spec,
        ],
        out_specs=[pl.BlockSpec((rows, wv), lambda bi: (bi, 0)), state_spec],
        out_shape=[
            jax.ShapeDtypeStruct((b * l, wv), BF16),
            jax.ShapeDtypeStruct((b, RET_HEADS, RET_DK, RET_DV), F32),
        ],
        compiler_params=_cparams(("arbitrary",)),
        name="retention_sample",
    )(c_dec, z, z, z, z, cos, sin, decay, _lane_bcast(q_dec), _lane_bcast(k_dec), s0)


CONV_PAD = 32
CONV_COLS = LANES


def _conv_strip(block, wts, nblk, store):
    lead = CONV_PAD - CONV_HIST
    taps = [((lead + w) // SUBLANES, (lead + w) % SUBLANES) for w in range(CONV_WIDTH)]
    rowid = lax.broadcasted_iota(I32, (SUBLANES, CONV_COLS), 0)

    def partial_sums(m, shifts):
        ys = {}
        for w, (a, s) in enumerate(taps):
            if s in shifts:
                term = block(m + a) * wts[w]
                ys[s] = term if s not in ys else ys[s] + term
        return ys

    shifted = range(1, SUBLANES)
    prev = partial_sums(0, shifted)
    for m in range(nblk):
        nxt = partial_sums(m + 1, shifted)
        acc = partial_sums(m, (0,))[0]
        for s in shifted:
            acc = acc + pltpu.roll(jnp.where(rowid >= s, prev[s], nxt[s]), SUBLANES - s, axis=0)
        store(m, acc)
        prev = nxt


def _conv_kernel(ca_ref, cb_ref, buf_ref, cw_ref, bias_ref, lnw_ref, lnb_ref, sw_ref, nbuf_ref,
                 upad_ref, c_ref, *, tl, nb, nl):
    li = pl.program_id(1)
    lead = CONV_PAD - CONV_HIST
    d = ca_ref.shape[1]
    for bt in range(nb):
        @pl.when(li == 0)
        def _():
            upad_ref[bt, 0:lead, :] = jnp.zeros((lead, d), F32)
            upad_ref[bt, lead:CONV_PAD, :] = buf_ref[bt]

        r0 = bt * tl
        upad_ref[bt, CONV_PAD:CONV_PAD + tl, :] = ca_ref[r0:r0 + tl, :] * jax.nn.sigmoid(cb_ref[r0:r0 + tl, :])
        for cc in range(d // CONV_COLS):
            cs = slice(cc * CONV_COLS, (cc + 1) * CONV_COLS)
            wts = [jnp.broadcast_to(cw_ref[w:w + 1, cs], (SUBLANES, CONV_COLS)) for w in range(CONV_WIDTH)]
            blocks = {}

            def block(j, cs=cs, blocks=blocks):
                if j not in blocks:
                    blocks[j] = upad_ref[bt, j * SUBLANES:(j + 1) * SUBLANES, cs]
                return blocks[j]

            def store(m, acc, cs=cs):
                c_ref[m * SUBLANES:(m + 1) * SUBLANES, cs] = acc

            _conv_strip(block, wts, tl // SUBLANES, store)
        cf = c_ref[...] + bias_ref[...]
        mu = jnp.mean(cf, axis=-1, keepdims=True)
        var = jnp.mean(jnp.square(cf - mu), axis=-1, keepdims=True)
        cn = (cf - mu) * lax.rsqrt(var + EPS) * lnw_ref[...] + lnb_ref[...]
        sw_ref[r0:r0 + tl, :] = jax.nn.silu(cn).astype(sw_ref.dtype)

        @pl.when(li == nl - 1)
        def _():
            nbuf_ref[bt] = upad_ref[bt, tl + lead:tl + CONV_PAD, :]

        if nl > 1:
            upad_ref[bt, 0:CONV_PAD, :] = upad_ref[bt, tl:tl + CONV_PAD, :]


def _conv_branch(z, buf, conv_w, conv_b, ln_w, ln_b, b, l, tl, nb):
    d = D_MODEL
    nl = l // tl
    assert nb == 1 or nl == 1
    rows = nb * tl
    vec = lambda a: a.reshape(1, d)
    const2 = lambda bi, li: (0, 0)
    return pl.pallas_call(
        functools.partial(_conv_kernel, tl=tl, nb=nb, nl=nl),
        grid=(b // nb, nl),
        in_specs=[
            pl.BlockSpec((rows, d), lambda bi, li: (bi * nl + li, COL_CA // d)),
            pl.BlockSpec((rows, d), lambda bi, li: (bi * nl + li, COL_CB // d)),
            pl.BlockSpec((nb, CONV_HIST, d), lambda bi, li: (bi, 0, 0)),
            pl.BlockSpec((CONV_WIDTH, d), const2),
            pl.BlockSpec((1, d), const2),
            pl.BlockSpec((1, d), const2),
            pl.BlockSpec((1, d), const2),
        ],
        out_specs=[
            pl.BlockSpec((rows, d), lambda bi, li: (bi * nl + li, 0)),
            pl.BlockSpec((nb, CONV_HIST, d), lambda bi, li: (bi, 0, 0)),
        ],
        out_shape=[
            jax.ShapeDtypeStruct((b * l, d), BF16),
            jax.ShapeDtypeStruct((b, CONV_HIST, d), F32),
        ],
        scratch_shapes=[pltpu.VMEM((nb, CONV_PAD + tl, d), F32), pltpu.VMEM((tl, d), F32)],
        compiler_params=_cparams(("arbitrary", "arbitrary")),
        name="conv_branch",
    )(z, z, buf, conv_w, vec(conv_b), vec(ln_w), vec(ln_b))


def _merge_kernel(og_ref, sw_ref, wr_ref, wc_ref, gr_ref, gc_ref, o_ref, wrb_ref, wcb_ref):
    @pl.when(pl.program_id(1) == 0)
    def _():
        wrb_ref[...] = wr_ref[...].astype(BF16)
        wcb_ref[...] = wc_ref[...].astype(BF16)

    ret_out = jnp.dot(og_ref[...], wrb_ref[...], preferred_element_type=F32)
    conv_out = jnp.dot(sw_ref[...], wcb_ref[...], preferred_element_type=F32)
    merged = jax.nn.sigmoid(gr_ref[...]) * ret_out + jax.nn.sigmoid(gc_ref[...]) * conv_out
    o_ref[...] = merged.astype(o_ref.dtype)


def _merge(og, sw, w_ret_o, w_conv_o, z, tm=512, tn=512):
    m = og.shape[0]
    n = w_ret_o.shape[1]
    tm = min(tm, m)
    kr = og.shape[1]
    kc = sw.shape[1]
    return pl.pallas_call(
        _merge_kernel,
        grid=(n // tn, m // tm),
        in_specs=[
            pl.BlockSpec((tm, kr), lambda j, i: (i, 0)),
            pl.BlockSpec((tm, kc), lambda j, i: (i, 0)),
            pl.BlockSpec((kr, tn), lambda j, i: (0, j)),
            pl.BlockSpec((kc, tn), lambda j, i: (0, j)),
            pl.BlockSpec((tm, tn), lambda j, i: (i, COL_GRET // tn + j)),
            pl.BlockSpec((tm, tn), lambda j, i: (i, COL_GCONV // tn + j)),
        ],
        out_specs=pl.BlockSpec((tm, tn), lambda j, i: (i, j)),
        out_shape=jax.ShapeDtypeStruct((m, n), BF16),
        scratch_shapes=[pltpu.VMEM((kr, tn), BF16), pltpu.VMEM((kc, tn), BF16)],
        compiler_params=_cparams(("arbitrary", "arbitrary")),
        name="merge",
    )(og, sw, w_ret_o, w_conv_o, z, z)


def _outproj_kernel(a_ref, w_ref, x_ref, *rest):
    o_ref, wb_ref = rest[-2], rest[-1]

    @pl.when(pl.program_id(1) == 0)
    def _():
        wb_ref[...] = w_ref[...].astype(BF16)

    o_ref[...] = x_ref[...] + jnp.dot(a_ref[...], wb_ref[...], preferred_element_type=F32)


def _outproj_residual(a, w, x, total_rows, row0, prev=None, tm=512, tn=512):
    m, k = a.shape
    n = w.shape[1]
    tm = min(tm, m)
    assert row0 % tm == 0
    off = row0 // tm
    in_specs = [
        pl.BlockSpec((tm, k), lambda j, i: (i, 0)),
        pl.BlockSpec((k, tn), lambda j, i: (0, j)),
        pl.BlockSpec((tm, tn), lambda j, i: (i, j)),
    ]
    args = [a, w, x]
    aliases = {}
    if prev is not None:
        in_specs.append(pl.BlockSpec(memory_space=pl.ANY))
        args.append(prev)
        aliases = {3: 0}
    return pl.pallas_call(
        _outproj_kernel,
        grid=(n // tn, m // tm),
        in_specs=in_specs,
        out_specs=pl.BlockSpec((tm, tn), lambda j, i: (i + off, j)),
        out_shape=jax.ShapeDtypeStruct((total_rows, n), F32),
        scratch_shapes=[pltpu.VMEM((k, tn), BF16)],
        input_output_aliases=aliases,
        compiler_params=_cparams(("arbitrary", "arbitrary")),
        name="outproj_residual",
    )(*args)


def _router_kernel(x_ref, nw_ref, w_ref, b_ref, h_ref, meta_ref, cnt_ref, base_ref):
    tm = x_ref.shape[0]

    @pl.when(pl.program_id(0) == 0)
    def _():
        base_ref[...] = jnp.zeros_like(base_ref)

    x = x_ref[...]
    ms = jnp.mean(x * x, axis=-1, keepdims=True)
    h = x * lax.rsqrt(ms + EPS) * nw_ref[...]
    h_ref[...] = h
    logits = jnp.dot(h, w_ref[...], preferred_element_type=F32, precision=lax.Precision.HIGHEST) + b_ref[...]
    lane = lax.broadcasted_iota(I32, (tm, ROUTER_COLS), 1)

    def first_argmax(vals):
        top = jnp.max(vals, axis=1, keepdims=True)
        return top, jnp.min(jnp.where(vals == top, lane, ROUTER_COLS), axis=1, keepdims=True)

    cmask = lane < N_GROUPS
    lc = jnp.where(cmask, logits, NEG)
    mc, grp = first_argmax(lc)
    p_grp = 1.0 / jnp.sum(jnp.where(cmask, jnp.exp(lc - mc), 0.0), axis=1, keepdims=True)
    lo = ROUTER_OFF + grp * EXPERTS_PER_GROUP
    lf = jnp.where((lane >= lo) & (lane < lo + EXPERTS_PER_GROUP), logits, NEG)
    v1, i1 = first_argmax(lf)
    v2, i2 = first_argmax(jnp.where(lane == i1, NEG, lf))
    e2 = jnp.exp(v2 - v1)
    den = 1.0 + e2
    g1 = p_grp * (1.0 / den)
    g2 = p_grp * (e2 / den)

    oh1 = lane == i1
    oh2 = lane == i2
    both = jnp.where(oh1 | oh2, 1.0, 0.0)
    row = lax.broadcasted_iota(I32, (tm, tm), 0)
    col = lax.broadcasted_iota(I32, (tm, tm), 1)
    earlier = jnp.where(col < row, 1.0, 0.0).astype(BF16)
    before = jnp.dot(earlier, both.astype(BF16), preferred_element_type=F32) + base_ref[...]
    r1 = jnp.sum(jnp.where(oh1, before, 0.0), axis=1, keepdims=True)
    r2 = jnp.sum(jnp.where(oh2, before, 0.0), axis=1, keepdims=True)
    base_ref[...] = base_ref[...] + jnp.sum(both, axis=0, keepdims=True)
    cnt_ref[...] = base_ref[...]

    fields = [(i1 - ROUTER_OFF).astype(F32), (i2 - ROUTER_OFF).astype(F32), g1, g2, r1, r2]
    meta = jnp.zeros((tm, ROUTER_COLS), F32)
    for k, f in enumerate(fields):
        meta = jnp.where(lane == k, f, meta)
    meta_ref[...] = meta


def _router(x, norm_w, w_router, b_router, tm=512):
    n, d = x.shape
    tm = min(tm, n)
    const2 = lambda i: (0, 0)
    return pl.pallas_call(
        _router_kernel,
        grid=(n // tm,),
        in_specs=[
            pl.BlockSpec((tm, d), lambda i: (i, 0)),
            pl.BlockSpec((1, d), const2),
            pl.BlockSpec((d, ROUTER_COLS), const2),
            pl.BlockSpec((1, ROUTER_COLS), const2),
        ],
        out_specs=[
            pl.BlockSpec((tm, d), lambda i: (i, 0)),
            pl.BlockSpec((tm, ROUTER_COLS), lambda i: (i, 0)),
            pl.BlockSpec((1, ROUTER_COLS), const2),
        ],
        out_shape=[
            jax.ShapeDtypeStruct((n, d), F32),
            jax.ShapeDtypeStruct((n, ROUTER_COLS), F32),
            jax.ShapeDtypeStruct((1, ROUTER_COLS), F32),
        ],
        scratch_shapes=[pltpu.VMEM((1, ROUTER_COLS), F32)],
        compiler_params=_cparams(("arbitrary",)),
        name="router",
    )(x, norm_w.reshape(1, d), w_router, b_router)


def _dispatch_kernel(rows_ref, span_ref, h_ref, xs_hbm, zero_ref, sem, zsem):
    td = h_ref.shape[0]
    i = pl.program_id(0)

    @pl.when(i == 0)
    def _():
        zero_ref[...] = jnp.zeros_like(zero_ref)

        def fill_copy(e):
            last = pl.multiple_of(span_ref[N_EXPERTS + e] - MOE_ROWS, MOE_ROWS)
            return pltpu.make_async_copy(zero_ref, xs_hbm.at[pl.ds(last, MOE_ROWS)], zsem)

        def nonempty(e):
            return span_ref[N_EXPERTS + e] > span_ref[e]

        def start(e, carry):
            @pl.when(nonempty(e))
            def _():
                fill_copy(e).start()
            return carry

        def wait(e, carry):
            @pl.when(nonempty(e))
            def _():
                fill_copy(e).wait()
            return carry

        lax.fori_loop(0, N_EXPERTS, start, 0)
        lax.fori_loop(0, N_EXPERTS, wait, 0)

    def token(t, carry):
        n = i * td + t
        for k in range(TOP_K_FINE):
            r = rows_ref[n * TOP_K_FINE + k]
            pltpu.make_async_copy(h_ref.at[pl.ds(t, 1)], xs_hbm.at[pl.ds(r, 1)], sem).start()
        return carry

    lax.fori_loop(0, td, token, 0)
    for _ in range(TOP_K_FINE):
        pltpu.make_async_copy(h_ref, xs_hbm.at[pl.ds(0, td)], sem).wait()


def _dispatch(rows_flat, span, h, n_rows, td=512):
    n_tok, d = h.shape
    td = min(td, n_tok)
    return pl.pallas_call(
        _dispatch_kernel,
        grid_spec=pltpu.PrefetchScalarGridSpec(
            num_scalar_prefetch=2,
            grid=(n_tok // td,),
            in_specs=[pl.BlockSpec((td, d), lambda i, rows, span: (i, 0))],
            out_specs=pl.BlockSpec(memory_space=pl.ANY),
            scratch_shapes=[pltpu.VMEM((MOE_ROWS, d), F32), pltpu.SemaphoreType.DMA(()),
                            pltpu.SemaphoreType.DMA(())],
        ),
        out_shape=jax.ShapeDtypeStruct((n_rows, d), F32),
        compiler_params=_cparams(("arbitrary",)),
        name="moe_dispatch",
    )(rows_flat, span, h)


def _expert_kernel(be_ref, nu_ref, x_ref, wg_ref, wu_ref, wd_ref, o_ref):
    del be_ref

    @pl.when(pl.program_id(0) < nu_ref[0])
    def _():
        xb = x_ref[...].astype(BF16)
        a = jnp.dot(xb, wg_ref[0], preferred_element_type=F32)
        b = jnp.dot(xb, wu_ref[0], preferred_element_type=F32)
        hb = (jax.nn.silu(a) * b).astype(BF16)
        o_ref[...] = jnp.dot(hb, wd_ref[0], preferred_element_type=F32)


def _experts(blk_exp, n_used, xs, wg, wu, wd):
    n_rows, d = xs.shape
    n_blocks = n_rows // MOE_ROWS
    de = wg.shape[2]
    row_map = lambda i, be, nu: (jnp.minimum(i, nu[0] - 1), 0)
    return pl.pallas_call(
        _expert_kernel,
        grid_spec=pltpu.PrefetchScalarGridSpec(
            num_scalar_prefetch=2,
            grid=(n_blocks,),
            in_specs=[
                pl.BlockSpec((MOE_ROWS, d), row_map),
                pl.BlockSpec((1, d, de), lambda i, be, nu: (be[i], 0, 0)),
                pl.BlockSpec((1, d, de), lambda i, be, nu: (be[i], 0, 0)),
                pl.BlockSpec((1, de, d), lambda i, be, nu: (be[i], 0, 0)),
            ],
            out_specs=pl.BlockSpec((MOE_ROWS, d), row_map),
        ),
        out_shape=jax.ShapeDtypeStruct((n_rows, d), F32),
        compiler_params=_cparams(("arbitrary",)),
        name="moe_experts",
    )(blk_exp, n_used, xs, wg, wu, wd)


def _combine_kernel(rows_ref, ys_hbm, x_ref, meta_ref, nf_ref, op_ref, os_ref, buf_ref, sem, *, p_tiles):
    tm = x_ref.shape[0]
    i = pl.program_id(0)

    def token(t, carry):
        n = i * tm + t
        for k in range(TOP_K_FINE):
            r = rows_ref[n * TOP_K_FINE + k]
            pltpu.make_async_copy(ys_hbm.at[pl.ds(r, 1)], buf_ref.at[k, pl.ds(t, 1)], sem).start()
        return carry

    lax.fori_loop(0, tm, token, 0)
    for k in range(TOP_K_FINE):
        pltpu.make_async_copy(ys_hbm.at[pl.ds(0, tm)], buf_ref.at[k], sem).wait()
    meta = meta_ref[...]
    y = x_ref[...] + (buf_ref[0] * meta[:, 2:3] + buf_ref[1] * meta[:, 3:4])
    ms = jnp.mean(y * y, axis=-1, keepdims=True)
    out = y * lax.rsqrt(ms + EPS) * nf_ref[...]

    @pl.when(i < p_tiles)
    def _():
        op_ref[...] = out

    @pl.when(i >= p_tiles)
    def _():
        os_ref[...] = out


def _combine(rows_flat, ys, x, meta, norm_f, n_p, tm=256):
    n, d = x.shape
    n_s = n - n_p
    tm = min(tm, n_p, n_s)
    assert n_p % tm == 0 and n_s % tm == 0
    p_tiles = n_p // tm
    return pl.pallas_call(
        functools.partial(_combine_kernel, p_tiles=p_tiles),
        grid_spec=pltpu.PrefetchScalarGridSpec(
            num_scalar_prefetch=1,
            grid=(n // tm,),
            in_specs=[
                pl.BlockSpec(memory_space=pl.ANY),
                pl.BlockSpec((tm, d), lambda i, rows: (i, 0)),
                pl.BlockSpec((tm, ROUTER_COLS), lambda i, rows: (i, 0)),
                pl.BlockSpec((1, d), lambda i, rows: (0, 0)),
            ],
            out_specs=[
                pl.BlockSpec((tm, d), lambda i, rows: (jnp.minimum(i, p_tiles - 1), 0)),
                pl.BlockSpec((tm, d), lambda i, rows: (jnp.maximum(i - p_tiles, 0), 0)),
            ],
            scratch_shapes=[pltpu.VMEM((TOP_K_FINE, tm, d), F32), pltpu.SemaphoreType.DMA(())],
        ),
        out_shape=[jax.ShapeDtypeStruct((n_p, d), F32), jax.ShapeDtypeStruct((n_s, d), F32)],
        compiler_params=_cparams(("arbitrary",)),
        name="moe_combine",
    )(rows_flat, ys, x, meta, norm_f.reshape(1, d))


def _moe_and_final_norm(x, n_p, norm_ffn, w_coarse, b_coarse, w_fine, b_fine, w_gate, w_up, w_down, norm_f):
    n_tok, d = x.shape
    pad = ROUTER_COLS - N_GROUPS - N_EXPERTS
    w_router = jnp.concatenate([w_coarse, w_fine, jnp.zeros((d, pad), F32)], axis=1)
    b_router = jnp.concatenate([b_coarse, b_fine, jnp.zeros((pad,), F32)]).reshape(1, ROUTER_COLS)
    h, meta, counts = _router(x, norm_ffn, w_router, b_router)

    n_asg = n_tok * TOP_K_FINE
    n_blocks = -(-n_asg // MOE_ROWS) + N_EXPERTS
    n_rows = n_blocks * MOE_ROWS
    cnt = counts[0, ROUTER_OFF:ROUTER_OFF + N_EXPERTS].astype(I32)
    padded = (cnt + MOE_ROWS - 1) // MOE_ROWS * MOE_ROWS
    pend = jnp.cumsum(padded)
    pstart = pend - padded
    expert = meta[:, 0:TOP_K_FINE].astype(I32)
    rank = meta[:, 4:4 + TOP_K_FINE].astype(I32)
    rows_flat = (pstart[expert] + rank).reshape(-1)
    n_used = (pend[-1] // MOE_ROWS).astype(I32)
    blk = jnp.arange(n_blocks, dtype=I32)
    blk_exp = jnp.searchsorted(pend, jnp.minimum(blk, n_used - 1) * MOE_ROWS, side='right').astype(I32)
    blk_exp = jnp.minimum(blk_exp, N_EXPERTS - 1)

    xs = _dispatch(rows_flat, jnp.concatenate([pstart, pend]).astype(I32), h, n_rows)
    ys = _experts(blk_exp, n_used.reshape(1), xs, w_gate.astype(BF16), w_up.astype(BF16), w_down.astype(BF16))
    return _combine(rows_flat, ys, x, meta, norm_f, n_p)


def _mixer(x, s_ret, s_conv, pos0, norm_mix, w_in, w_ret_o, conv_w, conv_b, conv_ln_w, conv_ln_b,
           w_conv_o, w_o, total_rows, row0, prev):
    b, l, d = x.shape
    x2 = x.reshape(b * l, d)
    h = _rmsnorm_cast(x2, norm_mix)
    z = _inproj(h, w_in)
    if s_ret is None:
        og, s_ret_new = _retention_prompt(z, b, l)
        buf = jnp.zeros((b, CONV_HIST, d), F32)
        sw, s_conv_new = _conv_branch(z, buf, conv_w, conv_b, conv_ln_w, conv_ln_b, b, l, tl=min(128, l), nb=1)
    else:
        og, s_ret_new = _retention_sample(z, s_ret, b, l, pos0)
        sw, s_conv_new = _conv_branch(z, s_conv, conv_w, conv_b, conv_ln_w, conv_ln_b, b, l, tl=l, nb=2)
    merged = _merge(og, sw, w_ret_o, w_conv_o, z)
    x1 = _outproj_residual(merged, w_o, x2, total_rows, row0, prev)
    return x1, s_ret_new, s_conv_new


def kernel(x_prompt, x_sample, state_ret, state_conv, norm_mix, w_in, w_ret_o, conv_w, conv_b, conv_ln_w,
           conv_ln_b, w_conv_o, w_o, norm_ffn, w_coarse, b_coarse, w_fine, b_fine, w_gate, w_up, w_down, norm_f):
    depth = w_in.shape[0]
    assert depth == 1, "single-layer trunk"
    bp, lp, d = x_prompt.shape
    bs, ls, _ = x_sample.shape
    n_p = bp * lp
    n_s = bs * ls
    mix_w = (norm_mix[0], w_in[0], w_ret_o[0], conv_w[0], conv_b[0], conv_ln_w[0], conv_ln_b[0], w_conv_o[0],
             w_o[0])
    x1, ret_p, conv_p = _mixer(x_prompt, None, None, 0, *mix_w, n_p + n_s, 0, None)
    x1, ret_s, conv_s = _mixer(x_sample, state_ret[0], state_conv[0], PAST_LEN, *mix_w, n_p + n_s, n_p, x1)
    yp, ys = _moe_and_final_norm(x1, n_p, norm_ffn[0], w_coarse[0], b_coarse[0], w_fine[0], b_fine[0],
                                 w_gate[0], w_up[0], w_down[0], norm_f)
    return (yp.reshape(bp, lp, d), ys.reshape(bs, ls, d),
            ret_p[None], conv_p[None], ret_s[None], conv_s[None])
```

```python
import functools

import jax
import jax.numpy as jnp
from jax import lax
from jax.experimental import pallas as pl
from jax.experimental.pallas import tpu as pltpu

F32 = jnp.float32
BF16 = jnp.bfloat16
I32 = jnp.int32

D_MODEL = 2048
PAST_LEN = 16384
RET_HEADS = 8
RET_DK = D_MODEL // RET_HEADS
RET_DV = 2 * D_MODEL // RET_HEADS
RET_HALF = RET_DK // 2
RET_CHUNK = 128
ROPE_BASE = 10000.0
CONV_WIDTH = 31
CONV_HIST = CONV_WIDTH - 1
N_GROUPS = 4
EXPERTS_PER_GROUP = 8
N_EXPERTS = N_GROUPS * EXPERTS_PER_GROUP
TOP_K_FINE = 2
D_EXPERT = D_MODEL // 2
EPS = 1e-6

COL_Q = 0
COL_K = COL_Q + RET_HEADS * RET_DK
COL_V = COL_K + RET_HEADS * RET_DK
COL_G = COL_V + RET_HEADS * RET_DV
COL_CA = COL_G + RET_HEADS * RET_DV
COL_CB = COL_CA + D_MODEL
COL_GRET = COL_CB + D_MODEL
COL_GCONV = COL_GRET + D_MODEL
IN_COLS = COL_GCONV + D_MODEL

LANES = 128
SUBLANES = 8
VMEM_LIMIT = 56 * 1024 * 1024
MOE_ROWS = 256
ROUTER_COLS = LANES
ROUTER_OFF = N_GROUPS
NEG = -3.0e38


def _cparams(sem):
    return pltpu.CompilerParams(dimension_semantics=sem, vmem_limit_bytes=VMEM_LIMIT)


def _rmsnorm_kernel(x_ref, w_ref, o_ref):
    x = x_ref[...]
    ms = jnp.mean(x * x, axis=-1, keepdims=True)
    o_ref[...] = (x * lax.rsqrt(ms + EPS) * w_ref[...]).astype(o_ref.dtype)


def _rmsnorm_cast(x, w, tm=512):
    m, d = x.shape
    tm = min(tm, m)
    return pl.pallas_call(
        _rmsnorm_kernel,
        grid=(m // tm,),
        in_specs=[pl.BlockSpec((tm, d), lambda i: (i, 0)), pl.BlockSpec((1, d), lambda i: (0, 0))],
        out_specs=pl.BlockSpec((tm, d), lambda i: (i, 0)),
        out_shape=jax.ShapeDtypeStruct((m, d), BF16),
        compiler_params=_cparams(("arbitrary",)),
        name="rmsnorm_cast",
    )(x, w.reshape(1, d))


def _side_steps(total_steps):
    return 1 << (total_steps.bit_length() - 1)


def _side_specs(arrays, n_steps, linear_step):
    def spec(a):
        rows = a.shape[0] // n_steps
        assert rows * n_steps == a.shape[0] and rows % 16 == 0, (a.shape, n_steps)
        return pl.BlockSpec((rows, a.shape[1]), lambda *g: (jnp.minimum(linear_step(*g), n_steps - 1), 0))
    return [spec(a) for a in arrays]


def _side_cast(step, n_steps, srcs, dsts):
    @pl.when(step < n_steps)
    def _():
        for src, dst in zip(srcs, dsts):
            dst[...] = src[...].astype(dst.dtype)


def _inproj_kernel(*refs, n_side, side_steps):
    a_ref, w_ref = refs[:2]
    side_in = refs[2:2 + n_side]
    o_ref = refs[2 + n_side]
    side_out = refs[3 + n_side:3 + 2 * n_side]
    wb_ref = refs[-1]

    @pl.when(pl.program_id(1) == 0)
    def _():
        wb_ref[...] = w_ref[...].astype(BF16)

    o_ref[...] = jnp.dot(a_ref[...], wb_ref[...], preferred_element_type=F32)
    if n_side:
        _side_cast(pl.program_id(0) * pl.num_programs(1) + pl.program_id(1), side_steps, side_in, side_out)


def _inproj(a, w, side=(), tm=1024, tn=1024):
    m, k = a.shape
    n = w.shape[1]
    tm = min(tm, m)
    grid = (n // tn, m // tm)
    side_steps = _side_steps(grid[0] * grid[1])
    side_specs = _side_specs(side, side_steps, lambda j, i: j * grid[1] + i)
    outs = pl.pallas_call(
        functools.partial(_inproj_kernel, n_side=len(side), side_steps=side_steps),
        grid=grid,
        in_specs=[pl.BlockSpec((tm, k), lambda j, i: (i, 0)), pl.BlockSpec((k, tn), lambda j, i: (0, j))]
                 + side_specs,
        out_specs=[pl.BlockSpec((tm, tn), lambda j, i: (i, j))] + side_specs,
        out_shape=[jax.ShapeDtypeStruct((m, n), F32)] + [jax.ShapeDtypeStruct(s.shape, BF16) for s in side],
        scratch_shapes=[pltpu.VMEM((k, tn), BF16)],
        compiler_params=_cparams(("arbitrary", "arbitrary")),
        name="inproj",
    )(a, w, *side)
    return outs[0], tuple(outs[1:])


def _rotary_pair(x1, x2, cos, sin):
    return jnp.concatenate([x1 * cos - x2 * sin, x1 * sin + x2 * cos], axis=1)


def _retention_tables(c):
    h = RET_HEADS
    log_g = jnp.log1p(-jnp.exp2(-5.0 - jnp.arange(h, dtype=F32)))
    idx = jnp.arange(c, dtype=F32)
    diff = idx[:, None] - idx[None, :]
    decay = jnp.where(diff[None] >= 0.0, jnp.exp(jnp.maximum(diff, 0.0)[None] * log_g[:, None, None]), 0.0)
    q_dec = jnp.exp((idx[:, None] + 1.0) * log_g[None, :])
    k_dec = jnp.exp((c - 1.0 - idx)[:, None] * log_g[None, :])
    c_dec = jnp.exp(c * log_g)
    return decay, q_dec, k_dec, c_dec


def _rope_tables(pos):
    inv = ROPE_BASE ** (-jnp.arange(RET_HALF, dtype=F32) / RET_HALF)
    ang = pos.astype(F32)[:, None] * inv[None, :]
    return jnp.cos(ang), jnp.sin(ang)


def _lane_bcast(t):
    return jnp.broadcast_to(t.T[:, :, None], (t.shape[1], t.shape[0], LANES))


def _group_norm_gate(o, g):
    ms = jnp.mean(o * o, axis=-1, keepdims=True)
    return jax.nn.silu(g) * (o * lax.rsqrt(ms + EPS))


def _ret_prompt_kernel(cdec_ref, q_ref, k_ref, v_ref, g_ref, cos_ref, sin_ref, dec_ref, qd_ref, kd_ref,
                       o_ref, s_ref):
    @pl.when(pl.program_id(1) == 0)
    def _():
        s_ref[...] = jnp.zeros_like(s_ref)

    cos = cos_ref[...]
    sin = sin_ref[...]
    for h in range(RET_HEADS):
        c0 = h * RET_DK
        qr = _rotary_pair(q_ref[:, c0:c0 + RET_HALF], q_ref[:, c0 + RET_HALF:c0 + RET_DK], cos, sin)
        kr = _rotary_pair(k_ref[:, c0:c0 + RET_HALF], k_ref[:, c0 + RET_HALF:c0 + RET_DK], cos, sin)
        kr = kr * (RET_DK ** -0.5)
        qb = qr.astype(BF16)
        kb = kr.astype(BF16)
        vb = v_ref[:, h * RET_DV:(h + 1) * RET_DV].astype(BF16)
        att = lax.dot_general(qb, kb, (((1,), (1,)), ((), ())), preferred_element_type=F32) * dec_ref[h]
        s = s_ref[0, h]
        o = (jnp.dot(att.astype(BF16), vb, preferred_element_type=F32)
             + qd_ref[h][:, :1] * jnp.dot(qb, s.astype(BF16), preferred_element_type=F32))
        kk = (kr * kd_ref[h][:, :1]).astype(BF16)
        s_ref[0, h] = s * cdec_ref[h] + lax.dot_general(kk, vb, (((0,), (0,)), ((), ())),
                                                         preferred_element_type=F32)
        og = _group_norm_gate(o, g_ref[:, h * RET_DV:(h + 1) * RET_DV])
        o_ref[:, h * RET_DV:(h + 1) * RET_DV] = og.astype(o_ref.dtype)


def _retention_prompt(z, b, l):
    c = RET_CHUNK if l % RET_CHUNK == 0 else l
    nc = l // c
    decay, q_dec, k_dec, c_dec = _retention_tables(c)
    cos, sin = _rope_tables(jnp.arange(l))
    wq = RET_HEADS * RET_DK
    wv = RET_HEADS * RET_DV
    full3 = lambda bi, ci: (0, 0, 0)
    return pl.pallas_call(
        _ret_prompt_kernel,
        grid=(b, nc),
        in_specs=[
            pl.BlockSpec(memory_space=pltpu.SMEM),
            pl.BlockSpec((c, wq), lambda bi, ci: (bi * nc + ci, COL_Q // wq)),
            pl.BlockSpec((c, wq), lambda bi, ci: (bi * nc + ci, COL_K // wq)),
            pl.BlockSpec((c, wv), lambda bi, ci: (bi * nc + ci, COL_V // wv)),
            pl.BlockSpec((c, wv), lambda bi, ci: (bi * nc + ci, COL_G // wv)),
            pl.BlockSpec((c, RET_HALF), lambda bi, ci: (ci, 0)),
            pl.BlockSpec((c, RET_HALF), lambda bi, ci: (ci, 0)),
            pl.BlockSpec((RET_HEADS, c, c), full3),
            pl.BlockSpec((RET_HEADS, c, LANES), full3),
            pl.BlockSpec((RET_HEADS, c, LANES), full3),
        ],
        out_specs=[
            pl.BlockSpec((c, wv), lambda bi, ci: (bi * nc + ci, 0)),
            pl.BlockSpec((1, RET_HEADS, RET_DK, RET_DV), lambda bi, ci: (bi, 0, 0, 0)),
        ],
        out_shape=[
            jax.ShapeDtypeStruct((b * l, wv), BF16),
            jax.ShapeDtypeStruct((b, RET_HEADS, RET_DK, RET_DV), F32),
        ],
        compiler_params=_cparams(("arbitrary", "arbitrary")),
        name="retention_prompt",
    )(c_dec, z, z, z, z, cos, sin, decay, _lane_bcast(q_dec), _lane_bcast(k_dec))


SAMPLE_PAD = 16


def _pad_rows(x, rows):
    return jnp.concatenate([x, jnp.zeros((rows - x.shape[0], x.shape[1]), x.dtype)], axis=0)


def _ret_sample_kernel(cdec_ref, q_ref, k_ref, v_ref, g_ref, cos_ref, sin_ref, dec_ref, qd_ref, kd_ref,
                       s0_ref, o_ref, s_ref, *, nb, l):
    cos = cos_ref[...]
    sin = sin_ref[...]
    for bt in range(nb):
        r0 = bt * l
        for h in range(RET_HEADS):
            c0 = h * RET_DK
            qr = _rotary_pair(q_ref[r0:r0 + l, c0:c0 + RET_HALF], q_ref[r0:r0 + l, c0 + RET_HALF:c0 + RET_DK],
                              cos, sin)
            kr = _rotary_pair(k_ref[r0:r0 + l, c0:c0 + RET_HALF], k_ref[r0:r0 + l, c0 + RET_HALF:c0 + RET_DK],
                              cos, sin)
            kr = kr * (RET_DK ** -0.5)
            qb = _pad_rows(qr, SAMPLE_PAD).astype(BF16)
            kb = _pad_rows(kr, SAMPLE_PAD).astype(BF16)
            vb = _pad_rows(v_ref[r0:r0 + l, h * RET_DV:(h + 1) * RET_DV], SAMPLE_PAD).astype(BF16)
            att = lax.dot_general(qb, kb, (((1,), (1,)), ((), ())), preferred_element_type=F32) * dec_ref[h]
            s = s0_ref[bt, h]
            o = (jnp.dot(att.astype(BF16), vb, preferred_element_type=F32)
                 + qd_ref[h][:, :1] * jnp.dot(qb, s.astype(BF16), preferred_element_type=F32))
            kk = _pad_rows(kr * kd_ref[h][:l, :1], SAMPLE_PAD).astype(BF16)
            s_ref[bt, h] = s * cdec_ref[h] + lax.dot_general(kk, vb, (((0,), (0,)), ((), ())),
                                                              preferred_element_type=F32)
            og = _group_norm_gate(o[:l], g_ref[r0:r0 + l, h * RET_DV:(h + 1) * RET_DV])
            o_ref[r0:r0 + l, h * RET_DV:(h + 1) * RET_DV] = og.astype(o_ref.dtype)


def _retention_sample(z, s0, b, l, pos0, nb=2):
    assert l % RET_CHUNK != 0 and l <= SAMPLE_PAD and b % nb == 0, (b, l)
    decay, q_dec, k_dec, c_dec = _retention_tables(l)
    p = SAMPLE_PAD - l
    decay = jnp.pad(decay, ((0, 0), (0, p), (0, p)))
    q_dec = jnp.pad(q_dec, ((0, p), (0, 0)))
    k_dec = jnp.pad(k_dec, ((0, p), (0, 0)))
    cos, sin = _rope_tables(pos0 + jnp.arange(l))
    wq = RET_HEADS * RET_DK
    wv = RET_HEADS * RET_DV
    rows = nb * l
    full2 = lambda bi: (0, 0)
    full3 = lambda bi: (0, 0, 0)
    state_spec = pl.BlockSpec((nb, RET_HEADS, RET_DK, RET_DV), lambda bi: (bi, 0, 0, 0))
    return pl.pallas_call(
        functools.partial(_ret_sample_kernel, nb=nb, l=l),
        grid=(b // nb,),
        in_specs=[
            pl.BlockSpec(memory_space=pltpu.SMEM),
            pl.BlockSpec((rows, wq), lambda bi: (bi, COL_Q // wq)),
            pl.BlockSpec((rows, wq), lambda bi: (bi, COL_K // wq)),
            pl.BlockSpec((rows, wv), lambda bi: (bi, COL_V // wv)),
            pl.BlockSpec((rows, wv), lambda bi: (bi, COL_G // wv)),
            pl.BlockSpec((l, RET_HALF), full2),
            pl.BlockSpec((l, RET_HALF), full2),
            pl.BlockSpec((RET_HEADS, SAMPLE_PAD, SAMPLE_PAD), full3),
            pl.BlockSpec((RET_HEADS, SAMPLE_PAD, LANES), full3),
            pl.BlockSpec((RET_HEADS, SAMPLE_PAD, LANES), full3),
            state_spec,
        ],
        out_specs=[pl.BlockSpec((rows, wv), lambda bi: (bi, 0)), state_spec],
        out_shape=[
            jax.ShapeDtypeStruct((b * l, wv), BF16),
            jax.ShapeDtypeStruct((b, RET_HEADS, RET_DK, RET_DV), F32),
        ],
        compiler_params=_cparams(("arbitrary",)),
        name="retention_sample",
    )(c_dec, z, z, z, z, cos, sin, decay, _lane_bcast(q_dec), _lane_bcast(k_dec), s0)


CONV_PAD = 32
CONV_COLS = LANES


def _conv_strip(block, wts, nblk, store):
    lead = CONV_PAD - CONV_HIST
    taps = [((lead + w) // SUBLANES, (lead + w) % SUBLANES) for w in range(CONV_WIDTH)]
    rowid = lax.broadcasted_iota(I32, (SUBLANES, CONV_COLS), 0)

    def partial_sums(m, shifts):
        ys = {}
        for w, (a, s) in enumerate(taps):
            if s in shifts:
                term = block(m + a) * wts[w]
                ys[s] = term if s not in ys else ys[s] + term
        return ys

    shifted = range(1, SUBLANES)
    prev = partial_sums(0, shifted)
    for m in range(nblk):
        nxt = partial_sums(m + 1, shifted)
        acc = partial_sums(m, (0,))[0]
        for s in shifted:
            acc = acc + pltpu.roll(jnp.where(rowid >= s, prev[s], nxt[s]), SUBLANES - s, axis=0)
        store(m, acc)
        prev = nxt


def _conv_kernel(*refs, tl, nb, nl, n_side, side_steps):
    ca_ref, cb_ref, buf_ref, cw_ref, bias_ref, lnw_ref, lnb_ref = refs[:7]
    side_in = refs[7:7 + n_side]
    sw_ref, nbuf_ref = refs[7 + n_side:9 + n_side]
    side_out = refs[9 + n_side:9 + 2 * n_side]
    upad_ref, c_ref = refs[-2:]
    if n_side:
        _side_cast(pl.program_id(0) * nl + pl.program_id(1), side_steps, side_in, side_out)
    li = pl.program_id(1)
    lead = CONV_PAD - CONV_HIST
    d = ca_ref.shape[1]
    for bt in range(nb):
        @pl.when(li == 0)
        def _():
            upad_ref[bt, 0:lead, :] = jnp.zeros((lead, d), F32)
            upad_ref[bt, lead:CONV_PAD, :] = buf_ref[bt]

        r0 = bt * tl
        upad_ref[bt, CONV_PAD:CONV_PAD + tl, :] = ca_ref[r0:r0 + tl, :] * jax.nn.sigmoid(cb_ref[r0:r0 + tl, :])
        for cc in range(d // CONV_COLS):
            cs = slice(cc * CONV_COLS, (cc + 1) * CONV_COLS)
            wts = [jnp.broadcast_to(cw_ref[w:w + 1, cs], (SUBLANES, CONV_COLS)) for w in range(CONV_WIDTH)]
            blocks = {}

            def block(j, cs=cs, blocks=blocks):
                if j not in blocks:
                    blocks[j] = upad_ref[bt, j * SUBLANES:(j + 1) * SUBLANES, cs]
                return blocks[j]

            def store(m, acc, cs=cs):
                c_ref[m * SUBLANES:(m + 1) * SUBLANES, cs] = acc

            _conv_strip(block, wts, tl // SUBLANES, store)
        cf = c_ref[...] + bias_ref[...]
        mu = jnp.mean(cf, axis=-1, keepdims=True)
        var = jnp.mean(jnp.square(cf - mu), axis=-1, keepdims=True)
        cn = (cf - mu) * lax.rsqrt(var + EPS) * lnw_ref[...] + lnb_ref[...]
        sw_ref[r0:r0 + tl, :] = jax.nn.silu(cn).astype(sw_ref.dtype)

        @pl.when(li == nl - 1)
        def _():
            nbuf_ref[bt] = upad_ref[bt, tl + lead:tl + CONV_PAD, :]

        if nl > 1:
            upad_ref[bt, 0:CONV_PAD, :] = upad_ref[bt, tl:tl + CONV_PAD, :]


def _conv_branch(z, buf, conv_w, conv_b, ln_w, ln_b, b, l, tl, nb, side=()):
    d = D_MODEL
    nl = l // tl
    assert nb == 1 or nl == 1
    rows = nb * tl
    vec = lambda a: a.reshape(1, d)
    const2 = lambda bi, li: (0, 0)
    grid = (b // nb, nl)
    side_steps = _side_steps(grid[0] * grid[1])
    side_specs = _side_specs(side, side_steps, lambda bi, li: bi * nl + li)
    outs = pl.pallas_call(
        functools.partial(_conv_kernel, tl=tl, nb=nb, nl=nl, n_side=len(side), side_steps=side_steps),
        grid=grid,
        in_specs=[
            pl.BlockSpec((rows, d), lambda bi, li: (bi * nl + li, COL_CA // d)),
            pl.BlockSpec((rows, d), lambda bi, li: (bi * nl + li, COL_CB // d)),
            pl.BlockSpec((nb, CONV_HIST, d), lambda bi, li: (bi, 0, 0)),
            pl.BlockSpec((CONV_WIDTH, d), const2),
            pl.BlockSpec((1, d), const2),
            pl.BlockSpec((1, d), const2),
            pl.BlockSpec((1, d), const2),
        ] + side_specs,
        out_specs=[
            pl.BlockSpec((rows, d), lambda bi, li: (bi * nl + li, 0)),
            pl.BlockSpec((nb, CONV_HIST, d), lambda bi, li: (bi, 0, 0)),
        ] + side_specs,
        out_shape=[
            jax.ShapeDtypeStruct((b * l, d), BF16),
            jax.ShapeDtypeStruct((b, CONV_HIST, d), F32),
        ] + [jax.ShapeDtypeStruct(s.shape, BF16) for s in side],
        scratch_shapes=[pltpu.VMEM((nb, CONV_PAD + tl, d), F32), pltpu.VMEM((tl, d), F32)],
        compiler_params=_cparams(("arbitrary", "arbitrary")),
        name="conv_branch",
    )(z, z, buf, conv_w, vec(conv_b), vec(ln_w), vec(ln_b), *side)
    return outs[0], outs[1], tuple(outs[2:])


def _merge_kernel(og_ref, sw_ref, wr_ref, wc_ref, gr_ref, gc_ref, o_ref, wrb_ref, wcb_ref):
    @pl.when(pl.program_id(1) == 0)
    def _():
        wrb_ref[...] = wr_ref[...].astype(BF16)
        wcb_ref[...] = wc_ref[...].astype(BF16)

    ret_out = jnp.dot(og_ref[...], wrb_ref[...], preferred_element_type=F32)
    conv_out = jnp.dot(sw_ref[...], wcb_ref[...], preferred_element_type=F32)
    merged = jax.nn.sigmoid(gr_ref[...]) * ret_out + jax.nn.sigmoid(gc_ref[...]) * conv_out
    o_ref[...] = merged.astype(o_ref.dtype)


def _merge(og, sw, w_ret_o, w_conv_o, z, tm=512, tn=512):
    m = og.shape[0]
    n = w_ret_o.shape[1]
    tm = min(tm, m)
    kr = og.shape[1]
    kc = sw.shape[1]
    return pl.pallas_call(
        _merge_kernel,
        grid=(n // tn, m // tm),
        in_specs=[
            pl.BlockSpec((tm, kr), lambda j, i: (i, 0)),
            pl.BlockSpec((tm, kc), lambda j, i: (i, 0)),
            pl.BlockSpec((kr, tn), lambda j, i: (0, j)),
            pl.BlockSpec((kc, tn), lambda j, i: (0, j)),
            pl.BlockSpec((tm, tn), lambda j, i: (i, COL_GRET // tn + j)),
            pl.BlockSpec((tm, tn), lambda j, i: (i, COL_GCONV // tn + j)),
        ],
        out_specs=pl.BlockSpec((tm, tn), lambda j, i: (i, j)),
        out_shape=jax.ShapeDtypeStruct((m, n), BF16),
        scratch_shapes=[pltpu.VMEM((kr, tn), BF16), pltpu.VMEM((kc, tn), BF16)],
        compiler_params=_cparams(("arbitrary", "arbitrary")),
        name="merge",
    )(og, sw, w_ret_o, w_conv_o, z, z)


def _outproj_kernel(a_ref, w_ref, x_ref, *rest):
    o_ref, wb_ref = rest[-2], rest[-1]

    @pl.when(pl.program_id(1) == 0)
    def _():
        wb_ref[...] = w_ref[...].astype(BF16)

    o_ref[...] = x_ref[...] + jnp.dot(a_ref[...], wb_ref[...], preferred_element_type=F32)


def _outproj_residual(a, w, x, total_rows, row0, prev=None, tm=512, tn=512):
    m, k = a.shape
    n = w.shape[1]
    tm = min(tm, m)
    assert row0 % tm == 0
    off = row0 // tm
    in_specs = [
        pl.BlockSpec((tm, k), lambda j, i: (i, 0)),
        pl.BlockSpec((k, tn), lambda j, i: (0, j)),
        pl.BlockSpec((tm, tn), lambda j, i: (i, j)),
    ]
    args = [a, w, x]
    aliases = {}
    if prev is not None:
        in_specs.append(pl.BlockSpec(memory_space=pl.ANY))
        args.append(prev)
        aliases = {3: 0}
    return pl.pallas_call(
        _outproj_kernel,
        grid=(n // tn, m // tm),
        in_specs=in_specs,
        out_specs=pl.BlockSpec((tm, tn), lambda j, i: (i + off, j)),
        out_shape=jax.ShapeDtypeStruct((total_rows, n), F32),
        scratch_shapes=[pltpu.VMEM((k, tn), BF16)],
        input_output_aliases=aliases,
        compiler_params=_cparams(("arbitrary", "arbitrary")),
        name="outproj_residual",
    )(*args)


def _router_kernel(x_ref, nw_ref, w_ref, b_ref, h_ref, meta_ref, meta_t_ref, cnt_ref, base_ref):
    tm = x_ref.shape[0]

    @pl.when(pl.program_id(0) == 0)
    def _():
        base_ref[...] = jnp.zeros_like(base_ref)

    x = x_ref[...]
    ms = jnp.mean(x * x, axis=-1, keepdims=True)
    h = x * lax.rsqrt(ms + EPS) * nw_ref[...]
    h_ref[...] = h
    logits = jnp.dot(h, w_ref[...], preferred_element_type=F32, precision=lax.Precision.HIGHEST) + b_ref[...]
    lane = lax.broadcasted_iota(I32, (tm, ROUTER_COLS), 1)

    def first_argmax(vals):
        top = jnp.max(vals, axis=1, keepdims=True)
        return top, jnp.min(jnp.where(vals == top, lane, ROUTER_COLS), axis=1, keepdims=True)

    cmask = lane < N_GROUPS
    lc = jnp.where(cmask, logits, NEG)
    mc, grp = first_argmax(lc)
    p_grp = 1.0 / jnp.sum(jnp.where(cmask, jnp.exp(lc - mc), 0.0), axis=1, keepdims=True)
    lo = ROUTER_OFF + grp * EXPERTS_PER_GROUP
    lf = jnp.where((lane >= lo) & (lane < lo + EXPERTS_PER_GROUP), logits, NEG)
    v1, i1 = first_argmax(lf)
    v2, i2 = first_argmax(jnp.where(lane == i1, NEG, lf))
    e2 = jnp.exp(v2 - v1)
    den = 1.0 + e2
    g1 = p_grp * (1.0 / den)
    g2 = p_grp * (e2 / den)

    oh1 = lane == i1
    oh2 = lane == i2
    both = jnp.where(oh1 | oh2, 1.0, 0.0)
    row = lax.broadcasted_iota(I32, (tm, tm), 0)
    col = lax.broadcasted_iota(I32, (tm, tm), 1)
    earlier = jnp.where(col < row, 1.0, 0.0).astype(BF16)
    before = jnp.dot(earlier, both.astype(BF16), preferred_element_type=F32) + base_ref[...]
    r1 = jnp.sum(jnp.where(oh1, before, 0.0), axis=1, keepdims=True)
    r2 = jnp.sum(jnp.where(oh2, before, 0.0), axis=1, keepdims=True)
    base_ref[...] = base_ref[...] + jnp.sum(both, axis=0, keepdims=True)
    cnt_ref[...] = base_ref[...]

    fields = [(i1 - ROUTER_OFF).astype(F32), (i2 - ROUTER_OFF).astype(F32), g1, g2, r1, r2]
    meta = jnp.zeros((tm, ROUTER_COLS), F32)
    for k, f in enumerate(fields):
        meta = jnp.where(lane == k, f, meta)
    meta_ref[...] = meta
    meta_t_ref[...] = meta.T[:SUBLANES]


def _router(x, norm_w, w_router, b_router, tm=512):
    n, d = x.shape
    tm = min(tm, n)
    const2 = lambda i: (0, 0)
    return pl.pallas_call(
        _router_kernel,
        grid=(n // tm,),
        in_specs=[
            pl.BlockSpec((tm, d), lambda i: (i, 0)),
            pl.BlockSpec((1, d), const2),
            pl.BlockSpec((d, ROUTER_COLS), const2),
            pl.BlockSpec((1, ROUTER_COLS), const2),
        ],
        out_specs=[
            pl.BlockSpec((tm, d), lambda i: (i, 0)),
            pl.BlockSpec((tm, ROUTER_COLS), lambda i: (i, 0)),
            pl.BlockSpec((SUBLANES, tm), lambda i: (0, i)),
            pl.BlockSpec((1, ROUTER_COLS), const2),
        ],
        out_shape=[
            jax.ShapeDtypeStruct((n, d), F32),
            jax.ShapeDtypeStruct((n, ROUTER_COLS), F32),
            jax.ShapeDtypeStruct((SUBLANES, n), F32),
            jax.ShapeDtypeStruct((1, ROUTER_COLS), F32),
        ],
        scratch_shapes=[pltpu.VMEM((1, ROUTER_COLS), F32)],
        compiler_params=_cparams(("arbitrary",)),
        name="router",
    )(x, norm_w.reshape(1, d), w_router, b_router)


def _dispatch_kernel(rows_ref, span_ref, h_ref, xs_hbm, zero_ref, sem, zsem):
    td = h_ref.shape[0]
    i = pl.program_id(0)

    @pl.when(i == 0)
    def _():
        zero_ref[...] = jnp.zeros_like(zero_ref)

        def fill_copy(e):
            last = pl.multiple_of(span_ref[N_EXPERTS + e] - MOE_ROWS, MOE_ROWS)
            return pltpu.make_async_copy(zero_ref, xs_hbm.at[pl.ds(last, MOE_ROWS)], zsem)

        def nonempty(e):
            return span_ref[N_EXPERTS + e] > span_ref[e]

        def start(e, carry):
            @pl.when(nonempty(e))
            def _():
                fill_copy(e).start()
            return carry

        def wait(e, carry):
            @pl.when(nonempty(e))
            def _():
                fill_copy(e).wait()
            return carry

        lax.fori_loop(0, N_EXPERTS, start, 0)
        lax.fori_loop(0, N_EXPERTS, wait, 0)

    n_tok = pl.num_programs(0) * td

    def token(t, carry):
        n = i * td + t
        for k in range(TOP_K_FINE):
            r = rows_ref[k * n_tok + n]
            pltpu.make_async_copy(h_ref.at[pl.ds(t, 1)], xs_hbm.at[pl.ds(r, 1)], sem).start()
        return carry

    lax.fori_loop(0, td, token, 0)
    for _ in range(TOP_K_FINE):
        pltpu.make_async_copy(h_ref, xs_hbm.at[pl.ds(0, td)], sem).wait()


def _dispatch(rows_flat, span, h, n_rows, td=512):
    n_tok, d = h.shape
    td = min(td, n_tok)
    return pl.pallas_call(
        _dispatch_kernel,
        grid_spec=pltpu.PrefetchScalarGridSpec(
            num_scalar_prefetch=2,
            grid=(n_tok // td,),
            in_specs=[pl.BlockSpec((td, d), lambda i, rows, span: (i, 0))],
            out_specs=pl.BlockSpec(memory_space=pl.ANY),
            scratch_shapes=[pltpu.VMEM((MOE_ROWS, d), F32), pltpu.SemaphoreType.DMA(()),
                            pltpu.SemaphoreType.DMA(())],
        ),
        out_shape=jax.ShapeDtypeStruct((n_rows, d), F32),
        compiler_params=_cparams(("arbitrary",)),
        name="moe_dispatch",
    )(rows_flat, span, h)


def _expert_kernel(be_ref, nu_ref, x_ref, wg_ref, wu_ref, wd_ref, o_ref):
    del be_ref

    @pl.when(pl.program_id(0) < nu_ref[0])
    def _():
        xb = x_ref[...].astype(BF16)
        a = jnp.dot(xb, wg_ref[0], preferred_element_type=F32)
        b = jnp.dot(xb, wu_ref[0], preferred_element_type=F32)
        hb = (jax.nn.silu(a) * b).astype(BF16)
        o_ref[...] = jnp.dot(hb, wd_ref[0], preferred_element_type=F32)


def _experts(blk_exp, n_used, xs, wg, wu, wd):
    n_rows, d = xs.shape
    n_blocks = n_rows // MOE_ROWS
    de = wg.shape[2]
    row_map = lambda i, be, nu: (jnp.minimum(i, jnp.maximum(nu[0] - 1, 0)), 0)
    return pl.pallas_call(
        _expert_kernel,
        grid_spec=pltpu.PrefetchScalarGridSpec(
            num_scalar_prefetch=2,
            grid=(n_blocks,),
            in_specs=[
                pl.BlockSpec((MOE_ROWS, d), row_map),
                pl.BlockSpec((1, d, de), lambda i, be, nu: (be[i], 0, 0)),
                pl.BlockSpec((1, d, de), lambda i, be, nu: (be[i], 0, 0)),
                pl.BlockSpec((1, de, d), lambda i, be, nu: (be[i], 0, 0)),
            ],
            out_specs=pl.BlockSpec((MOE_ROWS, d), row_map),
        ),
        out_shape=jax.ShapeDtypeStruct((n_rows, d), F32),
        compiler_params=_cparams(("arbitrary",)),
        name="moe_experts",
    )(blk_exp, n_used, xs, wg, wu, wd)


def _combine_kernel(rows_ref, ys_hbm, x_ref, meta_ref, nf_ref, op_ref, os_ref, buf_ref, sem, *, p_tiles):
    tm = x_ref.shape[0]
    i = pl.program_id(0)

    n_tok = pl.num_programs(0) * tm

    def token(t, carry):
        n = i * tm + t
        for k in range(TOP_K_FINE):
            r = rows_ref[k * n_tok + n]
            pltpu.make_async_copy(ys_hbm.at[pl.ds(r, 1)], buf_ref.at[k, pl.ds(t, 1)], sem).start()
        return carry

    lax.fori_loop(0, tm, token, 0)
    for k in range(TOP_K_FINE):
        pltpu.make_async_copy(ys_hbm.at[pl.ds(0, tm)], buf_ref.at[k], sem).wait()
    meta = meta_ref[...]
    y = x_ref[...] + (buf_ref[0] * meta[:, 2:3] + buf_ref[1] * meta[:, 3:4])
    ms = jnp.mean(y * y, axis=-1, keepdims=True)
    out = y * lax.rsqrt(ms + EPS) * nf_ref[...]

    @pl.when(i < p_tiles)
    def _():
        op_ref[...] = out

    @pl.when(i >= p_tiles)
    def _():
        os_ref[...] = out


def _combine(rows_flat, ys, x, meta, norm_f, n_p, tm=256):
    n, d = x.shape
    n_s = n - n_p
    tm = min(tm, n_p, n_s)
    assert n_p % tm == 0 and n_s % tm == 0
    p_tiles = n_p // tm
    return pl.pallas_call(
        functools.partial(_combine_kernel, p_tiles=p_tiles),
        grid_spec=pltpu.PrefetchScalarGridSpec(
            num_scalar_prefetch=1,
            grid=(n // tm,),
            in_specs=[
                pl.BlockSpec(memory_space=pl.ANY),
                pl.BlockSpec((tm, d), lambda i, rows: (i, 0)),
                pl.BlockSpec((tm, ROUTER_COLS), lambda i, rows: (i, 0)),
                pl.BlockSpec((1, d), lambda i, rows: (0, 0)),
            ],
            out_specs=[
                pl.BlockSpec((tm, d), lambda i, rows: (jnp.minimum(i, p_tiles - 1), 0)),
                pl.BlockSpec((tm, d), lambda i, rows: (jnp.maximum(i - p_tiles, 0), 0)),
            ],
            scratch_shapes=[pltpu.VMEM((TOP_K_FINE, tm, d), F32), pltpu.SemaphoreType.DMA(())],
        ),
        out_shape=[jax.ShapeDtypeStruct((n_p, d), F32), jax.ShapeDtypeStruct((n_s, d), F32)],
        compiler_params=_cparams(("arbitrary",)),
        name="moe_combine",
    )(rows_flat, ys, x, meta, norm_f.reshape(1, d))


def _moe_and_final_norm(x, n_p, norm_ffn, w_coarse, b_coarse, w_fine, b_fine, wg, wu, wd, norm_f):
    n_tok, d = x.shape
    pad = ROUTER_COLS - N_GROUPS - N_EXPERTS
    w_router = jnp.concatenate([w_coarse, w_fine, jnp.zeros((d, pad), F32)], axis=1)
    b_router = jnp.concatenate([b_coarse, b_fine, jnp.zeros((pad,), F32)]).reshape(1, ROUTER_COLS)
    h, meta, meta_t, counts = _router(x, norm_ffn, w_router, b_router)

    n_asg = n_tok * TOP_K_FINE
    n_blocks = -(-n_asg // MOE_ROWS) + N_EXPERTS
    n_rows = n_blocks * MOE_ROWS
    cnt = counts[0, ROUTER_OFF:ROUTER_OFF + N_EXPERTS].astype(I32)
    padded = (cnt + MOE_ROWS - 1) // MOE_ROWS * MOE_ROWS
    pend = jnp.cumsum(padded)
    pstart = pend - padded
    expert = meta_t[0:TOP_K_FINE].astype(I32)
    rank = meta_t[4:4 + TOP_K_FINE].astype(I32)
    rows_flat = (pstart[expert] + rank).reshape(-1)
    n_used = (pend[-1] // MOE_ROWS).astype(I32)
    blk = jnp.arange(n_blocks, dtype=I32)
    blk_exp = jnp.searchsorted(pend, jnp.minimum(blk, n_used - 1) * MOE_ROWS, side='right').astype(I32)
    blk_exp = jnp.minimum(blk_exp, N_EXPERTS - 1)

    xs = _dispatch(rows_flat, jnp.concatenate([pstart, pend]).astype(I32), h, n_rows)
    ys = _experts(blk_exp, n_used.reshape(1), xs, wg, wu, wd)
    return _combine(rows_flat, ys, x, meta, norm_f, n_p)


def _mixer(x, s_ret, s_conv, pos0, norm_mix, w_in, w_ret_o, conv_w, conv_b, conv_ln_w, conv_ln_b,
           w_conv_o, w_o, total_rows, row0, prev, side_inproj=(), side_conv=()):
    b, l, d = x.shape
    x2 = x.reshape(b * l, d)
    h = _rmsnorm_cast(x2, norm_mix)
    z, cast_a = _inproj(h, w_in, side_inproj)
    if s_ret is None:
        og, s_ret_new = _retention_prompt(z, b, l)
        buf = jnp.zeros((b, CONV_HIST, d), F32)
        sw, s_conv_new, cast_b = _conv_branch(z, buf, conv_w, conv_b, conv_ln_w, conv_ln_b, b, l,
                                              tl=min(128, l), nb=1, side=side_conv)
    else:
        og, s_ret_new = _retention_sample(z, s_ret, b, l, pos0)
        sw, s_conv_new, cast_b = _conv_branch(z, s_conv, conv_w, conv_b, conv_ln_w, conv_ln_b, b, l,
                                              tl=l, nb=2, side=side_conv)
    merged = _merge(og, sw, w_ret_o, w_conv_o, z)
    x1 = _outproj_residual(merged, w_o, x2, total_rows, row0, prev)
    return x1, s_ret_new, s_conv_new, cast_a + cast_b


def kernel(x_prompt, x_sample, state_ret, state_conv, norm_mix, w_in, w_ret_o, conv_w, conv_b, conv_ln_w,
           conv_ln_b, w_conv_o, w_o, norm_ffn, w_coarse, b_coarse, w_fine, b_fine, w_gate, w_up, w_down, norm_f):
    depth = w_in.shape[0]
    assert depth == 1, "single-layer trunk"
    bp, lp, d = x_prompt.shape
    bs, ls, _ = x_sample.shape
    n_p = bp * lp
    n_s = bs * ls
    mix_w = (norm_mix[0], w_in[0], w_ret_o[0], conv_w[0], conv_b[0], conv_ln_w[0], conv_ln_b[0], w_conv_o[0],
             w_o[0])
    n_exp, _, d_exp = w_gate.shape[1:]
    flat = lambda w: w.reshape(n_exp * w.shape[2], w.shape[3])
    x1, ret_p, conv_p, (wg, wu, wd) = _mixer(
        x_prompt, None, None, 0, *mix_w, n_p + n_s, 0, None,
        side_inproj=(flat(w_gate), flat(w_up)), side_conv=(flat(w_down),))
    x1, ret_s, conv_s, _ = _mixer(x_sample, state_ret[0], state_conv[0], PAST_LEN, *mix_w, n_p + n_s, n_p, x1)
    yp, ys = _moe_and_final_norm(x1, n_p, norm_ffn[0], w_coarse[0], b_coarse[0], w_fine[0], b_fine[0],
                                 wg.reshape(n_exp, d, d_exp), wu.reshape(n_exp, d, d_exp),
                                 wd.reshape(n_exp, d_exp, d), norm_f)
    return (yp.reshape(bp, lp, d), ys.reshape(bs, ls, d),
            ret_p[None], conv_p[None], ret_s[None], conv_s[None])
```

```python
import functools

import jax
import jax.numpy as jnp
from jax import lax
from jax.experimental import pallas as pl
from jax.experimental.pallas import tpu as pltpu

F32 = jnp.float32
BF16 = jnp.bfloat16
I32 = jnp.int32

D_MODEL = 2048
PAST_LEN = 16384
RET_HEADS = 8
RET_DK = D_MODEL // RET_HEADS
RET_DV = 2 * D_MODEL // RET_HEADS
RET_HALF = RET_DK // 2
RET_CHUNK = 128
ROPE_BASE = 10000.0
CONV_WIDTH = 31
CONV_HIST = CONV_WIDTH - 1
N_GROUPS = 4
EXPERTS_PER_GROUP = 8
N_EXPERTS = N_GROUPS * EXPERTS_PER_GROUP
TOP_K_FINE = 2
D_EXPERT = D_MODEL // 2
EPS = 1e-6

W_QK = 2 * RET_HEADS * RET_DK
W_V = RET_HEADS * RET_DV
COL_QK = 0
COL_V = COL_QK + W_QK
COL_REST = COL_V + W_V
REST_G = 0
REST_CA = REST_G + RET_HEADS * RET_DV
REST_CB = REST_CA + D_MODEL
REST_GRET = REST_CB + D_MODEL
REST_GCONV = REST_GRET + D_MODEL
W_REST = REST_GCONV + D_MODEL

LANES = 128
SUBLANES = 8
VMEM_LIMIT = 56 * 1024 * 1024
MOE_ROWS = 256
ROUTER_COLS = LANES
ROUTER_OFF = N_GROUPS
NEG = -3.0e38


def _cparams(sem):
    return pltpu.CompilerParams(dimension_semantics=sem, vmem_limit_bytes=VMEM_LIMIT)


def _rmsnorm_kernel(x_ref, w_ref, o_ref):
    x = x_ref[...]
    ms = jnp.mean(x * x, axis=-1, keepdims=True)
    o_ref[...] = (x * lax.rsqrt(ms + EPS) * w_ref[...]).astype(o_ref.dtype)


def _rmsnorm_cast(x, w, tm=512):
    m, d = x.shape
    tm = min(tm, m)
    return pl.pallas_call(
        _rmsnorm_kernel,
        grid=(m // tm,),
        in_specs=[pl.BlockSpec((tm, d), lambda i: (i, 0)), pl.BlockSpec((1, d), lambda i: (0, 0))],
        out_specs=pl.BlockSpec((tm, d), lambda i: (i, 0)),
        out_shape=jax.ShapeDtypeStruct((m, d), BF16),
        compiler_params=_cparams(("arbitrary",)),
        name="rmsnorm_cast",
    )(x, w.reshape(1, d))


def _side_steps(total_steps):
    return 1 << (total_steps.bit_length() - 1)


def _side_specs(arrays, n_steps, linear_step):
    def spec(a):
        rows = a.shape[0] // n_steps
        assert rows * n_steps == a.shape[0] and rows % 16 == 0, (a.shape, n_steps)
        return pl.BlockSpec((rows, a.shape[1]), lambda *g: (jnp.minimum(linear_step(*g), n_steps - 1), 0))
    return [spec(a) for a in arrays]


def _side_cast(step, n_steps, srcs, dsts):
    @pl.when(step < n_steps)
    def _():
        for src, dst in zip(srcs, dsts):
            dst[...] = src[...].astype(dst.dtype)


def _inproj_kernel(*refs, n_side, side_steps):
    a_ref, w_ref = refs[:2]
    side_in = refs[2:2 + n_side]
    o_ref = refs[2 + n_side]
    side_out = refs[3 + n_side:3 + 2 * n_side]
    wb_ref = refs[-1]

    @pl.when(pl.program_id(1) == 0)
    def _():
        wb_ref[...] = w_ref[...].astype(BF16)

    o_ref[...] = jnp.dot(a_ref[...], wb_ref[...], preferred_element_type=F32).astype(o_ref.dtype)
    if n_side:
        _side_cast(pl.program_id(0) * pl.num_programs(1) + pl.program_id(1), side_steps, side_in, side_out)


def _inproj(a, w, col0, ncols, out_dtype, side=(), tm=1024, tn=1024):
    m, k = a.shape
    tm = min(tm, m)
    assert col0 % tn == 0 and ncols % tn == 0
    grid = (ncols // tn, m // tm)
    side_steps = _side_steps(grid[0] * grid[1])
    side_specs = _side_specs(side, side_steps, lambda j, i: j * grid[1] + i)
    outs = pl.pallas_call(
        functools.partial(_inproj_kernel, n_side=len(side), side_steps=side_steps),
        grid=grid,
        in_specs=[pl.BlockSpec((tm, k), lambda j, i: (i, 0)),
                  pl.BlockSpec((k, tn), lambda j, i: (0, col0 // tn + j))] + side_specs,
        out_specs=[pl.BlockSpec((tm, tn), lambda j, i: (i, j))] + side_specs,
        out_shape=[jax.ShapeDtypeStruct((m, ncols), out_dtype)]
                  + [jax.ShapeDtypeStruct(s.shape, BF16) for s in side],
        scratch_shapes=[pltpu.VMEM((k, tn), BF16)],
        compiler_params=_cparams(("arbitrary", "arbitrary")),
        name="inproj",
    )(a, w, *side)
    return outs[0], tuple(outs[1:])


ROPE_ROWS = RET_CHUNK


def _inproj_rope_kernel(a_ref, w_ref, cos_ref, sin_ref, kd_ref, qk_ref, kk_ref, wb_ref, acc_ref, *, nq_tiles):
    j = pl.program_id(0)

    @pl.when(pl.program_id(1) == 0)
    def _():
        wb_ref[...] = w_ref[...].astype(BF16)

    acc_ref[...] = jnp.dot(a_ref[...], wb_ref[...], preferred_element_type=F32)
    tm, tn = acc_ref.shape
    scale = jnp.where(j >= nq_tiles, RET_DK ** -0.5, 1.0).astype(F32)
    for r in range(tm // ROPE_ROWS):
        rs = slice(r * ROPE_ROWS, (r + 1) * ROPE_ROWS)
        cos = cos_ref[rs, :]
        sin = sin_ref[rs, :]
        parts = []
        for hh in range(tn // RET_DK):
            x1 = acc_ref[rs, hh * RET_DK:hh * RET_DK + RET_HALF]
            x2 = acc_ref[rs, hh * RET_DK + RET_HALF:(hh + 1) * RET_DK]
            parts += [x1 * cos - x2 * sin, x1 * sin + x2 * cos]
        rot = jnp.concatenate(parts, axis=1) * scale
        qk_ref[rs, :] = rot.astype(qk_ref.dtype)

        @pl.when(j >= nq_tiles)
        def _():
            kk_ref[rs, :] = (rot * kd_ref[...]).astype(kk_ref.dtype)


def _inproj_rope(a, w, cos_rows, sin_rows, kd_rows, tm=1024, tn=1024):
    m, k = a.shape
    tm = min(tm, m)
    wq = W_QK // 2
    assert tm % ROPE_ROWS == 0 and wq % tn == 0 and tn % RET_DK == 0
    nq = wq // tn
    return pl.pallas_call(
        functools.partial(_inproj_rope_kernel, nq_tiles=nq),
        grid=(2 * nq, m // tm),
        in_specs=[
            pl.BlockSpec((tm, k), lambda j, i: (i, 0)),
            pl.BlockSpec((k, tn), lambda j, i: (0, COL_QK // tn + j)),
            pl.BlockSpec((tm, RET_HALF), lambda j, i: (i, 0)),
            pl.BlockSpec((tm, RET_HALF), lambda j, i: (i, 0)),
            pl.BlockSpec((ROPE_ROWS, tn), lambda j, i: (0, jnp.maximum(j - nq, 0))),
        ],
        out_specs=[
            pl.BlockSpec((tm, tn), lambda j, i: (i, j)),
            pl.BlockSpec((tm, tn), lambda j, i: (jnp.where(j >= nq, i, 0), jnp.maximum(j - nq, 0))),
        ],
        out_shape=[jax.ShapeDtypeStruct((m, W_QK), BF16), jax.ShapeDtypeStruct((m, wq), BF16)],
        scratch_shapes=[pltpu.VMEM((k, tn), BF16), pltpu.VMEM((tm, tn), F32)],
        compiler_params=_cparams(("arbitrary", "arbitrary")),
        name="inproj_rope",
    )(a, w, cos_rows, sin_rows, kd_rows)


def _retention_tables(c):
    h = RET_HEADS
    log_g = jnp.log1p(-jnp.exp2(-5.0 - jnp.arange(h, dtype=F32)))
    idx = jnp.arange(c, dtype=F32)
    diff = idx[:, None] - idx[None, :]
    decay = jnp.where(diff[None] >= 0.0, jnp.exp(jnp.maximum(diff, 0.0)[None] * log_g[:, None, None]), 0.0)
    q_dec = jnp.exp((idx[:, None] + 1.0) * log_g[None, :])
    k_dec = jnp.exp((c - 1.0 - idx)[:, None] * log_g[None, :])
    c_dec = jnp.exp(c * log_g)
    return decay, q_dec, k_dec, c_dec


def _rope_tables(pos):
    inv = ROPE_BASE ** (-jnp.arange(RET_HALF, dtype=F32) / RET_HALF)
    ang = pos.astype(F32)[:, None] * inv[None, :]
    return jnp.cos(ang), jnp.sin(ang)


def _lane_bcast(t):
    return jnp.broadcast_to(t.T[:, :, None], (t.shape[1], t.shape[0], LANES))


def _rope_inputs(b, l, pos0, c):
    cos, sin = _rope_tables(pos0 + jnp.arange(l))
    _, _, k_dec, _ = _retention_tables(c)
    kd = jnp.repeat(k_dec, RET_DK, axis=1)
    return jnp.tile(cos, (b, 1)), jnp.tile(sin, (b, 1)), jnp.tile(kd, (ROPE_ROWS // c, 1))


def _group_norm_gate(o, g):
    ms = jnp.mean(o * o, axis=-1, keepdims=True)
    return jax.nn.silu(g) * (o * lax.rsqrt(ms + EPS))


def _ret_prompt_kernel(*refs, n_side, side_steps):
    cdec_ref, q_ref, k_ref, kk_ref, v_ref, g_ref, dec_ref, qd_ref = refs[:8]
    side_in = refs[8:8 + n_side]
    o_ref, s_ref = refs[8 + n_side:10 + n_side]
    side_out = refs[10 + n_side:10 + 2 * n_side]
    if n_side:
        _side_cast(pl.program_id(0) * pl.num_programs(1) + pl.program_id(1), side_steps, side_in, side_out)

    @pl.when(pl.program_id(1) == 0)
    def _():
        s_ref[...] = jnp.zeros_like(s_ref)

    for h in range(RET_HEADS):
        qb = q_ref[:, h * RET_DK:(h + 1) * RET_DK]
        kb = k_ref[:, h * RET_DK:(h + 1) * RET_DK]
        kkb = kk_ref[:, h * RET_DK:(h + 1) * RET_DK]
        vb = v_ref[:, h * RET_DV:(h + 1) * RET_DV]
        att = lax.dot_general(qb, kb, (((1,), (1,)), ((), ())), preferred_element_type=F32) * dec_ref[h]
        s = s_ref[0, h]
        o = (jnp.dot(att.astype(BF16), vb, preferred_element_type=F32)
             + qd_ref[h][:, :1] * jnp.dot(qb, s.astype(BF16), preferred_element_type=F32))
        s_ref[0, h] = s * cdec_ref[h] + lax.dot_general(kkb, vb, (((0,), (0,)), ((), ())),
                                                         preferred_element_type=F32)
        og = _group_norm_gate(o, g_ref[:, h * RET_DV:(h + 1) * RET_DV])
        o_ref[:, h * RET_DV:(h + 1) * RET_DV] = og.astype(o_ref.dtype)


def _retention_prompt(qk, kk, v, rest, b, l, side=()):
    c = RET_CHUNK if l % RET_CHUNK == 0 else l
    nc = l // c
    decay, q_dec, _, c_dec = _retention_tables(c)
    wq = RET_HEADS * RET_DK
    wv = RET_HEADS * RET_DV
    full3 = lambda bi, ci: (0, 0, 0)
    side_steps = _side_steps(b * nc)
    side_specs = _side_specs(side, side_steps, lambda bi, ci: bi * nc + ci)
    outs = pl.pallas_call(
        functools.partial(_ret_prompt_kernel, n_side=len(side), side_steps=side_steps),
        grid=(b, nc),
        in_specs=[
            pl.BlockSpec(memory_space=pltpu.SMEM),
            pl.BlockSpec((c, wq), lambda bi, ci: (bi * nc + ci, 0)),
            pl.BlockSpec((c, wq), lambda bi, ci: (bi * nc + ci, 1)),
            pl.BlockSpec((c, wq), lambda bi, ci: (bi * nc + ci, 0)),
            pl.BlockSpec((c, wv), lambda bi, ci: (bi * nc + ci, 0)),
            pl.BlockSpec((c, wv), lambda bi, ci: (bi * nc + ci, REST_G // wv)),
            pl.BlockSpec((RET_HEADS, c, c), full3),
            pl.BlockSpec((RET_HEADS, c, LANES), full3),
        ] + side_specs,
        out_specs=[
            pl.BlockSpec((c, wv), lambda bi, ci: (bi * nc + ci, 0)),
            pl.BlockSpec((1, RET_HEADS, RET_DK, RET_DV), lambda bi, ci: (bi, 0, 0, 0)),
        ] + side_specs,
        out_shape=[
            jax.ShapeDtypeStruct((b * l, wv), BF16),
            jax.ShapeDtypeStruct((b, RET_HEADS, RET_DK, RET_DV), F32),
        ] + [jax.ShapeDtypeStruct(s.shape, BF16) for s in side],
        compiler_params=_cparams(("arbitrary", "arbitrary")),
        name="retention_prompt",
    )(c_dec, qk, qk, kk, v, rest, decay, _lane_bcast(q_dec), *side)
    return outs[0], outs[1], tuple(outs[2:])


SAMPLE_PAD = 16


def _pad_rows(x, rows):
    return jnp.concatenate([x, jnp.zeros((rows - x.shape[0], x.shape[1]), x.dtype)], axis=0)


def _ret_sample_kernel(cdec_ref, q_ref, k_ref, kk_ref, v_ref, g_ref, dec_ref, qd_ref, s0_ref, o_ref, s_ref,
                       *, nb, l):
    for h in range(RET_HEADS):
        qf = q_ref[:, h * RET_DK:(h + 1) * RET_DK].astype(F32)
        kf = k_ref[:, h * RET_DK:(h + 1) * RET_DK].astype(F32)
        kkf = kk_ref[:, h * RET_DK:(h + 1) * RET_DK].astype(F32)
        vf = v_ref[:, h * RET_DV:(h + 1) * RET_DV].astype(F32)
        for bt in range(nb):
            rs = slice(bt * l, (bt + 1) * l)
            qb = _pad_rows(qf[rs], SAMPLE_PAD).astype(BF16)
            kb = _pad_rows(kf[rs], SAMPLE_PAD).astype(BF16)
            kkb = _pad_rows(kkf[rs], SAMPLE_PAD).astype(BF16)
            vb = _pad_rows(vf[rs], SAMPLE_PAD).astype(BF16)
            att = lax.dot_general(qb, kb, (((1,), (1,)), ((), ())), preferred_element_type=F32) * dec_ref[h]
            s = s0_ref[bt, h]
            o = (jnp.dot(att.astype(BF16), vb, preferred_element_type=F32)
                 + qd_ref[h][:, :1] * jnp.dot(qb, s.astype(BF16), preferred_element_type=F32))
            s_ref[bt, h] = s * cdec_ref[h] + lax.dot_general(kkb, vb, (((0,), (0,)), ((), ())),
                                                              preferred_element_type=F32)
            og = _group_norm_gate(o[:l], g_ref[rs, h * RET_DV:(h + 1) * RET_DV])
            o_ref[rs, h * RET_DV:(h + 1) * RET_DV] = og.astype(o_ref.dtype)


def _retention_sample(qk, kk, v, rest, s0, b, l, nb=2):
    assert l % RET_CHUNK != 0 and l <= SAMPLE_PAD and b % nb == 0, (b, l)
    decay, q_dec, _, c_dec = _retention_tables(l)
    p = SAMPLE_PAD - l
    decay = jnp.pad(decay, ((0, 0), (0, p), (0, p)))
    q_dec = jnp.pad(q_dec, ((0, p), (0, 0)))
    wq = RET_HEADS * RET_DK
    wv = RET_HEADS * RET_DV
    rows = nb * l
    full3 = lambda bi: (0, 0, 0)
    state_spec = pl.BlockSpec((nb, RET_HEADS, RET_DK, RET_DV), lambda bi: (bi, 0, 0, 0))
    return pl.pallas_call(
        functools.partial(_ret_sample_kernel, nb=nb, l=l),
        grid=(b // nb,),
        in_specs=[
            pl.BlockSpec(memory_space=pltpu.SMEM),
            pl.BlockSpec((rows, wq), lambda bi: (bi, 0)),
            pl.BlockSpec((rows, wq), lambda bi: (bi, 1)),
            pl.BlockSpec((rows, wq), lambda bi: (bi, 0)),
            pl.BlockSpec((rows, wv), lambda bi: (bi, 0)),
            pl.BlockSpec((rows, wv), lambda bi: (bi, REST_G // wv)),
            pl.BlockSpec((RET_HEADS, SAMPLE_PAD, SAMPLE_PAD), full3),
            pl.BlockSpec((RET_HEADS, SAMPLE_PAD, LANES), full3),
            state_spec,
        ],
        out_specs=[pl.BlockSpec((rows, wv), lambda bi: (bi, 0)), state_spec],
        out_shape=[
            jax.ShapeDtypeStruct((b * l, wv), BF16),
            jax.ShapeDtypeStruct((b, RET_HEADS, RET_DK, RET_DV), F32),
        ],
        compiler_params=_cparams(("arbitrary",)),
        name="retention_sample",
    )(c_dec, qk, qk, kk, v, rest, decay, _lane_bcast(q_dec), s0)


CONV_PAD = 32
CONV_COLS = LANES


def _conv_strip(block, wts, nblk, store):
    lead = CONV_PAD - CONV_HIST
    taps = [((lead + w) // SUBLANES, (lead + w) % SUBLANES) for w in range(CONV_WIDTH)]
    rowid = lax.broadcasted_iota(I32, (SUBLANES, CONV_COLS), 0)

    def partial_sums(m, shifts):
        ys = {}
        for w, (a, s) in enumerate(taps):
            if s in shifts:
                term = block(m + a) * wts[w]
                ys[s] = term if s not in ys else ys[s] + term
        return ys

    shifted = range(1, SUBLANES)
    prev = partial_sums(0, shifted)
    for m in range(nblk):
        nxt = partial_sums(m + 1, shifted)
        acc = partial_sums(m, (0,))[0]
        for s in shifted:
            acc = acc + pltpu.roll(jnp.where(rowid >= s, prev[s], nxt[s]), SUBLANES - s, axis=0)
        store(m, acc)
        prev = nxt


def _conv_kernel(*refs, tl, nb, nl, n_side, side_steps):
    ca_ref, cb_ref, buf_ref, cw_ref, bias_ref, lnw_ref, lnb_ref = refs[:7]
    side_in = refs[7:7 + n_side]
    sw_ref, nbuf_ref = refs[7 + n_side:9 + n_side]
    side_out = refs[9 + n_side:9 + 2 * n_side]
    upad_ref, c_ref = refs[-2:]
    if n_side:
        _side_cast(pl.program_id(0) * nl + pl.program_id(1), side_steps, side_in, side_out)
    li = pl.program_id(1)
    lead = CONV_PAD - CONV_HIST
    d = ca_ref.shape[1]
    for bt in range(nb):
        @pl.when(li == 0)
        def _():
            upad_ref[bt, 0:lead, :] = jnp.zeros((lead, d), F32)
            upad_ref[bt, lead:CONV_PAD, :] = buf_ref[0, bt]

        r0 = bt * tl
        upad_ref[bt, CONV_PAD:CONV_PAD + tl, :] = ca_ref[r0:r0 + tl, :] * jax.nn.sigmoid(cb_ref[r0:r0 + tl, :])
        for cc in range(d // CONV_COLS):
            cs = slice(cc * CONV_COLS, (cc + 1) * CONV_COLS)
            wts = [jnp.broadcast_to(cw_ref[0, w:w + 1, cs], (SUBLANES, CONV_COLS)) for w in range(CONV_WIDTH)]
            blocks = {}

            def block(j, cs=cs, blocks=blocks):
                if j not in blocks:
                    blocks[j] = upad_ref[bt, j * SUBLANES:(j + 1) * SUBLANES, cs]
                return blocks[j]

            def store(m, acc, cs=cs):
                c_ref[m * SUBLANES:(m + 1) * SUBLANES, cs] = acc

            _conv_strip(block, wts, tl // SUBLANES, store)
        cf = c_ref[...] + bias_ref[...]
        mu = jnp.mean(cf, axis=-1, keepdims=True)
        var = jnp.mean(jnp.square(cf - mu), axis=-1, keepdims=True)
        cn = (cf - mu) * lax.rsqrt(var + EPS) * lnw_ref[...] + lnb_ref[...]
        sw_ref[r0:r0 + tl, :] = jax.nn.silu(cn).astype(sw_ref.dtype)

        @pl.when(li == nl - 1)
        def _():
            nbuf_ref[0, bt] = upad_ref[bt, tl + lead:tl + CONV_PAD, :]

        if nl > 1:
            upad_ref[bt, 0:CONV_PAD, :] = upad_ref[bt, tl:tl + CONV_PAD, :]


def _conv_branch(rest, buf, conv_w, conv_b, ln_w, ln_b, b, l, tl, nb, side=()):
    d = D_MODEL
    nl = l // tl
    assert nb == 1 or nl == 1
    rows = nb * tl
    const2 = lambda bi, li: (0, 0)
    grid = (b // nb, nl)
    side_steps = _side_steps(grid[0] * grid[1])
    side_specs = _side_specs(side, side_steps, lambda bi, li: bi * nl + li)
    state_spec = pl.BlockSpec((1, nb, CONV_HIST, d), lambda bi, li: (0, bi, 0, 0))
    outs = pl.pallas_call(
        functools.partial(_conv_kernel, tl=tl, nb=nb, nl=nl, n_side=len(side), side_steps=side_steps),
        grid=grid,
        in_specs=[
            pl.BlockSpec((rows, d), lambda bi, li: (bi * nl + li, REST_CA // d)),
            pl.BlockSpec((rows, d), lambda bi, li: (bi * nl + li, REST_CB // d)),
            state_spec,
            pl.BlockSpec((1, CONV_WIDTH, d), lambda bi, li: (0, 0, 0)),
            pl.BlockSpec((1, d), const2),
            pl.BlockSpec((1, d), const2),
            pl.BlockSpec((1, d), const2),
        ] + side_specs,
        out_specs=[pl.BlockSpec((rows, d), lambda bi, li: (bi * nl + li, 0)), state_spec] + side_specs,
        out_shape=[
            jax.ShapeDtypeStruct((b * l, d), BF16),
            jax.ShapeDtypeStruct((1, b, CONV_HIST, d), F32),
        ] + [jax.ShapeDtypeStruct(s.shape, BF16) for s in side],
        scratch_shapes=[pltpu.VMEM((nb, CONV_PAD + tl, d), F32), pltpu.VMEM((tl, d), F32)],
        compiler_params=_cparams(("arbitrary", "arbitrary")),
        name="conv_branch",
    )(rest, rest, buf, conv_w, conv_b, ln_w, ln_b, *side)
    return outs[0], outs[1], tuple(outs[2:])


def _merge_kernel(og_ref, sw_ref, wr_ref, wc_ref, gr_ref, gc_ref, o_ref, wrb_ref, wcb_ref):
    @pl.when(pl.program_id(1) == 0)
    def _():
        wrb_ref[...] = wr_ref[...].astype(BF16)
        wcb_ref[...] = wc_ref[...].astype(BF16)

    ret_out = jnp.dot(og_ref[...], wrb_ref[...], preferred_element_type=F32)
    conv_out = jnp.dot(sw_ref[...], wcb_ref[...], preferred_element_type=F32)
    merged = jax.nn.sigmoid(gr_ref[...]) * ret_out + jax.nn.sigmoid(gc_ref[...]) * conv_out
    o_ref[...] = merged.astype(o_ref.dtype)


def _merge(og, sw, w_ret_o, w_conv_o, rest, tm=512, tn=512):
    m = og.shape[0]
    n = w_ret_o.shape[1]
    tm = min(tm, m)
    kr = og.shape[1]
    kc = sw.shape[1]
    return pl.pallas_call(
        _merge_kernel,
        grid=(n // tn, m // tm),
        in_specs=[
            pl.BlockSpec((tm, kr), lambda j, i: (i, 0)),
            pl.BlockSpec((tm, kc), lambda j, i: (i, 0)),
            pl.BlockSpec((kr, tn), lambda j, i: (0, j)),
            pl.BlockSpec((kc, tn), lambda j, i: (0, j)),
            pl.BlockSpec((tm, tn), lambda j, i: (i, REST_GRET // tn + j)),
            pl.BlockSpec((tm, tn), lambda j, i: (i, REST_GCONV // tn + j)),
        ],
        out_specs=pl.BlockSpec((tm, tn), lambda j, i: (i, j)),
        out_shape=jax.ShapeDtypeStruct((m, n), BF16),
        scratch_shapes=[pltpu.VMEM((kr, tn), BF16), pltpu.VMEM((kc, tn), BF16)],
        compiler_params=_cparams(("arbitrary", "arbitrary")),
        name="merge",
    )(og, sw, w_ret_o, w_conv_o, rest, rest)


def _outproj_kernel(a_ref, w_ref, x_ref, *rest):
    o_ref, wb_ref = rest[-2], rest[-1]

    @pl.when(pl.program_id(1) == 0)
    def _():
        wb_ref[...] = w_ref[...].astype(BF16)

    o_ref[...] = x_ref[...] + jnp.dot(a_ref[...], wb_ref[...], preferred_element_type=F32)


def _outproj_residual(a, w, x, total_rows, row0, prev=None, tm=512, tn=512):
    m, k = a.shape
    n = w.shape[1]
    tm = min(tm, m)
    assert row0 % tm == 0
    off = row0 // tm
    in_specs = [
        pl.BlockSpec((tm, k), lambda j, i: (i, 0)),
        pl.BlockSpec((k, tn), lambda j, i: (0, j)),
        pl.BlockSpec((tm, tn), lambda j, i: (i, j)),
    ]
    args = [a, w, x]
    aliases = {}
    if prev is not None:
        in_specs.append(pl.BlockSpec(memory_space=pl.ANY))
        args.append(prev)
        aliases = {3: 0}
    return pl.pallas_call(
        _outproj_kernel,
        grid=(n // tn, m // tm),
        in_specs=in_specs,
        out_specs=pl.BlockSpec((tm, tn), lambda j, i: (i + off, j)),
        out_shape=jax.ShapeDtypeStruct((total_rows, n), F32),
        scratch_shapes=[pltpu.VMEM((k, tn), BF16)],
        input_output_aliases=aliases,
        compiler_params=_cparams(("arbitrary", "arbitrary")),
        name="outproj_residual",
    )(*args)


def _router_kernel(x_ref, nw_ref, w_ref, b_ref, h_ref, meta_ref, meta_t_ref, cnt_ref, base_ref):
    tm = x_ref.shape[0]

    @pl.when(pl.program_id(0) == 0)
    def _():
        base_ref[...] = jnp.zeros_like(base_ref)

    x = x_ref[...]
    ms = jnp.mean(x * x, axis=-1, keepdims=True)
    h = x * lax.rsqrt(ms + EPS) * nw_ref[...]
    h_ref[...] = h
    logits = jnp.dot(h, w_ref[...], preferred_element_type=F32, precision=lax.Precision.HIGHEST) + b_ref[...]
    lane = lax.broadcasted_iota(I32, (tm, ROUTER_COLS), 1)

    def first_argmax(vals):
        top = jnp.max(vals, axis=1, keepdims=True)
        return top, jnp.min(jnp.where(vals == top, lane, ROUTER_COLS), axis=1, keepdims=True)

    cmask = lane < N_GROUPS
    lc = jnp.where(cmask, logits, NEG)
    mc, grp = first_argmax(lc)
    p_grp = 1.0 / jnp.sum(jnp.where(cmask, jnp.exp(lc - mc), 0.0), axis=1, keepdims=True)
    lo = ROUTER_OFF + grp * EXPERTS_PER_GROUP
    lf = jnp.where((lane >= lo) & (lane < lo + EXPERTS_PER_GROUP), logits, NEG)
    v1, i1 = first_argmax(lf)
    v2, i2 = first_argmax(jnp.where(lane == i1, NEG, lf))
    e2 = jnp.exp(v2 - v1)
    den = 1.0 + e2
    g1 = p_grp * (1.0 / den)
    g2 = p_grp * (e2 / den)

    oh1 = lane == i1
    oh2 = lane == i2
    both = jnp.where(oh1 | oh2, 1.0, 0.0)
    row = lax.broadcasted_iota(I32, (tm, tm), 0)
    col = lax.broadcasted_iota(I32, (tm, tm), 1)
    earlier = jnp.where(col < row, 1.0, 0.0).astype(BF16)
    before = jnp.dot(earlier, both.astype(BF16), preferred_element_type=F32) + base_ref[...]
    r1 = jnp.sum(jnp.where(oh1, before, 0.0), axis=1, keepdims=True)
    r2 = jnp.sum(jnp.where(oh2, before, 0.0), axis=1, keepdims=True)
    base_ref[...] = base_ref[...] + jnp.sum(both, axis=0, keepdims=True)
    cnt_ref[...] = base_ref[...]

    fields = [(i1 - ROUTER_OFF).astype(F32), (i2 - ROUTER_OFF).astype(F32), g1, g2, r1, r2]
    meta = jnp.zeros((tm, ROUTER_COLS), F32)
    for k, f in enumerate(fields):
        meta = jnp.where(lane == k, f, meta)
    meta_ref[...] = meta
    meta_t_ref[...] = meta.T[:SUBLANES]


def _router(x, norm_w, w_router, b_router, tm=512):
    n, d = x.shape
    tm = min(tm, n)
    const2 = lambda i: (0, 0)
    return pl.pallas_call(
        _router_kernel,
        grid=(n // tm,),
        in_specs=[
            pl.BlockSpec((tm, d), lambda i: (i, 0)),
            pl.BlockSpec((1, d), const2),
            pl.BlockSpec((d, ROUTER_COLS), const2),
            pl.BlockSpec((1, ROUTER_COLS), const2),
        ],
        out_specs=[
            pl.BlockSpec((tm, d), lambda i: (i, 0)),
            pl.BlockSpec((tm, ROUTER_COLS), lambda i: (i, 0)),
            pl.BlockSpec((SUBLANES, tm), lambda i: (0, i)),
            pl.BlockSpec((1, ROUTER_COLS), const2),
        ],
        out_shape=[
            jax.ShapeDtypeStruct((n, d), F32),
            jax.ShapeDtypeStruct((n, ROUTER_COLS), F32),
            jax.ShapeDtypeStruct((SUBLANES, n), F32),
            jax.ShapeDtypeStruct((1, ROUTER_COLS), F32),
        ],
        scratch_shapes=[pltpu.VMEM((1, ROUTER_COLS), F32)],
        compiler_params=_cparams(("arbitrary",)),
        name="router",
    )(x, norm_w.reshape(1, d), w_router, b_router)


def _dispatch_kernel(rows_ref, span_ref, h_ref, xs_hbm, zero_ref, sem, zsem):
    td = h_ref.shape[0]
    i = pl.program_id(0)

    @pl.when(i == 0)
    def _():
        zero_ref[...] = jnp.zeros_like(zero_ref)

        def fill_copy(e):
            last = pl.multiple_of(span_ref[N_EXPERTS + e] - MOE_ROWS, MOE_ROWS)
            return pltpu.make_async_copy(zero_ref, xs_hbm.at[pl.ds(last, MOE_ROWS)], zsem)

        def nonempty(e):
            return span_ref[N_EXPERTS + e] > span_ref[e]

        def start(e, carry):
            @pl.when(nonempty(e))
            def _():
                fill_copy(e).start()
            return carry

        def wait(e, carry):
            @pl.when(nonempty(e))
            def _():
                fill_copy(e).wait()
            return carry

        lax.fori_loop(0, N_EXPERTS, start, 0)
        lax.fori_loop(0, N_EXPERTS, wait, 0)

    n_tok = pl.num_programs(0) * td

    def token(t, carry):
        n = i * td + t
        for k in range(TOP_K_FINE):
            r = rows_ref[k * n_tok + n]
            pltpu.make_async_copy(h_ref.at[pl.ds(t, 1)], xs_hbm.at[pl.ds(r, 1)], sem).start()
        return carry

    lax.fori_loop(0, td, token, 0)
    for _ in range(TOP_K_FINE):
        pltpu.make_async_copy(h_ref, xs_hbm.at[pl.ds(0, td)], sem).wait()


def _dispatch(rows_flat, span, h, n_rows, td=1024):
    n_tok, d = h.shape
    td = min(td, n_tok)
    return pl.pallas_call(
        _dispatch_kernel,
        grid_spec=pltpu.PrefetchScalarGridSpec(
            num_scalar_prefetch=2,
            grid=(n_tok // td,),
            in_specs=[pl.BlockSpec((td, d), lambda i, rows, span: (i, 0))],
            out_specs=pl.BlockSpec(memory_space=pl.ANY),
            scratch_shapes=[pltpu.VMEM((MOE_ROWS, d), F32), pltpu.SemaphoreType.DMA(()),
                            pltpu.SemaphoreType.DMA(())],
        ),
        out_shape=jax.ShapeDtypeStruct((n_rows, d), F32),
        compiler_params=_cparams(("arbitrary",)),
        name="moe_dispatch",
    )(rows_flat, span, h)


def _expert_kernel(be_ref, nu_ref, x_ref, wg_ref, wu_ref, wd_ref, o_ref):
    del be_ref

    @pl.when(pl.program_id(0) < nu_ref[0])
    def _():
        xb = x_ref[...].astype(BF16)
        a = jnp.dot(xb, wg_ref[0], preferred_element_type=F32)
        b = jnp.dot(xb, wu_ref[0], preferred_element_type=F32)
        hb = (jax.nn.silu(a) * b).astype(BF16)
        o_ref[...] = jnp.dot(hb, wd_ref[0], preferred_element_type=F32)


def _experts(blk_exp, n_used, xs, wg, wu, wd):
    n_rows, d = xs.shape
    n_blocks = n_rows // MOE_ROWS
    de = wg.shape[2]
    row_map = lambda i, be, nu: (jnp.minimum(i, jnp.maximum(nu[0] - 1, 0)), 0)
    return pl.pallas_call(
        _expert_kernel,
        grid_spec=pltpu.PrefetchScalarGridSpec(
            num_scalar_prefetch=2,
            grid=(n_blocks,),
            in_specs=[
                pl.BlockSpec((MOE_ROWS, d), row_map),
                pl.BlockSpec((1, d, de), lambda i, be, nu: (be[i], 0, 0)),
                pl.BlockSpec((1, d, de), lambda i, be, nu: (be[i], 0, 0)),
                pl.BlockSpec((1, de, d), lambda i, be, nu: (be[i], 0, 0)),
            ],
            out_specs=pl.BlockSpec((MOE_ROWS, d), row_map),
        ),
        out_shape=jax.ShapeDtypeStruct((n_rows, d), F32),
        compiler_params=_cparams(("arbitrary",)),
        name="moe_experts",
    )(blk_exp, n_used, xs, wg, wu, wd)


def _combine_kernel(rows_ref, ys_hbm, x_ref, meta_ref, nf_ref, op_ref, os_ref, buf_ref, sem, *, p_tiles):
    tm = x_ref.shape[0]
    i = pl.program_id(0)
    nt = pl.num_programs(0)
    n_tok = nt * tm
    slot = i % 2

    def gather(tile, dst_slot):
        def token(t, carry):
            n = tile * tm + t
            for k in range(TOP_K_FINE):
                r = rows_ref[k * n_tok + n]
                pltpu.make_async_copy(ys_hbm.at[pl.ds(r, 1)], buf_ref.at[dst_slot, k, pl.ds(t, 1)],
                                      sem.at[dst_slot]).start()
            return carry

        lax.fori_loop(0, tm, token, 0)

    @pl.when(i == 0)
    def _():
        gather(0, 0)

    @pl.when(i + 1 < nt)
    def _():
        gather(i + 1, 1 - slot)

    for k in range(TOP_K_FINE):
        pltpu.make_async_copy(ys_hbm.at[pl.ds(0, tm)], buf_ref.at[slot, k], sem.at[slot]).wait()
    meta = meta_ref[...]
    y = x_ref[...] + (buf_ref[slot, 0] * meta[:, 2:3] + buf_ref[slot, 1] * meta[:, 3:4])
    ms = jnp.mean(y * y, axis=-1, keepdims=True)
    out = y * lax.rsqrt(ms + EPS) * nf_ref[...]

    @pl.when(i < p_tiles)
    def _():
        op_ref[...] = out

    @pl.when(i >= p_tiles)
    def _():
        os_ref[...] = out


def _combine(rows_flat, ys, x, meta, norm_f, n_p, tm=256):
    n, d = x.shape
    n_s = n - n_p
    tm = min(tm, n_p, n_s)
    assert n_p % tm == 0 and n_s % tm == 0
    p_tiles = n_p // tm
    return pl.pallas_call(
        functools.partial(_combine_kernel, p_tiles=p_tiles),
        grid_spec=pltpu.PrefetchScalarGridSpec(
            num_scalar_prefetch=1,
            grid=(n // tm,),
            in_specs=[
                pl.BlockSpec(memory_space=pl.ANY),
                pl.BlockSpec((tm, d), lambda i, rows: (i, 0)),
                pl.BlockSpec((tm, ROUTER_COLS), lambda i, rows: (i, 0)),
                pl.BlockSpec((1, d), lambda i, rows: (0, 0)),
            ],
            out_specs=[
                pl.BlockSpec((tm, d), lambda i, rows: (jnp.minimum(i, p_tiles - 1), 0)),
                pl.BlockSpec((tm, d), lambda i, rows: (jnp.maximum(i - p_tiles, 0), 0)),
            ],
            scratch_shapes=[pltpu.VMEM((2, TOP_K_FINE, tm, d), F32), pltpu.SemaphoreType.DMA((2,))],
        ),
        out_shape=[jax.ShapeDtypeStruct((n_p, d), F32), jax.ShapeDtypeStruct((n_s, d), F32)],
        compiler_params=_cparams(("arbitrary",)),
        name="moe_combine",
    )(rows_flat, ys, x, meta, norm_f.reshape(1, d))


def _moe_and_final_norm(x, n_p, norm_ffn, w_coarse, b_coarse, w_fine, b_fine, wg, wu, wd, norm_f):
    n_tok, d = x.shape
    pad = ROUTER_COLS - N_GROUPS - N_EXPERTS
    w_router = jnp.concatenate([w_coarse, w_fine, jnp.zeros((d, pad), F32)], axis=1)
    b_router = jnp.concatenate([b_coarse, b_fine, jnp.zeros((pad,), F32)]).reshape(1, ROUTER_COLS)
    h, meta, meta_t, counts = _router(x, norm_ffn, w_router, b_router)

    n_asg = n_tok * TOP_K_FINE
    n_blocks = -(-n_asg // MOE_ROWS) + N_EXPERTS
    n_rows = n_blocks * MOE_ROWS
    cnt = counts[0, ROUTER_OFF:ROUTER_OFF + N_EXPERTS].astype(I32)
    padded = (cnt + MOE_ROWS - 1) // MOE_ROWS * MOE_ROWS
    pend = jnp.cumsum(padded)
    pstart = pend - padded
    expert = meta_t[0:TOP_K_FINE].astype(I32)
    rows = meta_t[4:4 + TOP_K_FINE].astype(I32)
    for e in range(N_EXPERTS):
        rows = rows + jnp.where(expert == e, pstart[e], 0)
    rows_flat = rows.reshape(-1)
    n_used = (pend[-1] // MOE_ROWS).astype(I32)
    blk = jnp.arange(n_blocks, dtype=I32)
    blk_exp = jnp.searchsorted(pend, jnp.minimum(blk, n_used - 1) * MOE_ROWS, side='right').astype(I32)
    blk_exp = jnp.minimum(blk_exp, N_EXPERTS - 1)

    xs = _dispatch(rows_flat, jnp.concatenate([pstart, pend]).astype(I32), h, n_rows)
    ys = _experts(blk_exp, n_used.reshape(1), xs, wg, wu, wd)
    return _combine(rows_flat, ys, x, meta, norm_f, n_p)


def _mixer(x, s_ret, s_conv, pos0, norm_mix, w_in, w_ret_o, conv_w, conv_b, conv_ln_w, conv_ln_b,
           w_conv_o, w_o, total_rows, row0, prev, side=((), (), ())):
    b, l, d = x.shape
    x2 = x.reshape(b * l, d)
    chunk = RET_CHUNK if l % RET_CHUNK == 0 else l
    h = _rmsnorm_cast(x2, norm_mix)
    qk, kk = _inproj_rope(h, w_in, *_rope_inputs(b, l, pos0, chunk))
    v, _ = _inproj(h, w_in, COL_V, W_V, BF16)
    rest, cast_a = _inproj(h, w_in, COL_REST, W_REST, F32, side[0])
    if s_ret is None:
        og, s_ret_new, cast_b = _retention_prompt(qk, kk, v, rest, b, l, side[1])
        sw, s_conv_new, cast_c = _conv_branch(rest, s_conv, conv_w, conv_b, conv_ln_w, conv_ln_b, b, l,
                                              tl=min(128, l), nb=1, side=side[2])
    else:
        assert not side[1]
        cast_b = ()
        og, s_ret_new = _retention_sample(qk, kk, v, rest, s_ret, b, l)
        sw, s_conv_new, cast_c = _conv_branch(rest, s_conv, conv_w, conv_b, conv_ln_w, conv_ln_b, b, l,
                                              tl=l, nb=2, side=side[2])
    merged = _merge(og, sw, w_ret_o, w_conv_o, rest)
    x1 = _outproj_residual(merged, w_o, x2, total_rows, row0, prev)
    return x1, s_ret_new, s_conv_new, cast_a + cast_b + cast_c


def kernel(x_prompt, x_sample, state_ret, state_conv, norm_mix, w_in, w_ret_o, conv_w, conv_b, conv_ln_w,
           conv_ln_b, w_conv_o, w_o, norm_ffn, w_coarse, b_coarse, w_fine, b_fine, w_gate, w_up, w_down, norm_f):
    depth = w_in.shape[0]
    assert depth == 1, "single-layer trunk"
    bp, lp, d = x_prompt.shape
    bs, ls, _ = x_sample.shape
    n_p = bp * lp
    n_s = bs * ls
    mix_w = (norm_mix[0], w_in[0], w_ret_o[0], conv_w, conv_b, conv_ln_w, conv_ln_b, w_conv_o[0], w_o[0])
    n_exp, _, d_exp = w_gate.shape[1:]
    flat = lambda w: w.reshape(n_exp * w.shape[2], w.shape[3])
    zero_conv = jnp.zeros((1, bp, CONV_HIST, d), F32)
    x1, ret_p, conv_p, (wg, wu, wd) = _mixer(
        x_prompt, None, zero_conv, 0, *mix_w, n_p + n_s, 0, None,
        side=((flat(w_gate),), (flat(w_up),), (flat(w_down),)))
    x1, ret_s, conv_s, _ = _mixer(x_sample, state_ret[0], state_conv, PAST_LEN, *mix_w, n_p + n_s, n_p, x1)
    yp, ys = _moe_and_final_norm(x1, n_p, norm_ffn[0], w_coarse[0], b_coarse[0], w_fine[0], b_fine[0],
                                 wg.reshape(n_exp, d, d_exp), wu.reshape(n_exp, d, d_exp),
                                 wd.reshape(n_exp, d_exp, d), norm_f)
    return (yp.reshape(bp, lp, d), ys.reshape(bs, ls, d), ret_p[None], conv_p, ret_s[None], conv_s)
```

```python
import functools

import jax
import jax.numpy as jnp
from jax import lax
from jax.experimental import pallas as pl
from jax.experimental.pallas import tpu as pltpu

F32 = jnp.float32
BF16 = jnp.bfloat16
I32 = jnp.int32

D_MODEL = 2048
PAST_LEN = 16384
RET_HEADS = 8
RET_DK = D_MODEL // RET_HEADS
RET_DV = 2 * D_MODEL // RET_HEADS
RET_HALF = RET_DK // 2
RET_CHUNK = 128
ROPE_BASE = 10000.0
CONV_WIDTH = 31
CONV_HIST = CONV_WIDTH - 1
N_GROUPS = 4
EXPERTS_PER_GROUP = 8
N_EXPERTS = N_GROUPS * EXPERTS_PER_GROUP
TOP_K_FINE = 2
D_EXPERT = D_MODEL // 2
EPS = 1e-6

W_QK = 2 * RET_HEADS * RET_DK
W_V = RET_HEADS * RET_DV
COL_QK = 0
COL_V = COL_QK + W_QK
COL_REST = COL_V + W_V
REST_G = 0
REST_CA = REST_G + RET_HEADS * RET_DV
REST_CB = REST_CA + D_MODEL
REST_GRET = REST_CB + D_MODEL
REST_GCONV = REST_GRET + D_MODEL
W_REST = REST_GCONV + D_MODEL

LANES = 128
SUBLANES = 8
VMEM_LIMIT = 56 * 1024 * 1024
MOE_ROWS = 256
MOE_PARTS = 2
MOE_BLOCK = MOE_PARTS * MOE_ROWS
ROUTER_COLS = LANES
ROUTER_OFF = N_GROUPS
NEG = -3.0e38


def _cparams(sem):
    return pltpu.CompilerParams(dimension_semantics=sem, vmem_limit_bytes=VMEM_LIMIT)


def _rmsnorm_kernel(x_ref, w_ref, o_ref):
    x = x_ref[...]
    ms = jnp.mean(x * x, axis=-1, keepdims=True)
    o_ref[...] = (x * lax.rsqrt(ms + EPS) * w_ref[...]).astype(o_ref.dtype)


def _rmsnorm_cast(x, w, tm=512):
    m, d = x.shape
    tm = min(tm, m)
    return pl.pallas_call(
        _rmsnorm_kernel,
        grid=(m // tm,),
        in_specs=[pl.BlockSpec((tm, d), lambda i: (i, 0)), pl.BlockSpec((1, d), lambda i: (0, 0))],
        out_specs=pl.BlockSpec((tm, d), lambda i: (i, 0)),
        out_shape=jax.ShapeDtypeStruct((m, d), BF16),
        compiler_params=_cparams(("arbitrary",)),
        name="rmsnorm_cast",
    )(x, w.reshape(1, d))


def _side_steps(total_steps):
    return 1 << (total_steps.bit_length() - 1)


def _side_specs(arrays, n_steps, linear_step):
    def spec(a):
        rows = a.shape[0] // n_steps
        assert rows * n_steps == a.shape[0] and rows % 16 == 0, (a.shape, n_steps)
        return pl.BlockSpec((rows, a.shape[1]), lambda *g: (jnp.minimum(linear_step(*g), n_steps - 1), 0))
    return [spec(a) for a in arrays]


def _side_cast(step, n_steps, srcs, dsts):
    @pl.when(step < n_steps)
    def _():
        for src, dst in zip(srcs, dsts):
            dst[...] = src[...].astype(dst.dtype)


def _inproj_kernel(*refs, n_side, side_steps):
    a_ref, w_ref = refs[:2]
    side_in = refs[2:2 + n_side]
    o_ref = refs[2 + n_side]
    side_out = refs[3 + n_side:3 + 2 * n_side]
    wb_ref = refs[-1]

    @pl.when(pl.program_id(1) == 0)
    def _():
        wb_ref[...] = w_ref[...].astype(BF16)

    o_ref[...] = jnp.dot(a_ref[...], wb_ref[...], preferred_element_type=F32).astype(o_ref.dtype)
    if n_side:
        _side_cast(pl.program_id(0) * pl.num_programs(1) + pl.program_id(1), side_steps, side_in, side_out)


def _inproj(a, w, col0, ncols, out_dtype, side=(), tm=1024, tn=1024):
    m, k = a.shape
    tm = min(tm, m)
    assert col0 % tn == 0 and ncols % tn == 0
    grid = (ncols // tn, m // tm)
    side_steps = _side_steps(grid[0] * grid[1])
    side_specs = _side_specs(side, side_steps, lambda j, i: j * grid[1] + i)
    outs = pl.pallas_call(
        functools.partial(_inproj_kernel, n_side=len(side), side_steps=side_steps),
        grid=grid,
        in_specs=[pl.BlockSpec((tm, k), lambda j, i: (i, 0)),
                  pl.BlockSpec((k, tn), lambda j, i: (0, col0 // tn + j))] + side_specs,
        out_specs=[pl.BlockSpec((tm, tn), lambda j, i: (i, j))] + side_specs,
        out_shape=[jax.ShapeDtypeStruct((m, ncols), out_dtype)]
                  + [jax.ShapeDtypeStruct(s.shape, BF16) for s in side],
        scratch_shapes=[pltpu.VMEM((k, tn), BF16)],
        compiler_params=_cparams(("arbitrary", "arbitrary")),
        name="inproj",
    )(a, w, *side)
    return outs[0], tuple(outs[1:])


ROPE_ROWS = RET_CHUNK


def _inproj_rope_kernel(a_ref, w_ref, cos_ref, sin_ref, kd_ref, qk_ref, kk_ref, wb_ref, acc_ref, *, nq_tiles):
    j = pl.program_id(0)

    @pl.when(pl.program_id(1) == 0)
    def _():
        wb_ref[...] = w_ref[...].astype(BF16)

    acc_ref[...] = jnp.dot(a_ref[...], wb_ref[...], preferred_element_type=F32)
    tm, tn = acc_ref.shape
    scale = jnp.where(j >= nq_tiles, RET_DK ** -0.5, 1.0).astype(F32)
    for r in range(tm // ROPE_ROWS):
        rs = slice(r * ROPE_ROWS, (r + 1) * ROPE_ROWS)
        cos = cos_ref[rs, :]
        sin = sin_ref[rs, :]
        parts = []
        for hh in range(tn // RET_DK):
            x1 = acc_ref[rs, hh * RET_DK:hh * RET_DK + RET_HALF]
            x2 = acc_ref[rs, hh * RET_DK + RET_HALF:(hh + 1) * RET_DK]
            parts += [x1 * cos - x2 * sin, x1 * sin + x2 * cos]
        rot = jnp.concatenate(parts, axis=1) * scale
        qk_ref[rs, :] = rot.astype(qk_ref.dtype)

        @pl.when(j >= nq_tiles)
        def _():
            kk_ref[rs, :] = (rot * kd_ref[...]).astype(kk_ref.dtype)


def _inproj_rope(a, w, cos_rows, sin_rows, kd_rows, tm=1024, tn=1024):
    m, k = a.shape
    tm = min(tm, m)
    wq = W_QK // 2
    assert tm % ROPE_ROWS == 0 and wq % tn == 0 and tn % RET_DK == 0
    nq = wq // tn
    return pl.pallas_call(
        functools.partial(_inproj_rope_kernel, nq_tiles=nq),
        grid=(2 * nq, m // tm),
        in_specs=[
            pl.BlockSpec((tm, k), lambda j, i: (i, 0)),
            pl.BlockSpec((k, tn), lambda j, i: (0, COL_QK // tn + j)),
            pl.BlockSpec((tm, RET_HALF), lambda j, i: (i, 0)),
            pl.BlockSpec((tm, RET_HALF), lambda j, i: (i, 0)),
            pl.BlockSpec((ROPE_ROWS, tn), lambda j, i: (0, jnp.maximum(j - nq, 0))),
        ],
        out_specs=[
            pl.BlockSpec((tm, tn), lambda j, i: (i, j)),
            pl.BlockSpec((tm, tn), lambda j, i: (jnp.where(j >= nq, i, 0), jnp.maximum(j - nq, 0))),
        ],
        out_shape=[jax.ShapeDtypeStruct((m, W_QK), BF16), jax.ShapeDtypeStruct((m, wq), BF16)],
        scratch_shapes=[pltpu.VMEM((k, tn), BF16), pltpu.VMEM((tm, tn), F32)],
        compiler_params=_cparams(("arbitrary", "arbitrary")),
        name="inproj_rope",
    )(a, w, cos_rows, sin_rows, kd_rows)


def _retention_tables(c):
    h = RET_HEADS
    log_g = jnp.log1p(-jnp.exp2(-5.0 - jnp.arange(h, dtype=F32)))
    idx = jnp.arange(c, dtype=F32)
    diff = idx[:, None] - idx[None, :]
    decay = jnp.where(diff[None] >= 0.0, jnp.exp(jnp.maximum(diff, 0.0)[None] * log_g[:, None, None]), 0.0)
    q_dec = jnp.exp((idx[:, None] + 1.0) * log_g[None, :])
    k_dec = jnp.exp((c - 1.0 - idx)[:, None] * log_g[None, :])
    c_dec = jnp.exp(c * log_g)
    return decay, q_dec, k_dec, c_dec


def _rope_tables(pos):
    inv = ROPE_BASE ** (-jnp.arange(RET_HALF, dtype=F32) / RET_HALF)
    ang = pos.astype(F32)[:, None] * inv[None, :]
    return jnp.cos(ang), jnp.sin(ang)


def _lane_bcast(t):
    return jnp.broadcast_to(t.T[:, :, None], (t.shape[1], t.shape[0], LANES))


def _rope_inputs(b, l, pos0, c):
    cos, sin = _rope_tables(pos0 + jnp.arange(l))
    _, _, k_dec, _ = _retention_tables(c)
    kd = jnp.repeat(k_dec, RET_DK, axis=1)
    return jnp.tile(cos, (b, 1)), jnp.tile(sin, (b, 1)), jnp.tile(kd, (ROPE_ROWS // c, 1))


def _group_norm_gate(o, g):
    ms = jnp.mean(o * o, axis=-1, keepdims=True)
    return jax.nn.silu(g) * (o * lax.rsqrt(ms + EPS))


def _ret_prompt_kernel(*refs, n_side, side_steps):
    cdec_ref, q_ref, k_ref, kk_ref, v_ref, g_ref, dec_ref, qd_ref = refs[:8]
    side_in = refs[8:8 + n_side]
    o_ref, s_ref = refs[8 + n_side:10 + n_side]
    side_out = refs[10 + n_side:10 + 2 * n_side]
    if n_side:
        _side_cast(pl.program_id(0) * pl.num_programs(1) + pl.program_id(1), side_steps, side_in, side_out)

    @pl.when(pl.program_id(1) == 0)
    def _():
        s_ref[...] = jnp.zeros_like(s_ref)

    for h in range(RET_HEADS):
        qb = q_ref[:, h * RET_DK:(h + 1) * RET_DK]
        kb = k_ref[:, h * RET_DK:(h + 1) * RET_DK]
        kkb = kk_ref[:, h * RET_DK:(h + 1) * RET_DK]
        vb = v_ref[:, h * RET_DV:(h + 1) * RET_DV]
        att = lax.dot_general(qb, kb, (((1,), (1,)), ((), ())), preferred_element_type=F32) * dec_ref[h]
        s = s_ref[0, h]
        o = (jnp.dot(att.astype(BF16), vb, preferred_element_type=F32)
             + qd_ref[h][:, :1] * jnp.dot(qb, s.astype(BF16), preferred_element_type=F32))
        s_ref[0, h] = s * cdec_ref[h] + lax.dot_general(kkb, vb, (((0,), (0,)), ((), ())),
                                                         preferred_element_type=F32)
        og = _group_norm_gate(o, g_ref[:, h * RET_DV:(h + 1) * RET_DV])
        o_ref[:, h * RET_DV:(h + 1) * RET_DV] = og.astype(o_ref.dtype)


def _retention_prompt(qk, kk, v, rest, b, l, side=()):
    c = RET_CHUNK if l % RET_CHUNK == 0 else l
    nc = l // c
    decay, q_dec, _, c_dec = _retention_tables(c)
    wq = RET_HEADS * RET_DK
    wv = RET_HEADS * RET_DV
    full3 = lambda bi, ci: (0, 0, 0)
    side_steps = _side_steps(b * nc)
    side_specs = _side_specs(side, side_steps, lambda bi, ci: bi * nc + ci)
    outs = pl.pallas_call(
        functools.partial(_ret_prompt_kernel, n_side=len(side), side_steps=side_steps),
        grid=(b, nc),
        in_specs=[
            pl.BlockSpec(memory_space=pltpu.SMEM),
            pl.BlockSpec((c, wq), lambda bi, ci: (bi * nc + ci, 0)),
            pl.BlockSpec((c, wq), lambda bi, ci: (bi * nc + ci, 1)),
            pl.BlockSpec((c, wq), lambda bi, ci: (bi * nc + ci, 0)),
            pl.BlockSpec((c, wv), lambda bi, ci: (bi * nc + ci, 0)),
            pl.BlockSpec((c, wv), lambda bi, ci: (bi * nc + ci, REST_G // wv)),
            pl.BlockSpec((RET_HEADS, c, c), full3),
            pl.BlockSpec((RET_HEADS, c, LANES), full3),
        ] + side_specs,
        out_specs=[
            pl.BlockSpec((c, wv), lambda bi, ci: (bi * nc + ci, 0)),
            pl.BlockSpec((1, RET_HEADS, RET_DK, RET_DV), lambda bi, ci: (bi, 0, 0, 0)),
        ] + side_specs,
        out_shape=[
            jax.ShapeDtypeStruct((b * l, wv), BF16),
            jax.ShapeDtypeStruct((b, RET_HEADS, RET_DK, RET_DV), F32),
        ] + [jax.ShapeDtypeStruct(s.shape, BF16) for s in side],
        compiler_params=_cparams(("arbitrary", "arbitrary")),
        name="retention_prompt",
    )(c_dec, qk, qk, kk, v, rest, decay, _lane_bcast(q_dec), *side)
    return outs[0], outs[1], tuple(outs[2:])


SAMPLE_PAD = 16


def _pad_rows(x, rows):
    return jnp.concatenate([x, jnp.zeros((rows - x.shape[0], x.shape[1]), x.dtype)], axis=0)


def _ret_sample_kernel(cdec_ref, q_ref, k_ref, kk_ref, v_ref, g_ref, dec_ref, qd_ref, s0_ref, o_ref, s_ref,
                       *, nb, l):
    for h in range(RET_HEADS):
        qf = q_ref[:, h * RET_DK:(h + 1) * RET_DK].astype(F32)
        kf = k_ref[:, h * RET_DK:(h + 1) * RET_DK].astype(F32)
        kkf = kk_ref[:, h * RET_DK:(h + 1) * RET_DK].astype(F32)
        vf = v_ref[:, h * RET_DV:(h + 1) * RET_DV].astype(F32)
        for bt in range(nb):
            rs = slice(bt * l, (bt + 1) * l)
            qb = _pad_rows(qf[rs], SAMPLE_PAD).astype(BF16)
            kb = _pad_rows(kf[rs], SAMPLE_PAD).astype(BF16)
            kkb = _pad_rows(kkf[rs], SAMPLE_PAD).astype(BF16)
            vb = _pad_rows(vf[rs], SAMPLE_PAD).astype(BF16)
            att = lax.dot_general(qb, kb, (((1,), (1,)), ((), ())), preferred_element_type=F32) * dec_ref[h]
            s = s0_ref[bt, h]
            o = (jnp.dot(att.astype(BF16), vb, preferred_element_type=F32)
                 + qd_ref[h][:, :1] * jnp.dot(qb, s.astype(BF16), preferred_element_type=F32))
            s_ref[bt, h] = s * cdec_ref[h] + lax.dot_general(kkb, vb, (((0,), (0,)), ((), ())),
                                                              preferred_element_type=F32)
            og = _group_norm_gate(o[:l], g_ref[rs, h * RET_DV:(h + 1) * RET_DV])
            o_ref[rs, h * RET_DV:(h + 1) * RET_DV] = og.astype(o_ref.dtype)


def _retention_sample(qk, kk, v, rest, s0, b, l, nb=2):
    assert l % RET_CHUNK != 0 and l <= SAMPLE_PAD and b % nb == 0, (b, l)
    decay, q_dec, _, c_dec = _retention_tables(l)
    p = SAMPLE_PAD - l
    decay = jnp.pad(decay, ((0, 0), (0, p), (0, p)))
    q_dec = jnp.pad(q_dec, ((0, p), (0, 0)))
    wq = RET_HEADS * RET_DK
    wv = RET_HEADS * RET_DV
    rows = nb * l
    full3 = lambda bi: (0, 0, 0)
    state_spec = pl.BlockSpec((nb, RET_HEADS, RET_DK, RET_DV), lambda bi: (bi, 0, 0, 0))
    return pl.pallas_call(
        functools.partial(_ret_sample_kernel, nb=nb, l=l),
        grid=(b // nb,),
        in_specs=[
            pl.BlockSpec(memory_space=pltpu.SMEM),
            pl.BlockSpec((rows, wq), lambda bi: (bi, 0)),
            pl.BlockSpec((rows, wq), lambda bi: (bi, 1)),
            pl.BlockSpec((rows, wq), lambda bi: (bi, 0)),
            pl.BlockSpec((rows, wv), lambda bi: (bi, 0)),
            pl.BlockSpec((rows, wv), lambda bi: (bi, REST_G // wv)),
            pl.BlockSpec((RET_HEADS, SAMPLE_PAD, SAMPLE_PAD), full3),
            pl.BlockSpec((RET_HEADS, SAMPLE_PAD, LANES), full3),
            state_spec,
        ],
        out_specs=[pl.BlockSpec((rows, wv), lambda bi: (bi, 0)), state_spec],
        out_shape=[
            jax.ShapeDtypeStruct((b * l, wv), BF16),
            jax.ShapeDtypeStruct((b, RET_HEADS, RET_DK, RET_DV), F32),
        ],
        compiler_params=_cparams(("arbitrary",)),
        name="retention_sample",
    )(c_dec, qk, qk, kk, v, rest, decay, _lane_bcast(q_dec), s0)


CONV_PAD = 32
CONV_COLS = LANES


def _conv_strip(block, wts, nblk, store):
    lead = CONV_PAD - CONV_HIST
    taps = [((lead + w) // SUBLANES, (lead + w) % SUBLANES) for w in range(CONV_WIDTH)]
    rowid = lax.broadcasted_iota(I32, (SUBLANES, CONV_COLS), 0)

    def partial_sums(m, shifts):
        ys = {}
        for w, (a, s) in enumerate(taps):
            if s in shifts:
                term = block(m + a) * wts[w]
                ys[s] = term if s not in ys else ys[s] + term
        return ys

    shifted = range(1, SUBLANES)
    prev = partial_sums(0, shifted)
    for m in range(nblk):
        nxt = partial_sums(m + 1, shifted)
        acc = partial_sums(m, (0,))[0]
        for s in shifted:
            acc = acc + pltpu.roll(jnp.where(rowid >= s, prev[s], nxt[s]), SUBLANES - s, axis=0)
        store(m, acc)
        prev = nxt


def _conv_kernel(*refs, tl, nb, nl, has_state, n_side, side_steps):
    ca_ref, cb_ref = refs[:2]
    buf_ref = refs[2] if has_state else None
    n_in = 3 if has_state else 2
    cw_ref, bias_ref, lnw_ref, lnb_ref = refs[n_in:n_in + 4]
    side_in = refs[n_in + 4:n_in + 4 + n_side]
    sw_ref, nbuf_ref = refs[n_in + 4 + n_side:n_in + 6 + n_side]
    side_out = refs[n_in + 6 + n_side:n_in + 6 + 2 * n_side]
    upad_ref, c_ref = refs[-2:]
    if n_side:
        _side_cast(pl.program_id(0) * nl + pl.program_id(1), side_steps, side_in, side_out)
    li = pl.program_id(1)
    lead = CONV_PAD - CONV_HIST
    d = ca_ref.shape[1]
    for bt in range(nb):
        @pl.when(li == 0)
        def _():
            if has_state:
                upad_ref[bt, 0:lead, :] = jnp.zeros((lead, d), F32)
                upad_ref[bt, lead:CONV_PAD, :] = buf_ref[0, :, bt, :]
            else:
                upad_ref[bt, 0:CONV_PAD, :] = jnp.zeros((CONV_PAD, d), F32)

        r0 = bt * tl
        upad_ref[bt, CONV_PAD:CONV_PAD + tl, :] = ca_ref[r0:r0 + tl, :] * jax.nn.sigmoid(cb_ref[r0:r0 + tl, :])
        for cc in range(d // CONV_COLS):
            cs = slice(cc * CONV_COLS, (cc + 1) * CONV_COLS)
            wts = [jnp.broadcast_to(cw_ref[0, w:w + 1, cs], (SUBLANES, CONV_COLS)) for w in range(CONV_WIDTH)]
            blocks = {}

            def block(j, cs=cs, blocks=blocks):
                if j not in blocks:
                    blocks[j] = upad_ref[bt, j * SUBLANES:(j + 1) * SUBLANES, cs]
                return blocks[j]

            def store(m, acc, cs=cs):
                c_ref[m * SUBLANES:(m + 1) * SUBLANES, cs] = acc

            _conv_strip(block, wts, tl // SUBLANES, store)
        cf = c_ref[...] + bias_ref[...]
        mu = jnp.mean(cf, axis=-1, keepdims=True)
        var = jnp.mean(jnp.square(cf - mu), axis=-1, keepdims=True)
        cn = (cf - mu) * lax.rsqrt(var + EPS) * lnw_ref[...] + lnb_ref[...]
        sw_ref[r0:r0 + tl, :] = jax.nn.silu(cn).astype(sw_ref.dtype)

        @pl.when(li == nl - 1)
        def _():
            nbuf_ref[0, bt] = upad_ref[bt, tl + lead:tl + CONV_PAD, :]

        if nl > 1:
            upad_ref[bt, 0:CONV_PAD, :] = upad_ref[bt, tl:tl + CONV_PAD, :]


def _conv_branch(rest, buf_t, conv_w, conv_b, ln_w, ln_b, b, l, tl, nb, side=()):
    d = D_MODEL
    nl = l // tl
    assert nb == 1 or nl == 1
    rows = nb * tl
    const2 = lambda bi, li: (0, 0)
    grid = (b // nb, nl)
    side_steps = _side_steps(grid[0] * grid[1])
    side_specs = _side_specs(side, side_steps, lambda bi, li: bi * nl + li)
    state_spec = pl.BlockSpec((1, nb, CONV_HIST, d), lambda bi, li: (0, bi, 0, 0))
    has_state = buf_t is not None
    state_in = [pl.BlockSpec((1, CONV_HIST, nb, d), lambda bi, li: (0, 0, bi, 0))] if has_state else []
    outs = pl.pallas_call(
        functools.partial(_conv_kernel, tl=tl, nb=nb, nl=nl, has_state=has_state, n_side=len(side),
                          side_steps=side_steps),
        grid=grid,
        in_specs=[
            pl.BlockSpec((rows, d), lambda bi, li: (bi * nl + li, REST_CA // d)),
            pl.BlockSpec((rows, d), lambda bi, li: (bi * nl + li, REST_CB // d)),
        ] + state_in + [
            pl.BlockSpec((1, CONV_WIDTH, d), lambda bi, li: (0, 0, 0)),
            pl.BlockSpec((1, d), const2),
            pl.BlockSpec((1, d), const2),
            pl.BlockSpec((1, d), const2),
        ] + side_specs,
        out_specs=[pl.BlockSpec((rows, d), lambda bi, li: (bi * nl + li, 0)), state_spec] + side_specs,
        out_shape=[
            jax.ShapeDtypeStruct((b * l, d), BF16),
            jax.ShapeDtypeStruct((1, b, CONV_HIST, d), F32),
        ] + [jax.ShapeDtypeStruct(s.shape, BF16) for s in side],
        scratch_shapes=[pltpu.VMEM((nb, CONV_PAD + tl, d), F32), pltpu.VMEM((tl, d), F32)],
        compiler_params=_cparams(("arbitrary", "arbitrary")),
        name="conv_branch",
    )(rest, rest, *([buf_t] if has_state else []), conv_w, conv_b, ln_w, ln_b, *side)
    return outs[0], outs[1], tuple(outs[2:])


def _merge_kernel(og_ref, sw_ref, wr_ref, wc_ref, gr_ref, gc_ref, o_ref, wrb_ref, wcb_ref):
    @pl.when(pl.program_id(1) == 0)
    def _():
        wrb_ref[...] = wr_ref[...].astype(BF16)
        wcb_ref[...] = wc_ref[...].astype(BF16)

    ret_out = jnp.dot(og_ref[...], wrb_ref[...], preferred_element_type=F32)
    conv_out = jnp.dot(sw_ref[...], wcb_ref[...], preferred_element_type=F32)
    merged = jax.nn.sigmoid(gr_ref[...]) * ret_out + jax.nn.sigmoid(gc_ref[...]) * conv_out
    o_ref[...] = merged.astype(o_ref.dtype)


def _merge(og, sw, w_ret_o, w_conv_o, rest, tm=512, tn=512):
    m = og.shape[0]
    n = w_ret_o.shape[1]
    tm = min(tm, m)
    kr = og.shape[1]
    kc = sw.shape[1]
    return pl.pallas_call(
        _merge_kernel,
        grid=(n // tn, m // tm),
        in_specs=[
            pl.BlockSpec((tm, kr), lambda j, i: (i, 0)),
            pl.BlockSpec((tm, kc), lambda j, i: (i, 0)),
            pl.BlockSpec((kr, tn), lambda j, i: (0, j)),
            pl.BlockSpec((kc, tn), lambda j, i: (0, j)),
            pl.BlockSpec((tm, tn), lambda j, i: (i, REST_GRET // tn + j)),
            pl.BlockSpec((tm, tn), lambda j, i: (i, REST_GCONV // tn + j)),
        ],
        out_specs=pl.BlockSpec((tm, tn), lambda j, i: (i, j)),
        out_shape=jax.ShapeDtypeStruct((m, n), BF16),
        scratch_shapes=[pltpu.VMEM((kr, tn), BF16), pltpu.VMEM((kc, tn), BF16)],
        compiler_params=_cparams(("arbitrary", "arbitrary")),
        name="merge",
    )(og, sw, w_ret_o, w_conv_o, rest, rest)


def _outproj_kernel(a_ref, w_ref, x_ref, *rest):
    o_ref, wb_ref = rest[-2], rest[-1]

    @pl.when(pl.program_id(1) == 0)
    def _():
        wb_ref[...] = w_ref[...].astype(BF16)

    o_ref[...] = x_ref[...] + jnp.dot(a_ref[...], wb_ref[...], preferred_element_type=F32)


def _outproj_residual(a, w, x, total_rows, row0, prev=None, tm=512, tn=512):
    m, k = a.shape
    n = w.shape[1]
    tm = min(tm, m)
    assert row0 % tm == 0
    off = row0 // tm
    in_specs = [
        pl.BlockSpec((tm, k), lambda j, i: (i, 0)),
        pl.BlockSpec((k, tn), lambda j, i: (0, j)),
        pl.BlockSpec((tm, tn), lambda j, i: (i, j)),
    ]
    args = [a, w, x]
    aliases = {}
    if prev is not None:
        in_specs.append(pl.BlockSpec(memory_space=pl.ANY))
        args.append(prev)
        aliases = {3: 0}
    return pl.pallas_call(
        _outproj_kernel,
        grid=(n // tn, m // tm),
        in_specs=in_specs,
        out_specs=pl.BlockSpec((tm, tn), lambda j, i: (i + off, j)),
        out_shape=jax.ShapeDtypeStruct((total_rows, n), F32),
        scratch_shapes=[pltpu.VMEM((k, tn), BF16)],
        input_output_aliases=aliases,
        compiler_params=_cparams(("arbitrary", "arbitrary")),
        name="outproj_residual",
    )(*args)


def _router_kernel(x_ref, nw_ref, w_ref, b_ref, h_ref, meta_ref, meta_t_ref, cnt_ref, base_ref):
    tm = x_ref.shape[0]

    @pl.when(pl.program_id(0) == 0)
    def _():
        base_ref[...] = jnp.zeros_like(base_ref)

    x = x_ref[...]
    ms = jnp.mean(x * x, axis=-1, keepdims=True)
    h = x * lax.rsqrt(ms + EPS) * nw_ref[...]
    h_ref[...] = h
    w = w_ref[...]
    h_hi = h.astype(BF16)
    h_lo = (h - h_hi.astype(F32)).astype(BF16)
    w_hi = w.astype(BF16)
    w_lo = (w - w_hi.astype(F32)).astype(BF16)
    logits = (jnp.dot(h_hi, w_lo, preferred_element_type=F32) + jnp.dot(h_lo, w_hi, preferred_element_type=F32)
              + jnp.dot(h_hi, w_hi, preferred_element_type=F32)) + b_ref[...]
    lane = lax.broadcasted_iota(I32, (tm, ROUTER_COLS), 1)

    def first_argmax(vals):
        top = jnp.max(vals, axis=1, keepdims=True)
        return top, jnp.min(jnp.where(vals == top, lane, ROUTER_COLS), axis=1, keepdims=True)

    cmask = lane < N_GROUPS
    lc = jnp.where(cmask, logits, NEG)
    mc, grp = first_argmax(lc)
    p_grp = 1.0 / jnp.sum(jnp.where(cmask, jnp.exp(lc - mc), 0.0), axis=1, keepdims=True)
    lo = ROUTER_OFF + grp * EXPERTS_PER_GROUP
    lf = jnp.where((lane >= lo) & (lane < lo + EXPERTS_PER_GROUP), logits, NEG)
    v1, i1 = first_argmax(lf)
    v2, i2 = first_argmax(jnp.where(lane == i1, NEG, lf))
    e2 = jnp.exp(v2 - v1)
    den = 1.0 + e2
    g1 = p_grp * (1.0 / den)
    g2 = p_grp * (e2 / den)

    oh1 = lane == i1
    oh2 = lane == i2
    both = jnp.where(oh1 | oh2, 1.0, 0.0)
    row = lax.broadcasted_iota(I32, (tm, tm), 0)
    col = lax.broadcasted_iota(I32, (tm, tm), 1)
    earlier = jnp.where(col < row, 1.0, 0.0).astype(BF16)
    before = jnp.dot(earlier, both.astype(BF16), preferred_element_type=F32) + base_ref[...]
    r1 = jnp.sum(jnp.where(oh1, before, 0.0), axis=1, keepdims=True)
    r2 = jnp.sum(jnp.where(oh2, before, 0.0), axis=1, keepdims=True)
    base_ref[...] = base_ref[...] + jnp.sum(both, axis=0, keepdims=True)
    cnt_ref[...] = base_ref[...]

    fields = [(i1 - ROUTER_OFF).astype(F32), (i2 - ROUTER_OFF).astype(F32), g1, g2, r1, r2]
    meta = jnp.zeros((tm, ROUTER_COLS), F32)
    for k, f in enumerate(fields):
        meta = jnp.where(lane == k, f, meta)
    meta_ref[...] = meta
    meta_t_ref[...] = meta.T[:SUBLANES]


def _router(x, norm_w, w_router, b_router, tm=512):
    n, d = x.shape
    tm = min(tm, n)
    const2 = lambda i: (0, 0)
    return pl.pallas_call(
        _router_kernel,
        grid=(n // tm,),
        in_specs=[
            pl.BlockSpec((tm, d), lambda i: (i, 0)),
            pl.BlockSpec((1, d), const2),
            pl.BlockSpec((d, ROUTER_COLS), const2),
            pl.BlockSpec((1, ROUTER_COLS), const2),
        ],
        out_specs=[
            pl.BlockSpec((tm, d), lambda i: (i, 0)),
            pl.BlockSpec((tm, ROUTER_COLS), lambda i: (i, 0)),
            pl.BlockSpec((SUBLANES, tm), lambda i: (0, i)),
            pl.BlockSpec((1, ROUTER_COLS), const2),
        ],
        out_shape=[
            jax.ShapeDtypeStruct((n, d), F32),
            jax.ShapeDtypeStruct((n, ROUTER_COLS), F32),
            jax.ShapeDtypeStruct((SUBLANES, n), F32),
            jax.ShapeDtypeStruct((1, ROUTER_COLS), F32),
        ],
        scratch_shapes=[pltpu.VMEM((1, ROUTER_COLS), F32)],
        compiler_params=_cparams(("arbitrary",)),
        name="router",
    )(x, norm_w.reshape(1, d), w_router, b_router)


def _dispatch_kernel(rows_ref, span_ref, h_ref, xs_hbm, zero_ref, sem, zsem):
    td = h_ref.shape[0]
    i = pl.program_id(0)

    @pl.when(i == 0)
    def _():
        zero_ref[...] = jnp.zeros_like(zero_ref)

        def fill_copy(e):
            last = pl.multiple_of(span_ref[N_EXPERTS + e] - MOE_ROWS, MOE_ROWS)
            return pltpu.make_async_copy(zero_ref, xs_hbm.at[pl.ds(last, MOE_ROWS)], zsem)

        def nonempty(e):
            return span_ref[N_EXPERTS + e] > span_ref[e]

        def start(e, carry):
            @pl.when(nonempty(e))
            def _():
                fill_copy(e).start()
            return carry

        def wait(e, carry):
            @pl.when(nonempty(e))
            def _():
                fill_copy(e).wait()
            return carry

        lax.fori_loop(0, N_EXPERTS, start, 0)
        lax.fori_loop(0, N_EXPERTS, wait, 0)

    n_tok = pl.num_programs(0) * td

    def token(t, carry):
        n = i * td + t
        for k in range(TOP_K_FINE):
            r = rows_ref[k * n_tok + n]
            pltpu.make_async_copy(h_ref.at[pl.ds(t, 1)], xs_hbm.at[pl.ds(r, 1)], sem).start()
        return carry

    lax.fori_loop(0, td, token, 0)
    for _ in range(TOP_K_FINE):
        pltpu.make_async_copy(h_ref, xs_hbm.at[pl.ds(0, td)], sem).wait()


def _dispatch(rows_flat, span, h, n_rows, td=1024):
    n_tok, d = h.shape
    td = min(td, n_tok)
    return pl.pallas_call(
        _dispatch_kernel,
        grid_spec=pltpu.PrefetchScalarGridSpec(
            num_scalar_prefetch=2,
            grid=(n_tok // td,),
            in_specs=[pl.BlockSpec((td, d), lambda i, rows, span: (i, 0))],
            out_specs=pl.BlockSpec(memory_space=pl.ANY),
            scratch_shapes=[pltpu.VMEM((MOE_ROWS, d), F32), pltpu.SemaphoreType.DMA(()),
                            pltpu.SemaphoreType.DMA(())],
        ),
        out_shape=jax.ShapeDtypeStruct((n_rows, d), F32),
        compiler_params=_cparams(("arbitrary",)),
        name="moe_dispatch",
    )(rows_flat, span, h)


def _expert_kernel(be_ref, nh_ref, nu_ref, x_ref, wg_ref, wu_ref, wd_ref, o_ref):
    del be_ref
    nh = nh_ref[pl.program_id(0)]

    def ffn(rows):
        xb = x_ref[0:rows, :].astype(BF16)
        a = jnp.dot(xb, wg_ref[0], preferred_element_type=F32)
        b = jnp.dot(xb, wu_ref[0], preferred_element_type=F32)
        hb = (jax.nn.silu(a) * b).astype(BF16)
        o_ref[0:rows, :] = jnp.dot(hb, wd_ref[0], preferred_element_type=F32)

    for parts in range(1, MOE_PARTS + 1):
        @pl.when(nh == parts)
        def _():
            ffn(parts * MOE_ROWS)
            if parts < MOE_PARTS:
                o_ref[parts * MOE_ROWS:, :] = jnp.zeros((MOE_BLOCK - parts * MOE_ROWS, o_ref.shape[1]), F32)


def _experts(blk_exp, blk_parts, n_used, xs, wg, wu, wd):
    n_rows, d = xs.shape
    n_blocks = n_rows // MOE_BLOCK
    de = wg.shape[2]
    row_map = lambda i, be, nh, nu: (jnp.minimum(i, jnp.maximum(nu[0] - 1, 0)), 0)
    w_map = lambda i, be, nh, nu: (be[i], 0, 0)
    return pl.pallas_call(
        _expert_kernel,
        grid_spec=pltpu.PrefetchScalarGridSpec(
            num_scalar_prefetch=3,
            grid=(n_blocks,),
            in_specs=[
                pl.BlockSpec((MOE_BLOCK, d), row_map),
                pl.BlockSpec((1, d, de), w_map),
                pl.BlockSpec((1, d, de), w_map),
                pl.BlockSpec((1, de, d), w_map),
            ],
            out_specs=pl.BlockSpec((MOE_BLOCK, d), row_map),
        ),
        out_shape=jax.ShapeDtypeStruct((n_rows, d), F32),
        compiler_params=_cparams(("arbitrary",)),
        name="moe_experts",
    )(blk_exp, blk_parts, n_used, xs, wg, wu, wd)


def _combine_kernel(rows_ref, ys_hbm, x_ref, meta_ref, nf_ref, op_ref, os_ref, buf_ref, sem, *, p_tiles):
    tm = x_ref.shape[0]
    i = pl.program_id(0)
    nt = pl.num_programs(0)
    n_tok = nt * tm
    slot = i % 2

    def gather(tile, dst_slot):
        def token(t, carry):
            n = tile * tm + t
            for k in range(TOP_K_FINE):
                r = rows_ref[k * n_tok + n]
                pltpu.make_async_copy(ys_hbm.at[pl.ds(r, 1)], buf_ref.at[dst_slot, k, pl.ds(t, 1)],
                                      sem.at[dst_slot]).start()
            return carry

        lax.fori_loop(0, tm, token, 0)

    @pl.when(i == 0)
    def _():
        gather(0, 0)

    @pl.when(i + 1 < nt)
    def _():
        gather(i + 1, 1 - slot)

    for k in range(TOP_K_FINE):
        pltpu.make_async_copy(ys_hbm.at[pl.ds(0, tm)], buf_ref.at[slot, k], sem.at[slot]).wait()
    meta = meta_ref[...]
    y = x_ref[...] + (buf_ref[slot, 0] * meta[:, 2:3] + buf_ref[slot, 1] * meta[:, 3:4])
    ms = jnp.mean(y * y, axis=-1, keepdims=True)
    out = y * lax.rsqrt(ms + EPS) * nf_ref[...]

    @pl.when(i < p_tiles)
    def _():
        op_ref[...] = out

    @pl.when(i >= p_tiles)
    def _():
        os_ref[...] = out


def _combine(rows_flat, ys, x, meta, norm_f, n_p, tm=256):
    n, d = x.shape
    n_s = n - n_p
    tm = min(tm, n_p, n_s)
    assert n_p % tm == 0 and n_s % tm == 0
    p_tiles = n_p // tm
    return pl.pallas_call(
        functools.partial(_combine_kernel, p_tiles=p_tiles),
        grid_spec=pltpu.PrefetchScalarGridSpec(
            num_scalar_prefetch=1,
            grid=(n // tm,),
            in_specs=[
                pl.BlockSpec(memory_space=pl.ANY),
                pl.BlockSpec((tm, d), lambda i, rows: (i, 0)),
                pl.BlockSpec((tm, ROUTER_COLS), lambda i, rows: (i, 0)),
                pl.BlockSpec((1, d), lambda i, rows: (0, 0)),
            ],
            out_specs=[
                pl.BlockSpec((tm, d), lambda i, rows: (jnp.minimum(i, p_tiles - 1), 0)),
                pl.BlockSpec((tm, d), lambda i, rows: (jnp.maximum(i - p_tiles, 0), 0)),
            ],
            scratch_shapes=[pltpu.VMEM((2, TOP_K_FINE, tm, d), F32), pltpu.SemaphoreType.DMA((2,))],
        ),
        out_shape=[jax.ShapeDtypeStruct((n_p, d), F32), jax.ShapeDtypeStruct((n_s, d), F32)],
        compiler_params=_cparams(("arbitrary",)),
        name="moe_combine",
    )(rows_flat, ys, x, meta, norm_f.reshape(1, d))


def _moe_and_final_norm(x, n_p, norm_ffn, w_coarse, b_coarse, w_fine, b_fine, wg, wu, wd, norm_f):
    n_tok, d = x.shape
    pad = ROUTER_COLS - N_GROUPS - N_EXPERTS
    w_router = jnp.concatenate([w_coarse, w_fine, jnp.zeros((d, pad), F32)], axis=1)
    b_router = jnp.concatenate([b_coarse, b_fine, jnp.zeros((pad,), F32)]).reshape(1, ROUTER_COLS)
    h, meta, meta_t, counts = _router(x, norm_ffn, w_router, b_router)

    n_asg = n_tok * TOP_K_FINE
    n_blocks = -(-n_asg // MOE_BLOCK) + N_EXPERTS
    n_rows = n_blocks * MOE_BLOCK
    cnt = counts[0, ROUTER_OFF:ROUTER_OFF + N_EXPERTS].astype(I32)
    padded = (cnt + MOE_BLOCK - 1) // MOE_BLOCK * MOE_BLOCK
    pend = jnp.cumsum(padded)
    pstart = pend - padded
    cend = pstart + (cnt + MOE_ROWS - 1) // MOE_ROWS * MOE_ROWS
    expert = meta_t[0:TOP_K_FINE].astype(I32)
    rows = meta_t[4:4 + TOP_K_FINE].astype(I32)
    for e in range(N_EXPERTS):
        rows = rows + jnp.where(expert == e, pstart[e], 0)
    rows_flat = rows.reshape(-1)
    n_used = (pend[-1] // MOE_BLOCK).astype(I32)
    blk = jnp.arange(n_blocks, dtype=I32)
    blk_exp = jnp.searchsorted(pend, jnp.minimum(blk, n_used - 1) * MOE_BLOCK, side='right').astype(I32)
    blk_exp = jnp.minimum(blk_exp, N_EXPERTS - 1)
    blk_parts = jnp.clip((cend[blk_exp] - blk * MOE_BLOCK) // MOE_ROWS, 0, MOE_PARTS)
    blk_parts = jnp.where(blk < n_used, blk_parts, 0).astype(I32)

    xs = _dispatch(rows_flat, jnp.concatenate([pstart, cend]).astype(I32), h, n_rows)
    ys = _experts(blk_exp, blk_parts, n_used.reshape(1), xs, wg, wu, wd)
    return _combine(rows_flat, ys, x, meta, norm_f, n_p)


def _mixer(x, s_ret, s_conv, pos0, norm_mix, w_in, w_ret_o, conv_w, conv_b, conv_ln_w, conv_ln_b,
           w_conv_o, w_o, total_rows, row0, prev, side=((), (), ())):
    b, l, d = x.shape
    x2 = x.reshape(b * l, d)
    chunk = RET_CHUNK if l % RET_CHUNK == 0 else l
    h = _rmsnorm_cast(x2, norm_mix)
    qk, kk = _inproj_rope(h, w_in, *_rope_inputs(b, l, pos0, chunk))
    v, _ = _inproj(h, w_in, COL_V, W_V, BF16)
    rest, cast_a = _inproj(h, w_in, COL_REST, W_REST, F32, side[0])
    if s_ret is None:
        og, s_ret_new, cast_b = _retention_prompt(qk, kk, v, rest, b, l, side[1])
        sw, s_conv_new, cast_c = _conv_branch(rest, s_conv, conv_w, conv_b, conv_ln_w, conv_ln_b, b, l,
                                              tl=min(128, l), nb=1, side=side[2])
    else:
        assert not side[1]
        cast_b = ()
        og, s_ret_new = _retention_sample(qk, kk, v, rest, s_ret, b, l)
        sw, s_conv_new, cast_c = _conv_branch(rest, s_conv, conv_w, conv_b, conv_ln_w, conv_ln_b, b, l,
                                              tl=l, nb=min(SUBLANES, b), side=side[2])
    merged = _merge(og, sw, w_ret_o, w_conv_o, rest)
    x1 = _outproj_residual(merged, w_o, x2, total_rows, row0, prev)
    return x1, s_ret_new, s_conv_new, cast_a + cast_b + cast_c


def kernel(x_prompt, x_sample, state_ret, state_conv, norm_mix, w_in, w_ret_o, conv_w, conv_b, conv_ln_w,
           conv_ln_b, w_conv_o, w_o, norm_ffn, w_coarse, b_coarse, w_fine, b_fine, w_gate, w_up, w_down, norm_f):
    depth = w_in.shape[0]
    assert depth == 1, "single-layer trunk"
    bp, lp, d = x_prompt.shape
    bs, ls, _ = x_sample.shape
    n_p = bp * lp
    n_s = bs * ls
    mix_w = (norm_mix[0], w_in[0], w_ret_o[0], conv_w, conv_b, conv_ln_w, conv_ln_b, w_conv_o[0], w_o[0])
    n_exp, _, d_exp = w_gate.shape[1:]
    flat = lambda w: w.reshape(n_exp * w.shape[2], w.shape[3])
    x1, ret_p, conv_p, (wg, wu, wd) = _mixer(
        x_prompt, None, None, 0, *mix_w, n_p + n_s, 0, None,
        side=((flat(w_gate),), (), (flat(w_up), flat(w_down))))
    conv_hist = jnp.transpose(state_conv, (0, 2, 1, 3))
    x1, ret_s, conv_s, _ = _mixer(x_sample, state_ret[0], conv_hist, PAST_LEN, *mix_w, n_p + n_s, n_p, x1)
    yp, ys = _moe_and_final_norm(x1, n_p, norm_ffn[0], w_coarse[0], b_coarse[0], w_fine[0], b_fine[0],
                                 wg.reshape(n_exp, d, d_exp), wu.reshape(n_exp, d, d_exp),
                                 wd.reshape(n_exp, d_exp, d), norm_f)
    return (yp.reshape(bp, lp, d), ys.reshape(bs, ls, d), ret_p[None], conv_p, ret_s[None], conv_s)
```

```python
import functools

import jax
import jax.numpy as jnp
from jax import lax
from jax.experimental import pallas as pl
from jax.experimental.pallas import tpu as pltpu

F32 = jnp.float32
BF16 = jnp.bfloat16
I32 = jnp.int32

D_MODEL = 2048
PAST_LEN = 16384
RET_HEADS = 8
RET_DK = D_MODEL // RET_HEADS
RET_DV = 2 * D_MODEL // RET_HEADS
RET_HALF = RET_DK // 2
RET_CHUNK = 256
ROPE_BASE = 10000.0
CONV_WIDTH = 31
CONV_HIST = CONV_WIDTH - 1
N_GROUPS = 4
EXPERTS_PER_GROUP = 8
N_EXPERTS = N_GROUPS * EXPERTS_PER_GROUP
TOP_K_FINE = 2
D_EXPERT = D_MODEL // 2
EPS = 1e-6

W_QK = 2 * RET_HEADS * RET_DK
W_V = RET_HEADS * RET_DV
COL_QK = 0
COL_V = COL_QK + W_QK
COL_REST = COL_V + W_V
REST_G = 0
REST_CA = REST_G + RET_HEADS * RET_DV
REST_CB = REST_CA + D_MODEL
REST_GRET = REST_CB + D_MODEL
REST_GCONV = REST_GRET + D_MODEL
W_REST = REST_GCONV + D_MODEL

LANES = 128
SUBLANES = 8
VMEM_LIMIT = 56 * 1024 * 1024
MOE_ROWS = 256
MOE_PARTS = 2
MOE_BLOCK = MOE_PARTS * MOE_ROWS
ROUTER_COLS = LANES
ROUTER_OFF = N_GROUPS
NEG = -3.0e38
ISSUE_UNROLL = 8


def _cparams(sem):
    return pltpu.CompilerParams(dimension_semantics=sem, vmem_limit_bytes=VMEM_LIMIT)


def _rmsnorm_kernel(x_ref, w_ref, o_ref):
    x = x_ref[...]
    ms = jnp.mean(x * x, axis=-1, keepdims=True)
    o_ref[...] = (x * lax.rsqrt(ms + EPS) * w_ref[...]).astype(o_ref.dtype)


def _rmsnorm_cast(x, w, tm=512):
    m, d = x.shape
    tm = min(tm, m)
    return pl.pallas_call(
        _rmsnorm_kernel,
        grid=(m // tm,),
        in_specs=[pl.BlockSpec((tm, d), lambda i: (i, 0)), pl.BlockSpec((1, d), lambda i: (0, 0))],
        out_specs=pl.BlockSpec((tm, d), lambda i: (i, 0)),
        out_shape=jax.ShapeDtypeStruct((m, d), BF16),
        compiler_params=_cparams(("arbitrary",)),
        name="rmsnorm_cast",
    )(x, w.reshape(1, d))


def _side_steps(total_steps):
    return 1 << (total_steps.bit_length() - 1)


def _side_specs(arrays, n_steps, linear_step):
    def spec(a):
        rows = a.shape[0] // n_steps
        assert rows * n_steps == a.shape[0] and rows % 16 == 0, (a.shape, n_steps)
        return pl.BlockSpec((rows, a.shape[1]), lambda *g: (jnp.minimum(linear_step(*g), n_steps - 1), 0))
    return [spec(a) for a in arrays]


def _side_cast(step, n_steps, srcs, dsts):
    @pl.when(step < n_steps)
    def _():
        for src, dst in zip(srcs, dsts):
            dst[...] = src[...].astype(dst.dtype)


def _inproj_kernel(*refs, n_side, side_steps):
    a_ref, w_ref = refs[:2]
    side_in = refs[2:2 + n_side]
    o_ref = refs[2 + n_side]
    side_out = refs[3 + n_side:3 + 2 * n_side]
    wb_ref = refs[-1]

    @pl.when(pl.program_id(1) == 0)
    def _():
        wb_ref[...] = w_ref[...].astype(BF16)

    o_ref[...] = jnp.dot(a_ref[...], wb_ref[...], preferred_element_type=F32).astype(o_ref.dtype)
    if n_side:
        _side_cast(pl.program_id(0) * pl.num_programs(1) + pl.program_id(1), side_steps, side_in, side_out)


def _inproj(a, w, col0, ncols, out_dtype, side=(), tm=1024, tn=1024):
    m, k = a.shape
    tm = min(tm, m)
    assert col0 % tn == 0 and ncols % tn == 0
    grid = (ncols // tn, m // tm)
    side_steps = _side_steps(grid[0] * grid[1])
    side_specs = _side_specs(side, side_steps, lambda j, i: j * grid[1] + i)
    outs = pl.pallas_call(
        functools.partial(_inproj_kernel, n_side=len(side), side_steps=side_steps),
        grid=grid,
        in_specs=[pl.BlockSpec((tm, k), lambda j, i: (i, 0)),
                  pl.BlockSpec((k, tn), lambda j, i: (0, col0 // tn + j))] + side_specs,
        out_specs=[pl.BlockSpec((tm, tn), lambda j, i: (i, j))] + side_specs,
        out_shape=[jax.ShapeDtypeStruct((m, ncols), out_dtype)]
                  + [jax.ShapeDtypeStruct(s.shape, BF16) for s in side],
        scratch_shapes=[pltpu.VMEM((k, tn), BF16)],
        compiler_params=_cparams(("arbitrary", "arbitrary")),
        name="inproj",
    )(a, w, *side)
    return outs[0], tuple(outs[1:])


ROPE_ROWS = RET_CHUNK


def _inproj_rope_kernel(a_ref, w_ref, cos_ref, sin_ref, kd_ref, qk_ref, kk_ref, wb_ref, acc_ref, *, nq_tiles):
    j = pl.program_id(0)

    @pl.when(pl.program_id(1) == 0)
    def _():
        wb_ref[...] = w_ref[...].astype(BF16)

    acc_ref[...] = jnp.dot(a_ref[...], wb_ref[...], preferred_element_type=F32)
    tm, tn = acc_ref.shape
    scale = jnp.where(j >= nq_tiles, RET_DK ** -0.5, 1.0).astype(F32)
    for r in range(tm // ROPE_ROWS):
        rs = slice(r * ROPE_ROWS, (r + 1) * ROPE_ROWS)
        cos = cos_ref[rs, :]
        sin = sin_ref[rs, :]
        parts = []
        for hh in range(tn // RET_DK):
            x1 = acc_ref[rs, hh * RET_DK:hh * RET_DK + RET_HALF]
            x2 = acc_ref[rs, hh * RET_DK + RET_HALF:(hh + 1) * RET_DK]
            parts += [x1 * cos - x2 * sin, x1 * sin + x2 * cos]
        rot = jnp.concatenate(parts, axis=1) * scale
        qk_ref[rs, :] = rot.astype(qk_ref.dtype)

        @pl.when(j >= nq_tiles)
        def _():
            kk_ref[rs, :] = (rot * kd_ref[...]).astype(kk_ref.dtype)


def _inproj_rope(a, w, cos_rows, sin_rows, kd_rows, tm=1024, tn=1024):
    m, k = a.shape
    tm = min(tm, m)
    wq = W_QK // 2
    assert tm % ROPE_ROWS == 0 and wq % tn == 0 and tn % RET_DK == 0
    nq = wq // tn
    return pl.pallas_call(
        functools.partial(_inproj_rope_kernel, nq_tiles=nq),
        grid=(2 * nq, m // tm),
        in_specs=[
            pl.BlockSpec((tm, k), lambda j, i: (i, 0)),
            pl.BlockSpec((k, tn), lambda j, i: (0, COL_QK // tn + j)),
            pl.BlockSpec((tm, RET_HALF), lambda j, i: (i, 0)),
            pl.BlockSpec((tm, RET_HALF), lambda j, i: (i, 0)),
            pl.BlockSpec((ROPE_ROWS, tn), lambda j, i: (0, jnp.maximum(j - nq, 0))),
        ],
        out_specs=[
            pl.BlockSpec((tm, tn), lambda j, i: (i, j)),
            pl.BlockSpec((tm, tn), lambda j, i: (jnp.where(j >= nq, i, 0), jnp.maximum(j - nq, 0))),
        ],
        out_shape=[jax.ShapeDtypeStruct((m, W_QK), BF16), jax.ShapeDtypeStruct((m, wq), BF16)],
        scratch_shapes=[pltpu.VMEM((k, tn), BF16), pltpu.VMEM((tm, tn), F32)],
        compiler_params=_cparams(("arbitrary", "arbitrary")),
        name="inproj_rope",
    )(a, w, cos_rows, sin_rows, kd_rows)


def _retention_tables(c):
    h = RET_HEADS
    log_g = jnp.log1p(-jnp.exp2(-5.0 - jnp.arange(h, dtype=F32)))
    idx = jnp.arange(c, dtype=F32)
    diff = idx[:, None] - idx[None, :]
    decay = jnp.where(diff[None] >= 0.0, jnp.exp(jnp.maximum(diff, 0.0)[None] * log_g[:, None, None]), 0.0)
    q_dec = jnp.exp((idx[:, None] + 1.0) * log_g[None, :])
    k_dec = jnp.exp((c - 1.0 - idx)[:, None] * log_g[None, :])
    c_dec = jnp.exp(c * log_g)
    return decay, q_dec, k_dec, c_dec


def _rope_tables(pos):
    inv = ROPE_BASE ** (-jnp.arange(RET_HALF, dtype=F32) / RET_HALF)
    ang = pos.astype(F32)[:, None] * inv[None, :]
    return jnp.cos(ang), jnp.sin(ang)


def _lane_bcast(t):
    return jnp.broadcast_to(t.T[:, :, None], (t.shape[1], t.shape[0], LANES))


def _rope_inputs(b, l, pos0, c):
    cos, sin = _rope_tables(pos0 + jnp.arange(l))
    _, _, k_dec, _ = _retention_tables(c)
    kd = jnp.repeat(k_dec, RET_DK, axis=1)
    return jnp.tile(cos, (b, 1)), jnp.tile(sin, (b, 1)), jnp.tile(kd, (ROPE_ROWS // c, 1))


def _group_norm_gate(o, g):
    ms = jnp.mean(o * o, axis=-1, keepdims=True)
    return jax.nn.silu(g) * (o * lax.rsqrt(ms + EPS))


def _ret_prompt_kernel(*refs, n_side, side_steps):
    cdec_ref, q_ref, k_ref, kk_ref, v_ref, g_ref, dec_ref, qd_ref = refs[:8]
    side_in = refs[8:8 + n_side]
    o_ref, s_ref = refs[8 + n_side:10 + n_side]
    side_out = refs[10 + n_side:10 + 2 * n_side]
    if n_side:
        _side_cast(pl.program_id(0) * pl.num_programs(1) + pl.program_id(1), side_steps, side_in, side_out)

    @pl.when(pl.program_id(1) == 0)
    def _():
        s_ref[...] = jnp.zeros_like(s_ref)

    for h in range(RET_HEADS):
        qb = q_ref[:, h * RET_DK:(h + 1) * RET_DK]
        kb = k_ref[:, h * RET_DK:(h + 1) * RET_DK]
        kkb = kk_ref[:, h * RET_DK:(h + 1) * RET_DK]
        vb = v_ref[:, h * RET_DV:(h + 1) * RET_DV]
        att = lax.dot_general(qb, kb, (((1,), (1,)), ((), ())), preferred_element_type=F32) * dec_ref[h]
        s = s_ref[0, h]
        o = (jnp.dot(att.astype(BF16), vb, preferred_element_type=F32)
             + qd_ref[h][:, :1] * jnp.dot(qb, s.astype(BF16), preferred_element_type=F32))
        s_ref[0, h] = s * cdec_ref[h] + lax.dot_general(kkb, vb, (((0,), (0,)), ((), ())),
                                                         preferred_element_type=F32)
        og = _group_norm_gate(o, g_ref[:, h * RET_DV:(h + 1) * RET_DV])
        o_ref[:, h * RET_DV:(h + 1) * RET_DV] = og.astype(o_ref.dtype)


def _retention_prompt(qk, kk, v, rest, b, l, side=()):
    c = RET_CHUNK if l % RET_CHUNK == 0 else l
    nc = l // c
    decay, q_dec, _, c_dec = _retention_tables(c)
    wq = RET_HEADS * RET_DK
    wv = RET_HEADS * RET_DV
    full3 = lambda bi, ci: (0, 0, 0)
    side_steps = _side_steps(b * nc)
    side_specs = _side_specs(side, side_steps, lambda bi, ci: bi * nc + ci)
    outs = pl.pallas_call(
        functools.partial(_ret_prompt_kernel, n_side=len(side), side_steps=side_steps),
        grid=(b, nc),
        in_specs=[
            pl.BlockSpec(memory_space=pltpu.SMEM),
            pl.BlockSpec((c, wq), lambda bi, ci: (bi * nc + ci, 0)),
            pl.BlockSpec((c, wq), lambda bi, ci: (bi * nc + ci, 1)),
            pl.BlockSpec((c, wq), lambda bi, ci: (bi * nc + ci, 0)),
            pl.BlockSpec((c, wv), lambda bi, ci: (bi * nc + ci, 0)),
            pl.BlockSpec((c, wv), lambda bi, ci: (bi * nc + ci, REST_G // wv)),
            pl.BlockSpec((RET_HEADS, c, c), full3),
            pl.BlockSpec((RET_HEADS, c, LANES), full3),
        ] + side_specs,
        out_specs=[
            pl.BlockSpec((c, wv), lambda bi, ci: (bi * nc + ci, 0)),
            pl.BlockSpec((1, RET_HEADS, RET_DK, RET_DV), lambda bi, ci: (bi, 0, 0, 0)),
        ] + side_specs,
        out_shape=[
            jax.ShapeDtypeStruct((b * l, wv), BF16),
            jax.ShapeDtypeStruct((b, RET_HEADS, RET_DK, RET_DV), F32),
        ] + [jax.ShapeDtypeStruct(s.shape, BF16) for s in side],
        compiler_params=_cparams(("arbitrary", "arbitrary")),
        name="retention_prompt",
    )(c_dec, qk, qk, kk, v, rest, decay, _lane_bcast(q_dec), *side)
    return outs[0], outs[1], tuple(outs[2:])


SAMPLE_PAD = 16


def _pad_rows(x, rows):
    return jnp.concatenate([x, jnp.zeros((rows - x.shape[0], x.shape[1]), x.dtype)], axis=0)


def _ret_sample_kernel(cdec_ref, q_ref, k_ref, kk_ref, v_ref, g_ref, dec_ref, qd_ref, s0_ref, o_ref, s_ref,
                       *, nb, l):
    for h in range(RET_HEADS):
        qf = q_ref[:, h * RET_DK:(h + 1) * RET_DK].astype(F32)
        kf = k_ref[:, h * RET_DK:(h + 1) * RET_DK].astype(F32)
        kkf = kk_ref[:, h * RET_DK:(h + 1) * RET_DK].astype(F32)
        vf = v_ref[:, h * RET_DV:(h + 1) * RET_DV].astype(F32)
        for bt in range(nb):
            rs = slice(bt * l, (bt + 1) * l)
            qb = _pad_rows(qf[rs], SAMPLE_PAD).astype(BF16)
            kb = _pad_rows(kf[rs], SAMPLE_PAD).astype(BF16)
            kkb = _pad_rows(kkf[rs], SAMPLE_PAD).astype(BF16)
            vb = _pad_rows(vf[rs], SAMPLE_PAD).astype(BF16)
            att = lax.dot_general(qb, kb, (((1,), (1,)), ((), ())), preferred_element_type=F32) * dec_ref[h]
            s = s0_ref[bt, h]
            o = (jnp.dot(att.astype(BF16), vb, preferred_element_type=F32)
                 + qd_ref[h][:, :1] * jnp.dot(qb, s.astype(BF16), preferred_element_type=F32))
            s_ref[bt, h] = s * cdec_ref[h] + lax.dot_general(kkb, vb, (((0,), (0,)), ((), ())),
                                                              preferred_element_type=F32)
            og = _group_norm_gate(o[:l], g_ref[rs, h * RET_DV:(h + 1) * RET_DV])
            o_ref[rs, h * RET_DV:(h + 1) * RET_DV] = og.astype(o_ref.dtype)


def _retention_sample(qk, kk, v, rest, s0, b, l, nb=2):
    assert l % RET_CHUNK != 0 and l <= SAMPLE_PAD and b % nb == 0, (b, l)
    decay, q_dec, _, c_dec = _retention_tables(l)
    p = SAMPLE_PAD - l
    decay = jnp.pad(decay, ((0, 0), (0, p), (0, p)))
    q_dec = jnp.pad(q_dec, ((0, p), (0, 0)))
    wq = RET_HEADS * RET_DK
    wv = RET_HEADS * RET_DV
    rows = nb * l
    full3 = lambda bi: (0, 0, 0)
    state_spec = pl.BlockSpec((nb, RET_HEADS, RET_DK, RET_DV), lambda bi: (bi, 0, 0, 0))
    return pl.pallas_call(
        functools.partial(_ret_sample_kernel, nb=nb, l=l),
        grid=(b // nb,),
        in_specs=[
            pl.BlockSpec(memory_space=pltpu.SMEM),
            pl.BlockSpec((rows, wq), lambda bi: (bi, 0)),
            pl.BlockSpec((rows, wq), lambda bi: (bi, 1)),
            pl.BlockSpec((rows, wq), lambda bi: (bi, 0)),
            pl.BlockSpec((rows, wv), lambda bi: (bi, 0)),
            pl.BlockSpec((rows, wv), lambda bi: (bi, REST_G // wv)),
            pl.BlockSpec((RET_HEADS, SAMPLE_PAD, SAMPLE_PAD), full3),
            pl.BlockSpec((RET_HEADS, SAMPLE_PAD, LANES), full3),
            state_spec,
        ],
        out_specs=[pl.BlockSpec((rows, wv), lambda bi: (bi, 0)), state_spec],
        out_shape=[
            jax.ShapeDtypeStruct((b * l, wv), BF16),
            jax.ShapeDtypeStruct((b, RET_HEADS, RET_DK, RET_DV), F32),
        ],
        compiler_params=_cparams(("arbitrary",)),
        name="retention_sample",
    )(c_dec, qk, qk, kk, v, rest, decay, _lane_bcast(q_dec), s0)


CONV_PAD = 32
CONV_COLS = LANES


def _conv_strip(block, wts, nblk, store):
    lead = CONV_PAD - CONV_HIST
    taps = [((lead + w) // SUBLANES, (lead + w) % SUBLANES) for w in range(CONV_WIDTH)]
    rowid = lax.broadcasted_iota(I32, (SUBLANES, CONV_COLS), 0)

    def partial_sums(m, shifts):
        ys = {}
        for w, (a, s) in enumerate(taps):
            if s in shifts:
                term = block(m + a) * wts[w]
                ys[s] = term if s not in ys else ys[s] + term
        return ys

    shifted = range(1, SUBLANES)
    prev = partial_sums(0, shifted)
    for m in range(nblk):
        nxt = partial_sums(m + 1, shifted)
        acc = partial_sums(m, (0,))[0]
        for s in shifted:
            acc = acc + pltpu.roll(jnp.where(rowid >= s, prev[s], nxt[s]), SUBLANES - s, axis=0)
        store(m, acc)
        prev = nxt


def _conv_kernel(*refs, tl, nb, nl, has_state, n_side, side_steps):
    ca_ref, cb_ref = refs[:2]
    buf_ref = refs[2] if has_state else None
    n_in = 3 if has_state else 2
    cw_ref, bias_ref, lnw_ref, lnb_ref = refs[n_in:n_in + 4]
    side_in = refs[n_in + 4:n_in + 4 + n_side]
    sw_ref, nbuf_ref = refs[n_in + 4 + n_side:n_in + 6 + n_side]
    side_out = refs[n_in + 6 + n_side:n_in + 6 + 2 * n_side]
    upad_ref, c_ref = refs[-2:]
    if n_side:
        _side_cast(pl.program_id(0) * nl + pl.program_id(1), side_steps, side_in, side_out)
    li = pl.program_id(1)
    lead = CONV_PAD - CONV_HIST
    d = ca_ref.shape[1]
    for bt in range(nb):
        @pl.when(li == 0)
        def _():
            if has_state:
                upad_ref[bt, 0:lead, :] = jnp.zeros((lead, d), F32)
                upad_ref[bt, lead:CONV_PAD, :] = buf_ref[0, :, bt, :]
            else:
                upad_ref[bt, 0:CONV_PAD, :] = jnp.zeros((CONV_PAD, d), F32)

        r0 = bt * tl
        upad_ref[bt, CONV_PAD:CONV_PAD + tl, :] = ca_ref[r0:r0 + tl, :] * jax.nn.sigmoid(cb_ref[r0:r0 + tl, :])
        for cc in range(d // CONV_COLS):
            cs = slice(cc * CONV_COLS, (cc + 1) * CONV_COLS)
            wts = [jnp.broadcast_to(cw_ref[0, w:w + 1, cs], (SUBLANES, CONV_COLS)) for w in range(CONV_WIDTH)]
            blocks = {}

            def block(j, cs=cs, blocks=blocks):
                if j not in blocks:
                    blocks[j] = upad_ref[bt, j * SUBLANES:(j + 1) * SUBLANES, cs]
                return blocks[j]

            def store(m, acc, cs=cs):
                c_ref[m * SUBLANES:(m + 1) * SUBLANES, cs] = acc

            _conv_strip(block, wts, tl // SUBLANES, store)
        cf = c_ref[...] + bias_ref[...]
        mu = jnp.mean(cf, axis=-1, keepdims=True)
        var = jnp.mean(jnp.square(cf - mu), axis=-1, keepdims=True)
        cn = (cf - mu) * lax.rsqrt(var + EPS) * lnw_ref[...] + lnb_ref[...]
        sw_ref[r0:r0 + tl, :] = jax.nn.silu(cn).astype(sw_ref.dtype)

        @pl.when(li == nl - 1)
        def _():
            nbuf_ref[0, bt] = upad_ref[bt, tl + lead:tl + CONV_PAD, :]

        if nl > 1:
            upad_ref[bt, 0:CONV_PAD, :] = upad_ref[bt, tl:tl + CONV_PAD, :]


def _conv_branch(rest, buf_t, conv_w, conv_b, ln_w, ln_b, b, l, tl, nb, side=()):
    d = D_MODEL
    nl = l // tl
    assert nb == 1 or nl == 1
    rows = nb * tl
    const2 = lambda bi, li: (0, 0)
    grid = (b // nb, nl)
    side_steps = _side_steps(grid[0] * grid[1])
    side_specs = _side_specs(side, side_steps, lambda bi, li: bi * nl + li)
    state_spec = pl.BlockSpec((1, nb, CONV_HIST, d), lambda bi, li: (0, bi, 0, 0))
    has_state = buf_t is not None
    state_in = [pl.BlockSpec((1, CONV_HIST, nb, d), lambda bi, li: (0, 0, bi, 0))] if has_state else []
    outs = pl.pallas_call(
        functools.partial(_conv_kernel, tl=tl, nb=nb, nl=nl, has_state=has_state, n_side=len(side),
                          side_steps=side_steps),
        grid=grid,
        in_specs=[
            pl.BlockSpec((rows, d), lambda bi, li: (bi * nl + li, REST_CA // d)),
            pl.BlockSpec((rows, d), lambda bi, li: (bi * nl + li, REST_CB // d)),
        ] + state_in + [
            pl.BlockSpec((1, CONV_WIDTH, d), lambda bi, li: (0, 0, 0)),
            pl.BlockSpec((1, d), const2),
            pl.BlockSpec((1, d), const2),
            pl.BlockSpec((1, d), const2),
        ] + side_specs,
        out_specs=[pl.BlockSpec((rows, d), lambda bi, li: (bi * nl + li, 0)), state_spec] + side_specs,
        out_shape=[
            jax.ShapeDtypeStruct((b * l, d), BF16),
            jax.ShapeDtypeStruct((1, b, CONV_HIST, d), F32),
        ] + [jax.ShapeDtypeStruct(s.shape, BF16) for s in side],
        scratch_shapes=[pltpu.VMEM((nb, CONV_PAD + tl, d), F32), pltpu.VMEM((tl, d), F32)],
        compiler_params=_cparams(("arbitrary", "arbitrary")),
        name="conv_branch",
    )(rest, rest, *([buf_t] if has_state else []), conv_w, conv_b, ln_w, ln_b, *side)
    return outs[0], outs[1], tuple(outs[2:])


def _merge_kernel(og_ref, sw_ref, wr_ref, wc_ref, gr_ref, gc_ref, o_ref, wrb_ref, wcb_ref):
    @pl.when(pl.program_id(1) == 0)
    def _():
        wrb_ref[...] = wr_ref[...].astype(BF16)
        wcb_ref[...] = wc_ref[...].astype(BF16)

    ret_out = jnp.dot(og_ref[...], wrb_ref[...], preferred_element_type=F32)
    conv_out = jnp.dot(sw_ref[...], wcb_ref[...], preferred_element_type=F32)
    merged = jax.nn.sigmoid(gr_ref[...]) * ret_out + jax.nn.sigmoid(gc_ref[...]) * conv_out
    o_ref[...] = merged.astype(o_ref.dtype)


def _merge(og, sw, w_ret_o, w_conv_o, rest, tm=512, tn=512):
    m = og.shape[0]
    n = w_ret_o.shape[1]
    tm = min(tm, m)
    kr = og.shape[1]
    kc = sw.shape[1]
    return pl.pallas_call(
        _merge_kernel,
        grid=(n // tn, m // tm),
        in_specs=[
            pl.BlockSpec((tm, kr), lambda j, i: (i, 0)),
            pl.BlockSpec((tm, kc), lambda j, i: (i, 0)),
            pl.BlockSpec((kr, tn), lambda j, i: (0, j)),
            pl.BlockSpec((kc, tn), lambda j, i: (0, j)),
            pl.BlockSpec((tm, tn), lambda j, i: (i, REST_GRET // tn + j)),
            pl.BlockSpec((tm, tn), lambda j, i: (i, REST_GCONV // tn + j)),
        ],
        out_specs=pl.BlockSpec((tm, tn), lambda j, i: (i, j)),
        out_shape=jax.ShapeDtypeStruct((m, n), BF16),
        scratch_shapes=[pltpu.VMEM((kr, tn), BF16), pltpu.VMEM((kc, tn), BF16)],
        compiler_params=_cparams(("arbitrary", "arbitrary")),
        name="merge",
    )(og, sw, w_ret_o, w_conv_o, rest, rest)


def _outproj_kernel(a_ref, w_ref, x_ref, *rest):
    o_ref, wb_ref = rest[-2], rest[-1]

    @pl.when(pl.program_id(1) == 0)
    def _():
        wb_ref[...] = w_ref[...].astype(BF16)

    o_ref[...] = x_ref[...] + jnp.dot(a_ref[...], wb_ref[...], preferred_element_type=F32)


def _outproj_residual(a, w, x, total_rows, row0, prev=None, tm=1024, tn=1024):
    m, k = a.shape
    n = w.shape[1]
    tm = min(tm, m)
    assert row0 % tm == 0
    off = row0 // tm
    in_specs = [
        pl.BlockSpec((tm, k), lambda j, i: (i, 0)),
        pl.BlockSpec((k, tn), lambda j, i: (0, j)),
        pl.BlockSpec((tm, tn), lambda j, i: (i, j)),
    ]
    args = [a, w, x]
    aliases = {}
    if prev is not None:
        in_specs.append(pl.BlockSpec(memory_space=pl.ANY))
        args.append(prev)
        aliases = {3: 0}
    return pl.pallas_call(
        _outproj_kernel,
        grid=(n // tn, m // tm),
        in_specs=in_specs,
        out_specs=pl.BlockSpec((tm, tn), lambda j, i: (i + off, j)),
        out_shape=jax.ShapeDtypeStruct((total_rows, n), F32),
        scratch_shapes=[pltpu.VMEM((k, tn), BF16)],
        input_output_aliases=aliases,
        compiler_params=_cparams(("arbitrary", "arbitrary")),
        name="outproj_residual",
    )(*args)


def _router_kernel(x_ref, nw_ref, w_ref, b_ref, h_ref, meta_ref, meta_t_ref, cnt_ref, base_ref):
    tm = x_ref.shape[0]

    @pl.when(pl.program_id(0) == 0)
    def _():
        base_ref[...] = jnp.zeros_like(base_ref)

    x = x_ref[...]
    ms = jnp.mean(x * x, axis=-1, keepdims=True)
    h = x * lax.rsqrt(ms + EPS) * nw_ref[...]
    h_ref[...] = h
    w = w_ref[...]
    h_hi = h.astype(BF16)
    h_lo = (h - h_hi.astype(F32)).astype(BF16)
    w_hi = w.astype(BF16)
    w_lo = (w - w_hi.astype(F32)).astype(BF16)
    logits = (jnp.dot(h_hi, w_lo, preferred_element_type=F32) + jnp.dot(h_lo, w_hi, preferred_element_type=F32)
              + jnp.dot(h_hi, w_hi, preferred_element_type=F32)) + b_ref[...]
    lane = lax.broadcasted_iota(I32, (tm, ROUTER_COLS), 1)

    def first_argmax(vals):
        top = jnp.max(vals, axis=1, keepdims=True)
        return top, jnp.min(jnp.where(vals == top, lane, ROUTER_COLS), axis=1, keepdims=True)

    cmask = lane < N_GROUPS
    lc = jnp.where(cmask, logits, NEG)
    mc, grp = first_argmax(lc)
    p_grp = 1.0 / jnp.sum(jnp.where(cmask, jnp.exp(lc - mc), 0.0), axis=1, keepdims=True)
    lo = ROUTER_OFF + grp * EXPERTS_PER_GROUP
    lf = jnp.where((lane >= lo) & (lane < lo + EXPERTS_PER_GROUP), logits, NEG)
    v1, i1 = first_argmax(lf)
    v2, i2 = first_argmax(jnp.where(lane == i1, NEG, lf))
    e2 = jnp.exp(v2 - v1)
    den = 1.0 + e2
    g1 = p_grp * (1.0 / den)
    g2 = p_grp * (e2 / den)

    oh1 = lane == i1
    oh2 = lane == i2
    both = jnp.where(oh1 | oh2, 1.0, 0.0)
    row = lax.broadcasted_iota(I32, (tm, tm), 0)
    col = lax.broadcasted_iota(I32, (tm, tm), 1)
    earlier = jnp.where(col < row, 1.0, 0.0).astype(BF16)
    before = jnp.dot(earlier, both.astype(BF16), preferred_element_type=F32) + base_ref[...]
    r1 = jnp.sum(jnp.where(oh1, before, 0.0), axis=1, keepdims=True)
    r2 = jnp.sum(jnp.where(oh2, before, 0.0), axis=1, keepdims=True)
    base_ref[...] = base_ref[...] + jnp.sum(both, axis=0, keepdims=True)
    cnt_ref[...] = base_ref[...]

    fields = [(i1 - ROUTER_OFF).astype(F32), (i2 - ROUTER_OFF).astype(F32), g1, g2, r1, r2]
    meta = jnp.zeros((tm, ROUTER_COLS), F32)
    for k, f in enumerate(fields):
        meta = jnp.where(lane == k, f, meta)
    meta_ref[...] = meta
    meta_t_ref[...] = meta.T[:SUBLANES]


def _router(x, norm_w, w_router, b_router, tm=512):
    n, d = x.shape
    tm = min(tm, n)
    const2 = lambda i: (0, 0)
    return pl.pallas_call(
        _router_kernel,
        grid=(n // tm,),
        in_specs=[
            pl.BlockSpec((tm, d), lambda i: (i, 0)),
            pl.BlockSpec((1, d), const2),
            pl.BlockSpec((d, ROUTER_COLS), const2),
            pl.BlockSpec((1, ROUTER_COLS), const2),
        ],
        out_specs=[
            pl.BlockSpec((tm, d), lambda i: (i, 0)),
            pl.BlockSpec((tm, ROUTER_COLS), lambda i: (i, 0)),
            pl.BlockSpec((SUBLANES, tm), lambda i: (0, i)),
            pl.BlockSpec((1, ROUTER_COLS), const2),
        ],
        out_shape=[
            jax.ShapeDtypeStruct((n, d), F32),
            jax.ShapeDtypeStruct((n, ROUTER_COLS), F32),
            jax.ShapeDtypeStruct((SUBLANES, n), F32),
            jax.ShapeDtypeStruct((1, ROUTER_COLS), F32),
        ],
        scratch_shapes=[pltpu.VMEM((1, ROUTER_COLS), F32)],
        compiler_params=_cparams(("arbitrary",)),
        name="router",
    )(x, norm_w.reshape(1, d), w_router, b_router)


def _dispatch_kernel(rows_ref, span_ref, h_ref, xs_hbm, zero_ref, sem, zsem):
    td = h_ref.shape[0]
    i = pl.program_id(0)

    @pl.when(i == 0)
    def _():
        zero_ref[...] = jnp.zeros_like(zero_ref)

        def fill_copy(e):
            last = pl.multiple_of(span_ref[N_EXPERTS + e] - MOE_ROWS, MOE_ROWS)
            return pltpu.make_async_copy(zero_ref, xs_hbm.at[pl.ds(last, MOE_ROWS)], zsem)

        def nonempty(e):
            return span_ref[N_EXPERTS + e] > span_ref[e]

        def start(e, carry):
            @pl.when(nonempty(e))
            def _():
                fill_copy(e).start()
            return carry

        def wait(e, carry):
            @pl.when(nonempty(e))
            def _():
                fill_copy(e).wait()
            return carry

        lax.fori_loop(0, N_EXPERTS, start, 0)
        lax.fori_loop(0, N_EXPERTS, wait, 0)

    n_tok = pl.num_programs(0) * td

    def tokens(g, carry):
        t0 = pl.multiple_of(g * SUBLANES, SUBLANES)
        for u in range(SUBLANES):
            for k in range(TOP_K_FINE):
                r = rows_ref[k * n_tok + i * td + t0 + u]
                pltpu.make_async_copy(h_ref.at[pl.ds(t0 + u, 1)], xs_hbm.at[pl.ds(r, 1)], sem).start()
        return carry

    lax.fori_loop(0, td // SUBLANES, tokens, 0)
    for _ in range(TOP_K_FINE):
        pltpu.make_async_copy(h_ref, xs_hbm.at[pl.ds(0, td)], sem).wait()


def _dispatch(rows_flat, span, h, n_rows, td=1024):
    n_tok, d = h.shape
    td = min(td, n_tok)
    return pl.pallas_call(
        _dispatch_kernel,
        grid_spec=pltpu.PrefetchScalarGridSpec(
            num_scalar_prefetch=2,
            grid=(n_tok // td,),
            in_specs=[pl.BlockSpec((td, d), lambda i, rows, span: (i, 0))],
            out_specs=pl.BlockSpec(memory_space=pl.ANY),
            scratch_shapes=[pltpu.VMEM((MOE_ROWS, d), F32), pltpu.SemaphoreType.DMA(()),
                            pltpu.SemaphoreType.DMA(())],
        ),
        out_shape=jax.ShapeDtypeStruct((n_rows, d), F32),
        compiler_params=_cparams(("arbitrary",)),
        name="moe_dispatch",
    )(rows_flat, span, h)


def _expert_kernel(be_ref, nh_ref, nu_ref, x_ref, wg_ref, wu_ref, wd_ref, o_ref):
    del be_ref
    nh = nh_ref[pl.program_id(0)]

    def ffn(rows):
        xb = x_ref[0:rows, :].astype(BF16)
        a = jnp.dot(xb, wg_ref[0], preferred_element_type=F32)
        b = jnp.dot(xb, wu_ref[0], preferred_element_type=F32)
        hb = (jax.nn.silu(a) * b).astype(BF16)
        o_ref[0:rows, :] = jnp.dot(hb, wd_ref[0], preferred_element_type=F32)

    for parts in range(1, MOE_PARTS + 1):
        @pl.when(nh == parts)
        def _():
            ffn(parts * MOE_ROWS)
            if parts < MOE_PARTS:
                o_ref[parts * MOE_ROWS:, :] = jnp.zeros((MOE_BLOCK - parts * MOE_ROWS, o_ref.shape[1]), F32)


def _experts(blk_exp, blk_parts, n_used, xs, wg, wu, wd):
    n_rows, d = xs.shape
    n_blocks = n_rows // MOE_BLOCK
    de = wg.shape[2]
    row_map = lambda i, be, nh, nu: (jnp.minimum(i, jnp.maximum(nu[0] - 1, 0)), 0)
    w_map = lambda i, be, nh, nu: (be[i], 0, 0)
    return pl.pallas_call(
        _expert_kernel,
        grid_spec=pltpu.PrefetchScalarGridSpec(
            num_scalar_prefetch=3,
            grid=(n_blocks,),
            in_specs=[
                pl.BlockSpec((MOE_BLOCK, d), row_map),
                pl.BlockSpec((1, d, de), w_map),
                pl.BlockSpec((1, d, de), w_map),
                pl.BlockSpec((1, de, d), w_map),
            ],
            out_specs=pl.BlockSpec((MOE_BLOCK, d), row_map),
        ),
        out_shape=jax.ShapeDtypeStruct((n_rows, d), F32),
        compiler_params=_cparams(("arbitrary",)),
        name="moe_experts",
    )(blk_exp, blk_parts, n_used, xs, wg, wu, wd)


def _combine_kernel(rows_ref, ys_hbm, x_ref, meta_ref, nf_ref, op_ref, os_ref, buf_ref, sem, *, p_tiles):
    tm = x_ref.shape[0]
    i = pl.program_id(0)
    nt = pl.num_programs(0)
    n_tok = nt * tm
    slot = i % 2

    def gather(tile, dst_slot):
        def tokens(g, carry):
            t0 = pl.multiple_of(g * SUBLANES, SUBLANES)
            for u in range(SUBLANES):
                for k in range(TOP_K_FINE):
                    r = rows_ref[k * n_tok + tile * tm + t0 + u]
                    pltpu.make_async_copy(ys_hbm.at[pl.ds(r, 1)], buf_ref.at[dst_slot, k, pl.ds(t0 + u, 1)],
                                          sem.at[dst_slot]).start()
            return carry

        lax.fori_loop(0, tm // SUBLANES, tokens, 0)

    @pl.when(i == 0)
    def _():
        gather(0, 0)

    @pl.when(i + 1 < nt)
    def _():
        gather(i + 1, 1 - slot)

    for k in range(TOP_K_FINE):
        pltpu.make_async_copy(ys_hbm.at[pl.ds(0, tm)], buf_ref.at[slot, k], sem.at[slot]).wait()
    meta = meta_ref[...]
    y = x_ref[...] + (buf_ref[slot, 0] * meta[:, 2:3] + buf_ref[slot, 1] * meta[:, 3:4])
    ms = jnp.mean(y * y, axis=-1, keepdims=True)
    out = y * lax.rsqrt(ms + EPS) * nf_ref[...]

    @pl.when(i < p_tiles)
    def _():
        op_ref[...] = out

    @pl.when(i >= p_tiles)
    def _():
        os_ref[...] = out


def _combine(rows_flat, ys, x, meta, norm_f, n_p, tm=256):
    n, d = x.shape
    n_s = n - n_p
    tm = min(tm, n_p, n_s)
    assert n_p % tm == 0 and n_s % tm == 0
    p_tiles = n_p // tm
    return pl.pallas_call(
        functools.partial(_combine_kernel, p_tiles=p_tiles),
        grid_spec=pltpu.PrefetchScalarGridSpec(
            num_scalar_prefetch=1,
            grid=(n // tm,),
            in_specs=[
                pl.BlockSpec(memory_space=pl.ANY),
                pl.BlockSpec((tm, d), lambda i, rows: (i, 0)),
                pl.BlockSpec((tm, ROUTER_COLS), lambda i, rows: (i, 0)),
                pl.BlockSpec((1, d), lambda i, rows: (0, 0)),
            ],
            out_specs=[
                pl.BlockSpec((tm, d), lambda i, rows: (jnp.minimum(i, p_tiles - 1), 0)),
                pl.BlockSpec((tm, d), lambda i, rows: (jnp.maximum(i - p_tiles, 0), 0)),
            ],
            scratch_shapes=[pltpu.VMEM((2, TOP_K_FINE, tm, d), F32), pltpu.SemaphoreType.DMA((2,))],
        ),
        out_shape=[jax.ShapeDtypeStruct((n_p, d), F32), jax.ShapeDtypeStruct((n_s, d), F32)],
        compiler_params=_cparams(("arbitrary",)),
        name="moe_combine",
    )(rows_flat, ys, x, meta, norm_f.reshape(1, d))


def _moe_and_final_norm(x, n_p, norm_ffn, w_coarse, b_coarse, w_fine, b_fine, wg, wu, wd, norm_f):
    n_tok, d = x.shape
    pad = ROUTER_COLS - N_GROUPS - N_EXPERTS
    w_router = jnp.concatenate([w_coarse, w_fine, jnp.zeros((d, pad), F32)], axis=1)
    b_router = jnp.concatenate([b_coarse, b_fine, jnp.zeros((pad,), F32)]).reshape(1, ROUTER_COLS)
    h, meta, meta_t, counts = _router(x, norm_ffn, w_router, b_router)

    n_asg = n_tok * TOP_K_FINE
    n_blocks = -(-n_asg // MOE_BLOCK) + N_EXPERTS
    n_rows = n_blocks * MOE_BLOCK
    cnt = counts[0, ROUTER_OFF:ROUTER_OFF + N_EXPERTS].astype(I32)
    padded = (cnt + MOE_BLOCK - 1) // MOE_BLOCK * MOE_BLOCK
    pend = jnp.cumsum(padded)
    pstart = pend - padded
    cend = pstart + (cnt + MOE_ROWS - 1) // MOE_ROWS * MOE_ROWS
    expert = meta_t[0:TOP_K_FINE].astype(I32)
    rows = meta_t[4:4 + TOP_K_FINE].astype(I32)
    for e in range(N_EXPERTS):
        rows = rows + jnp.where(expert == e, pstart[e], 0)
    rows_flat = rows.reshape(-1)
    n_used = (pend[-1] // MOE_BLOCK).astype(I32)
    blk = jnp.arange(n_blocks, dtype=I32)
    blk_exp = jnp.searchsorted(pend, jnp.minimum(blk, n_used - 1) * MOE_BLOCK, side='right').astype(I32)
    blk_exp = jnp.minimum(blk_exp, N_EXPERTS - 1)
    blk_parts = jnp.clip((cend[blk_exp] - blk * MOE_BLOCK) // MOE_ROWS, 0, MOE_PARTS)
    blk_parts = jnp.where(blk < n_used, blk_parts, 0).astype(I32)

    xs = _dispatch(rows_flat, jnp.concatenate([pstart, cend]).astype(I32), h, n_rows)
    ys = _experts(blk_exp, blk_parts, n_used.reshape(1), xs, wg, wu, wd)
    return _combine(rows_flat, ys, x, meta, norm_f, n_p)


def _mixer(x, s_ret, s_conv, pos0, norm_mix, w_in, w_ret_o, conv_w, conv_b, conv_ln_w, conv_ln_b,
           w_conv_o, w_o, total_rows, row0, prev, side=((), (), ())):
    b, l, d = x.shape
    x2 = x.reshape(b * l, d)
    chunk = RET_CHUNK if l % RET_CHUNK == 0 else l
    h = _rmsnorm_cast(x2, norm_mix)
    qk, kk = _inproj_rope(h, w_in, *_rope_inputs(b, l, pos0, chunk))
    v, _ = _inproj(h, w_in, COL_V, W_V, BF16)
    rest, cast_a = _inproj(h, w_in, COL_REST, W_REST, F32, side[0])
    if s_ret is None:
        og, s_ret_new, cast_b = _retention_prompt(qk, kk, v, rest, b, l, side[1])
        sw, s_conv_new, cast_c = _conv_branch(rest, s_conv, conv_w, conv_b, conv_ln_w, conv_ln_b, b, l,
                                              tl=min(128, l), nb=1, side=side[2])
    else:
        assert not side[1]
        cast_b = ()
        og, s_ret_new = _retention_sample(qk, kk, v, rest, s_ret, b, l)
        sw, s_conv_new, cast_c = _conv_branch(rest, s_conv, conv_w, conv_b, conv_ln_w, conv_ln_b, b, l,
                                              tl=l, nb=min(SUBLANES, b), side=side[2])
    merged = _merge(og, sw, w_ret_o, w_conv_o, rest)
    x1 = _outproj_residual(merged, w_o, x2, total_rows, row0, prev)
    return x1, s_ret_new, s_conv_new, cast_a + cast_b + cast_c


def kernel(x_prompt, x_sample, state_ret, state_conv, norm_mix, w_in, w_ret_o, conv_w, conv_b, conv_ln_w,
           conv_ln_b, w_conv_o, w_o, norm_ffn, w_coarse, b_coarse, w_fine, b_fine, w_gate, w_up, w_down, norm_f):
    depth = w_in.shape[0]
    assert depth == 1, "single-layer trunk"
    bp, lp, d = x_prompt.shape
    bs, ls, _ = x_sample.shape
    n_p = bp * lp
    n_s = bs * ls
    mix_w = (norm_mix[0], w_in[0], w_ret_o[0], conv_w, conv_b, conv_ln_w, conv_ln_b, w_conv_o[0], w_o[0])
    n_exp, _, d_exp = w_gate.shape[1:]
    flat = lambda w: w.reshape(n_exp * w.shape[2], w.shape[3])
    x1, ret_p, conv_p, (wg, wu, wd) = _mixer(
        x_prompt, None, None, 0, *mix_w, n_p + n_s, 0, None,
        side=((flat(w_gate),), (), (flat(w_up), flat(w_down))))
    conv_hist = jnp.transpose(state_conv, (0, 2, 1, 3))
    x1, ret_s, conv_s, _ = _mixer(x_sample, state_ret[0], conv_hist, PAST_LEN, *mix_w, n_p + n_s, n_p, x1)
    yp, ys = _moe_and_final_norm(x1, n_p, norm_ffn[0], w_coarse[0], b_coarse[0], w_fine[0], b_fine[0],
                                 wg.reshape(n_exp, d, d_exp), wu.reshape(n_exp, d, d_exp),
                                 wd.reshape(n_exp, d_exp, d), norm_f)
    return (yp.reshape(bp, lp, d), ys.reshape(bs, ls, d), ret_p[None], conv_p, ret_s[None], conv_s)
```

```python
import functools

import jax
import jax.numpy as jnp
from jax import lax
from jax.experimental import pallas as pl
from jax.experimental.pallas import tpu as pltpu

F32 = jnp.float32
BF16 = jnp.bfloat16
I32 = jnp.int32

D_MODEL = 2048
PAST_LEN = 16384
RET_HEADS = 8
RET_DK = D_MODEL // RET_HEADS
RET_DV = 2 * D_MODEL // RET_HEADS
RET_HALF = RET_DK // 2
RET_CHUNK = 256
ROPE_BASE = 10000.0
CONV_WIDTH = 31
CONV_HIST = CONV_WIDTH - 1
N_GROUPS = 4
EXPERTS_PER_GROUP = 8
N_EXPERTS = N_GROUPS * EXPERTS_PER_GROUP
TOP_K_FINE = 2
D_EXPERT = D_MODEL // 2
EPS = 1e-6

W_QK = 2 * RET_HEADS * RET_DK
W_V = RET_HEADS * RET_DV
COL_QK = 0
COL_V = COL_QK + W_QK
COL_REST = COL_V + W_V
REST_G = 0
REST_CA = REST_G + RET_HEADS * RET_DV
REST_CB = REST_CA + D_MODEL
REST_GRET = REST_CB + D_MODEL
REST_GCONV = REST_GRET + D_MODEL
W_REST = REST_GCONV + D_MODEL

LANES = 128
SUBLANES = 8
VMEM_LIMIT = 56 * 1024 * 1024
MOE_ROWS = 128
MOE_PARTS = 4
MOE_BLOCK = MOE_PARTS * MOE_ROWS
ROUTER_COLS = LANES
ROUTER_OFF = N_GROUPS
NEG = -3.0e38
ISSUE_UNROLL = 8


def _cparams(sem):
    return pltpu.CompilerParams(dimension_semantics=sem, vmem_limit_bytes=VMEM_LIMIT)


def _rmsnorm_kernel(x_ref, w_ref, o_ref):
    x = x_ref[...]
    ms = jnp.mean(x * x, axis=-1, keepdims=True)
    o_ref[...] = (x * lax.rsqrt(ms + EPS) * w_ref[...]).astype(o_ref.dtype)


def _rmsnorm_cast(x, w, tm=512):
    m, d = x.shape
    tm = min(tm, m)
    return pl.pallas_call(
        _rmsnorm_kernel,
        grid=(m // tm,),
        in_specs=[pl.BlockSpec((tm, d), lambda i: (i, 0)), pl.BlockSpec((1, d), lambda i: (0, 0))],
        out_specs=pl.BlockSpec((tm, d), lambda i: (i, 0)),
        out_shape=jax.ShapeDtypeStruct((m, d), BF16),
        compiler_params=_cparams(("arbitrary",)),
        name="rmsnorm_cast",
    )(x, w.reshape(1, d))


def _side_steps(total_steps):
    return 1 << (total_steps.bit_length() - 1)


def _side_specs(arrays, n_steps, linear_step):
    def spec(a):
        rows = a.shape[0] // n_steps
        assert rows * n_steps == a.shape[0] and rows % 16 == 0, (a.shape, n_steps)
        return pl.BlockSpec((rows, a.shape[1]), lambda *g: (jnp.minimum(linear_step(*g), n_steps - 1), 0))
    return [spec(a) for a in arrays]


def _side_cast(step, n_steps, srcs, dsts):
    @pl.when(step < n_steps)
    def _():
        for src, dst in zip(srcs, dsts):
            dst[...] = src[...].astype(dst.dtype)


def _inproj_kernel(*refs, n_side, side_steps):
    a_ref, w_ref = refs[:2]
    side_in = refs[2:2 + n_side]
    o_ref = refs[2 + n_side]
    side_out = refs[3 + n_side:3 + 2 * n_side]
    wb_ref = refs[-1]

    @pl.when(pl.program_id(1) == 0)
    def _():
        wb_ref[...] = w_ref[...].astype(BF16)

    o_ref[...] = jnp.dot(a_ref[...], wb_ref[...], preferred_element_type=F32).astype(o_ref.dtype)
    if n_side:
        _side_cast(pl.program_id(0) * pl.num_programs(1) + pl.program_id(1), side_steps, side_in, side_out)


def _inproj(a, w, col0, ncols, out_dtype, side=(), tm=1024, tn=1024):
    m, k = a.shape
    tm = min(tm, m)
    assert col0 % tn == 0 and ncols % tn == 0
    grid = (ncols // tn, m // tm)
    side_steps = _side_steps(grid[0] * grid[1])
    side_specs = _side_specs(side, side_steps, lambda j, i: j * grid[1] + i)
    outs = pl.pallas_call(
        functools.partial(_inproj_kernel, n_side=len(side), side_steps=side_steps),
        grid=grid,
        in_specs=[pl.BlockSpec((tm, k), lambda j, i: (i, 0)),
                  pl.BlockSpec((k, tn), lambda j, i: (0, col0 // tn + j))] + side_specs,
        out_specs=[pl.BlockSpec((tm, tn), lambda j, i: (i, j))] + side_specs,
        out_shape=[jax.ShapeDtypeStruct((m, ncols), out_dtype)]
                  + [jax.ShapeDtypeStruct(s.shape, BF16) for s in side],
        scratch_shapes=[pltpu.VMEM((k, tn), BF16)],
        compiler_params=_cparams(("arbitrary", "arbitrary")),
        name="inproj",
    )(a, w, *side)
    return outs[0], tuple(outs[1:])


ROPE_ROWS = RET_CHUNK


def _inproj_rope_kernel(*refs, nq_tiles, n_side, side_steps):
    a_ref, w_ref, cos_ref, sin_ref, kd_ref = refs[:5]
    side_in = refs[5:5 + n_side]
    qk_ref, kk_ref = refs[5 + n_side:7 + n_side]
    side_out = refs[7 + n_side:7 + 2 * n_side]
    wb_ref, acc_ref = refs[-2:]
    j = pl.program_id(0)
    if n_side:
        _side_cast(j * pl.num_programs(1) + pl.program_id(1), side_steps, side_in, side_out)

    @pl.when(pl.program_id(1) == 0)
    def _():
        wb_ref[...] = w_ref[...].astype(BF16)

    acc_ref[...] = jnp.dot(a_ref[...], wb_ref[...], preferred_element_type=F32)
    tm, tn = acc_ref.shape
    scale = jnp.where(j >= nq_tiles, RET_DK ** -0.5, 1.0).astype(F32)
    for r in range(tm // ROPE_ROWS):
        rs = slice(r * ROPE_ROWS, (r + 1) * ROPE_ROWS)
        cos = cos_ref[rs, :]
        sin = sin_ref[rs, :]
        parts = []
        for hh in range(tn // RET_DK):
            x1 = acc_ref[rs, hh * RET_DK:hh * RET_DK + RET_HALF]
            x2 = acc_ref[rs, hh * RET_DK + RET_HALF:(hh + 1) * RET_DK]
            parts += [x1 * cos - x2 * sin, x1 * sin + x2 * cos]
        rot = jnp.concatenate(parts, axis=1) * scale
        qk_ref[rs, :] = rot.astype(qk_ref.dtype)

        @pl.when(j >= nq_tiles)
        def _():
            kk_ref[rs, :] = (rot * kd_ref[...]).astype(kk_ref.dtype)


def _inproj_rope(a, w, cos_rows, sin_rows, kd_rows, side=(), tm=1024, tn=1024):
    m, k = a.shape
    tm = min(tm, m)
    wq = W_QK // 2
    assert tm % ROPE_ROWS == 0 and wq % tn == 0 and tn % RET_DK == 0
    nq = wq // tn
    grid = (2 * nq, m // tm)
    side_steps = _side_steps(grid[0] * grid[1])
    side_specs = _side_specs(side, side_steps, lambda j, i: j * grid[1] + i)
    outs = pl.pallas_call(
        functools.partial(_inproj_rope_kernel, nq_tiles=nq, n_side=len(side), side_steps=side_steps),
        grid=grid,
        in_specs=[
            pl.BlockSpec((tm, k), lambda j, i: (i, 0)),
            pl.BlockSpec((k, tn), lambda j, i: (0, COL_QK // tn + j)),
            pl.BlockSpec((tm, RET_HALF), lambda j, i: (i, 0)),
            pl.BlockSpec((tm, RET_HALF), lambda j, i: (i, 0)),
            pl.BlockSpec((ROPE_ROWS, tn), lambda j, i: (0, jnp.maximum(j - nq, 0))),
        ] + side_specs,
        out_specs=[
            pl.BlockSpec((tm, tn), lambda j, i: (i, j)),
            pl.BlockSpec((tm, tn), lambda j, i: (jnp.where(j >= nq, i, 0), jnp.maximum(j - nq, 0))),
        ] + side_specs,
        out_shape=[jax.ShapeDtypeStruct((m, W_QK), BF16), jax.ShapeDtypeStruct((m, wq), BF16)]
                  + [jax.ShapeDtypeStruct(s.shape, BF16) for s in side],
        scratch_shapes=[pltpu.VMEM((k, tn), BF16), pltpu.VMEM((tm, tn), F32)],
        compiler_params=_cparams(("arbitrary", "arbitrary")),
        name="inproj_rope",
    )(a, w, cos_rows, sin_rows, kd_rows, *side)
    return outs[0], outs[1], tuple(outs[2:])


def _retention_tables(c):
    h = RET_HEADS
    log_g = jnp.log1p(-jnp.exp2(-5.0 - jnp.arange(h, dtype=F32)))
    idx = jnp.arange(c, dtype=F32)
    diff = idx[:, None] - idx[None, :]
    decay = jnp.where(diff[None] >= 0.0, jnp.exp(jnp.maximum(diff, 0.0)[None] * log_g[:, None, None]), 0.0)
    q_dec = jnp.exp((idx[:, None] + 1.0) * log_g[None, :])
    k_dec = jnp.exp((c - 1.0 - idx)[:, None] * log_g[None, :])
    c_dec = jnp.exp(c * log_g)
    return decay, q_dec, k_dec, c_dec


def _rope_tables(pos):
    inv = ROPE_BASE ** (-jnp.arange(RET_HALF, dtype=F32) / RET_HALF)
    ang = pos.astype(F32)[:, None] * inv[None, :]
    return jnp.cos(ang), jnp.sin(ang)


def _lane_bcast(t):
    return jnp.broadcast_to(t.T[:, :, None], (t.shape[1], t.shape[0], LANES))


def _rope_inputs(b, l, pos0, c):
    cos, sin = _rope_tables(pos0 + jnp.arange(l))
    _, _, k_dec, _ = _retention_tables(c)
    kd = jnp.repeat(k_dec, RET_DK, axis=1)
    return jnp.tile(cos, (b, 1)), jnp.tile(sin, (b, 1)), jnp.tile(kd, (ROPE_ROWS // c, 1))


def _group_norm_gate(o, g):
    ms = jnp.mean(o * o, axis=-1, keepdims=True)
    return jax.nn.silu(g) * (o * lax.rsqrt(ms + EPS))


def _ret_prompt_kernel(*refs, n_side, side_steps):
    cdec_ref, q_ref, k_ref, kk_ref, v_ref, g_ref, dec_ref, qd_ref = refs[:8]
    side_in = refs[8:8 + n_side]
    o_ref, s_ref = refs[8 + n_side:10 + n_side]
    side_out = refs[10 + n_side:10 + 2 * n_side]
    if n_side:
        _side_cast(pl.program_id(0) * pl.num_programs(1) + pl.program_id(1), side_steps, side_in, side_out)

    @pl.when(pl.program_id(1) == 0)
    def _():
        s_ref[...] = jnp.zeros_like(s_ref)

    for h in range(RET_HEADS):
        qb = q_ref[:, h * RET_DK:(h + 1) * RET_DK]
        kb = k_ref[:, h * RET_DK:(h + 1) * RET_DK]
        kkb = kk_ref[:, h * RET_DK:(h + 1) * RET_DK]
        vb = v_ref[:, h * RET_DV:(h + 1) * RET_DV]
        att = lax.dot_general(qb, kb, (((1,), (1,)), ((), ())), preferred_element_type=F32) * dec_ref[h]
        s = s_ref[0, h]
        o = (jnp.dot(att.astype(BF16), vb, preferred_element_type=F32)
             + qd_ref[h][:, :1] * jnp.dot(qb, s.astype(BF16), preferred_element_type=F32))
        s_ref[0, h] = s * cdec_ref[h] + lax.dot_general(kkb, vb, (((0,), (0,)), ((), ())),
                                                         preferred_element_type=F32)
        og = _group_norm_gate(o, g_ref[:, h * RET_DV:(h + 1) * RET_DV])
        o_ref[:, h * RET_DV:(h + 1) * RET_DV] = og.astype(o_ref.dtype)


def _retention_prompt(qk, kk, v, rest, b, l, side=()):
    c = RET_CHUNK if l % RET_CHUNK == 0 else l
    nc = l // c
    decay, q_dec, _, c_dec = _retention_tables(c)
    wq = RET_HEADS * RET_DK
    wv = RET_HEADS * RET_DV
    full3 = lambda bi, ci: (0, 0, 0)
    side_steps = _side_steps(b * nc)
    side_specs = _side_specs(side, side_steps, lambda bi, ci: bi * nc + ci)
    outs = pl.pallas_call(
        functools.partial(_ret_prompt_kernel, n_side=len(side), side_steps=side_steps),
        grid=(b, nc),
        in_specs=[
            pl.BlockSpec(memory_space=pltpu.SMEM),
            pl.BlockSpec((c, wq), lambda bi, ci: (bi * nc + ci, 0)),
            pl.BlockSpec((c, wq), lambda bi, ci: (bi * nc + ci, 1)),
            pl.BlockSpec((c, wq), lambda bi, ci: (bi * nc + ci, 0)),
            pl.BlockSpec((c, wv), lambda bi, ci: (bi * nc + ci, 0)),
            pl.BlockSpec((c, wv), lambda bi, ci: (bi * nc + ci, REST_G // wv)),
            pl.BlockSpec((RET_HEADS, c, c), full3),
            pl.BlockSpec((RET_HEADS, c, LANES), full3),
        ] + side_specs,
        out_specs=[
            pl.BlockSpec((c, wv), lambda bi, ci: (bi * nc + ci, 0)),
            pl.BlockSpec((1, RET_HEADS, RET_DK, RET_DV), lambda bi, ci: (bi, 0, 0, 0)),
        ] + side_specs,
        out_shape=[
            jax.ShapeDtypeStruct((b * l, wv), BF16),
            jax.ShapeDtypeStruct((b, RET_HEADS, RET_DK, RET_DV), F32),
        ] + [jax.ShapeDtypeStruct(s.shape, BF16) for s in side],
        compiler_params=_cparams(("arbitrary", "arbitrary")),
        name="retention_prompt",
    )(c_dec, qk, qk, kk, v, rest, decay, _lane_bcast(q_dec), *side)
    return outs[0], outs[1], tuple(outs[2:])


SAMPLE_PAD = 16


def _pad_rows(x, rows):
    return jnp.concatenate([x, jnp.zeros((rows - x.shape[0], x.shape[1]), x.dtype)], axis=0)


def _ret_sample_kernel(cdec_ref, q_ref, k_ref, kk_ref, v_ref, g_ref, dec_ref, qd_ref, s0_ref, o_ref, s_ref,
                       *, nb, l):
    for h in range(RET_HEADS):
        qf = q_ref[:, h * RET_DK:(h + 1) * RET_DK].astype(F32)
        kf = k_ref[:, h * RET_DK:(h + 1) * RET_DK].astype(F32)
        kkf = kk_ref[:, h * RET_DK:(h + 1) * RET_DK].astype(F32)
        vf = v_ref[:, h * RET_DV:(h + 1) * RET_DV].astype(F32)
        for bt in range(nb):
            rs = slice(bt * l, (bt + 1) * l)
            qb = _pad_rows(qf[rs], SAMPLE_PAD).astype(BF16)
            kb = _pad_rows(kf[rs], SAMPLE_PAD).astype(BF16)
            kkb = _pad_rows(kkf[rs], SAMPLE_PAD).astype(BF16)
            vb = _pad_rows(vf[rs], SAMPLE_PAD).astype(BF16)
            att = lax.dot_general(qb, kb, (((1,), (1,)), ((), ())), preferred_element_type=F32) * dec_ref[h]
            s = s0_ref[bt, h]
            o = (jnp.dot(att.astype(BF16), vb, preferred_element_type=F32)
                 + qd_ref[h][:, :1] * jnp.dot(qb, s.astype(BF16), preferred_element_type=F32))
            s_ref[bt, h] = s * cdec_ref[h] + lax.dot_general(kkb, vb, (((0,), (0,)), ((), ())),
                                                              preferred_element_type=F32)
            og = _group_norm_gate(o[:l], g_ref[rs, h * RET_DV:(h + 1) * RET_DV])
            o_ref[rs, h * RET_DV:(h + 1) * RET_DV] = og.astype(o_ref.dtype)


def _retention_sample(qk, kk, v, rest, s0, b, l, nb=2):
    assert l % RET_CHUNK != 0 and l <= SAMPLE_PAD and b % nb == 0, (b, l)
    decay, q_dec, _, c_dec = _retention_tables(l)
    p = SAMPLE_PAD - l
    decay = jnp.pad(decay, ((0, 0), (0, p), (0, p)))
    q_dec = jnp.pad(q_dec, ((0, p), (0, 0)))
    wq = RET_HEADS * RET_DK
    wv = RET_HEADS * RET_DV
    rows = nb * l
    full3 = lambda bi: (0, 0, 0)
    state_spec = pl.BlockSpec((nb, RET_HEADS, RET_DK, RET_DV), lambda bi: (bi, 0, 0, 0))
    return pl.pallas_call(
        functools.partial(_ret_sample_kernel, nb=nb, l=l),
        grid=(b // nb,),
        in_specs=[
            pl.BlockSpec(memory_space=pltpu.SMEM),
            pl.BlockSpec((rows, wq), lambda bi: (bi, 0)),
            pl.BlockSpec((rows, wq), lambda bi: (bi, 1)),
            pl.BlockSpec((rows, wq), lambda bi: (bi, 0)),
            pl.BlockSpec((rows, wv), lambda bi: (bi, 0)),
            pl.BlockSpec((rows, wv), lambda bi: (bi, REST_G // wv)),
            pl.BlockSpec((RET_HEADS, SAMPLE_PAD, SAMPLE_PAD), full3),
            pl.BlockSpec((RET_HEADS, SAMPLE_PAD, LANES), full3),
            state_spec,
        ],
        out_specs=[pl.BlockSpec((rows, wv), lambda bi: (bi, 0)), state_spec],
        out_shape=[
            jax.ShapeDtypeStruct((b * l, wv), BF16),
            jax.ShapeDtypeStruct((b, RET_HEADS, RET_DK, RET_DV), F32),
        ],
        compiler_params=_cparams(("arbitrary",)),
        name="retention_sample",
    )(c_dec, qk, qk, kk, v, rest, decay, _lane_bcast(q_dec), s0)


CONV_PAD = 32
CONV_COLS = LANES


def _conv_strip(block, wts, nblk, store):
    lead = CONV_PAD - CONV_HIST
    taps = [((lead + w) // SUBLANES, (lead + w) % SUBLANES) for w in range(CONV_WIDTH)]
    rowid = lax.broadcasted_iota(I32, (SUBLANES, CONV_COLS), 0)

    def partial_sums(m, shifts):
        ys = {}
        for w, (a, s) in enumerate(taps):
            if s in shifts:
                term = block(m + a) * wts[w]
                ys[s] = term if s not in ys else ys[s] + term
        return ys

    shifted = range(1, SUBLANES)
    prev = partial_sums(0, shifted)
    for m in range(nblk):
        nxt = partial_sums(m + 1, shifted)
        acc = partial_sums(m, (0,))[0]
        for s in shifted:
            acc = acc + pltpu.roll(jnp.where(rowid >= s, prev[s], nxt[s]), SUBLANES - s, axis=0)
        store(m, acc)
        prev = nxt


def _conv_kernel(*refs, tl, nb, nl, has_state, n_side, side_steps):
    ca_ref, cb_ref = refs[:2]
    buf_ref = refs[2] if has_state else None
    n_in = 3 if has_state else 2
    cw_ref, bias_ref, lnw_ref, lnb_ref = refs[n_in:n_in + 4]
    side_in = refs[n_in + 4:n_in + 4 + n_side]
    sw_ref, nbuf_ref = refs[n_in + 4 + n_side:n_in + 6 + n_side]
    side_out = refs[n_in + 6 + n_side:n_in + 6 + 2 * n_side]
    upad_ref, c_ref = refs[-2:]
    if n_side:
        _side_cast(pl.program_id(0) * nl + pl.program_id(1), side_steps, side_in, side_out)
    li = pl.program_id(1)
    lead = CONV_PAD - CONV_HIST
    d = ca_ref.shape[1]
    for bt in range(nb):
        @pl.when(li == 0)
        def _():
            if has_state:
                upad_ref[bt, 0:lead, :] = jnp.zeros((lead, d), F32)
                upad_ref[bt, lead:CONV_PAD, :] = buf_ref[0, :, bt, :]
            else:
                upad_ref[bt, 0:CONV_PAD, :] = jnp.zeros((CONV_PAD, d), F32)

        r0 = bt * tl
        upad_ref[bt, CONV_PAD:CONV_PAD + tl, :] = ca_ref[r0:r0 + tl, :] * jax.nn.sigmoid(cb_ref[r0:r0 + tl, :])
        for cc in range(d // CONV_COLS):
            cs = slice(cc * CONV_COLS, (cc + 1) * CONV_COLS)
            wts = [jnp.broadcast_to(cw_ref[0, w:w + 1, cs], (SUBLANES, CONV_COLS)) for w in range(CONV_WIDTH)]
            blocks = {}

            def block(j, cs=cs, blocks=blocks):
                if j not in blocks:
                    blocks[j] = upad_ref[bt, j * SUBLANES:(j + 1) * SUBLANES, cs]
                return blocks[j]

            def store(m, acc, cs=cs):
                c_ref[m * SUBLANES:(m + 1) * SUBLANES, cs] = acc

            _conv_strip(block, wts, tl // SUBLANES, store)
        cf = c_ref[...] + bias_ref[...]
        mu = jnp.mean(cf, axis=-1, keepdims=True)
        var = jnp.mean(jnp.square(cf - mu), axis=-1, keepdims=True)
        cn = (cf - mu) * lax.rsqrt(var + EPS) * lnw_ref[...] + lnb_ref[...]
        sw_ref[r0:r0 + tl, :] = jax.nn.silu(cn).astype(sw_ref.dtype)

        @pl.when(li == nl - 1)
        def _():
            nbuf_ref[0, bt] = upad_ref[bt, tl + lead:tl + CONV_PAD, :]

        if nl > 1:
            upad_ref[bt, 0:CONV_PAD, :] = upad_ref[bt, tl:tl + CONV_PAD, :]


def _conv_branch(rest, buf_t, conv_w, conv_b, ln_w, ln_b, b, l, tl, nb, side=()):
    d = D_MODEL
    nl = l // tl
    assert nb == 1 or nl == 1
    rows = nb * tl
    const2 = lambda bi, li: (0, 0)
    grid = (b // nb, nl)
    side_steps = _side_steps(grid[0] * grid[1])
    side_specs = _side_specs(side, side_steps, lambda bi, li: bi * nl + li)
    state_spec = pl.BlockSpec((1, nb, CONV_HIST, d), lambda bi, li: (0, bi, 0, 0))
    has_state = buf_t is not None
    state_in = [pl.BlockSpec((1, CONV_HIST, nb, d), lambda bi, li: (0, 0, bi, 0))] if has_state else []
    outs = pl.pallas_call(
        functools.partial(_conv_kernel, tl=tl, nb=nb, nl=nl, has_state=has_state, n_side=len(side),
                          side_steps=side_steps),
        grid=grid,
        in_specs=[
            pl.BlockSpec((rows, d), lambda bi, li: (bi * nl + li, REST_CA // d)),
            pl.BlockSpec((rows, d), lambda bi, li: (bi * nl + li, REST_CB // d)),
        ] + state_in + [
            pl.BlockSpec((1, CONV_WIDTH, d), lambda bi, li: (0, 0, 0)),
            pl.BlockSpec((1, d), const2),
            pl.BlockSpec((1, d), const2),
            pl.BlockSpec((1, d), const2),
        ] + side_specs,
        out_specs=[pl.BlockSpec((rows, d), lambda bi, li: (bi * nl + li, 0)), state_spec] + side_specs,
        out_shape=[
            jax.ShapeDtypeStruct((b * l, d), BF16),
            jax.ShapeDtypeStruct((1, b, CONV_HIST, d), F32),
        ] + [jax.ShapeDtypeStruct(s.shape, BF16) for s in side],
        scratch_shapes=[pltpu.VMEM((nb, CONV_PAD + tl, d), F32), pltpu.VMEM((tl, d), F32)],
        compiler_params=_cparams(("arbitrary", "arbitrary")),
        name="conv_branch",
    )(rest, rest, *([buf_t] if has_state else []), conv_w, conv_b, ln_w, ln_b, *side)
    return outs[0], outs[1], tuple(outs[2:])


def _merge_kernel(og_ref, sw_ref, wr_ref, wc_ref, gr_ref, gc_ref, o_ref):
    ret_out = jnp.dot(og_ref[...], wr_ref[...], preferred_element_type=F32)
    conv_out = jnp.dot(sw_ref[...], wc_ref[...], preferred_element_type=F32)
    merged = jax.nn.sigmoid(gr_ref[...]) * ret_out + jax.nn.sigmoid(gc_ref[...]) * conv_out
    o_ref[...] = merged.astype(o_ref.dtype)


def _merge(og, sw, w_ret_o, w_conv_o, rest, tm=512, tn=1024):
    m = og.shape[0]
    n = w_ret_o.shape[1]
    tm = min(tm, m)
    kr = og.shape[1]
    kc = sw.shape[1]
    return pl.pallas_call(
        _merge_kernel,
        grid=(n // tn, m // tm),
        in_specs=[
            pl.BlockSpec((tm, kr), lambda j, i: (i, 0)),
            pl.BlockSpec((tm, kc), lambda j, i: (i, 0)),
            pl.BlockSpec((kr, tn), lambda j, i: (0, j)),
            pl.BlockSpec((kc, tn), lambda j, i: (0, j)),
            pl.BlockSpec((tm, tn), lambda j, i: (i, REST_GRET // tn + j)),
            pl.BlockSpec((tm, tn), lambda j, i: (i, REST_GCONV // tn + j)),
        ],
        out_specs=pl.BlockSpec((tm, tn), lambda j, i: (i, j)),
        out_shape=jax.ShapeDtypeStruct((m, n), BF16),
        compiler_params=_cparams(("arbitrary", "arbitrary")),
        name="merge",
    )(og, sw, w_ret_o, w_conv_o, rest, rest)


def _outproj_kernel(a_ref, w_ref, x_ref, *rest):
    o_ref, wb_ref = rest[-2], rest[-1]

    @pl.when(pl.program_id(1) == 0)
    def _():
        wb_ref[...] = w_ref[...].astype(BF16)

    o_ref[...] = x_ref[...] + jnp.dot(a_ref[...], wb_ref[...], preferred_element_type=F32)


def _outproj_residual(a, w, x, total_rows, row0, prev=None, tm=1024, tn=1024):
    m, k = a.shape
    n = w.shape[1]
    tm = min(tm, m)
    assert row0 % tm == 0
    off = row0 // tm
    in_specs = [
        pl.BlockSpec((tm, k), lambda j, i: (i, 0)),
        pl.BlockSpec((k, tn), lambda j, i: (0, j)),
        pl.BlockSpec((tm, tn), lambda j, i: (i, j)),
    ]
    args = [a, w, x]
    aliases = {}
    if prev is not None:
        in_specs.append(pl.BlockSpec(memory_space=pl.ANY))
        args.append(prev)
        aliases = {3: 0}
    return pl.pallas_call(
        _outproj_kernel,
        grid=(n // tn, m // tm),
        in_specs=in_specs,
        out_specs=pl.BlockSpec((tm, tn), lambda j, i: (i + off, j)),
        out_shape=jax.ShapeDtypeStruct((total_rows, n), F32),
        scratch_shapes=[pltpu.VMEM((k, tn), BF16)],
        input_output_aliases=aliases,
        compiler_params=_cparams(("arbitrary", "arbitrary")),
        name="outproj_residual",
    )(*args)


def _router_kernel(x_ref, nw_ref, w_ref, b_ref, h_ref, meta_ref, meta_t_ref, cnt_ref, base_ref):
    tm = x_ref.shape[0]

    @pl.when(pl.program_id(0) == 0)
    def _():
        base_ref[...] = jnp.zeros_like(base_ref)

    x = x_ref[...]
    ms = jnp.mean(x * x, axis=-1, keepdims=True)
    h = x * lax.rsqrt(ms + EPS) * nw_ref[...]
    h_ref[...] = h
    w = w_ref[...]
    h_hi = h.astype(BF16)
    h_lo = (h - h_hi.astype(F32)).astype(BF16)
    w_hi = w.astype(BF16)
    w_lo = (w - w_hi.astype(F32)).astype(BF16)
    logits = (jnp.dot(h_hi, w_lo, preferred_element_type=F32) + jnp.dot(h_lo, w_hi, preferred_element_type=F32)
              + jnp.dot(h_hi, w_hi, preferred_element_type=F32)) + b_ref[...]
    lane = lax.broadcasted_iota(I32, (tm, ROUTER_COLS), 1)

    def first_argmax(vals):
        top = jnp.max(vals, axis=1, keepdims=True)
        return top, jnp.min(jnp.where(vals == top, lane, ROUTER_COLS), axis=1, keepdims=True)

    cmask = lane < N_GROUPS
    lc = jnp.where(cmask, logits, NEG)
    mc, grp = first_argmax(lc)
    p_grp = 1.0 / jnp.sum(jnp.where(cmask, jnp.exp(lc - mc), 0.0), axis=1, keepdims=True)
    lo = ROUTER_OFF + grp * EXPERTS_PER_GROUP
    lf = jnp.where((lane >= lo) & (lane < lo + EXPERTS_PER_GROUP), logits, NEG)
    v1, i1 = first_argmax(lf)
    v2, i2 = first_argmax(jnp.where(lane == i1, NEG, lf))
    e2 = jnp.exp(v2 - v1)
    den = 1.0 + e2
    g1 = p_grp * (1.0 / den)
    g2 = p_grp * (e2 / den)

    oh1 = lane == i1
    oh2 = lane == i2
    both = jnp.where(oh1 | oh2, 1.0, 0.0)
    row = lax.broadcasted_iota(I32, (tm, tm), 0)
    col = lax.broadcasted_iota(I32, (tm, tm), 1)
    earlier = jnp.where(col < row, 1.0, 0.0).astype(BF16)
    before = jnp.dot(earlier, both.astype(BF16), preferred_element_type=F32) + base_ref[...]
    r1 = jnp.sum(jnp.where(oh1, before, 0.0), axis=1, keepdims=True)
    r2 = jnp.sum(jnp.where(oh2, before, 0.0), axis=1, keepdims=True)
    base_ref[...] = base_ref[...] + jnp.sum(both, axis=0, keepdims=True)
    cnt_ref[...] = base_ref[...]

    fields = [(i1 - ROUTER_OFF).astype(F32), (i2 - ROUTER_OFF).astype(F32), g1, g2, r1, r2]
    meta = jnp.zeros((tm, ROUTER_COLS), F32)
    for k, f in enumerate(fields):
        meta = jnp.where(lane == k, f, meta)
    meta_ref[...] = meta
    meta_t_ref[...] = meta.T[:SUBLANES]


def _router(x, norm_w, w_router, b_router, tm=512):
    n, d = x.shape
    tm = min(tm, n)
    const2 = lambda i: (0, 0)
    return pl.pallas_call(
        _router_kernel,
        grid=(n // tm,),
        in_specs=[
            pl.BlockSpec((tm, d), lambda i: (i, 0)),
            pl.BlockSpec((1, d), const2),
            pl.BlockSpec((d, ROUTER_COLS), const2),
            pl.BlockSpec((1, ROUTER_COLS), const2),
        ],
        out_specs=[
            pl.BlockSpec((tm, d), lambda i: (i, 0)),
            pl.BlockSpec((tm, ROUTER_COLS), lambda i: (i, 0)),
            pl.BlockSpec((SUBLANES, tm), lambda i: (0, i)),
            pl.BlockSpec((1, ROUTER_COLS), const2),
        ],
        out_shape=[
            jax.ShapeDtypeStruct((n, d), F32),
            jax.ShapeDtypeStruct((n, ROUTER_COLS), F32),
            jax.ShapeDtypeStruct((SUBLANES, n), F32),
            jax.ShapeDtypeStruct((1, ROUTER_COLS), F32),
        ],
        scratch_shapes=[pltpu.VMEM((1, ROUTER_COLS), F32)],
        compiler_params=_cparams(("arbitrary",)),
        name="router",
    )(x, norm_w.reshape(1, d), w_router, b_router)


def _dispatch_kernel(rows_ref, span_ref, h_ref, xs_hbm, zero_ref, sem, zsem):
    td = h_ref.shape[0]
    i = pl.program_id(0)

    @pl.when(i == 0)
    def _():
        zero_ref[...] = jnp.zeros_like(zero_ref)

        def fill_copy(e):
            last = pl.multiple_of(span_ref[N_EXPERTS + e] - MOE_ROWS, MOE_ROWS)
            return pltpu.make_async_copy(zero_ref, xs_hbm.at[pl.ds(last, MOE_ROWS)], zsem)

        def nonempty(e):
            return span_ref[N_EXPERTS + e] > span_ref[e]

        def start(e, carry):
            @pl.when(nonempty(e))
            def _():
                fill_copy(e).start()
            return carry

        def wait(e, carry):
            @pl.when(nonempty(e))
            def _():
                fill_copy(e).wait()
            return carry

        lax.fori_loop(0, N_EXPERTS, start, 0)
        lax.fori_loop(0, N_EXPERTS, wait, 0)

    n_tok = pl.num_programs(0) * td

    def tokens(g, carry):
        t0 = pl.multiple_of(g * SUBLANES, SUBLANES)
        for u in range(SUBLANES):
            for k in range(TOP_K_FINE):
                r = rows_ref[k * n_tok + i * td + t0 + u]
                pltpu.make_async_copy(h_ref.at[pl.ds(t0 + u, 1)], xs_hbm.at[pl.ds(r, 1)], sem).start()
        return carry

    lax.fori_loop(0, td // SUBLANES, tokens, 0)
    for _ in range(TOP_K_FINE):
        pltpu.make_async_copy(h_ref, xs_hbm.at[pl.ds(0, td)], sem).wait()


def _dispatch(rows_flat, span, h, n_rows, td=1024):
    n_tok, d = h.shape
    td = min(td, n_tok)
    return pl.pallas_call(
        _dispatch_kernel,
        grid_spec=pltpu.PrefetchScalarGridSpec(
            num_scalar_prefetch=2,
            grid=(n_tok // td,),
            in_specs=[pl.BlockSpec((td, d), lambda i, rows, span: (i, 0))],
            out_specs=pl.BlockSpec(memory_space=pl.ANY),
            scratch_shapes=[pltpu.VMEM((MOE_ROWS, d), F32), pltpu.SemaphoreType.DMA(()),
                            pltpu.SemaphoreType.DMA(())],
        ),
        out_shape=jax.ShapeDtypeStruct((n_rows, d), F32),
        compiler_params=_cparams(("arbitrary",)),
        name="moe_dispatch",
    )(rows_flat, span, h)


def _expert_kernel(be_ref, nh_ref, nu_ref, x_ref, wg_ref, wu_ref, wd_ref, o_ref):
    del be_ref
    nh = nh_ref[pl.program_id(0)]

    def ffn(rows):
        xb = x_ref[0:rows, :].astype(BF16)
        a = jnp.dot(xb, wg_ref[0], preferred_element_type=F32)
        b = jnp.dot(xb, wu_ref[0], preferred_element_type=F32)
        hb = (jax.nn.silu(a) * b).astype(BF16)
        o_ref[0:rows, :] = jnp.dot(hb, wd_ref[0], preferred_element_type=F32)

    for parts in range(1, MOE_PARTS + 1):
        @pl.when(nh == parts)
        def _():
            ffn(parts * MOE_ROWS)
            if parts < MOE_PARTS:
                o_ref[parts * MOE_ROWS:, :] = jnp.zeros((MOE_BLOCK - parts * MOE_ROWS, o_ref.shape[1]), F32)


def _experts(blk_exp, blk_parts, n_used, xs, wg, wu, wd):
    n_rows, d = xs.shape
    n_blocks = n_rows // MOE_BLOCK
    de = wg.shape[2]
    row_map = lambda i, be, nh, nu: (jnp.minimum(i, jnp.maximum(nu[0] - 1, 0)), 0)
    w_map = lambda i, be, nh, nu: (be[i], 0, 0)
    return pl.pallas_call(
        _expert_kernel,
        grid_spec=pltpu.PrefetchScalarGridSpec(
            num_scalar_prefetch=3,
            grid=(n_blocks,),
            in_specs=[
                pl.BlockSpec((MOE_BLOCK, d), row_map),
                pl.BlockSpec((1, d, de), w_map),
                pl.BlockSpec((1, d, de), w_map),
                pl.BlockSpec((1, de, d), w_map),
            ],
            out_specs=pl.BlockSpec((MOE_BLOCK, d), row_map),
        ),
        out_shape=jax.ShapeDtypeStruct((n_rows, d), F32),
        compiler_params=_cparams(("arbitrary",)),
        name="moe_experts",
    )(blk_exp, blk_parts, n_used, xs, wg, wu, wd)


def _combine_kernel(rows_ref, ys_hbm, x_ref, meta_ref, nf_ref, op_ref, os_ref, buf_ref, sem, *, p_tiles):
    tm = x_ref.shape[0]
    i = pl.program_id(0)
    nt = pl.num_programs(0)
    n_tok = nt * tm
    slot = i % 2

    def gather(tile, dst_slot):
        def tokens(g, carry):
            t0 = pl.multiple_of(g * SUBLANES, SUBLANES)
            for u in range(SUBLANES):
                for k in range(TOP_K_FINE):
                    r = rows_ref[k * n_tok + tile * tm + t0 + u]
                    pltpu.make_async_copy(ys_hbm.at[pl.ds(r, 1)], buf_ref.at[dst_slot, k, pl.ds(t0 + u, 1)],
                                          sem.at[dst_slot]).start()
            return carry

        lax.fori_loop(0, tm // SUBLANES, tokens, 0)

    @pl.when(i == 0)
    def _():
        gather(0, 0)

    @pl.when(i + 1 < nt)
    def _():
        gather(i + 1, 1 - slot)

    for k in range(TOP_K_FINE):
        pltpu.make_async_copy(ys_hbm.at[pl.ds(0, tm)], buf_ref.at[slot, k], sem.at[slot]).wait()
    meta = meta_ref[...]
    y = x_ref[...] + (buf_ref[slot, 0] * meta[:, 2:3] + buf_ref[slot, 1] * meta[:, 3:4])
    ms = jnp.mean(y * y, axis=-1, keepdims=True)
    out = y * lax.rsqrt(ms + EPS) * nf_ref[...]

    @pl.when(i < p_tiles)
    def _():
        op_ref[...] = out

    @pl.when(i >= p_tiles)
    def _():
        os_ref[...] = out


def _combine(rows_flat, ys, x, meta, norm_f, n_p, tm=256):
    n, d = x.shape
    n_s = n - n_p
    tm = min(tm, n_p, n_s)
    assert n_p % tm == 0 and n_s % tm == 0
    p_tiles = n_p // tm
    return pl.pallas_call(
        functools.partial(_combine_kernel, p_tiles=p_tiles),
        grid_spec=pltpu.PrefetchScalarGridSpec(
            num_scalar_prefetch=1,
            grid=(n // tm,),
            in_specs=[
                pl.BlockSpec(memory_space=pl.ANY),
                pl.BlockSpec((tm, d), lambda i, rows: (i, 0)),
                pl.BlockSpec((tm, ROUTER_COLS), lambda i, rows: (i, 0)),
                pl.BlockSpec((1, d), lambda i, rows: (0, 0)),
            ],
            out_specs=[
                pl.BlockSpec((tm, d), lambda i, rows: (jnp.minimum(i, p_tiles - 1), 0)),
                pl.BlockSpec((tm, d), lambda i, rows: (jnp.maximum(i - p_tiles, 0), 0)),
            ],
            scratch_shapes=[pltpu.VMEM((2, TOP_K_FINE, tm, d), F32), pltpu.SemaphoreType.DMA((2,))],
        ),
        out_shape=[jax.ShapeDtypeStruct((n_p, d), F32), jax.ShapeDtypeStruct((n_s, d), F32)],
        compiler_params=_cparams(("arbitrary",)),
        name="moe_combine",
    )(rows_flat, ys, x, meta, norm_f.reshape(1, d))


def _moe_and_final_norm(x, n_p, norm_ffn, w_coarse, b_coarse, w_fine, b_fine, wg, wu, wd, norm_f):
    n_tok, d = x.shape
    pad = ROUTER_COLS - N_GROUPS - N_EXPERTS
    w_router = jnp.concatenate([w_coarse, w_fine, jnp.zeros((d, pad), F32)], axis=1)
    b_router = jnp.concatenate([b_coarse, b_fine, jnp.zeros((pad,), F32)]).reshape(1, ROUTER_COLS)
    h, meta, meta_t, counts = _router(x, norm_ffn, w_router, b_router)

    n_asg = n_tok * TOP_K_FINE
    n_blocks = -(-n_asg // MOE_BLOCK) + N_EXPERTS
    n_rows = n_blocks * MOE_BLOCK
    cnt = counts[0, ROUTER_OFF:ROUTER_OFF + N_EXPERTS].astype(I32)
    padded = (cnt + MOE_BLOCK - 1) // MOE_BLOCK * MOE_BLOCK
    pend = jnp.cumsum(padded)
    pstart = pend - padded
    cend = pstart + (cnt + MOE_ROWS - 1) // MOE_ROWS * MOE_ROWS
    expert = meta_t[0:TOP_K_FINE].astype(I32)
    rows = meta_t[4:4 + TOP_K_FINE].astype(I32)
    for e in range(N_EXPERTS):
        rows = rows + jnp.where(expert == e, pstart[e], 0)
    rows_flat = rows.reshape(-1)
    n_used = (pend[-1] // MOE_BLOCK).astype(I32)
    blk = jnp.arange(n_blocks, dtype=I32)
    blk_exp = jnp.searchsorted(pend, jnp.minimum(blk, n_used - 1) * MOE_BLOCK, side='right').astype(I32)
    blk_exp = jnp.minimum(blk_exp, N_EXPERTS - 1)
    blk_parts = jnp.clip((cend[blk_exp] - blk * MOE_BLOCK) // MOE_ROWS, 0, MOE_PARTS)
    blk_parts = jnp.where(blk < n_used, blk_parts, 0).astype(I32)

    xs = _dispatch(rows_flat, jnp.concatenate([pstart, cend]).astype(I32), h, n_rows)
    ys = _experts(blk_exp, blk_parts, n_used.reshape(1), xs, wg, wu, wd)
    return _combine(rows_flat, ys, x, meta, norm_f, n_p)


def _mixer(x, s_ret, s_conv, pos0, norm_mix, w_in, conv_w, conv_b, conv_ln_w, conv_ln_b, w_o,
           total_rows, row0, prev, merge_w=None, side_rope=(), side_rest=(), side_conv=()):
    b, l, d = x.shape
    x2 = x.reshape(b * l, d)
    chunk = RET_CHUNK if l % RET_CHUNK == 0 else l
    h = _rmsnorm_cast(x2, norm_mix)
    qk, kk, cast_a = _inproj_rope(h, w_in, *_rope_inputs(b, l, pos0, chunk), side=side_rope)
    v, _ = _inproj(h, w_in, COL_V, W_V, BF16)
    rest, cast_b = _inproj(h, w_in, COL_REST, W_REST, F32, side_rest)
    if s_ret is None:
        og, s_ret_new, _ = _retention_prompt(qk, kk, v, rest, b, l)
        sw, s_conv_new, cast_c = _conv_branch(rest, s_conv, conv_w, conv_b, conv_ln_w, conv_ln_b, b, l,
                                              tl=min(128, l), nb=1, side=side_conv)
    else:
        og, s_ret_new = _retention_sample(qk, kk, v, rest, s_ret, b, l)
        sw, s_conv_new, cast_c = _conv_branch(rest, s_conv, conv_w, conv_b, conv_ln_w, conv_ln_b, b, l,
                                              tl=l, nb=min(SUBLANES, b), side=side_conv)
    if merge_w is None:
        merge_w = cast_a[:2]
    merged = _merge(og, sw, *merge_w, rest)
    x1 = _outproj_residual(merged, w_o, x2, total_rows, row0, prev)
    return x1, s_ret_new, s_conv_new, cast_a + cast_b + cast_c


def kernel(x_prompt, x_sample, state_ret, state_conv, norm_mix, w_in, w_ret_o, conv_w, conv_b, conv_ln_w,
           conv_ln_b, w_conv_o, w_o, norm_ffn, w_coarse, b_coarse, w_fine, b_fine, w_gate, w_up, w_down, norm_f):
    depth = w_in.shape[0]
    assert depth == 1, "single-layer trunk"
    bp, lp, d = x_prompt.shape
    bs, ls, _ = x_sample.shape
    n_p = bp * lp
    n_s = bs * ls
    mix_w = (norm_mix[0], w_in[0], conv_w, conv_b, conv_ln_w, conv_ln_b, w_o[0])
    n_exp, _, d_exp = w_gate.shape[1:]
    flat = lambda w: w.reshape(n_exp * w.shape[2], w.shape[3])
    x1, ret_p, conv_p, (wr, wc, wg, wu, wd) = _mixer(
        x_prompt, None, None, 0, *mix_w, n_p + n_s, 0, None,
        side_rope=(w_ret_o[0], w_conv_o[0]), side_rest=(flat(w_gate),), side_conv=(flat(w_up), flat(w_down)))
    conv_hist = jnp.transpose(state_conv, (0, 2, 1, 3))
    x1, ret_s, conv_s, _ = _mixer(x_sample, state_ret[0], conv_hist, PAST_LEN, *mix_w, n_p + n_s, n_p, x1,
                                  merge_w=(wr, wc))
    yp, ys = _moe_and_final_norm(x1, n_p, norm_ffn[0], w_coarse[0], b_coarse[0], w_fine[0], b_fine[0],
                                 wg.reshape(n_exp, d, d_exp), wu.reshape(n_exp, d, d_exp),
                                 wd.reshape(n_exp, d_exp, d), norm_f)
    return (yp.reshape(bp, lp, d), ys.reshape(bs, ls, d), ret_p[None], conv_p, ret_s[None], conv_s)
```

```python
import functools

import jax
import jax.numpy as jnp
from jax import lax
from jax.experimental import pallas as pl
from jax.experimental.pallas import tpu as pltpu

F32 = jnp.float32
BF16 = jnp.bfloat16
I32 = jnp.int32

D_MODEL = 2048
PAST_LEN = 16384
RET_HEADS = 8
RET_DK = D_MODEL // RET_HEADS
RET_DV = 2 * D_MODEL // RET_HEADS
RET_HALF = RET_DK // 2
RET_CHUNK = 256
ROPE_BASE = 10000.0
CONV_WIDTH = 31
CONV_HIST = CONV_WIDTH - 1
N_GROUPS = 4
EXPERTS_PER_GROUP = 8
N_EXPERTS = N_GROUPS * EXPERTS_PER_GROUP
TOP_K_FINE = 2
D_EXPERT = D_MODEL // 2
EPS = 1e-6

W_QK = 2 * RET_HEADS * RET_DK
W_V = RET_HEADS * RET_DV
W_G = RET_HEADS * RET_DV
COL_QK = 0
COL_V = COL_QK + W_QK
COL_G = COL_V + W_V
COL_CA = COL_G + W_G
COL_CB = COL_CA + D_MODEL
COL_GATES = COL_CB + D_MODEL
REST_G = 0
REST_GRET = REST_G + W_G
REST_GCONV = REST_GRET + D_MODEL
W_REST = REST_GCONV + D_MODEL

LANES = 128
SUBLANES = 8
VMEM_LIMIT = 56 * 1024 * 1024
MOE_ROWS = 128
MOE_PARTS = 4
MOE_BLOCK = MOE_PARTS * MOE_ROWS
ROUTER_COLS = LANES
ROUTER_OFF = N_GROUPS
NEG = -3.0e38
ISSUE_UNROLL = 8


def _cparams(sem):
    return pltpu.CompilerParams(dimension_semantics=sem, vmem_limit_bytes=VMEM_LIMIT)


def _rmsnorm_kernel(x_ref, w_ref, o_ref):
    x = x_ref[...]
    ms = jnp.mean(x * x, axis=-1, keepdims=True)
    o_ref[...] = (x * lax.rsqrt(ms + EPS) * w_ref[...]).astype(o_ref.dtype)


def _rmsnorm_cast(x, w, tm=512):
    m, d = x.shape
    tm = min(tm, m)
    return pl.pallas_call(
        _rmsnorm_kernel,
        grid=(m // tm,),
        in_specs=[pl.BlockSpec((tm, d), lambda i: (i, 0)), pl.BlockSpec((1, d), lambda i: (0, 0))],
        out_specs=pl.BlockSpec((tm, d), lambda i: (i, 0)),
        out_shape=jax.ShapeDtypeStruct((m, d), BF16),
        compiler_params=_cparams(("arbitrary",)),
        name="rmsnorm_cast",
    )(x, w.reshape(1, d))


def _side_steps(total_steps):
    return 1 << (total_steps.bit_length() - 1)


def _side_specs(arrays, n_steps, linear_step):
    def spec(a):
        rows = a.shape[0] // n_steps
        assert rows * n_steps == a.shape[0] and rows % 16 == 0, (a.shape, n_steps)
        return pl.BlockSpec((rows, a.shape[1]), lambda *g: (jnp.minimum(linear_step(*g), n_steps - 1), 0))
    return [spec(a) for a in arrays]


def _side_cast(step, n_steps, srcs, dsts):
    @pl.when(step < n_steps)
    def _():
        for src, dst in zip(srcs, dsts):
            dst[...] = src[...].astype(dst.dtype)


def _inproj_kernel(*refs, n_side, side_steps):
    a_ref, w_ref = refs[:2]
    side_in = refs[2:2 + n_side]
    o_ref = refs[2 + n_side]
    side_out = refs[3 + n_side:3 + 2 * n_side]
    wb_ref = refs[-1]

    @pl.when(pl.program_id(1) == 0)
    def _():
        wb_ref[...] = w_ref[...].astype(BF16)

    o_ref[...] = jnp.dot(a_ref[...], wb_ref[...], preferred_element_type=F32).astype(o_ref.dtype)
    if n_side:
        _side_cast(pl.program_id(0) * pl.num_programs(1) + pl.program_id(1), side_steps, side_in, side_out)


def _inproj(a, w, col_spans, out_dtype, side=(), tm=1024, tn=1024):
    m, k = a.shape
    tm = min(tm, m)
    assert all(c0 % tn == 0 and n % tn == 0 for c0, n in col_spans)
    ncols = sum(n for _, n in col_spans)
    grid = (ncols // tn, m // tm)

    def w_tile(j):
        tile, first = 0, 0
        for c0, n in col_spans:
            tile = jnp.where(j >= first, c0 // tn + j - first, tile)
            first += n // tn
        return tile

    side_steps = _side_steps(grid[0] * grid[1])
    side_specs = _side_specs(side, side_steps, lambda j, i: j * grid[1] + i)
    outs = pl.pallas_call(
        functools.partial(_inproj_kernel, n_side=len(side), side_steps=side_steps),
        grid=grid,
        in_specs=[pl.BlockSpec((tm, k), lambda j, i: (i, 0)),
                  pl.BlockSpec((k, tn), lambda j, i: (0, w_tile(j)))] + side_specs,
        out_specs=[pl.BlockSpec((tm, tn), lambda j, i: (i, j))] + side_specs,
        out_shape=[jax.ShapeDtypeStruct((m, ncols), out_dtype)]
                  + [jax.ShapeDtypeStruct(s.shape, BF16) for s in side],
        scratch_shapes=[pltpu.VMEM((k, tn), BF16)],
        compiler_params=_cparams(("arbitrary", "arbitrary")),
        name="inproj",
    )(a, w, *side)
    return outs[0], tuple(outs[1:])


def _inproj_glu_kernel(a_ref, wa_ref, wb_ref, o_ref, wab_ref, wbb_ref):
    @pl.when(pl.program_id(1) == 0)
    def _():
        wab_ref[...] = wa_ref[...].astype(BF16)
        wbb_ref[...] = wb_ref[...].astype(BF16)

    a = a_ref[...]
    ca = jnp.dot(a, wab_ref[...], preferred_element_type=F32)
    cb = jnp.dot(a, wbb_ref[...], preferred_element_type=F32)
    o_ref[...] = ca * jax.nn.sigmoid(cb)


def _inproj_glu(a, w, tm=1024, tn=512):
    m, k = a.shape
    tm = min(tm, m)
    n = D_MODEL
    assert COL_CA % tn == 0 and COL_CB % tn == 0 and n % tn == 0
    return pl.pallas_call(
        _inproj_glu_kernel,
        grid=(n // tn, m // tm),
        in_specs=[
            pl.BlockSpec((tm, k), lambda j, i: (i, 0)),
            pl.BlockSpec((k, tn), lambda j, i: (0, COL_CA // tn + j)),
            pl.BlockSpec((k, tn), lambda j, i: (0, COL_CB // tn + j)),
        ],
        out_specs=pl.BlockSpec((tm, tn), lambda j, i: (i, j)),
        out_shape=jax.ShapeDtypeStruct((m, n), F32),
        scratch_shapes=[pltpu.VMEM((k, tn), BF16), pltpu.VMEM((k, tn), BF16)],
        compiler_params=_cparams(("arbitrary", "arbitrary")),
        name="inproj_glu",
    )(a, w, w)


ROPE_ROWS = RET_CHUNK


def _inproj_rope_kernel(*refs, nq_tiles, n_side, side_steps):
    a_ref, w_ref, cos_ref, sin_ref, kd_ref = refs[:5]
    side_in = refs[5:5 + n_side]
    qk_ref, kk_ref = refs[5 + n_side:7 + n_side]
    side_out = refs[7 + n_side:7 + 2 * n_side]
    wb_ref, acc_ref = refs[-2:]
    j = pl.program_id(0)
    if n_side:
        _side_cast(j * pl.num_programs(1) + pl.program_id(1), side_steps, side_in, side_out)

    @pl.when(pl.program_id(1) == 0)
    def _():
        wb_ref[...] = w_ref[...].astype(BF16)

    acc_ref[...] = jnp.dot(a_ref[...], wb_ref[...], preferred_element_type=F32)
    tm, tn = acc_ref.shape
    scale = jnp.where(j >= nq_tiles, RET_DK ** -0.5, 1.0).astype(F32)
    for r in range(tm // ROPE_ROWS):
        rs = slice(r * ROPE_ROWS, (r + 1) * ROPE_ROWS)
        cos = cos_ref[rs, :]
        sin = sin_ref[rs, :]
        parts = []
        for hh in range(tn // RET_DK):
            x1 = acc_ref[rs, hh * RET_DK:hh * RET_DK + RET_HALF]
            x2 = acc_ref[rs, hh * RET_DK + RET_HALF:(hh + 1) * RET_DK]
            parts += [x1 * cos - x2 * sin, x1 * sin + x2 * cos]
        rot = jnp.concatenate(parts, axis=1) * scale
        qk_ref[rs, :] = rot.astype(qk_ref.dtype)

        @pl.when(j >= nq_tiles)
        def _():
            kk_ref[rs, :] = (rot * kd_ref[...]).astype(kk_ref.dtype)


def _inproj_rope(a, w, cos_rows, sin_rows, kd_rows, side=(), tm=1024, tn=1024):
    m, k = a.shape
    tm = min(tm, m)
    wq = W_QK // 2
    assert tm % ROPE_ROWS == 0 and wq % tn == 0 and tn % RET_DK == 0
    nq = wq // tn
    grid = (2 * nq, m // tm)
    side_steps = _side_steps(grid[0] * grid[1])
    side_specs = _side_specs(side, side_steps, lambda j, i: j * grid[1] + i)
    outs = pl.pallas_call(
        functools.partial(_inproj_rope_kernel, nq_tiles=nq, n_side=len(side), side_steps=side_steps),
        grid=grid,
        in_specs=[
            pl.BlockSpec((tm, k), lambda j, i: (i, 0)),
            pl.BlockSpec((k, tn), lambda j, i: (0, COL_QK // tn + j)),
            pl.BlockSpec((tm, RET_HALF), lambda j, i: (i, 0)),
            pl.BlockSpec((tm, RET_HALF), lambda j, i: (i, 0)),
            pl.BlockSpec((ROPE_ROWS, tn), lambda j, i: (0, jnp.maximum(j - nq, 0))),
        ] + side_specs,
        out_specs=[
            pl.BlockSpec((tm, tn), lambda j, i: (i, j)),
            pl.BlockSpec((tm, tn), lambda j, i: (jnp.where(j >= nq, i, 0), jnp.maximum(j - nq, 0))),
        ] + side_specs,
        out_shape=[jax.ShapeDtypeStruct((m, W_QK), BF16), jax.ShapeDtypeStruct((m, wq), BF16)]
                  + [jax.ShapeDtypeStruct(s.shape, BF16) for s in side],
        scratch_shapes=[pltpu.VMEM((k, tn), BF16), pltpu.VMEM((tm, tn), F32)],
        compiler_params=_cparams(("arbitrary", "arbitrary")),
        name="inproj_rope",
    )(a, w, cos_rows, sin_rows, kd_rows, *side)
    return outs[0], outs[1], tuple(outs[2:])


def _retention_tables(c):
    h = RET_HEADS
    log_g = jnp.log1p(-jnp.exp2(-5.0 - jnp.arange(h, dtype=F32)))
    idx = jnp.arange(c, dtype=F32)
    diff = idx[:, None] - idx[None, :]
    decay = jnp.where(diff[None] >= 0.0, jnp.exp(jnp.maximum(diff, 0.0)[None] * log_g[:, None, None]), 0.0)
    q_dec = jnp.exp((idx[:, None] + 1.0) * log_g[None, :])
    k_dec = jnp.exp((c - 1.0 - idx)[:, None] * log_g[None, :])
    c_dec = jnp.exp(c * log_g)
    return decay, q_dec, k_dec, c_dec


def _rope_tables(pos):
    inv = ROPE_BASE ** (-jnp.arange(RET_HALF, dtype=F32) / RET_HALF)
    ang = pos.astype(F32)[:, None] * inv[None, :]
    return jnp.cos(ang), jnp.sin(ang)


def _lane_bcast(t):
    return jnp.broadcast_to(t.T[:, :, None], (t.shape[1], t.shape[0], LANES))


def _rope_inputs(b, l, pos0, c):
    cos, sin = _rope_tables(pos0 + jnp.arange(l))
    _, _, k_dec, _ = _retention_tables(c)
    kd = jnp.repeat(k_dec, RET_DK, axis=1)
    return jnp.tile(cos, (b, 1)), jnp.tile(sin, (b, 1)), jnp.tile(kd, (ROPE_ROWS // c, 1))


def _group_norm_gate(o, g):
    ms = jnp.mean(o * o, axis=-1, keepdims=True)
    return jax.nn.silu(g) * (o * lax.rsqrt(ms + EPS))


def _ret_prompt_kernel(*refs, n_side, side_steps):
    cdec_ref, q_ref, k_ref, kk_ref, v_ref, g_ref, dec_ref, qd_ref = refs[:8]
    side_in = refs[8:8 + n_side]
    o_ref, s_ref = refs[8 + n_side:10 + n_side]
    side_out = refs[10 + n_side:10 + 2 * n_side]
    if n_side:
        _side_cast(pl.program_id(0) * pl.num_programs(1) + pl.program_id(1), side_steps, side_in, side_out)

    @pl.when(pl.program_id(1) == 0)
    def _():
        s_ref[...] = jnp.zeros_like(s_ref)

    for h in range(RET_HEADS):
        qb = q_ref[:, h * RET_DK:(h + 1) * RET_DK]
        kb = k_ref[:, h * RET_DK:(h + 1) * RET_DK]
        kkb = kk_ref[:, h * RET_DK:(h + 1) * RET_DK]
        vb = v_ref[:, h * RET_DV:(h + 1) * RET_DV]
        att = lax.dot_general(qb, kb, (((1,), (1,)), ((), ())), preferred_element_type=F32) * dec_ref[h]
        s = s_ref[0, h]
        o = (jnp.dot(att.astype(BF16), vb, preferred_element_type=F32)
             + qd_ref[h][:, :1] * jnp.dot(qb, s.astype(BF16), preferred_element_type=F32))
        s_ref[0, h] = s * cdec_ref[h] + lax.dot_general(kkb, vb, (((0,), (0,)), ((), ())),
                                                         preferred_element_type=F32)
        og = _group_norm_gate(o, g_ref[:, h * RET_DV:(h + 1) * RET_DV])
        o_ref[:, h * RET_DV:(h + 1) * RET_DV] = og.astype(o_ref.dtype)


def _retention_prompt(qk, kk, v, rest, b, l, side=()):
    c = RET_CHUNK if l % RET_CHUNK == 0 else l
    nc = l // c
    decay, q_dec, _, c_dec = _retention_tables(c)
    wq = RET_HEADS * RET_DK
    wv = RET_HEADS * RET_DV
    full3 = lambda bi, ci: (0, 0, 0)
    side_steps = _side_steps(b * nc)
    side_specs = _side_specs(side, side_steps, lambda bi, ci: bi * nc + ci)
    outs = pl.pallas_call(
        functools.partial(_ret_prompt_kernel, n_side=len(side), side_steps=side_steps),
        grid=(b, nc),
        in_specs=[
            pl.BlockSpec(memory_space=pltpu.SMEM),
            pl.BlockSpec((c, wq), lambda bi, ci: (bi * nc + ci, 0)),
            pl.BlockSpec((c, wq), lambda bi, ci: (bi * nc + ci, 1)),
            pl.BlockSpec((c, wq), lambda bi, ci: (bi * nc + ci, 0)),
            pl.BlockSpec((c, wv), lambda bi, ci: (bi * nc + ci, 0)),
            pl.BlockSpec((c, wv), lambda bi, ci: (bi * nc + ci, REST_G // wv)),
            pl.BlockSpec((RET_HEADS, c, c), full3),
            pl.BlockSpec((RET_HEADS, c, LANES), full3),
        ] + side_specs,
        out_specs=[
            pl.BlockSpec((c, wv), lambda bi, ci: (bi * nc + ci, 0)),
            pl.BlockSpec((1, RET_HEADS, RET_DK, RET_DV), lambda bi, ci: (bi, 0, 0, 0)),
        ] + side_specs,
        out_shape=[
            jax.ShapeDtypeStruct((b * l, wv), BF16),
            jax.ShapeDtypeStruct((b, RET_HEADS, RET_DK, RET_DV), F32),
        ] + [jax.ShapeDtypeStruct(s.shape, BF16) for s in side],
        compiler_params=_cparams(("arbitrary", "arbitrary")),
        name="retention_prompt",
    )(c_dec, qk, qk, kk, v, rest, decay, _lane_bcast(q_dec), *side)
    return outs[0], outs[1], tuple(outs[2:])


SAMPLE_PAD = 16


def _pad_rows(x, rows):
    return jnp.concatenate([x, jnp.zeros((rows - x.shape[0], x.shape[1]), x.dtype)], axis=0)


def _ret_sample_kernel(cdec_ref, q_ref, k_ref, kk_ref, v_ref, g_ref, dec_ref, qd_ref, s0_ref, o_ref, s_ref,
                       *, nb, l):
    for h in range(RET_HEADS):
        qf = q_ref[:, h * RET_DK:(h + 1) * RET_DK].astype(F32)
        kf = k_ref[:, h * RET_DK:(h + 1) * RET_DK].astype(F32)
        kkf = kk_ref[:, h * RET_DK:(h + 1) * RET_DK].astype(F32)
        vf = v_ref[:, h * RET_DV:(h + 1) * RET_DV].astype(F32)
        for bt in range(nb):
            rs = slice(bt * l, (bt + 1) * l)
            qb = _pad_rows(qf[rs], SAMPLE_PAD).astype(BF16)
            kb = _pad_rows(kf[rs], SAMPLE_PAD).astype(BF16)
            kkb = _pad_rows(kkf[rs], SAMPLE_PAD).astype(BF16)
            vb = _pad_rows(vf[rs], SAMPLE_PAD).astype(BF16)
            att = lax.dot_general(qb, kb, (((1,), (1,)), ((), ())), preferred_element_type=F32) * dec_ref[h]
            s = s0_ref[bt, h]
            o = (jnp.dot(att.astype(BF16), vb, preferred_element_type=F32)
                 + qd_ref[h][:, :1] * jnp.dot(qb, s.astype(BF16), preferred_element_type=F32))
            s_ref[bt, h] = s * cdec_ref[h] + lax.dot_general(kkb, vb, (((0,), (0,)), ((), ())),
                                                              preferred_element_type=F32)
            og = _group_norm_gate(o[:l], g_ref[rs, h * RET_DV:(h + 1) * RET_DV])
            o_ref[rs, h * RET_DV:(h + 1) * RET_DV] = og.astype(o_ref.dtype)


def _retention_sample(qk, kk, v, rest, s0, b, l, nb=2):
    assert l % RET_CHUNK != 0 and l <= SAMPLE_PAD and b % nb == 0, (b, l)
    decay, q_dec, _, c_dec = _retention_tables(l)
    p = SAMPLE_PAD - l
    decay = jnp.pad(decay, ((0, 0), (0, p), (0, p)))
    q_dec = jnp.pad(q_dec, ((0, p), (0, 0)))
    wq = RET_HEADS * RET_DK
    wv = RET_HEADS * RET_DV
    rows = nb * l
    full3 = lambda bi: (0, 0, 0)
    state_spec = pl.BlockSpec((nb, RET_HEADS, RET_DK, RET_DV), lambda bi: (bi, 0, 0, 0))
    return pl.pallas_call(
        functools.partial(_ret_sample_kernel, nb=nb, l=l),
        grid=(b // nb,),
        in_specs=[
            pl.BlockSpec(memory_space=pltpu.SMEM),
            pl.BlockSpec((rows, wq), lambda bi: (bi, 0)),
            pl.BlockSpec((rows, wq), lambda bi: (bi, 1)),
            pl.BlockSpec((rows, wq), lambda bi: (bi, 0)),
            pl.BlockSpec((rows, wv), lambda bi: (bi, 0)),
            pl.BlockSpec((rows, wv), lambda bi: (bi, REST_G // wv)),
            pl.BlockSpec((RET_HEADS, SAMPLE_PAD, SAMPLE_PAD), full3),
            pl.BlockSpec((RET_HEADS, SAMPLE_PAD, LANES), full3),
            state_spec,
        ],
        out_specs=[pl.BlockSpec((rows, wv), lambda bi: (bi, 0)), state_spec],
        out_shape=[
            jax.ShapeDtypeStruct((b * l, wv), BF16),
            jax.ShapeDtypeStruct((b, RET_HEADS, RET_DK, RET_DV), F32),
        ],
        compiler_params=_cparams(("arbitrary",)),
        name="retention_sample",
    )(c_dec, qk, qk, kk, v, rest, decay, _lane_bcast(q_dec), s0)


CONV_PAD = 32
CONV_COLS = LANES


def _conv_strip(block, wts, nblk, store):
    lead = CONV_PAD - CONV_HIST
    taps = [((lead + w) // SUBLANES, (lead + w) % SUBLANES) for w in range(CONV_WIDTH)]
    rowid = lax.broadcasted_iota(I32, (SUBLANES, CONV_COLS), 0)

    def partial_sums(m, shifts):
        ys = {}
        for w, (a, s) in enumerate(taps):
            if s in shifts:
                term = block(m + a) * wts[w]
                ys[s] = term if s not in ys else ys[s] + term
        return ys

    shifted = range(1, SUBLANES)
    prev = partial_sums(0, shifted)
    for m in range(nblk):
        nxt = partial_sums(m + 1, shifted)
        acc = partial_sums(m, (0,))[0]
        for s in shifted:
            acc = acc + pltpu.roll(jnp.where(rowid >= s, prev[s], nxt[s]), SUBLANES - s, axis=0)
        store(m, acc)
        prev = nxt


def _conv_kernel(*refs, tl, nb, nl, has_state, n_side, side_steps):
    u_ref = refs[0]
    buf_ref = refs[1] if has_state else None
    n_in = 2 if has_state else 1
    cw_ref, bias_ref, lnw_ref, lnb_ref = refs[n_in:n_in + 4]
    side_in = refs[n_in + 4:n_in + 4 + n_side]
    sw_ref, nbuf_ref = refs[n_in + 4 + n_side:n_in + 6 + n_side]
    side_out = refs[n_in + 6 + n_side:n_in + 6 + 2 * n_side]
    upad_ref, c_ref = refs[-2:]
    if n_side:
        _side_cast(pl.program_id(0) * nl + pl.program_id(1), side_steps, side_in, side_out)
    li = pl.program_id(1)
    lead = CONV_PAD - CONV_HIST
    d = u_ref.shape[1]
    for bt in range(nb):
        @pl.when(li == 0)
        def _():
            if has_state:
                upad_ref[bt, 0:lead, :] = jnp.zeros((lead, d), F32)
                upad_ref[bt, lead:CONV_PAD, :] = buf_ref[0, :, bt, :]
            else:
                upad_ref[bt, 0:CONV_PAD, :] = jnp.zeros((CONV_PAD, d), F32)

        r0 = bt * tl
        upad_ref[bt, CONV_PAD:CONV_PAD + tl, :] = u_ref[r0:r0 + tl, :]
        for cc in range(d // CONV_COLS):
            cs = slice(cc * CONV_COLS, (cc + 1) * CONV_COLS)
            wts = [jnp.broadcast_to(cw_ref[0, w:w + 1, cs], (SUBLANES, CONV_COLS)) for w in range(CONV_WIDTH)]
            blocks = {}

            def block(j, cs=cs, blocks=blocks):
                if j not in blocks:
                    blocks[j] = upad_ref[bt, j * SUBLANES:(j + 1) * SUBLANES, cs]
                return blocks[j]

            def store(m, acc, cs=cs):
                c_ref[m * SUBLANES:(m + 1) * SUBLANES, cs] = acc

            _conv_strip(block, wts, tl // SUBLANES, store)
        cf = c_ref[...] + bias_ref[...]
        mu = jnp.mean(cf, axis=-1, keepdims=True)
        var = jnp.mean(jnp.square(cf - mu), axis=-1, keepdims=True)
        cn = (cf - mu) * lax.rsqrt(var + EPS) * lnw_ref[...] + lnb_ref[...]
        sw_ref[r0:r0 + tl, :] = jax.nn.silu(cn).astype(sw_ref.dtype)

        @pl.when(li == nl - 1)
        def _():
            nbuf_ref[0, bt] = upad_ref[bt, tl + lead:tl + CONV_PAD, :]

        if nl > 1:
            upad_ref[bt, 0:CONV_PAD, :] = upad_ref[bt, tl:tl + CONV_PAD, :]


def _conv_branch(u, buf_t, conv_w, conv_b, ln_w, ln_b, b, l, tl, nb, side=()):
    d = D_MODEL
    nl = l // tl
    assert nb == 1 or nl == 1
    rows = nb * tl
    const2 = lambda bi, li: (0, 0)
    grid = (b // nb, nl)
    side_steps = _side_steps(grid[0] * grid[1])
    side_specs = _side_specs(side, side_steps, lambda bi, li: bi * nl + li)
    state_spec = pl.BlockSpec((1, nb, CONV_HIST, d), lambda bi, li: (0, bi, 0, 0))
    has_state = buf_t is not None
    state_in = [pl.BlockSpec((1, CONV_HIST, nb, d), lambda bi, li: (0, 0, bi, 0))] if has_state else []
    outs = pl.pallas_call(
        functools.partial(_conv_kernel, tl=tl, nb=nb, nl=nl, has_state=has_state, n_side=len(side),
                          side_steps=side_steps),
        grid=grid,
        in_specs=[
            pl.BlockSpec((rows, d), lambda bi, li: (bi * nl + li, 0)),
        ] + state_in + [
            pl.BlockSpec((1, CONV_WIDTH, d), lambda bi, li: (0, 0, 0)),
            pl.BlockSpec((1, d), const2),
            pl.BlockSpec((1, d), const2),
            pl.BlockSpec((1, d), const2),
        ] + side_specs,
        out_specs=[pl.BlockSpec((rows, d), lambda bi, li: (bi * nl + li, 0)), state_spec] + side_specs,
        out_shape=[
            jax.ShapeDtypeStruct((b * l, d), BF16),
            jax.ShapeDtypeStruct((1, b, CONV_HIST, d), F32),
        ] + [jax.ShapeDtypeStruct(s.shape, BF16) for s in side],
        scratch_shapes=[pltpu.VMEM((nb, CONV_PAD + tl, d), F32), pltpu.VMEM((tl, d), F32)],
        compiler_params=_cparams(("arbitrary", "arbitrary")),
        name="conv_branch",
    )(u, *([buf_t] if has_state else []), conv_w, conv_b, ln_w, ln_b, *side)
    return outs[0], outs[1], tuple(outs[2:])


def _merge_kernel(og_ref, sw_ref, wr_ref, wc_ref, gr_ref, gc_ref, o_ref):
    ret_out = jnp.dot(og_ref[...], wr_ref[...], preferred_element_type=F32)
    conv_out = jnp.dot(sw_ref[...], wc_ref[...], preferred_element_type=F32)
    merged = jax.nn.sigmoid(gr_ref[...]) * ret_out + jax.nn.sigmoid(gc_ref[...]) * conv_out
    o_ref[...] = merged.astype(o_ref.dtype)


def _merge(og, sw, w_ret_o, w_conv_o, rest, tm=512, tn=1024):
    m = og.shape[0]
    n = w_ret_o.shape[1]
    tm = min(tm, m)
    kr = og.shape[1]
    kc = sw.shape[1]
    return pl.pallas_call(
        _merge_kernel,
        grid=(n // tn, m // tm),
        in_specs=[
            pl.BlockSpec((tm, kr), lambda j, i: (i, 0)),
            pl.BlockSpec((tm, kc), lambda j, i: (i, 0)),
            pl.BlockSpec((kr, tn), lambda j, i: (0, j)),
            pl.BlockSpec((kc, tn), lambda j, i: (0, j)),
            pl.BlockSpec((tm, tn), lambda j, i: (i, REST_GRET // tn + j)),
            pl.BlockSpec((tm, tn), lambda j, i: (i, REST_GCONV // tn + j)),
        ],
        out_specs=pl.BlockSpec((tm, tn), lambda j, i: (i, j)),
        out_shape=jax.ShapeDtypeStruct((m, n), BF16),
        compiler_params=_cparams(("arbitrary", "arbitrary")),
        name="merge",
    )(og, sw, w_ret_o, w_conv_o, rest, rest)


def _outproj_kernel(a_ref, w_ref, x_ref, *rest):
    o_ref, wb_ref = rest[-2], rest[-1]

    @pl.when(pl.program_id(1) == 0)
    def _():
        wb_ref[...] = w_ref[...].astype(BF16)

    o_ref[...] = x_ref[...] + jnp.dot(a_ref[...], wb_ref[...], preferred_element_type=F32)


def _outproj_residual(a, w, x, total_rows, row0, prev=None, tm=1024, tn=1024):
    m, k = a.shape
    n = w.shape[1]
    tm = min(tm, m)
    assert row0 % tm == 0
    off = row0 // tm
    in_specs = [
        pl.BlockSpec((tm, k), lambda j, i: (i, 0)),
        pl.BlockSpec((k, tn), lambda j, i: (0, j)),
        pl.BlockSpec((tm, tn), lambda j, i: (i, j)),
    ]
    args = [a, w, x]
    aliases = {}
    if prev is not None:
        in_specs.append(pl.BlockSpec(memory_space=pl.ANY))
        args.append(prev)
        aliases = {3: 0}
    return pl.pallas_call(
        _outproj_kernel,
        grid=(n // tn, m // tm),
        in_specs=in_specs,
        out_specs=pl.BlockSpec((tm, tn), lambda j, i: (i + off, j)),
        out_shape=jax.ShapeDtypeStruct((total_rows, n), F32),
        scratch_shapes=[pltpu.VMEM((k, tn), BF16)],
        input_output_aliases=aliases,
        compiler_params=_cparams(("arbitrary", "arbitrary")),
        name="outproj_residual",
    )(*args)


def _router_kernel(x_ref, nw_ref, w_ref, b_ref, h_ref, meta_ref, meta_t_ref, cnt_ref, base_ref):
    tm = x_ref.shape[0]

    @pl.when(pl.program_id(0) == 0)
    def _():
        base_ref[...] = jnp.zeros_like(base_ref)

    x = x_ref[...]
    ms = jnp.mean(x * x, axis=-1, keepdims=True)
    h = x * lax.rsqrt(ms + EPS) * nw_ref[...]
    h_ref[...] = h
    w = w_ref[...]
    h_hi = h.astype(BF16)
    h_lo = (h - h_hi.astype(F32)).astype(BF16)
    w_hi = w.astype(BF16)
    w_lo = (w - w_hi.astype(F32)).astype(BF16)
    logits = (jnp.dot(h_hi, w_lo, preferred_element_type=F32) + jnp.dot(h_lo, w_hi, preferred_element_type=F32)
              + jnp.dot(h_hi, w_hi, preferred_element_type=F32)) + b_ref[...]
    lane = lax.broadcasted_iota(I32, (tm, ROUTER_COLS), 1)

    def first_argmax(vals):
        top = jnp.max(vals, axis=1, keepdims=True)
        return top, jnp.min(jnp.where(vals == top, lane, ROUTER_COLS), axis=1, keepdims=True)

    cmask = lane < N_GROUPS
    lc = jnp.where(cmask, logits, NEG)
    mc, grp = first_argmax(lc)
    p_grp = 1.0 / jnp.sum(jnp.where(cmask, jnp.exp(lc - mc), 0.0), axis=1, keepdims=True)
    lo = ROUTER_OFF + grp * EXPERTS_PER_GROUP
    lf = jnp.where((lane >= lo) & (lane < lo + EXPERTS_PER_GROUP), logits, NEG)
    v1, i1 = first_argmax(lf)
    v2, i2 = first_argmax(jnp.where(lane == i1, NEG, lf))
    e2 = jnp.exp(v2 - v1)
    den = 1.0 + e2
    g1 = p_grp * (1.0 / den)
    g2 = p_grp * (e2 / den)

    oh1 = lane == i1
    oh2 = lane == i2
    both = jnp.where(oh1 | oh2, 1.0, 0.0)
    row = lax.broadcasted_iota(I32, (tm, tm), 0)
    col = lax.broadcasted_iota(I32, (tm, tm), 1)
    earlier = jnp.where(col < row, 1.0, 0.0).astype(BF16)
    before = jnp.dot(earlier, both.astype(BF16), preferred_element_type=F32) + base_ref[...]
    r1 = jnp.sum(jnp.where(oh1, before, 0.0), axis=1, keepdims=True)
    r2 = jnp.sum(jnp.where(oh2, before, 0.0), axis=1, keepdims=True)
    base_ref[...] = base_ref[...] + jnp.sum(both, axis=0, keepdims=True)
    cnt_ref[...] = base_ref[...]

    fields = [(i1 - ROUTER_OFF).astype(F32), (i2 - ROUTER_OFF).astype(F32), g1, g2, r1, r2]
    meta = jnp.zeros((tm, ROUTER_COLS), F32)
    for k, f in enumerate(fields):
        meta = jnp.where(lane == k, f, meta)
    meta_ref[...] = meta
    meta_t_ref[...] = meta.T[:SUBLANES]


def _router(x, norm_w, w_router, b_router, tm=512):
    n, d = x.shape
    tm = min(tm, n)
    const2 = lambda i: (0, 0)
    return pl.pallas_call(
        _router_kernel,
        grid=(n // tm,),
        in_specs=[
            pl.BlockSpec((tm, d), lambda i: (i, 0)),
            pl.BlockSpec((1, d), const2),
            pl.BlockSpec((d, ROUTER_COLS), const2),
            pl.BlockSpec((1, ROUTER_COLS), const2),
        ],
        out_specs=[
            pl.BlockSpec((tm, d), lambda i: (i, 0)),
            pl.BlockSpec((tm, ROUTER_COLS), lambda i: (i, 0)),
            pl.BlockSpec((SUBLANES, tm), lambda i: (0, i)),
            pl.BlockSpec((1, ROUTER_COLS), const2),
        ],
        out_shape=[
            jax.ShapeDtypeStruct((n, d), F32),
            jax.ShapeDtypeStruct((n, ROUTER_COLS), F32),
            jax.ShapeDtypeStruct((SUBLANES, n), F32),
            jax.ShapeDtypeStruct((1, ROUTER_COLS), F32),
        ],
        scratch_shapes=[pltpu.VMEM((1, ROUTER_COLS), F32)],
        compiler_params=_cparams(("arbitrary",)),
        name="router",
    )(x, norm_w.reshape(1, d), w_router, b_router)


def _dispatch_kernel(rows_ref, span_ref, h_ref, xs_hbm, zero_ref, sem, zsem):
    td = h_ref.shape[0]
    i = pl.program_id(0)

    @pl.when(i == 0)
    def _():
        zero_ref[...] = jnp.zeros_like(zero_ref)

        def fill_copy(e):
            last = pl.multiple_of(span_ref[N_EXPERTS + e] - MOE_ROWS, MOE_ROWS)
            return pltpu.make_async_copy(zero_ref, xs_hbm.at[pl.ds(last, MOE_ROWS)], zsem)

        def nonempty(e):
            return span_ref[N_EXPERTS + e] > span_ref[e]

        def start(e, carry):
            @pl.when(nonempty(e))
            def _():
                fill_copy(e).start()
            return carry

        def wait(e, carry):
            @pl.when(nonempty(e))
            def _():
                fill_copy(e).wait()
            return carry

        lax.fori_loop(0, N_EXPERTS, start, 0)
        lax.fori_loop(0, N_EXPERTS, wait, 0)

    n_tok = pl.num_programs(0) * td

    def tokens(g, carry):
        t0 = pl.multiple_of(g * SUBLANES, SUBLANES)
        for u in range(SUBLANES):
            for k in range(TOP_K_FINE):
                r = rows_ref[k * n_tok + i * td + t0 + u]
                pltpu.make_async_copy(h_ref.at[pl.ds(t0 + u, 1)], xs_hbm.at[pl.ds(r, 1)], sem).start()
        return carry

    lax.fori_loop(0, td // SUBLANES, tokens, 0)
    for _ in range(TOP_K_FINE):
        pltpu.make_async_copy(h_ref, xs_hbm.at[pl.ds(0, td)], sem).wait()


def _dispatch(rows_flat, span, h, n_rows, td=1024):
    n_tok, d = h.shape
    td = min(td, n_tok)
    return pl.pallas_call(
        _dispatch_kernel,
        grid_spec=pltpu.PrefetchScalarGridSpec(
            num_scalar_prefetch=2,
            grid=(n_tok // td,),
            in_specs=[pl.BlockSpec((td, d), lambda i, rows, span: (i, 0))],
            out_specs=pl.BlockSpec(memory_space=pl.ANY),
            scratch_shapes=[pltpu.VMEM((MOE_ROWS, d), F32), pltpu.SemaphoreType.DMA(()),
                            pltpu.SemaphoreType.DMA(())],
        ),
        out_shape=jax.ShapeDtypeStruct((n_rows, d), F32),
        compiler_params=_cparams(("arbitrary",)),
        name="moe_dispatch",
    )(rows_flat, span, h)


def _expert_kernel(be_ref, nh_ref, nu_ref, x_ref, wg_ref, wu_ref, wd_ref, o_ref):
    del be_ref
    nh = nh_ref[pl.program_id(0)]

    def ffn(rows):
        xb = x_ref[0:rows, :].astype(BF16)
        a = jnp.dot(xb, wg_ref[0], preferred_element_type=F32)
        b = jnp.dot(xb, wu_ref[0], preferred_element_type=F32)
        hb = (jax.nn.silu(a) * b).astype(BF16)
        o_ref[0:rows, :] = jnp.dot(hb, wd_ref[0], preferred_element_type=F32)

    for parts in range(1, MOE_PARTS + 1):
        @pl.when(nh == parts)
        def _():
            ffn(parts * MOE_ROWS)
            if parts < MOE_PARTS:
                o_ref[parts * MOE_ROWS:, :] = jnp.zeros((MOE_BLOCK - parts * MOE_ROWS, o_ref.shape[1]), F32)


def _experts(blk_exp, blk_parts, n_used, xs, wg, wu, wd):
    n_rows, d = xs.shape
    n_blocks = n_rows // MOE_BLOCK
    de = wg.shape[2]
    row_map = lambda i, be, nh, nu: (jnp.minimum(i, jnp.maximum(nu[0] - 1, 0)), 0)
    w_map = lambda i, be, nh, nu: (be[i], 0, 0)
    return pl.pallas_call(
        _expert_kernel,
        grid_spec=pltpu.PrefetchScalarGridSpec(
            num_scalar_prefetch=3,
            grid=(n_blocks,),
            in_specs=[
                pl.BlockSpec((MOE_BLOCK, d), row_map),
                pl.BlockSpec((1, d, de), w_map),
                pl.BlockSpec((1, d, de), w_map),
                pl.BlockSpec((1, de, d), w_map),
            ],
            out_specs=pl.BlockSpec((MOE_BLOCK, d), row_map),
        ),
        out_shape=jax.ShapeDtypeStruct((n_rows, d), F32),
        compiler_params=_cparams(("arbitrary",)),
        name="moe_experts",
    )(blk_exp, blk_parts, n_used, xs, wg, wu, wd)


def _combine_kernel(rows_ref, ys_hbm, x_ref, meta_ref, nf_ref, op_ref, os_ref, buf_ref, sem, *, p_tiles):
    tm = x_ref.shape[0]
    i = pl.program_id(0)
    nt = pl.num_programs(0)
    n_tok = nt * tm
    slot = i % 2

    def gather(tile, dst_slot):
        def tokens(g, carry):
            t0 = pl.multiple_of(g * SUBLANES, SUBLANES)
            for u in range(SUBLANES):
                for k in range(TOP_K_FINE):
                    r = rows_ref[k * n_tok + tile * tm + t0 + u]
                    pltpu.make_async_copy(ys_hbm.at[pl.ds(r, 1)], buf_ref.at[dst_slot, k, pl.ds(t0 + u, 1)],
                                          sem.at[dst_slot]).start()
            return carry

        lax.fori_loop(0, tm // SUBLANES, tokens, 0)

    @pl.when(i == 0)
    def _():
        gather(0, 0)

    @pl.when(i + 1 < nt)
    def _():
        gather(i + 1, 1 - slot)

    for k in range(TOP_K_FINE):
        pltpu.make_async_copy(ys_hbm.at[pl.ds(0, tm)], buf_ref.at[slot, k], sem.at[slot]).wait()
    meta = meta_ref[...]
    y = x_ref[...] + (buf_ref[slot, 0] * meta[:, 2:3] + buf_ref[slot, 1] * meta[:, 3:4])
    ms = jnp.mean(y * y, axis=-1, keepdims=True)
    out = y * lax.rsqrt(ms + EPS) * nf_ref[...]

    @pl.when(i < p_tiles)
    def _():
        op_ref[...] = out

    @pl.when(i >= p_tiles)
    def _():
        os_ref[...] = out


def _combine(rows_flat, ys, x, meta, norm_f, n_p, tm=256):
    n, d = x.shape
    n_s = n - n_p
    tm = min(tm, n_p, n_s)
    assert n_p % tm == 0 and n_s % tm == 0
    p_tiles = n_p // tm
    return pl.pallas_call(
        functools.partial(_combine_kernel, p_tiles=p_tiles),
        grid_spec=pltpu.PrefetchScalarGridSpec(
            num_scalar_prefetch=1,
            grid=(n // tm,),
            in_specs=[
                pl.BlockSpec(memory_space=pl.ANY),
                pl.BlockSpec((tm, d), lambda i, rows: (i, 0)),
                pl.BlockSpec((tm, ROUTER_COLS), lambda i, rows: (i, 0)),
                pl.BlockSpec((1, d), lambda i, rows: (0, 0)),
            ],
            out_specs=[
                pl.BlockSpec((tm, d), lambda i, rows: (jnp.minimum(i, p_tiles - 1), 0)),
                pl.BlockSpec((tm, d), lambda i, rows: (jnp.maximum(i - p_tiles, 0), 0)),
            ],
            scratch_shapes=[pltpu.VMEM((2, TOP_K_FINE, tm, d), F32), pltpu.SemaphoreType.DMA((2,))],
        ),
        out_shape=[jax.ShapeDtypeStruct((n_p, d), F32), jax.ShapeDtypeStruct((n_s, d), F32)],
        compiler_params=_cparams(("arbitrary",)),
        name="moe_combine",
    )(rows_flat, ys, x, meta, norm_f.reshape(1, d))


def _moe_and_final_norm(x, n_p, norm_ffn, w_coarse, b_coarse, w_fine, b_fine, wg, wu, wd, norm_f):
    n_tok, d = x.shape
    pad = ROUTER_COLS - N_GROUPS - N_EXPERTS
    w_router = jnp.concatenate([w_coarse, w_fine, jnp.zeros((d, pad), F32)], axis=1)
    b_router = jnp.concatenate([b_coarse, b_fine, jnp.zeros((pad,), F32)]).reshape(1, ROUTER_COLS)
    h, meta, meta_t, counts = _router(x, norm_ffn, w_router, b_router)

    n_asg = n_tok * TOP_K_FINE
    n_blocks = -(-n_asg // MOE_BLOCK) + N_EXPERTS
    n_rows = n_blocks * MOE_BLOCK
    cnt = counts[0, ROUTER_OFF:ROUTER_OFF + N_EXPERTS].astype(I32)
    nblk = (cnt + MOE_BLOCK - 1) // MOE_BLOCK
    padded = nblk * MOE_BLOCK
    pend = jnp.cumsum(padded)
    pstart = pend - padded
    first = cnt - (nblk - 1) * MOE_BLOCK
    first_rows = jnp.where(cnt > 0, (first + MOE_ROWS - 1) // MOE_ROWS * MOE_ROWS, 0)
    expert = meta_t[0:TOP_K_FINE].astype(I32)
    rank = meta_t[4:4 + TOP_K_FINE].astype(I32)
    rows = rank
    for e in range(N_EXPERTS):
        rows = rows + jnp.where(expert == e, pstart[e] + jnp.where(rank >= first[e], MOE_BLOCK - first[e], 0), 0)
    rows_flat = rows.reshape(-1)
    n_used = (pend[-1] // MOE_BLOCK).astype(I32)
    blk = jnp.arange(n_blocks, dtype=I32)
    blk_exp = jnp.searchsorted(pend, jnp.minimum(blk, n_used - 1) * MOE_BLOCK, side='right').astype(I32)
    blk_exp = jnp.minimum(blk_exp, N_EXPERTS - 1)
    blk_parts = jnp.where(blk * MOE_BLOCK == pstart[blk_exp], first_rows[blk_exp] // MOE_ROWS, MOE_PARTS)
    blk_parts = jnp.where(blk < n_used, blk_parts, 0).astype(I32)
    cend = pstart + first_rows

    xs = _dispatch(rows_flat, jnp.concatenate([pstart, cend]).astype(I32), h, n_rows)
    ys = _experts(blk_exp, blk_parts, n_used.reshape(1), xs, wg, wu, wd)
    return _combine(rows_flat, ys, x, meta, norm_f, n_p)


def _mixer(x, s_ret, s_conv, pos0, norm_mix, w_in, conv_w, conv_b, conv_ln_w, conv_ln_b, w_o,
           total_rows, row0, prev, merge_w=None, side_rope=(), side_rest=(), side_conv=()):
    b, l, d = x.shape
    x2 = x.reshape(b * l, d)
    chunk = RET_CHUNK if l % RET_CHUNK == 0 else l
    h = _rmsnorm_cast(x2, norm_mix)
    qk, kk, cast_a = _inproj_rope(h, w_in, *_rope_inputs(b, l, pos0, chunk), side=side_rope)
    v, _ = _inproj(h, w_in, [(COL_V, W_V)], BF16)
    u = _inproj_glu(h, w_in)
    rest, cast_b = _inproj(h, w_in, [(COL_G, W_G), (COL_GATES, W_REST - W_G)], F32, side_rest)
    if s_ret is None:
        og, s_ret_new, _ = _retention_prompt(qk, kk, v, rest, b, l)
        sw, s_conv_new, cast_c = _conv_branch(u, s_conv, conv_w, conv_b, conv_ln_w, conv_ln_b, b, l,
                                              tl=min(128, l), nb=1, side=side_conv)
    else:
        og, s_ret_new = _retention_sample(qk, kk, v, rest, s_ret, b, l)
        sw, s_conv_new, cast_c = _conv_branch(u, s_conv, conv_w, conv_b, conv_ln_w, conv_ln_b, b, l,
                                              tl=l, nb=min(SUBLANES, b), side=side_conv)
    if merge_w is None:
        merge_w = cast_a[:2]
    merged = _merge(og, sw, *merge_w, rest)
    x1 = _outproj_residual(merged, w_o, x2, total_rows, row0, prev)
    return x1, s_ret_new, s_conv_new, cast_a + cast_b + cast_c


def kernel(x_prompt, x_sample, state_ret, state_conv, norm_mix, w_in, w_ret_o, conv_w, conv_b, conv_ln_w,
           conv_ln_b, w_conv_o, w_o, norm_ffn, w_coarse, b_coarse, w_fine, b_fine, w_gate, w_up, w_down, norm_f):
    depth = w_in.shape[0]
    assert depth == 1, "single-layer trunk"
    bp, lp, d = x_prompt.shape
    bs, ls, _ = x_sample.shape
    n_p = bp * lp
    n_s = bs * ls
    mix_w = (norm_mix[0], w_in[0], conv_w, conv_b, conv_ln_w, conv_ln_b, w_o[0])
    n_exp, _, d_exp = w_gate.shape[1:]
    flat = lambda w: w.reshape(n_exp * w.shape[2], w.shape[3])
    x1, ret_p, conv_p, (wr, wc, wg, wu, wd) = _mixer(
        x_prompt, None, None, 0, *mix_w, n_p + n_s, 0, None,
        side_rope=(w_ret_o[0], w_conv_o[0]), side_rest=(flat(w_gate),), side_conv=(flat(w_up), flat(w_down)))
    conv_hist = jnp.transpose(state_conv, (0, 2, 1, 3))
    x1, ret_s, conv_s, _ = _mixer(x_sample, state_ret[0], conv_hist, PAST_LEN, *mix_w, n_p + n_s, n_p, x1,
                                  merge_w=(wr, wc))
    yp, ys = _moe_and_final_norm(x1, n_p, norm_ffn[0], w_coarse[0], b_coarse[0], w_fine[0], b_fine[0],
                                 wg.reshape(n_exp, d, d_exp), wu.reshape(n_exp, d, d_exp),
                                 wd.reshape(n_exp, d_exp, d), norm_f)
    return (yp.reshape(bp, lp, d), ys.reshape(bs, ls, d), ret_p[None], conv_p, ret_s[None], conv_s)
```

```python
import functools

import jax
import jax.numpy as jnp
from jax import lax
from jax.experimental import pallas as pl
from jax.experimental.pallas import tpu as pltpu

F32 = jnp.float32
BF16 = jnp.bfloat16
I32 = jnp.int32

D_MODEL = 2048
PAST_LEN = 16384
RET_HEADS = 8
RET_DK = D_MODEL // RET_HEADS
RET_DV = 2 * D_MODEL // RET_HEADS
RET_HALF = RET_DK // 2
RET_CHUNK = 256
ROPE_BASE = 10000.0
CONV_WIDTH = 31
CONV_HIST = CONV_WIDTH - 1
N_GROUPS = 4
EXPERTS_PER_GROUP = 8
N_EXPERTS = N_GROUPS * EXPERTS_PER_GROUP
TOP_K_FINE = 2
D_EXPERT = D_MODEL // 2
EPS = 1e-6

W_QK = 2 * RET_HEADS * RET_DK
W_V = RET_HEADS * RET_DV
W_G = RET_HEADS * RET_DV
COL_QK = 0
COL_V = COL_QK + W_QK
COL_G = COL_V + W_V
COL_CA = COL_G + W_G
COL_CB = COL_CA + D_MODEL
COL_GATES = COL_CB + D_MODEL
REST_G = 0
REST_GRET = REST_G + W_G
REST_GCONV = REST_GRET + D_MODEL
W_REST = REST_GCONV + D_MODEL

LANES = 128
SUBLANES = 8
VMEM_LIMIT = 56 * 1024 * 1024
MOE_ROWS = 128
MOE_PARTS = 4
MOE_BLOCK = MOE_PARTS * MOE_ROWS
ROUTER_COLS = LANES
ROUTER_OFF = N_GROUPS
NEG = -3.0e38
ISSUE_UNROLL = 8


def _cparams(sem):
    return pltpu.CompilerParams(dimension_semantics=sem, vmem_limit_bytes=VMEM_LIMIT)


def _rmsnorm_kernel(x_ref, w_ref, o_ref):
    x = x_ref[...]
    ms = jnp.mean(x * x, axis=-1, keepdims=True)
    o_ref[...] = (x * lax.rsqrt(ms + EPS) * w_ref[...]).astype(o_ref.dtype)


def _rmsnorm_cast(x, w, tm=512):
    m, d = x.shape
    tm = min(tm, m)
    return pl.pallas_call(
        _rmsnorm_kernel,
        grid=(m // tm,),
        in_specs=[pl.BlockSpec((tm, d), lambda i: (i, 0)), pl.BlockSpec((1, d), lambda i: (0, 0))],
        out_specs=pl.BlockSpec((tm, d), lambda i: (i, 0)),
        out_shape=jax.ShapeDtypeStruct((m, d), BF16),
        compiler_params=_cparams(("arbitrary",)),
        name="rmsnorm_cast",
    )(x, w.reshape(1, d))


def _side_steps(total_steps):
    return 1 << (total_steps.bit_length() - 1)


def _side_specs(arrays, n_steps, linear_step):
    def spec(a):
        rows = a.shape[0] // n_steps
        assert rows * n_steps == a.shape[0] and rows % 16 == 0, (a.shape, n_steps)
        return pl.BlockSpec((rows, a.shape[1]), lambda *g: (jnp.minimum(linear_step(*g), n_steps - 1), 0))
    return [spec(a) for a in arrays]


def _side_cast(step, n_steps, srcs, dsts):
    def cast():
        for src, dst in zip(srcs, dsts):
            dst[...] = src[...].astype(dst.dtype)

    if n_steps is None:
        cast()
    else:
        pl.when(step < n_steps)(cast)


def _side_guard(n_steps, total_steps):
    return None if n_steps == total_steps else n_steps


def _inproj_kernel(*refs, n_side, side_steps):
    a_ref, w_ref = refs[:2]
    side_in = refs[2:2 + n_side]
    o_ref = refs[2 + n_side]
    side_out = refs[3 + n_side:3 + 2 * n_side]
    wb_ref = refs[-1]

    @pl.when(pl.program_id(1) == 0)
    def _():
        wb_ref[...] = w_ref[...].astype(BF16)

    o_ref[...] = jnp.dot(a_ref[...], wb_ref[...], preferred_element_type=F32).astype(o_ref.dtype)
    if n_side:
        _side_cast(pl.program_id(0) * pl.num_programs(1) + pl.program_id(1), side_steps, side_in, side_out)


def _inproj(a, w, col_spans, out_dtype, side=(), tm=1024, tn=1024):
    m, k = a.shape
    tm = min(tm, m)
    assert all(c0 % tn == 0 and n % tn == 0 for c0, n in col_spans)
    ncols = sum(n for _, n in col_spans)
    grid = (ncols // tn, m // tm)

    def w_tile(j):
        tile, first = 0, 0
        for c0, n in col_spans:
            tile = jnp.where(j >= first, c0 // tn + j - first, tile)
            first += n // tn
        return tile

    side_steps = _side_steps(grid[0] * grid[1])
    side_specs = _side_specs(side, side_steps, lambda j, i: j * grid[1] + i)
    outs = pl.pallas_call(
        functools.partial(_inproj_kernel, n_side=len(side),
                          side_steps=_side_guard(side_steps, grid[0] * grid[1])),
        grid=grid,
        in_specs=[pl.BlockSpec((tm, k), lambda j, i: (i, 0)),
                  pl.BlockSpec((k, tn), lambda j, i: (0, w_tile(j)))] + side_specs,
        out_specs=[pl.BlockSpec((tm, tn), lambda j, i: (i, j))] + side_specs,
        out_shape=[jax.ShapeDtypeStruct((m, ncols), out_dtype)]
                  + [jax.ShapeDtypeStruct(s.shape, BF16) for s in side],
        scratch_shapes=[pltpu.VMEM((k, tn), BF16)],
        compiler_params=_cparams(("arbitrary", "arbitrary")),
        name="inproj",
    )(a, w, *side)
    return outs[0], tuple(outs[1:])


def _inproj_glu_kernel(a_ref, wa_ref, wb_ref, o_ref, wab_ref, wbb_ref):
    @pl.when(pl.program_id(1) == 0)
    def _():
        wab_ref[...] = wa_ref[...].astype(BF16)
        wbb_ref[...] = wb_ref[...].astype(BF16)

    a = a_ref[...]
    ca = jnp.dot(a, wab_ref[...], preferred_element_type=F32)
    cb = jnp.dot(a, wbb_ref[...], preferred_element_type=F32)
    o_ref[...] = ca * jax.nn.sigmoid(cb)


def _inproj_glu(a, w, tm=1024, tn=512):
    m, k = a.shape
    tm = min(tm, m)
    n = D_MODEL
    assert COL_CA % tn == 0 and COL_CB % tn == 0 and n % tn == 0
    return pl.pallas_call(
        _inproj_glu_kernel,
        grid=(n // tn, m // tm),
        in_specs=[
            pl.BlockSpec((tm, k), lambda j, i: (i, 0)),
            pl.BlockSpec((k, tn), lambda j, i: (0, COL_CA // tn + j)),
            pl.BlockSpec((k, tn), lambda j, i: (0, COL_CB // tn + j)),
        ],
        out_specs=pl.BlockSpec((tm, tn), lambda j, i: (i, j)),
        out_shape=jax.ShapeDtypeStruct((m, n), F32),
        scratch_shapes=[pltpu.VMEM((k, tn), BF16), pltpu.VMEM((k, tn), BF16)],
        compiler_params=_cparams(("arbitrary", "arbitrary")),
        name="inproj_glu",
    )(a, w, w)


ROPE_ROWS = RET_CHUNK


def _inproj_rope_kernel(*refs, nq_tiles, n_side, side_steps):
    a_ref, w_ref, cos_ref, sin_ref, kd_ref = refs[:5]
    side_in = refs[5:5 + n_side]
    qk_ref, kk_ref = refs[5 + n_side:7 + n_side]
    side_out = refs[7 + n_side:7 + 2 * n_side]
    wb_ref, acc_ref = refs[-2:]
    j = pl.program_id(0)
    if n_side:
        _side_cast(j * pl.num_programs(1) + pl.program_id(1), side_steps, side_in, side_out)

    @pl.when(pl.program_id(1) == 0)
    def _():
        wb_ref[...] = w_ref[...].astype(BF16)

    acc_ref[...] = jnp.dot(a_ref[...], wb_ref[...], preferred_element_type=F32)
    tm, tn = acc_ref.shape
    scale = jnp.where(j >= nq_tiles, RET_DK ** -0.5, 1.0).astype(F32)
    for r in range(tm // ROPE_ROWS):
        rs = slice(r * ROPE_ROWS, (r + 1) * ROPE_ROWS)
        cos = cos_ref[rs, :]
        sin = sin_ref[rs, :]
        parts = []
        for hh in range(tn // RET_DK):
            x1 = acc_ref[rs, hh * RET_DK:hh * RET_DK + RET_HALF]
            x2 = acc_ref[rs, hh * RET_DK + RET_HALF:(hh + 1) * RET_DK]
            parts += [x1 * cos - x2 * sin, x1 * sin + x2 * cos]
        rot = jnp.concatenate(parts, axis=1) * scale
        qk_ref[rs, :] = rot.astype(qk_ref.dtype)

        @pl.when(j >= nq_tiles)
        def _():
            kk_ref[rs, :] = (rot * kd_ref[...]).astype(kk_ref.dtype)


def _inproj_rope(a, w, cos_rows, sin_rows, kd_rows, side=(), tm=1024, tn=1024):
    m, k = a.shape
    tm = min(tm, m)
    wq = W_QK // 2
    assert tm % ROPE_ROWS == 0 and wq % tn == 0 and tn % RET_DK == 0
    nq = wq // tn
    grid = (2 * nq, m // tm)
    side_steps = _side_steps(grid[0] * grid[1])
    side_specs = _side_specs(side, side_steps, lambda j, i: j * grid[1] + i)
    outs = pl.pallas_call(
        functools.partial(_inproj_rope_kernel, nq_tiles=nq, n_side=len(side),
                          side_steps=_side_guard(side_steps, grid[0] * grid[1])),
        grid=grid,
        in_specs=[
            pl.BlockSpec((tm, k), lambda j, i: (i, 0)),
            pl.BlockSpec((k, tn), lambda j, i: (0, COL_QK // tn + j)),
            pl.BlockSpec((tm, RET_HALF), lambda j, i: (i, 0)),
            pl.BlockSpec((tm, RET_HALF), lambda j, i: (i, 0)),
            pl.BlockSpec((ROPE_ROWS, tn), lambda j, i: (0, jnp.maximum(j - nq, 0))),
        ] + side_specs,
        out_specs=[
            pl.BlockSpec((tm, tn), lambda j, i: (i, j)),
            pl.BlockSpec((tm, tn), lambda j, i: (jnp.where(j >= nq, i, 0), jnp.maximum(j - nq, 0))),
        ] + side_specs,
        out_shape=[jax.ShapeDtypeStruct((m, W_QK), BF16), jax.ShapeDtypeStruct((m, wq), BF16)]
                  + [jax.ShapeDtypeStruct(s.shape, BF16) for s in side],
        scratch_shapes=[pltpu.VMEM((k, tn), BF16), pltpu.VMEM((tm, tn), F32)],
        compiler_params=_cparams(("arbitrary", "arbitrary")),
        name="inproj_rope",
    )(a, w, cos_rows, sin_rows, kd_rows, *side)
    return outs[0], outs[1], tuple(outs[2:])


def _retention_tables(c):
    h = RET_HEADS
    log_g = jnp.log1p(-jnp.exp2(-5.0 - jnp.arange(h, dtype=F32)))
    idx = jnp.arange(c, dtype=F32)
    diff = idx[:, None] - idx[None, :]
    decay = jnp.where(diff[None] >= 0.0, jnp.exp(jnp.maximum(diff, 0.0)[None] * log_g[:, None, None]), 0.0)
    q_dec = jnp.exp((idx[:, None] + 1.0) * log_g[None, :])
    k_dec = jnp.exp((c - 1.0 - idx)[:, None] * log_g[None, :])
    c_dec = jnp.exp(c * log_g)
    return decay, q_dec, k_dec, c_dec


def _rope_tables(pos):
    inv = ROPE_BASE ** (-jnp.arange(RET_HALF, dtype=F32) / RET_HALF)
    ang = pos.astype(F32)[:, None] * inv[None, :]
    return jnp.cos(ang), jnp.sin(ang)


def _lane_bcast(t):
    return jnp.broadcast_to(t.T[:, :, None], (t.shape[1], t.shape[0], LANES))


def _rope_inputs(b, l, pos0, c):
    cos, sin = _rope_tables(pos0 + jnp.arange(l))
    _, _, k_dec, _ = _retention_tables(c)
    kd = jnp.repeat(k_dec, RET_DK, axis=1)
    return jnp.tile(cos, (b, 1)), jnp.tile(sin, (b, 1)), jnp.tile(kd, (ROPE_ROWS // c, 1))


def _group_norm_gate(o, g):
    ms = jnp.mean(o * o, axis=-1, keepdims=True)
    return jax.nn.silu(g) * (o * lax.rsqrt(ms + EPS))


def _ret_prompt_kernel(*refs, n_side, side_steps):
    cdec_ref, q_ref, k_ref, kk_ref, v_ref, g_ref, dec_ref, qd_ref = refs[:8]
    side_in = refs[8:8 + n_side]
    o_ref, s_ref = refs[8 + n_side:10 + n_side]
    side_out = refs[10 + n_side:10 + 2 * n_side]
    if n_side:
        _side_cast(pl.program_id(0) * pl.num_programs(1) + pl.program_id(1), side_steps, side_in, side_out)

    @pl.when(pl.program_id(1) == 0)
    def _():
        s_ref[...] = jnp.zeros_like(s_ref)

    for h in range(RET_HEADS):
        qb = q_ref[:, h * RET_DK:(h + 1) * RET_DK]
        kb = k_ref[:, h * RET_DK:(h + 1) * RET_DK]
        kkb = kk_ref[:, h * RET_DK:(h + 1) * RET_DK]
        vb = v_ref[:, h * RET_DV:(h + 1) * RET_DV]
        att = lax.dot_general(qb, kb, (((1,), (1,)), ((), ())), preferred_element_type=F32) * dec_ref[h]
        s = s_ref[0, h]
        o = (jnp.dot(att.astype(BF16), vb, preferred_element_type=F32)
             + qd_ref[h][:, :1] * jnp.dot(qb, s.astype(BF16), preferred_element_type=F32))
        s_ref[0, h] = s * cdec_ref[h] + lax.dot_general(kkb, vb, (((0,), (0,)), ((), ())),
                                                         preferred_element_type=F32)
        og = _group_norm_gate(o, g_ref[:, h * RET_DV:(h + 1) * RET_DV])
        o_ref[:, h * RET_DV:(h + 1) * RET_DV] = og.astype(o_ref.dtype)


def _retention_prompt(qk, kk, v, rest, b, l, side=()):
    c = RET_CHUNK if l % RET_CHUNK == 0 else l
    nc = l // c
    decay, q_dec, _, c_dec = _retention_tables(c)
    wq = RET_HEADS * RET_DK
    wv = RET_HEADS * RET_DV
    full3 = lambda bi, ci: (0, 0, 0)
    side_steps = _side_steps(b * nc)
    side_specs = _side_specs(side, side_steps, lambda bi, ci: bi * nc + ci)
    outs = pl.pallas_call(
        functools.partial(_ret_prompt_kernel, n_side=len(side), side_steps=_side_guard(side_steps, b * nc)),
        grid=(b, nc),
        in_specs=[
            pl.BlockSpec(memory_space=pltpu.SMEM),
            pl.BlockSpec((c, wq), lambda bi, ci: (bi * nc + ci, 0)),
            pl.BlockSpec((c, wq), lambda bi, ci: (bi * nc + ci, 1)),
            pl.BlockSpec((c, wq), lambda bi, ci: (bi * nc + ci, 0)),
            pl.BlockSpec((c, wv), lambda bi, ci: (bi * nc + ci, 0)),
            pl.BlockSpec((c, wv), lambda bi, ci: (bi * nc + ci, REST_G // wv)),
            pl.BlockSpec((RET_HEADS, c, c), full3),
            pl.BlockSpec((RET_HEADS, c, LANES), full3),
        ] + side_specs,
        out_specs=[
            pl.BlockSpec((c, wv), lambda bi, ci: (bi * nc + ci, 0)),
            pl.BlockSpec((1, RET_HEADS, RET_DK, RET_DV), lambda bi, ci: (bi, 0, 0, 0)),
        ] + side_specs,
        out_shape=[
            jax.ShapeDtypeStruct((b * l, wv), BF16),
            jax.ShapeDtypeStruct((b, RET_HEADS, RET_DK, RET_DV), F32),
        ] + [jax.ShapeDtypeStruct(s.shape, BF16) for s in side],
        compiler_params=_cparams(("arbitrary", "arbitrary")),
        name="retention_prompt",
    )(c_dec, qk, qk, kk, v, rest, decay, _lane_bcast(q_dec), *side)
    return outs[0], outs[1], tuple(outs[2:])


SAMPLE_PAD = 16


def _pad_rows(x, rows):
    return jnp.concatenate([x, jnp.zeros((rows - x.shape[0], x.shape[1]), x.dtype)], axis=0)


def _ret_sample_kernel(cdec_ref, q_ref, k_ref, kk_ref, v_ref, g_ref, dec_ref, qd_ref, s0_ref, o_ref, s_ref,
                       *, nb, l):
    for h in range(RET_HEADS):
        qf = q_ref[:, h * RET_DK:(h + 1) * RET_DK].astype(F32)
        kf = k_ref[:, h * RET_DK:(h + 1) * RET_DK].astype(F32)
        kkf = kk_ref[:, h * RET_DK:(h + 1) * RET_DK].astype(F32)
        vf = v_ref[:, h * RET_DV:(h + 1) * RET_DV].astype(F32)
        for bt in range(nb):
            rs = slice(bt * l, (bt + 1) * l)
            qb = _pad_rows(qf[rs], SAMPLE_PAD).astype(BF16)
            kb = _pad_rows(kf[rs], SAMPLE_PAD).astype(BF16)
            kkb = _pad_rows(kkf[rs], SAMPLE_PAD).astype(BF16)
            vb = _pad_rows(vf[rs], SAMPLE_PAD).astype(BF16)
            att = lax.dot_general(qb, kb, (((1,), (1,)), ((), ())), preferred_element_type=F32) * dec_ref[h]
            s = s0_ref[bt, h]
            o = (jnp.dot(att.astype(BF16), vb, preferred_element_type=F32)
                 + qd_ref[h][:, :1] * jnp.dot(qb, s.astype(BF16), preferred_element_type=F32))
            s_ref[bt, h] = s * cdec_ref[h] + lax.dot_general(kkb, vb, (((0,), (0,)), ((), ())),
                                                              preferred_element_type=F32)
            og = _group_norm_gate(o[:l], g_ref[rs, h * RET_DV:(h + 1) * RET_DV])
            o_ref[rs, h * RET_DV:(h + 1) * RET_DV] = og.astype(o_ref.dtype)


def _retention_sample(qk, kk, v, rest, s0, b, l, nb=2):
    assert l % RET_CHUNK != 0 and l <= SAMPLE_PAD and b % nb == 0, (b, l)
    decay, q_dec, _, c_dec = _retention_tables(l)
    p = SAMPLE_PAD - l
    decay = jnp.pad(decay, ((0, 0), (0, p), (0, p)))
    q_dec = jnp.pad(q_dec, ((0, p), (0, 0)))
    wq = RET_HEADS * RET_DK
    wv = RET_HEADS * RET_DV
    rows = nb * l
    full3 = lambda bi: (0, 0, 0)
    state_spec = pl.BlockSpec((nb, RET_HEADS, RET_DK, RET_DV), lambda bi: (bi, 0, 0, 0))
    return pl.pallas_call(
        functools.partial(_ret_sample_kernel, nb=nb, l=l),
        grid=(b // nb,),
        in_specs=[
            pl.BlockSpec(memory_space=pltpu.SMEM),
            pl.BlockSpec((rows, wq), lambda bi: (bi, 0)),
            pl.BlockSpec((rows, wq), lambda bi: (bi, 1)),
            pl.BlockSpec((rows, wq), lambda bi: (bi, 0)),
            pl.BlockSpec((rows, wv), lambda bi: (bi, 0)),
            pl.BlockSpec((rows, wv), lambda bi: (bi, REST_G // wv)),
            pl.BlockSpec((RET_HEADS, SAMPLE_PAD, SAMPLE_PAD), full3),
            pl.BlockSpec((RET_HEADS, SAMPLE_PAD, LANES), full3),
            state_spec,
        ],
        out_specs=[pl.BlockSpec((rows, wv), lambda bi: (bi, 0)), state_spec],
        out_shape=[
            jax.ShapeDtypeStruct((b * l, wv), BF16),
            jax.ShapeDtypeStruct((b, RET_HEADS, RET_DK, RET_DV), F32),
        ],
        compiler_params=_cparams(("arbitrary",)),
        name="retention_sample",
    )(c_dec, qk, qk, kk, v, rest, decay, _lane_bcast(q_dec), s0)


CONV_PAD = 32
CONV_COLS = LANES


def _conv_strip(block, wts, nblk, store):
    lead = CONV_PAD - CONV_HIST
    taps = [((lead + w) // SUBLANES, (lead + w) % SUBLANES) for w in range(CONV_WIDTH)]
    rowid = lax.broadcasted_iota(I32, (SUBLANES, CONV_COLS), 0)

    def partial_sums(m, shifts):
        ys = {}
        for w, (a, s) in enumerate(taps):
            if s in shifts:
                term = block(m + a) * wts[w]
                ys[s] = term if s not in ys else ys[s] + term
        return ys

    shifted = range(1, SUBLANES)
    prev = partial_sums(0, shifted)
    for m in range(nblk):
        nxt = partial_sums(m + 1, shifted)
        acc = partial_sums(m, (0,))[0]
        for s in shifted:
            acc = acc + pltpu.roll(jnp.where(rowid >= s, prev[s], nxt[s]), SUBLANES - s, axis=0)
        store(m, acc)
        prev = nxt


def _conv_kernel(*refs, tl, nb, nl, has_state, n_side, side_steps):
    u_ref = refs[0]
    buf_ref = refs[1] if has_state else None
    n_in = 2 if has_state else 1
    cw_ref, bias_ref, lnw_ref, lnb_ref = refs[n_in:n_in + 4]
    side_in = refs[n_in + 4:n_in + 4 + n_side]
    sw_ref, nbuf_ref = refs[n_in + 4 + n_side:n_in + 6 + n_side]
    side_out = refs[n_in + 6 + n_side:n_in + 6 + 2 * n_side]
    upad_ref, c_ref = refs[-2:]
    if n_side:
        _side_cast(pl.program_id(0) * nl + pl.program_id(1), side_steps, side_in, side_out)
    li = pl.program_id(1)
    lead = CONV_PAD - CONV_HIST
    d = u_ref.shape[1]
    for bt in range(nb):
        @pl.when(li == 0)
        def _():
            if has_state:
                upad_ref[bt, 0:lead, :] = jnp.zeros((lead, d), F32)
                upad_ref[bt, lead:CONV_PAD, :] = buf_ref[0, :, bt, :]
            else:
                upad_ref[bt, 0:CONV_PAD, :] = jnp.zeros((CONV_PAD, d), F32)

        r0 = bt * tl
        upad_ref[bt, CONV_PAD:CONV_PAD + tl, :] = u_ref[r0:r0 + tl, :]
        for cc in range(d // CONV_COLS):
            cs = slice(cc * CONV_COLS, (cc + 1) * CONV_COLS)
            wts = [jnp.broadcast_to(cw_ref[0, w:w + 1, cs], (SUBLANES, CONV_COLS)) for w in range(CONV_WIDTH)]
            blocks = {}

            def block(j, cs=cs, blocks=blocks):
                if j not in blocks:
                    blocks[j] = upad_ref[bt, j * SUBLANES:(j + 1) * SUBLANES, cs]
                return blocks[j]

            def store(m, acc, cs=cs):
                c_ref[m * SUBLANES:(m + 1) * SUBLANES, cs] = acc

            _conv_strip(block, wts, tl // SUBLANES, store)
        cf = c_ref[...] + bias_ref[...]
        mu = jnp.mean(cf, axis=-1, keepdims=True)
        var = jnp.mean(jnp.square(cf - mu), axis=-1, keepdims=True)
        cn = (cf - mu) * lax.rsqrt(var + EPS) * lnw_ref[...] + lnb_ref[...]
        sw_ref[r0:r0 + tl, :] = jax.nn.silu(cn).astype(sw_ref.dtype)

        @pl.when(li == nl - 1)
        def _():
            nbuf_ref[0, bt] = upad_ref[bt, tl + lead:tl + CONV_PAD, :]

        if nl > 1:
            upad_ref[bt, 0:CONV_PAD, :] = upad_ref[bt, tl:tl + CONV_PAD, :]


def _conv_branch(u, buf_t, conv_w, conv_b, ln_w, ln_b, b, l, tl, nb, side=()):
    d = D_MODEL
    nl = l // tl
    assert nb == 1 or nl == 1
    rows = nb * tl
    const2 = lambda bi, li: (0, 0)
    grid = (b // nb, nl)
    side_steps = _side_steps(grid[0] * grid[1])
    side_specs = _side_specs(side, side_steps, lambda bi, li: bi * nl + li)
    state_spec = pl.BlockSpec((1, nb, CONV_HIST, d), lambda bi, li: (0, bi, 0, 0))
    has_state = buf_t is not None
    state_in = [pl.BlockSpec((1, CONV_HIST, nb, d), lambda bi, li: (0, 0, bi, 0))] if has_state else []
    outs = pl.pallas_call(
        functools.partial(_conv_kernel, tl=tl, nb=nb, nl=nl, has_state=has_state, n_side=len(side),
                          side_steps=_side_guard(side_steps, grid[0] * grid[1])),
        grid=grid,
        in_specs=[
            pl.BlockSpec((rows, d), lambda bi, li: (bi * nl + li, 0)),
        ] + state_in + [
            pl.BlockSpec((1, CONV_WIDTH, d), lambda bi, li: (0, 0, 0)),
            pl.BlockSpec((1, d), const2),
            pl.BlockSpec((1, d), const2),
            pl.BlockSpec((1, d), const2),
        ] + side_specs,
        out_specs=[pl.BlockSpec((rows, d), lambda bi, li: (bi * nl + li, 0)), state_spec] + side_specs,
        out_shape=[
            jax.ShapeDtypeStruct((b * l, d), BF16),
            jax.ShapeDtypeStruct((1, b, CONV_HIST, d), F32),
        ] + [jax.ShapeDtypeStruct(s.shape, BF16) for s in side],
        scratch_shapes=[pltpu.VMEM((nb, CONV_PAD + tl, d), F32), pltpu.VMEM((tl, d), F32)],
        compiler_params=_cparams(("arbitrary", "arbitrary")),
        name="conv_branch",
    )(u, *([buf_t] if has_state else []), conv_w, conv_b, ln_w, ln_b, *side)
    return outs[0], outs[1], tuple(outs[2:])


def _merge_kernel(og_ref, sw_ref, wr_ref, wc_ref, gr_ref, gc_ref, o_ref):
    ret_out = jnp.dot(og_ref[...], wr_ref[...], preferred_element_type=F32)
    conv_out = jnp.dot(sw_ref[...], wc_ref[...], preferred_element_type=F32)
    merged = jax.nn.sigmoid(gr_ref[...]) * ret_out + jax.nn.sigmoid(gc_ref[...]) * conv_out
    o_ref[...] = merged.astype(o_ref.dtype)


def _merge(og, sw, w_ret_o, w_conv_o, rest, tm=512, tn=1024):
    m = og.shape[0]
    n = w_ret_o.shape[1]
    tm = min(tm, m)
    kr = og.shape[1]
    kc = sw.shape[1]
    return pl.pallas_call(
        _merge_kernel,
        grid=(n // tn, m // tm),
        in_specs=[
            pl.BlockSpec((tm, kr), lambda j, i: (i, 0)),
            pl.BlockSpec((tm, kc), lambda j, i: (i, 0)),
            pl.BlockSpec((kr, tn), lambda j, i: (0, j)),
            pl.BlockSpec((kc, tn), lambda j, i: (0, j)),
            pl.BlockSpec((tm, tn), lambda j, i: (i, REST_GRET // tn + j)),
            pl.BlockSpec((tm, tn), lambda j, i: (i, REST_GCONV // tn + j)),
        ],
        out_specs=pl.BlockSpec((tm, tn), lambda j, i: (i, j)),
        out_shape=jax.ShapeDtypeStruct((m, n), BF16),
        compiler_params=_cparams(("arbitrary", "arbitrary")),
        name="merge",
    )(og, sw, w_ret_o, w_conv_o, rest, rest)


def _outproj_kernel(a_ref, w_ref, x_ref, *rest):
    o_ref, wb_ref = rest[-2], rest[-1]

    @pl.when(pl.program_id(1) == 0)
    def _():
        wb_ref[...] = w_ref[...].astype(BF16)

    o_ref[...] = x_ref[...] + jnp.dot(a_ref[...], wb_ref[...], preferred_element_type=F32)


def _outproj_residual(a, w, x, total_rows, row0, prev=None, tm=1024, tn=1024):
    m, k = a.shape
    n = w.shape[1]
    tm = min(tm, m)
    assert row0 % tm == 0
    off = row0 // tm
    in_specs = [
        pl.BlockSpec((tm, k), lambda j, i: (i, 0)),
        pl.BlockSpec((k, tn), lambda j, i: (0, j)),
        pl.BlockSpec((tm, tn), lambda j, i: (i, j)),
    ]
    args = [a, w, x]
    aliases = {}
    if prev is not None:
        in_specs.append(pl.BlockSpec(memory_space=pl.ANY))
        args.append(prev)
        aliases = {3: 0}
    return pl.pallas_call(
        _outproj_kernel,
        grid=(n // tn, m // tm),
        in_specs=in_specs,
        out_specs=pl.BlockSpec((tm, tn), lambda j, i: (i + off, j)),
        out_shape=jax.ShapeDtypeStruct((total_rows, n), F32),
        scratch_shapes=[pltpu.VMEM((k, tn), BF16)],
        input_output_aliases=aliases,
        compiler_params=_cparams(("arbitrary", "arbitrary")),
        name="outproj_residual",
    )(*args)


def _router_kernel(x_ref, nw_ref, w_ref, b_ref, h_ref, meta_ref, meta_t_ref, cnt_ref, base_ref):
    tm = x_ref.shape[0]

    @pl.when(pl.program_id(0) == 0)
    def _():
        base_ref[...] = jnp.zeros_like(base_ref)

    x = x_ref[...]
    ms = jnp.mean(x * x, axis=-1, keepdims=True)
    h = x * lax.rsqrt(ms + EPS) * nw_ref[...]
    h_ref[...] = h
    w = w_ref[...]
    h_hi = h.astype(BF16)
    h_lo = (h - h_hi.astype(F32)).astype(BF16)
    w_hi = w.astype(BF16)
    w_lo = (w - w_hi.astype(F32)).astype(BF16)
    logits = (jnp.dot(h_hi, w_lo, preferred_element_type=F32) + jnp.dot(h_lo, w_hi, preferred_element_type=F32)
              + jnp.dot(h_hi, w_hi, preferred_element_type=F32)) + b_ref[...]
    lane = lax.broadcasted_iota(I32, (tm, ROUTER_COLS), 1)

    def first_argmax(vals):
        top = jnp.max(vals, axis=1, keepdims=True)
        return top, jnp.min(jnp.where(vals == top, lane, ROUTER_COLS), axis=1, keepdims=True)

    cmask = lane < N_GROUPS
    lc = jnp.where(cmask, logits, NEG)
    mc, grp = first_argmax(lc)
    p_grp = 1.0 / jnp.sum(jnp.where(cmask, jnp.exp(lc - mc), 0.0), axis=1, keepdims=True)
    lo = ROUTER_OFF + grp * EXPERTS_PER_GROUP
    lf = jnp.where((lane >= lo) & (lane < lo + EXPERTS_PER_GROUP), logits, NEG)
    v1, i1 = first_argmax(lf)
    v2, i2 = first_argmax(jnp.where(lane == i1, NEG, lf))
    e2 = jnp.exp(v2 - v1)
    den = 1.0 + e2
    g1 = p_grp * (1.0 / den)
    g2 = p_grp * (e2 / den)

    oh1 = lane == i1
    oh2 = lane == i2
    both = jnp.where(oh1 | oh2, 1.0, 0.0)
    row = lax.broadcasted_iota(I32, (tm, tm), 0)
    col = lax.broadcasted_iota(I32, (tm, tm), 1)
    earlier = jnp.where(col < row, 1.0, 0.0).astype(BF16)
    before = jnp.dot(earlier, both.astype(BF16), preferred_element_type=F32) + base_ref[...]
    r1 = jnp.sum(jnp.where(oh1, before, 0.0), axis=1, keepdims=True)
    r2 = jnp.sum(jnp.where(oh2, before, 0.0), axis=1, keepdims=True)
    base_ref[...] = base_ref[...] + jnp.sum(both, axis=0, keepdims=True)
    cnt_ref[...] = base_ref[...]

    fields = [(i1 - ROUTER_OFF).astype(F32), (i2 - ROUTER_OFF).astype(F32), g1, g2, r1, r2]
    meta = jnp.zeros((tm, ROUTER_COLS), F32)
    for k, f in enumerate(fields):
        meta = jnp.where(lane == k, f, meta)
    meta_ref[...] = meta
    meta_t_ref[...] = meta.T[:SUBLANES]


def _router(x, norm_w, w_router, b_router, tm=512):
    n, d = x.shape
    tm = min(tm, n)
    const2 = lambda i: (0, 0)
    return pl.pallas_call(
        _router_kernel,
        grid=(n // tm,),
        in_specs=[
            pl.BlockSpec((tm, d), lambda i: (i, 0)),
            pl.BlockSpec((1, d), const2),
            pl.BlockSpec((d, ROUTER_COLS), const2),
            pl.BlockSpec((1, ROUTER_COLS), const2),
        ],
        out_specs=[
            pl.BlockSpec((tm, d), lambda i: (i, 0)),
            pl.BlockSpec((tm, ROUTER_COLS), lambda i: (i, 0)),
            pl.BlockSpec((SUBLANES, tm), lambda i: (0, i)),
            pl.BlockSpec((1, ROUTER_COLS), const2),
        ],
        out_shape=[
            jax.ShapeDtypeStruct((n, d), F32),
            jax.ShapeDtypeStruct((n, ROUTER_COLS), F32),
            jax.ShapeDtypeStruct((SUBLANES, n), F32),
            jax.ShapeDtypeStruct((1, ROUTER_COLS), F32),
        ],
        scratch_shapes=[pltpu.VMEM((1, ROUTER_COLS), F32)],
        compiler_params=_cparams(("arbitrary",)),
        name="router",
    )(x, norm_w.reshape(1, d), w_router, b_router)


def _dispatch_kernel(rows_ref, span_ref, h_ref, xs_hbm, zero_ref, sem, zsem):
    td = h_ref.shape[0]
    i = pl.program_id(0)

    @pl.when(i == 0)
    def _():
        zero_ref[...] = jnp.zeros_like(zero_ref)

        def fill_copy(e):
            last = pl.multiple_of(span_ref[N_EXPERTS + e] - MOE_ROWS, MOE_ROWS)
            return pltpu.make_async_copy(zero_ref, xs_hbm.at[pl.ds(last, MOE_ROWS)], zsem)

        def nonempty(e):
            return span_ref[N_EXPERTS + e] > span_ref[e]

        def start(e, carry):
            @pl.when(nonempty(e))
            def _():
                fill_copy(e).start()
            return carry

        def wait(e, carry):
            @pl.when(nonempty(e))
            def _():
                fill_copy(e).wait()
            return carry

        lax.fori_loop(0, N_EXPERTS, start, 0)
        lax.fori_loop(0, N_EXPERTS, wait, 0)

    n_tok = pl.num_programs(0) * td

    def tokens(g, carry):
        t0 = pl.multiple_of(g * SUBLANES, SUBLANES)
        for u in range(SUBLANES):
            for k in range(TOP_K_FINE):
                r = rows_ref[k * n_tok + i * td + t0 + u]
                pltpu.make_async_copy(h_ref.at[pl.ds(t0 + u, 1)], xs_hbm.at[pl.ds(r, 1)], sem).start()
        return carry

    lax.fori_loop(0, td // SUBLANES, tokens, 0)
    for _ in range(TOP_K_FINE):
        pltpu.make_async_copy(h_ref, xs_hbm.at[pl.ds(0, td)], sem).wait()


def _dispatch(rows_flat, span, h, n_rows, td=1024):
    n_tok, d = h.shape
    td = min(td, n_tok)
    return pl.pallas_call(
        _dispatch_kernel,
        grid_spec=pltpu.PrefetchScalarGridSpec(
            num_scalar_prefetch=2,
            grid=(n_tok // td,),
            in_specs=[pl.BlockSpec((td, d), lambda i, rows, span: (i, 0))],
            out_specs=pl.BlockSpec(memory_space=pl.ANY),
            scratch_shapes=[pltpu.VMEM((MOE_ROWS, d), F32), pltpu.SemaphoreType.DMA(()),
                            pltpu.SemaphoreType.DMA(())],
        ),
        out_shape=jax.ShapeDtypeStruct((n_rows, d), F32),
        compiler_params=_cparams(("arbitrary",)),
        name="moe_dispatch",
    )(rows_flat, span, h)


def _expert_kernel(be_ref, nh_ref, nu_ref, x_ref, wg_ref, wu_ref, wd_ref, o_ref):
    del be_ref
    nh = nh_ref[pl.program_id(0)]

    def ffn(rows):
        xb = x_ref[0:rows, :].astype(BF16)
        a = jnp.dot(xb, wg_ref[0], preferred_element_type=F32)
        b = jnp.dot(xb, wu_ref[0], preferred_element_type=F32)
        hb = (jax.nn.silu(a) * b).astype(BF16)
        o_ref[0:rows, :] = jnp.dot(hb, wd_ref[0], preferred_element_type=F32)

    for parts in range(1, MOE_PARTS + 1):
        @pl.when(nh == parts)
        def _():
            ffn(parts * MOE_ROWS)
            if parts < MOE_PARTS:
                o_ref[parts * MOE_ROWS:, :] = jnp.zeros((MOE_BLOCK - parts * MOE_ROWS, o_ref.shape[1]), F32)


def _experts(blk_exp, blk_parts, n_used, xs, wg, wu, wd):
    n_rows, d = xs.shape
    n_blocks = n_rows // MOE_BLOCK
    de = wg.shape[2]
    row_map = lambda i, be, nh, nu: (jnp.minimum(i, jnp.maximum(nu[0] - 1, 0)), 0)
    w_map = lambda i, be, nh, nu: (be[i], 0, 0)
    return pl.pallas_call(
        _expert_kernel,
        grid_spec=pltpu.PrefetchScalarGridSpec(
            num_scalar_prefetch=3,
            grid=(n_blocks,),
            in_specs=[
                pl.BlockSpec((MOE_BLOCK, d), row_map),
                pl.BlockSpec((1, d, de), w_map),
                pl.BlockSpec((1, d, de), w_map),
                pl.BlockSpec((1, de, d), w_map),
            ],
            out_specs=pl.BlockSpec((MOE_BLOCK, d), row_map),
        ),
        out_shape=jax.ShapeDtypeStruct((n_rows, d), F32),
        compiler_params=_cparams(("arbitrary",)),
        name="moe_experts",
    )(blk_exp, blk_parts, n_used, xs, wg, wu, wd)


def _combine_kernel(rows_ref, ys_hbm, x_ref, meta_ref, nf_ref, op_ref, os_ref, buf_ref, sem, *, p_tiles):
    tm = x_ref.shape[0]
    i = pl.program_id(0)
    nt = pl.num_programs(0)
    n_tok = nt * tm
    slot = i % 2

    def gather(tile, dst_slot):
        def tokens(g, carry):
            t0 = pl.multiple_of(g * SUBLANES, SUBLANES)
            for u in range(SUBLANES):
                for k in range(TOP_K_FINE):
                    r = rows_ref[k * n_tok + tile * tm + t0 + u]
                    pltpu.make_async_copy(ys_hbm.at[pl.ds(r, 1)], buf_ref.at[dst_slot, k, pl.ds(t0 + u, 1)],
                                          sem.at[dst_slot]).start()
            return carry

        lax.fori_loop(0, tm // SUBLANES, tokens, 0)

    @pl.when(i == 0)
    def _():
        gather(0, 0)

    @pl.when(i + 1 < nt)
    def _():
        gather(i + 1, 1 - slot)

    for k in range(TOP_K_FINE):
        pltpu.make_async_copy(ys_hbm.at[pl.ds(0, tm)], buf_ref.at[slot, k], sem.at[slot]).wait()
    meta = meta_ref[...]
    y = x_ref[...] + (buf_ref[slot, 0] * meta[:, 2:3] + buf_ref[slot, 1] * meta[:, 3:4])
    ms = jnp.mean(y * y, axis=-1, keepdims=True)
    out = y * lax.rsqrt(ms + EPS) * nf_ref[...]

    @pl.when(i < p_tiles)
    def _():
        op_ref[...] = out

    @pl.when(i >= p_tiles)
    def _():
        os_ref[...] = out


def _combine(rows_flat, ys, x, meta, norm_f, n_p, tm=512):
    n, d = x.shape
    n_s = n - n_p
    tm = min(tm, n_p, n_s)
    assert n_p % tm == 0 and n_s % tm == 0
    p_tiles = n_p // tm
    return pl.pallas_call(
        functools.partial(_combine_kernel, p_tiles=p_tiles),
        grid_spec=pltpu.PrefetchScalarGridSpec(
            num_scalar_prefetch=1,
            grid=(n // tm,),
            in_specs=[
                pl.BlockSpec(memory_space=pl.ANY),
                pl.BlockSpec((tm, d), lambda i, rows: (i, 0)),
                pl.BlockSpec((tm, ROUTER_COLS), lambda i, rows: (i, 0)),
                pl.BlockSpec((1, d), lambda i, rows: (0, 0)),
            ],
            out_specs=[
                pl.BlockSpec((tm, d), lambda i, rows: (jnp.minimum(i, p_tiles - 1), 0)),
                pl.BlockSpec((tm, d), lambda i, rows: (jnp.maximum(i - p_tiles, 0), 0)),
            ],
            scratch_shapes=[pltpu.VMEM((2, TOP_K_FINE, tm, d), F32), pltpu.SemaphoreType.DMA((2,))],
        ),
        out_shape=[jax.ShapeDtypeStruct((n_p, d), F32), jax.ShapeDtypeStruct((n_s, d), F32)],
        compiler_params=_cparams(("arbitrary",)),
        name="moe_combine",
    )(rows_flat, ys, x, meta, norm_f.reshape(1, d))


def _moe_and_final_norm(x, n_p, norm_ffn, w_coarse, b_coarse, w_fine, b_fine, wg, wu, wd, norm_f):
    n_tok, d = x.shape
    pad = ROUTER_COLS - N_GROUPS - N_EXPERTS
    w_router = jnp.concatenate([w_coarse, w_fine, jnp.zeros((d, pad), F32)], axis=1)
    b_router = jnp.concatenate([b_coarse, b_fine, jnp.zeros((pad,), F32)]).reshape(1, ROUTER_COLS)
    h, meta, meta_t, counts = _router(x, norm_ffn, w_router, b_router)

    n_asg = n_tok * TOP_K_FINE
    n_blocks = -(-n_asg // MOE_BLOCK) + N_EXPERTS
    n_rows = n_blocks * MOE_BLOCK
    cnt = counts[0, ROUTER_OFF:ROUTER_OFF + N_EXPERTS].astype(I32)
    nblk = (cnt + MOE_BLOCK - 1) // MOE_BLOCK
    padded = nblk * MOE_BLOCK
    pend = jnp.cumsum(padded)
    pstart = pend - padded
    first = cnt - (nblk - 1) * MOE_BLOCK
    first_rows = jnp.where(cnt > 0, (first + MOE_ROWS - 1) // MOE_ROWS * MOE_ROWS, 0)
    expert = meta_t[0:TOP_K_FINE].astype(I32).reshape(-1, LANES)
    rank = meta_t[4:4 + TOP_K_FINE].astype(I32).reshape(-1, LANES)
    rows = rank
    for e in range(N_EXPERTS):
        rows = rows + jnp.where(expert == e, pstart[e] + jnp.where(rank >= first[e], MOE_BLOCK - first[e], 0), 0)
    rows_flat = rows.reshape(-1)
    n_used = (pend[-1] // MOE_BLOCK).astype(I32)
    blk = jnp.arange(n_blocks, dtype=I32)
    blk_exp = jnp.searchsorted(pend, jnp.minimum(blk, n_used - 1) * MOE_BLOCK, side='right').astype(I32)
    blk_exp = jnp.minimum(blk_exp, N_EXPERTS - 1)
    blk_parts = jnp.where(blk * MOE_BLOCK == pstart[blk_exp], first_rows[blk_exp] // MOE_ROWS, MOE_PARTS)
    blk_parts = jnp.where(blk < n_used, blk_parts, 0).astype(I32)
    cend = pstart + first_rows

    xs = _dispatch(rows_flat, jnp.concatenate([pstart, cend]).astype(I32), h, n_rows)
    ys = _experts(blk_exp, blk_parts, n_used.reshape(1), xs, wg, wu, wd)
    return _combine(rows_flat, ys, x, meta, norm_f, n_p)


def _mixer(x, s_ret, s_conv, pos0, norm_mix, w_in, conv_w, conv_b, conv_ln_w, conv_ln_b, w_o,
           total_rows, row0, prev, merge_w=None, side_rope=(), side_rest=(), side_conv=()):
    b, l, d = x.shape
    x2 = x.reshape(b * l, d)
    chunk = RET_CHUNK if l % RET_CHUNK == 0 else l
    h = _rmsnorm_cast(x2, norm_mix)
    qk, kk, cast_a = _inproj_rope(h, w_in, *_rope_inputs(b, l, pos0, chunk), side=side_rope)
    v, _ = _inproj(h, w_in, [(COL_V, W_V)], BF16)
    u = _inproj_glu(h, w_in)
    rest, cast_b = _inproj(h, w_in, [(COL_G, W_G), (COL_GATES, W_REST - W_G)], F32, side_rest)
    if s_ret is None:
        og, s_ret_new, _ = _retention_prompt(qk, kk, v, rest, b, l)
        sw, s_conv_new, cast_c = _conv_branch(u, s_conv, conv_w, conv_b, conv_ln_w, conv_ln_b, b, l,
                                              tl=min(128, l), nb=1, side=side_conv)
    else:
        og, s_ret_new = _retention_sample(qk, kk, v, rest, s_ret, b, l)
        sw, s_conv_new, cast_c = _conv_branch(u, s_conv, conv_w, conv_b, conv_ln_w, conv_ln_b, b, l,
                                              tl=l, nb=min(SUBLANES, b), side=side_conv)
    if merge_w is None:
        merge_w = cast_a[:2]
    merged = _merge(og, sw, *merge_w, rest)
    x1 = _outproj_residual(merged, w_o, x2, total_rows, row0, prev)
    return x1, s_ret_new, s_conv_new, cast_a + cast_b + cast_c


def kernel(x_prompt, x_sample, state_ret, state_conv, norm_mix, w_in, w_ret_o, conv_w, conv_b, conv_ln_w,
           conv_ln_b, w_conv_o, w_o, norm_ffn, w_coarse, b_coarse, w_fine, b_fine, w_gate, w_up, w_down, norm_f):
    depth = w_in.shape[0]
    assert depth == 1, "single-layer trunk"
    bp, lp, d = x_prompt.shape
    bs, ls, _ = x_sample.shape
    n_p = bp * lp
    n_s = bs * ls
    mix_w = (norm_mix[0], w_in[0], conv_w, conv_b, conv_ln_w, conv_ln_b, w_o[0])
    n_exp, _, d_exp = w_gate.shape[1:]
    flat = lambda w: w.reshape(n_exp * w.shape[2], w.shape[3])
    x1, ret_p, conv_p, (wr, wc, wg, wu, wd) = _mixer(
        x_prompt, None, None, 0, *mix_w, n_p + n_s, 0, None,
        side_rope=(w_ret_o[0], w_conv_o[0]), side_rest=(flat(w_gate),), side_conv=(flat(w_up), flat(w_down)))
    conv_hist = jnp.transpose(state_conv, (0, 2, 1, 3))
    x1, ret_s, conv_s, _ = _mixer(x_sample, state_ret[0], conv_hist, PAST_LEN, *mix_w, n_p + n_s, n_p, x1,
                                  merge_w=(wr, wc))
    yp, ys = _moe_and_final_norm(x1, n_p, norm_ffn[0], w_coarse[0], b_coarse[0], w_fine[0], b_fine[0],
                                 wg.reshape(n_exp, d, d_exp), wu.reshape(n_exp, d, d_exp),
                                 wd.reshape(n_exp, d_exp, d), norm_f)
    return (yp.reshape(bp, lp, d), ys.reshape(bs, ls, d), ret_p[None], conv_p, ret_s[None], conv_s)
```

```python
import functools

import numpy as np
import jax
import jax.numpy as jnp
from jax import lax
from jax.experimental import pallas as pl
from jax.experimental.pallas import tpu as pltpu

F32 = jnp.float32
BF16 = jnp.bfloat16
I32 = jnp.int32

D_MODEL = 2048
PAST_LEN = 16384
RET_HEADS = 8
RET_DK = D_MODEL // RET_HEADS
RET_DV = 2 * D_MODEL // RET_HEADS
RET_HALF = RET_DK // 2
RET_CHUNK = 256
ROPE_BASE = 10000.0
CONV_WIDTH = 31
CONV_HIST = CONV_WIDTH - 1
N_GROUPS = 4
EXPERTS_PER_GROUP = 8
N_EXPERTS = N_GROUPS * EXPERTS_PER_GROUP
TOP_K_FINE = 2
D_EXPERT = D_MODEL // 2
EPS = 1e-6

W_QK = 2 * RET_HEADS * RET_DK
W_V = RET_HEADS * RET_DV
W_G = RET_HEADS * RET_DV
COL_QK = 0
COL_V = COL_QK + W_QK
COL_G = COL_V + W_V
COL_CA = COL_G + W_G
COL_CB = COL_CA + D_MODEL
COL_GATES = COL_CB + D_MODEL
REST_G = 0
REST_GRET = REST_G + W_G
REST_GCONV = REST_GRET + D_MODEL
W_REST = REST_GCONV + D_MODEL

LANES = 128
SUBLANES = 8
VMEM_LIMIT = 56 * 1024 * 1024
MOE_ROWS = 128
MOE_PARTS = 4
MOE_BLOCK = MOE_PARTS * MOE_ROWS
ROUTER_COLS = LANES
ROUTER_OFF = N_GROUPS
NEG = -3.0e38
ISSUE_UNROLL = 8


def _cparams(sem):
    return pltpu.CompilerParams(dimension_semantics=sem, vmem_limit_bytes=VMEM_LIMIT)


def _rmsnorm_kernel(x_ref, w_ref, o_ref):
    x = x_ref[...]
    ms = jnp.mean(x * x, axis=-1, keepdims=True)
    o_ref[...] = (x * lax.rsqrt(ms + EPS) * w_ref[...]).astype(o_ref.dtype)


def _rmsnorm_cast(x, w, tm=512):
    m, d = x.shape
    tm = min(tm, m)
    return pl.pallas_call(
        _rmsnorm_kernel,
        grid=(m // tm,),
        in_specs=[pl.BlockSpec((tm, d), lambda i: (i, 0)), pl.BlockSpec((1, d), lambda i: (0, 0))],
        out_specs=pl.BlockSpec((tm, d), lambda i: (i, 0)),
        out_shape=jax.ShapeDtypeStruct((m, d), BF16),
        compiler_params=_cparams(("arbitrary",)),
        name="rmsnorm_cast",
    )(x, w.reshape(1, d))


def _side_steps(total_steps):
    return 1 << (total_steps.bit_length() - 1)


def _side_specs(arrays, n_steps, linear_step):
    def spec(a):
        rows = a.shape[0] // n_steps
        assert rows * n_steps == a.shape[0] and rows % 16 == 0, (a.shape, n_steps)
        return pl.BlockSpec((rows, a.shape[1]), lambda *g: (jnp.minimum(linear_step(*g), n_steps - 1), 0))
    return [spec(a) for a in arrays]


def _side_cast(step, n_steps, srcs, dsts):
    def cast():
        for src, dst in zip(srcs, dsts):
            dst[...] = src[...].astype(dst.dtype)

    if n_steps is None:
        cast()
    else:
        pl.when(step < n_steps)(cast)


def _side_guard(n_steps, total_steps):
    return None if n_steps == total_steps else n_steps


def _inproj_kernel(*refs, n_side, side_steps):
    a_ref, w_ref = refs[:2]
    side_in = refs[2:2 + n_side]
    o_ref = refs[2 + n_side]
    side_out = refs[3 + n_side:3 + 2 * n_side]
    wb_ref = refs[-1]

    @pl.when(pl.program_id(1) == 0)
    def _():
        wb_ref[...] = w_ref[...].astype(BF16)

    o_ref[...] = jnp.dot(a_ref[...], wb_ref[...], preferred_element_type=F32).astype(o_ref.dtype)
    if n_side:
        _side_cast(pl.program_id(0) * pl.num_programs(1) + pl.program_id(1), side_steps, side_in, side_out)


def _inproj(a, w, col_spans, out_dtype, side=(), tm=1024, tn=1024):
    m, k = a.shape
    tm = min(tm, m)
    assert all(c0 % tn == 0 and n % tn == 0 for c0, n in col_spans)
    ncols = sum(n for _, n in col_spans)
    grid = (ncols // tn, m // tm)

    def w_tile(j):
        tile, first = 0, 0
        for c0, n in col_spans:
            tile = jnp.where(j >= first, c0 // tn + j - first, tile)
            first += n // tn
        return tile

    side_steps = _side_steps(grid[0] * grid[1])
    side_specs = _side_specs(side, side_steps, lambda j, i: j * grid[1] + i)
    outs = pl.pallas_call(
        functools.partial(_inproj_kernel, n_side=len(side),
                          side_steps=_side_guard(side_steps, grid[0] * grid[1])),
        grid=grid,
        in_specs=[pl.BlockSpec((tm, k), lambda j, i: (i, 0)),
                  pl.BlockSpec((k, tn), lambda j, i: (0, w_tile(j)))] + side_specs,
        out_specs=[pl.BlockSpec((tm, tn), lambda j, i: (i, j))] + side_specs,
        out_shape=[jax.ShapeDtypeStruct((m, ncols), out_dtype)]
                  + [jax.ShapeDtypeStruct(s.shape, BF16) for s in side],
        scratch_shapes=[pltpu.VMEM((k, tn), BF16)],
        compiler_params=_cparams(("arbitrary", "arbitrary")),
        name="inproj",
    )(a, w, *side)
    return outs[0], tuple(outs[1:])


def _inproj_glu_kernel(a_ref, wa_ref, wb_ref, o_ref, wab_ref, wbb_ref):
    @pl.when(pl.program_id(1) == 0)
    def _():
        wab_ref[...] = wa_ref[...].astype(BF16)
        wbb_ref[...] = wb_ref[...].astype(BF16)

    a = a_ref[...]
    ca = jnp.dot(a, wab_ref[...], preferred_element_type=F32)
    cb = jnp.dot(a, wbb_ref[...], preferred_element_type=F32)
    o_ref[...] = ca * jax.nn.sigmoid(cb)


def _inproj_glu(a, w, tm=1024, tn=512):
    m, k = a.shape
    tm = min(tm, m)
    n = D_MODEL
    assert COL_CA % tn == 0 and COL_CB % tn == 0 and n % tn == 0
    return pl.pallas_call(
        _inproj_glu_kernel,
        grid=(n // tn, m // tm),
        in_specs=[
            pl.BlockSpec((tm, k), lambda j, i: (i, 0)),
            pl.BlockSpec((k, tn), lambda j, i: (0, COL_CA // tn + j)),
            pl.BlockSpec((k, tn), lambda j, i: (0, COL_CB // tn + j)),
        ],
        out_specs=pl.BlockSpec((tm, tn), lambda j, i: (i, j)),
        out_shape=jax.ShapeDtypeStruct((m, n), F32),
        scratch_shapes=[pltpu.VMEM((k, tn), BF16), pltpu.VMEM((k, tn), BF16)],
        compiler_params=_cparams(("arbitrary", "arbitrary")),
        name="inproj_glu",
    )(a, w, w)


ROPE_ROWS = RET_CHUNK


def _inproj_rope_kernel(*refs, nq_tiles, n_side, side_steps):
    a_ref, w_ref, cos_ref, sin_ref, kd_ref = refs[:5]
    side_in = refs[5:5 + n_side]
    qk_ref, kk_ref = refs[5 + n_side:7 + n_side]
    side_out = refs[7 + n_side:7 + 2 * n_side]
    wb_ref, acc_ref = refs[-2:]
    j = pl.program_id(0)
    if n_side:
        _side_cast(j * pl.num_programs(1) + pl.program_id(1), side_steps, side_in, side_out)

    @pl.when(pl.program_id(1) == 0)
    def _():
        wb_ref[...] = w_ref[...].astype(BF16)

    acc_ref[...] = jnp.dot(a_ref[...], wb_ref[...], preferred_element_type=F32)
    tm, tn = acc_ref.shape
    scale = jnp.where(j >= nq_tiles, RET_DK ** -0.5, 1.0).astype(F32)
    for r in range(tm // ROPE_ROWS):
        rs = slice(r * ROPE_ROWS, (r + 1) * ROPE_ROWS)
        cos = cos_ref[rs, :]
        sin = sin_ref[rs, :]
        parts = []
        for hh in range(tn // RET_DK):
            x1 = acc_ref[rs, hh * RET_DK:hh * RET_DK + RET_HALF]
            x2 = acc_ref[rs, hh * RET_DK + RET_HALF:(hh + 1) * RET_DK]
            parts += [x1 * cos - x2 * sin, x1 * sin + x2 * cos]
        rot = jnp.concatenate(parts, axis=1) * scale
        qk_ref[rs, :] = rot.astype(qk_ref.dtype)

        @pl.when(j >= nq_tiles)
        def _():
            kk_ref[rs, :] = (rot * kd_ref[...]).astype(kk_ref.dtype)


ROPE_TM = 1024


def _inproj_rope(a, w, cos_rows, sin_rows, kd_rows, side=(), tn=1024):
    m, k = a.shape
    tm = min(ROPE_TM, m)
    wq = W_QK // 2
    assert tm % ROPE_ROWS == 0 and wq % tn == 0 and tn % RET_DK == 0 and cos_rows.shape[0] % tm == 0
    nq = wq // tn
    period = cos_rows.shape[0] // tm
    grid = (2 * nq, m // tm)
    side_steps = _side_steps(grid[0] * grid[1])
    side_specs = _side_specs(side, side_steps, lambda j, i: j * grid[1] + i)
    outs = pl.pallas_call(
        functools.partial(_inproj_rope_kernel, nq_tiles=nq, n_side=len(side),
                          side_steps=_side_guard(side_steps, grid[0] * grid[1])),
        grid=grid,
        in_specs=[
            pl.BlockSpec((tm, k), lambda j, i: (i, 0)),
            pl.BlockSpec((k, tn), lambda j, i: (0, COL_QK // tn + j)),
            pl.BlockSpec((tm, RET_HALF), lambda j, i: (i % period, 0)),
            pl.BlockSpec((tm, RET_HALF), lambda j, i: (i % period, 0)),
            pl.BlockSpec((ROPE_ROWS, tn), lambda j, i: (0, jnp.maximum(j - nq, 0))),
        ] + side_specs,
        out_specs=[
            pl.BlockSpec((tm, tn), lambda j, i: (i, j)),
            pl.BlockSpec((tm, tn), lambda j, i: (jnp.where(j >= nq, i, 0), jnp.maximum(j - nq, 0))),
        ] + side_specs,
        out_shape=[jax.ShapeDtypeStruct((m, W_QK), BF16), jax.ShapeDtypeStruct((m, wq), BF16)]
                  + [jax.ShapeDtypeStruct(s.shape, BF16) for s in side],
        scratch_shapes=[pltpu.VMEM((k, tn), BF16), pltpu.VMEM((tm, tn), F32)],
        compiler_params=_cparams(("arbitrary", "arbitrary")),
        name="inproj_rope",
    )(a, w, cos_rows, sin_rows, kd_rows, *side)
    return outs[0], outs[1], tuple(outs[2:])


def _retention_tables(c):
    h = RET_HEADS
    f32 = np.float32
    log_g = np.log1p(-np.exp2(f32(-5.0) - np.arange(h, dtype=f32)))
    idx = np.arange(c, dtype=f32)
    diff = idx[:, None] - idx[None, :]
    decay = np.where(diff[None] >= 0.0, np.exp(np.maximum(diff, f32(0.0))[None] * log_g[:, None, None]), f32(0.0))
    q_dec = np.exp((idx[:, None] + f32(1.0)) * log_g[None, :])
    k_dec = np.exp((f32(c) - f32(1.0) - idx)[:, None] * log_g[None, :])
    c_dec = np.exp(f32(c) * log_g)
    return decay.astype(f32), q_dec.astype(f32), k_dec.astype(f32), c_dec.astype(f32)


def _rope_tables(pos):
    f32 = np.float32
    inv = f32(ROPE_BASE) ** (-np.arange(RET_HALF, dtype=f32) / f32(RET_HALF))
    ang = pos.astype(f32)[:, None] * inv[None, :].astype(f32)
    return np.cos(ang).astype(f32), np.sin(ang).astype(f32)


def _lane_bcast(t):
    return np.ascontiguousarray(np.broadcast_to(t.T[:, :, None], (t.shape[1], t.shape[0], LANES)))


def _rope_inputs(l, pos0, c, tm):
    rows = max(l, tm)
    assert rows % l == 0 and rows % tm == 0 and ROPE_ROWS % c == 0
    cos, sin = _rope_tables(pos0 + np.arange(l))
    _, _, k_dec, _ = _retention_tables(c)
    kd = np.repeat(k_dec, RET_DK, axis=1)
    return np.tile(cos, (rows // l, 1)), np.tile(sin, (rows // l, 1)), np.tile(kd, (ROPE_ROWS // c, 1))


def _group_norm_gate(o, g):
    ms = jnp.mean(o * o, axis=-1, keepdims=True)
    return jax.nn.silu(g) * (o * lax.rsqrt(ms + EPS))


def _ret_prompt_kernel(*refs, n_side, side_steps):
    cdec_ref, q_ref, k_ref, kk_ref, v_ref, g_ref, dec_ref, qd_ref = refs[:8]
    side_in = refs[8:8 + n_side]
    o_ref, s_ref = refs[8 + n_side:10 + n_side]
    side_out = refs[10 + n_side:10 + 2 * n_side]
    if n_side:
        _side_cast(pl.program_id(0) * pl.num_programs(1) + pl.program_id(1), side_steps, side_in, side_out)

    @pl.when(pl.program_id(1) == 0)
    def _():
        s_ref[...] = jnp.zeros_like(s_ref)

    for h in range(RET_HEADS):
        qb = q_ref[:, h * RET_DK:(h + 1) * RET_DK]
        kb = k_ref[:, h * RET_DK:(h + 1) * RET_DK]
        kkb = kk_ref[:, h * RET_DK:(h + 1) * RET_DK]
        vb = v_ref[:, h * RET_DV:(h + 1) * RET_DV]
        att = lax.dot_general(qb, kb, (((1,), (1,)), ((), ())), preferred_element_type=F32) * dec_ref[h]
        s = s_ref[0, h]
        o = (jnp.dot(att.astype(BF16), vb, preferred_element_type=F32)
             + qd_ref[h][:, :1] * jnp.dot(qb, s.astype(BF16), preferred_element_type=F32))
        s_ref[0, h] = s * cdec_ref[h] + lax.dot_general(kkb, vb, (((0,), (0,)), ((), ())),
                                                         preferred_element_type=F32)
        og = _group_norm_gate(o, g_ref[:, h * RET_DV:(h + 1) * RET_DV])
        o_ref[:, h * RET_DV:(h + 1) * RET_DV] = og.astype(o_ref.dtype)


def _retention_prompt(qk, kk, v, rest, b, l, side=()):
    c = RET_CHUNK if l % RET_CHUNK == 0 else l
    nc = l // c
    decay, q_dec, _, c_dec = _retention_tables(c)
    wq = RET_HEADS * RET_DK
    wv = RET_HEADS * RET_DV
    full3 = lambda bi, ci: (0, 0, 0)
    side_steps = _side_steps(b * nc)
    side_specs = _side_specs(side, side_steps, lambda bi, ci: bi * nc + ci)
    outs = pl.pallas_call(
        functools.partial(_ret_prompt_kernel, n_side=len(side), side_steps=_side_guard(side_steps, b * nc)),
        grid=(b, nc),
        in_specs=[
            pl.BlockSpec(memory_space=pltpu.SMEM),
            pl.BlockSpec((c, wq), lambda bi, ci: (bi * nc + ci, 0)),
            pl.BlockSpec((c, wq), lambda bi, ci: (bi * nc + ci, 1)),
            pl.BlockSpec((c, wq), lambda bi, ci: (bi * nc + ci, 0)),
            pl.BlockSpec((c, wv), lambda bi, ci: (bi * nc + ci, 0)),
            pl.BlockSpec((c, wv), lambda bi, ci: (bi * nc + ci, REST_G // wv)),
            pl.BlockSpec((RET_HEADS, c, c), full3),
            pl.BlockSpec((RET_HEADS, c, LANES), full3),
        ] + side_specs,
        out_specs=[
            pl.BlockSpec((c, wv), lambda bi, ci: (bi * nc + ci, 0)),
            pl.BlockSpec((1, RET_HEADS, RET_DK, RET_DV), lambda bi, ci: (bi, 0, 0, 0)),
        ] + side_specs,
        out_shape=[
            jax.ShapeDtypeStruct((b * l, wv), BF16),
            jax.ShapeDtypeStruct((b, RET_HEADS, RET_DK, RET_DV), F32),
        ] + [jax.ShapeDtypeStruct(s.shape, BF16) for s in side],
        compiler_params=_cparams(("arbitrary", "arbitrary")),
        name="retention_prompt",
    )(c_dec, qk, qk, kk, v, rest, decay, _lane_bcast(q_dec), *side)
    return outs[0], outs[1], tuple(outs[2:])


SAMPLE_PAD = 16


def _pad_rows(x, rows):
    return jnp.concatenate([x, jnp.zeros((rows - x.shape[0], x.shape[1]), x.dtype)], axis=0)


def _ret_sample_kernel(cdec_ref, q_ref, k_ref, kk_ref, v_ref, g_ref, dec_ref, qd_ref, s0_ref, o_ref, s_ref,
                       *, nb, l):
    for h in range(RET_HEADS):
        qf = q_ref[:, h * RET_DK:(h + 1) * RET_DK].astype(F32)
        kf = k_ref[:, h * RET_DK:(h + 1) * RET_DK].astype(F32)
        kkf = kk_ref[:, h * RET_DK:(h + 1) * RET_DK].astype(F32)
        vf = v_ref[:, h * RET_DV:(h + 1) * RET_DV].astype(F32)
        for bt in range(nb):
            rs = slice(bt * l, (bt + 1) * l)
            qb = _pad_rows(qf[rs], SAMPLE_PAD).astype(BF16)
            kb = _pad_rows(kf[rs], SAMPLE_PAD).astype(BF16)
            kkb = _pad_rows(kkf[rs], SAMPLE_PAD).astype(BF16)
            vb = _pad_rows(vf[rs], SAMPLE_PAD).astype(BF16)
            att = lax.dot_general(qb, kb, (((1,), (1,)), ((), ())), preferred_element_type=F32) * dec_ref[h]
            s = s0_ref[bt, h]
            o = (jnp.dot(att.astype(BF16), vb, preferred_element_type=F32)
                 + qd_ref[h][:, :1] * jnp.dot(qb, s.astype(BF16), preferred_element_type=F32))
            s_ref[bt, h] = s * cdec_ref[h] + lax.dot_general(kkb, vb, (((0,), (0,)), ((), ())),
                                                              preferred_element_type=F32)
            og = _group_norm_gate(o[:l], g_ref[rs, h * RET_DV:(h + 1) * RET_DV])
            o_ref[rs, h * RET_DV:(h + 1) * RET_DV] = og.astype(o_ref.dtype)


def _retention_sample(qk, kk, v, rest, s0, b, l, nb=2):
    assert l % RET_CHUNK != 0 and l <= SAMPLE_PAD and b % nb == 0, (b, l)
    decay, q_dec, _, c_dec = _retention_tables(l)
    p = SAMPLE_PAD - l
    decay = np.pad(decay, ((0, 0), (0, p), (0, p)))
    q_dec = np.pad(q_dec, ((0, p), (0, 0)))
    wq = RET_HEADS * RET_DK
    wv = RET_HEADS * RET_DV
    rows = nb * l
    full3 = lambda bi: (0, 0, 0)
    state_spec = pl.BlockSpec((nb, RET_HEADS, RET_DK, RET_DV), lambda bi: (bi, 0, 0, 0))
    return pl.pallas_call(
        functools.partial(_ret_sample_kernel, nb=nb, l=l),
        grid=(b // nb,),
        in_specs=[
            pl.BlockSpec(memory_space=pltpu.SMEM),
            pl.BlockSpec((rows, wq), lambda bi: (bi, 0)),
            pl.BlockSpec((rows, wq), lambda bi: (bi, 1)),
            pl.BlockSpec((rows, wq), lambda bi: (bi, 0)),
            pl.BlockSpec((rows, wv), lambda bi: (bi, 0)),
            pl.BlockSpec((rows, wv), lambda bi: (bi, REST_G // wv)),
            pl.BlockSpec((RET_HEADS, SAMPLE_PAD, SAMPLE_PAD), full3),
            pl.BlockSpec((RET_HEADS, SAMPLE_PAD, LANES), full3),
            state_spec,
        ],
        out_specs=[pl.BlockSpec((rows, wv), lambda bi: (bi, 0)), state_spec],
        out_shape=[
            jax.ShapeDtypeStruct((b * l, wv), BF16),
            jax.ShapeDtypeStruct((b, RET_HEADS, RET_DK, RET_DV), F32),
        ],
        compiler_params=_cparams(("arbitrary",)),
        name="retention_sample",
    )(c_dec, qk, qk, kk, v, rest, decay, _lane_bcast(q_dec), s0)


CONV_PAD = 32
CONV_COLS = LANES


def _conv_strip(block, wts, nblk, store):
    lead = CONV_PAD - CONV_HIST
    taps = [((lead + w) // SUBLANES, (lead + w) % SUBLANES) for w in range(CONV_WIDTH)]
    rowid = lax.broadcasted_iota(I32, (SUBLANES, CONV_COLS), 0)

    def partial_sums(m, shifts):
        ys = {}
        for w, (a, s) in enumerate(taps):
            if s in shifts:
                term = block(m + a) * wts[w]
                ys[s] = term if s not in ys else ys[s] + term
        return ys

    shifted = range(1, SUBLANES)
    prev = partial_sums(0, shifted)
    for m in range(nblk):
        nxt = partial_sums(m + 1, shifted)
        acc = partial_sums(m, (0,))[0]
        for s in shifted:
            acc = acc + pltpu.roll(jnp.where(rowid >= s, prev[s], nxt[s]), SUBLANES - s, axis=0)
        store(m, acc)
        prev = nxt


def _conv_kernel(*refs, tl, nb, nl, has_state, n_side, side_steps):
    u_ref = refs[0]
    buf_ref = refs[1] if has_state else None
    n_in = 2 if has_state else 1
    cw_ref, bias_ref, lnw_ref, lnb_ref = refs[n_in:n_in + 4]
    side_in = refs[n_in + 4:n_in + 4 + n_side]
    sw_ref, nbuf_ref = refs[n_in + 4 + n_side:n_in + 6 + n_side]
    side_out = refs[n_in + 6 + n_side:n_in + 6 + 2 * n_side]
    upad_ref, c_ref = refs[-2:]
    if n_side:
        _side_cast(pl.program_id(0) * nl + pl.program_id(1), side_steps, side_in, side_out)
    li = pl.program_id(1)
    lead = CONV_PAD - CONV_HIST
    d = u_ref.shape[1]
    for bt in range(nb):
        @pl.when(li == 0)
        def _():
            if has_state:
                upad_ref[bt, 0:lead, :] = jnp.zeros((lead, d), F32)
                upad_ref[bt, lead:CONV_PAD, :] = buf_ref[0, :, bt, :]
            else:
                upad_ref[bt, 0:CONV_PAD, :] = jnp.zeros((CONV_PAD, d), F32)

        r0 = bt * tl
        upad_ref[bt, CONV_PAD:CONV_PAD + tl, :] = u_ref[r0:r0 + tl, :]
        for cc in range(d // CONV_COLS):
            cs = slice(cc * CONV_COLS, (cc + 1) * CONV_COLS)
            wts = [jnp.broadcast_to(cw_ref[0, w:w + 1, cs], (SUBLANES, CONV_COLS)) for w in range(CONV_WIDTH)]
            blocks = {}

            def block(j, cs=cs, blocks=blocks):
                if j not in blocks:
                    blocks[j] = upad_ref[bt, j * SUBLANES:(j + 1) * SUBLANES, cs]
                return blocks[j]

            def store(m, acc, cs=cs):
                c_ref[m * SUBLANES:(m + 1) * SUBLANES, cs] = acc

            _conv_strip(block, wts, tl // SUBLANES, store)
        cf = c_ref[...] + bias_ref[...]
        mu = jnp.mean(cf, axis=-1, keepdims=True)
        var = jnp.mean(jnp.square(cf - mu), axis=-1, keepdims=True)
        cn = (cf - mu) * lax.rsqrt(var + EPS) * lnw_ref[...] + lnb_ref[...]
        sw_ref[r0:r0 + tl, :] = jax.nn.silu(cn).astype(sw_ref.dtype)

        @pl.when(li == nl - 1)
        def _():
            nbuf_ref[0, bt] = upad_ref[bt, tl + lead:tl + CONV_PAD, :]

        if nl > 1:
            upad_ref[bt, 0:CONV_PAD, :] = upad_ref[bt, tl:tl + CONV_PAD, :]


def _conv_branch(u, buf_t, conv_w, conv_b, ln_w, ln_b, b, l, tl, nb, side=()):
    d = D_MODEL
    nl = l // tl
    assert nb == 1 or nl == 1
    rows = nb * tl
    const2 = lambda bi, li: (0, 0)
    grid = (b // nb, nl)
    side_steps = _side_steps(grid[0] * grid[1])
    side_specs = _side_specs(side, side_steps, lambda bi, li: bi * nl + li)
    state_spec = pl.BlockSpec((1, nb, CONV_HIST, d), lambda bi, li: (0, bi, 0, 0))
    has_state = buf_t is not None
    state_in = [pl.BlockSpec((1, CONV_HIST, nb, d), lambda bi, li: (0, 0, bi, 0))] if has_state else []
    outs = pl.pallas_call(
        functools.partial(_conv_kernel, tl=tl, nb=nb, nl=nl, has_state=has_state, n_side=len(side),
                          side_steps=_side_guard(side_steps, grid[0] * grid[1])),
        grid=grid,
        in_specs=[
            pl.BlockSpec((rows, d), lambda bi, li: (bi * nl + li, 0)),
        ] + state_in + [
            pl.BlockSpec((1, CONV_WIDTH, d), lambda bi, li: (0, 0, 0)),
            pl.BlockSpec((1, d), const2),
            pl.BlockSpec((1, d), const2),
            pl.BlockSpec((1, d), const2),
        ] + side_specs,
        out_specs=[pl.BlockSpec((rows, d), lambda bi, li: (bi * nl + li, 0)), state_spec] + side_specs,
        out_shape=[
            jax.ShapeDtypeStruct((b * l, d), BF16),
            jax.ShapeDtypeStruct((1, b, CONV_HIST, d), F32),
        ] + [jax.ShapeDtypeStruct(s.shape, BF16) for s in side],
        scratch_shapes=[pltpu.VMEM((nb, CONV_PAD + tl, d), F32), pltpu.VMEM((tl, d), F32)],
        compiler_params=_cparams(("arbitrary", "arbitrary")),
        name="conv_branch",
    )(u, *([buf_t] if has_state else []), conv_w, conv_b, ln_w, ln_b, *side)
    return outs[0], outs[1], tuple(outs[2:])


def _merge_kernel(og_ref, sw_ref, wr_ref, wc_ref, gr_ref, gc_ref, o_ref):
    ret_out = jnp.dot(og_ref[...], wr_ref[...], preferred_element_type=F32)
    conv_out = jnp.dot(sw_ref[...], wc_ref[...], preferred_element_type=F32)
    merged = jax.nn.sigmoid(gr_ref[...]) * ret_out + jax.nn.sigmoid(gc_ref[...]) * conv_out
    o_ref[...] = merged.astype(o_ref.dtype)


def _merge(og, sw, w_ret_o, w_conv_o, rest, tm=512, tn=1024):
    m = og.shape[0]
    n = w_ret_o.shape[1]
    tm = min(tm, m)
    kr = og.shape[1]
    kc = sw.shape[1]
    return pl.pallas_call(
        _merge_kernel,
        grid=(n // tn, m // tm),
        in_specs=[
            pl.BlockSpec((tm, kr), lambda j, i: (i, 0)),
            pl.BlockSpec((tm, kc), lambda j, i: (i, 0)),
            pl.BlockSpec((kr, tn), lambda j, i: (0, j)),
            pl.BlockSpec((kc, tn), lambda j, i: (0, j)),
            pl.BlockSpec((tm, tn), lambda j, i: (i, REST_GRET // tn + j)),
            pl.BlockSpec((tm, tn), lambda j, i: (i, REST_GCONV // tn + j)),
        ],
        out_specs=pl.BlockSpec((tm, tn), lambda j, i: (i, j)),
        out_shape=jax.ShapeDtypeStruct((m, n), BF16),
        compiler_params=_cparams(("arbitrary", "arbitrary")),
        name="merge",
    )(og, sw, w_ret_o, w_conv_o, rest, rest)


def _outproj_kernel(a_ref, w_ref, x_ref, *rest):
    o_ref, wb_ref = rest[-2], rest[-1]

    @pl.when(pl.program_id(1) == 0)
    def _():
        wb_ref[...] = w_ref[...].astype(BF16)

    o_ref[...] = x_ref[...] + jnp.dot(a_ref[...], wb_ref[...], preferred_element_type=F32)


def _outproj_residual(a, w, x, total_rows, row0, prev=None, tm=1024, tn=1024):
    m, k = a.shape
    n = w.shape[1]
    tm = min(tm, m)
    assert row0 % tm == 0
    off = row0 // tm
    in_specs = [
        pl.BlockSpec((tm, k), lambda j, i: (i, 0)),
        pl.BlockSpec((k, tn), lambda j, i: (0, j)),
        pl.BlockSpec((tm, tn), lambda j, i: (i, j)),
    ]
    args = [a, w, x]
    aliases = {}
    if prev is not None:
        in_specs.append(pl.BlockSpec(memory_space=pl.ANY))
        args.append(prev)
        aliases = {3: 0}
    return pl.pallas_call(
        _outproj_kernel,
        grid=(n // tn, m // tm),
        in_specs=in_specs,
        out_specs=pl.BlockSpec((tm, tn), lambda j, i: (i + off, j)),
        out_shape=jax.ShapeDtypeStruct((total_rows, n), F32),
        scratch_shapes=[pltpu.VMEM((k, tn), BF16)],
        input_output_aliases=aliases,
        compiler_params=_cparams(("arbitrary", "arbitrary")),
        name="outproj_residual",
    )(*args)


def _router_kernel(x_ref, nw_ref, w_ref, b_ref, h_ref, meta_ref, meta_t_ref, cnt_ref, base_ref):
    tm = x_ref.shape[0]

    @pl.when(pl.program_id(0) == 0)
    def _():
        base_ref[...] = jnp.zeros_like(base_ref)

    x = x_ref[...]
    ms = jnp.mean(x * x, axis=-1, keepdims=True)
    h = x * lax.rsqrt(ms + EPS) * nw_ref[...]
    h_ref[...] = h
    w = w_ref[...]
    h_hi = h.astype(BF16)
    h_lo = (h - h_hi.astype(F32)).astype(BF16)
    w_hi = w.astype(BF16)
    w_lo = (w - w_hi.astype(F32)).astype(BF16)
    logits = (jnp.dot(h_hi, w_lo, preferred_element_type=F32) + jnp.dot(h_lo, w_hi, preferred_element_type=F32)
              + jnp.dot(h_hi, w_hi, preferred_element_type=F32)) + b_ref[...]
    lane = lax.broadcasted_iota(I32, (tm, ROUTER_COLS), 1)

    def first_argmax(vals):
        top = jnp.max(vals, axis=1, keepdims=True)
        return top, jnp.min(jnp.where(vals == top, lane, ROUTER_COLS), axis=1, keepdims=True)

    cmask = lane < N_GROUPS
    lc = jnp.where(cmask, logits, NEG)
    mc, grp = first_argmax(lc)
    p_grp = 1.0 / jnp.sum(jnp.where(cmask, jnp.exp(lc - mc), 0.0), axis=1, keepdims=True)
    lo = ROUTER_OFF + grp * EXPERTS_PER_GROUP
    lf = jnp.where((lane >= lo) & (lane < lo + EXPERTS_PER_GROUP), logits, NEG)
    v1, i1 = first_argmax(lf)
    v2, i2 = first_argmax(jnp.where(lane == i1, NEG, lf))
    e2 = jnp.exp(v2 - v1)
    den = 1.0 + e2
    g1 = p_grp * (1.0 / den)
    g2 = p_grp * (e2 / den)

    oh1 = lane == i1
    oh2 = lane == i2
    both = jnp.where(oh1 | oh2, 1.0, 0.0)
    row = lax.broadcasted_iota(I32, (tm, tm), 0)
    col = lax.broadcasted_iota(I32, (tm, tm), 1)
    earlier = jnp.where(col < row, 1.0, 0.0).astype(BF16)
    before = jnp.dot(earlier, both.astype(BF16), preferred_element_type=F32) + base_ref[...]
    r1 = jnp.sum(jnp.where(oh1, before, 0.0), axis=1, keepdims=True)
    r2 = jnp.sum(jnp.where(oh2, before, 0.0), axis=1, keepdims=True)
    base_ref[...] = base_ref[...] + jnp.sum(both, axis=0, keepdims=True)
    cnt_ref[...] = base_ref[...]

    fields = [(i1 - ROUTER_OFF).astype(F32), (i2 - ROUTER_OFF).astype(F32), g1, g2, r1, r2]
    meta = jnp.zeros((tm, ROUTER_COLS), F32)
    for k, f in enumerate(fields):
        meta = jnp.where(lane == k, f, meta)
    meta_ref[...] = meta
    meta_t_ref[...] = meta.T[:SUBLANES]


def _router(x, norm_w, w_router, b_router, tm=512):
    n, d = x.shape
    tm = min(tm, n)
    const2 = lambda i: (0, 0)
    return pl.pallas_call(
        _router_kernel,
        grid=(n // tm,),
        in_specs=[
            pl.BlockSpec((tm, d), lambda i: (i, 0)),
            pl.BlockSpec((1, d), const2),
            pl.BlockSpec((d, ROUTER_COLS), const2),
            pl.BlockSpec((1, ROUTER_COLS), const2),
        ],
        out_specs=[
            pl.BlockSpec((tm, d), lambda i: (i, 0)),
            pl.BlockSpec((tm, ROUTER_COLS), lambda i: (i, 0)),
            pl.BlockSpec((SUBLANES, tm), lambda i: (0, i)),
            pl.BlockSpec((1, ROUTER_COLS), const2),
        ],
        out_shape=[
            jax.ShapeDtypeStruct((n, d), F32),
            jax.ShapeDtypeStruct((n, ROUTER_COLS), F32),
            jax.ShapeDtypeStruct((SUBLANES, n), F32),
            jax.ShapeDtypeStruct((1, ROUTER_COLS), F32),
        ],
        scratch_shapes=[pltpu.VMEM((1, ROUTER_COLS), F32)],
        compiler_params=_cparams(("arbitrary",)),
        name="router",
    )(x, norm_w.reshape(1, d), w_router, b_router)


def _rows_kernel(pstart_ref, first_ref, meta_ref, rows_ref):
    expert = meta_ref[0:TOP_K_FINE, :].astype(I32)
    rank = meta_ref[4:4 + TOP_K_FINE, :].astype(I32)
    rows = rank
    for e in range(N_EXPERTS):
        first = first_ref[e]
        rows = rows + jnp.where(expert == e, pstart_ref[e] + jnp.where(rank >= first, MOE_BLOCK - first, 0), 0)
    rows_ref[...] = rows


def _assignment_rows(pstart, first, meta_t):
    n = meta_t.shape[1]
    return pl.pallas_call(
        _rows_kernel,
        grid_spec=pltpu.PrefetchScalarGridSpec(
            num_scalar_prefetch=2,
            grid=(1,),
            in_specs=[pl.BlockSpec((SUBLANES, n), lambda i, ps, fs: (0, 0))],
            out_specs=pl.BlockSpec((TOP_K_FINE, n), lambda i, ps, fs: (0, 0)),
        ),
        out_shape=jax.ShapeDtypeStruct((TOP_K_FINE, n), I32),
        compiler_params=_cparams(("arbitrary",)),
        name="moe_rows",
    )(pstart, first, meta_t)


def _dispatch_kernel(rows_ref, span_ref, h_ref, xs_hbm, zero_ref, sem, zsem):
    td = h_ref.shape[0]
    i = pl.program_id(0)

    @pl.when(i == 0)
    def _():
        zero_ref[...] = jnp.zeros_like(zero_ref)

        def fill_copy(e):
            last = pl.multiple_of(span_ref[N_EXPERTS + e] - MOE_ROWS, MOE_ROWS)
            return pltpu.make_async_copy(zero_ref, xs_hbm.at[pl.ds(last, MOE_ROWS)], zsem)

        def nonempty(e):
            return span_ref[N_EXPERTS + e] > span_ref[e]

        def start(e, carry):
            @pl.when(nonempty(e))
            def _():
                fill_copy(e).start()
            return carry

        def wait(e, carry):
            @pl.when(nonempty(e))
            def _():
                fill_copy(e).wait()
            return carry

        lax.fori_loop(0, N_EXPERTS, start, 0)
        lax.fori_loop(0, N_EXPERTS, wait, 0)

    n_tok = pl.num_programs(0) * td

    def tokens(g, carry):
        t0 = pl.multiple_of(g * SUBLANES, SUBLANES)
        for u in range(SUBLANES):
            for k in range(TOP_K_FINE):
                r = rows_ref[k * n_tok + i * td + t0 + u]
                pltpu.make_async_copy(h_ref.at[pl.ds(t0 + u, 1)], xs_hbm.at[pl.ds(r, 1)], sem).start()
        return carry

    lax.fori_loop(0, td // SUBLANES, tokens, 0)
    for _ in range(TOP_K_FINE):
        pltpu.make_async_copy(h_ref, xs_hbm.at[pl.ds(0, td)], sem).wait()


def _dispatch(rows_flat, span, h, n_rows, td=1024):
    n_tok, d = h.shape
    td = min(td, n_tok)
    return pl.pallas_call(
        _dispatch_kernel,
        grid_spec=pltpu.PrefetchScalarGridSpec(
            num_scalar_prefetch=2,
            grid=(n_tok // td,),
            in_specs=[pl.BlockSpec((td, d), lambda i, rows, span: (i, 0))],
            out_specs=pl.BlockSpec(memory_space=pl.ANY),
            scratch_shapes=[pltpu.VMEM((MOE_ROWS, d), F32), pltpu.SemaphoreType.DMA(()),
                            pltpu.SemaphoreType.DMA(())],
        ),
        out_shape=jax.ShapeDtypeStruct((n_rows, d), F32),
        compiler_params=_cparams(("arbitrary",)),
        name="moe_dispatch",
    )(rows_flat, span, h)


def _expert_kernel(be_ref, nh_ref, nu_ref, x_ref, wg_ref, wu_ref, wd_ref, o_ref):
    del be_ref
    nh = nh_ref[pl.program_id(0)]

    def ffn(rows):
        xb = x_ref[0:rows, :].astype(BF16)
        a = jnp.dot(xb, wg_ref[0], preferred_element_type=F32)
        b = jnp.dot(xb, wu_ref[0], preferred_element_type=F32)
        hb = (jax.nn.silu(a) * b).astype(BF16)
        o_ref[0:rows, :] = jnp.dot(hb, wd_ref[0], preferred_element_type=F32)

    for parts in range(1, MOE_PARTS + 1):
        @pl.when(nh == parts)
        def _():
            ffn(parts * MOE_ROWS)
            if parts < MOE_PARTS:
                o_ref[parts * MOE_ROWS:, :] = jnp.zeros((MOE_BLOCK - parts * MOE_ROWS, o_ref.shape[1]), F32)


def _experts(blk_exp, blk_parts, n_used, xs, wg, wu, wd):
    n_rows, d = xs.shape
    n_blocks = n_rows // MOE_BLOCK
    de = wg.shape[2]
    row_map = lambda i, be, nh, nu: (jnp.minimum(i, jnp.maximum(nu[0] - 1, 0)), 0)
    w_map = lambda i, be, nh, nu: (be[i], 0, 0)
    return pl.pallas_call(
        _expert_kernel,
        grid_spec=pltpu.PrefetchScalarGridSpec(
            num_scalar_prefetch=3,
            grid=(n_blocks,),
            in_specs=[
                pl.BlockSpec((MOE_BLOCK, d), row_map),
                pl.BlockSpec((1, d, de), w_map),
                pl.BlockSpec((1, d, de), w_map),
                pl.BlockSpec((1, de, d), w_map),
            ],
            out_specs=pl.BlockSpec((MOE_BLOCK, d), row_map),
        ),
        out_shape=jax.ShapeDtypeStruct((n_rows, d), F32),
        compiler_params=_cparams(("arbitrary",)),
        name="moe_experts",
    )(blk_exp, blk_parts, n_used, xs, wg, wu, wd)


def _combine_kernel(rows_ref, ys_hbm, x_ref, meta_ref, nf_ref, op_ref, os_ref, buf_ref, sem, *, p_tiles):
    tm = x_ref.shape[0]
    i = pl.program_id(0)
    nt = pl.num_programs(0)
    n_tok = nt * tm
    slot = i % 2

    def gather(tile, dst_slot):
        def tokens(g, carry):
            t0 = pl.multiple_of(g * SUBLANES, SUBLANES)
            for u in range(SUBLANES):
                for k in range(TOP_K_FINE):
                    r = rows_ref[k * n_tok + tile * tm + t0 + u]
                    pltpu.make_async_copy(ys_hbm.at[pl.ds(r, 1)], buf_ref.at[dst_slot, k, pl.ds(t0 + u, 1)],
                                          sem.at[dst_slot]).start()
            return carry

        lax.fori_loop(0, tm // SUBLANES, tokens, 0)

    @pl.when(i == 0)
    def _():
        gather(0, 0)

    @pl.when(i + 1 < nt)
    def _():
        gather(i + 1, 1 - slot)

    for k in range(TOP_K_FINE):
        pltpu.make_async_copy(ys_hbm.at[pl.ds(0, tm)], buf_ref.at[slot, k], sem.at[slot]).wait()
    meta = meta_ref[...]
    y = x_ref[...] + (buf_ref[slot, 0] * meta[:, 2:3] + buf_ref[slot, 1] * meta[:, 3:4])
    ms = jnp.mean(y * y, axis=-1, keepdims=True)
    out = y * lax.rsqrt(ms + EPS) * nf_ref[...]

    @pl.when(i < p_tiles)
    def _():
        op_ref[...] = out

    @pl.when(i >= p_tiles)
    def _():
        os_ref[...] = out


def _combine(rows_flat, ys, x, meta, norm_f, n_p, tm=512):
    n, d = x.shape
    n_s = n - n_p
    tm = min(tm, n_p, n_s)
    assert n_p % tm == 0 and n_s % tm == 0
    p_tiles = n_p // tm
    return pl.pallas_call(
        functools.partial(_combine_kernel, p_tiles=p_tiles),
        grid_spec=pltpu.PrefetchScalarGridSpec(
            num_scalar_prefetch=1,
            grid=(n // tm,),
            in_specs=[
                pl.BlockSpec(memory_space=pl.ANY),
                pl.BlockSpec((tm, d), lambda i, rows: (i, 0)),
                pl.BlockSpec((tm, ROUTER_COLS), lambda i, rows: (i, 0)),
                pl.BlockSpec((1, d), lambda i, rows: (0, 0)),
            ],
            out_specs=[
                pl.BlockSpec((tm, d), lambda i, rows: (jnp.minimum(i, p_tiles - 1), 0)),
                pl.BlockSpec((tm, d), lambda i, rows: (jnp.maximum(i - p_tiles, 0), 0)),
            ],
            scratch_shapes=[pltpu.VMEM((2, TOP_K_FINE, tm, d), F32), pltpu.SemaphoreType.DMA((2,))],
        ),
        out_shape=[jax.ShapeDtypeStruct((n_p, d), F32), jax.ShapeDtypeStruct((n_s, d), F32)],
        compiler_params=_cparams(("arbitrary",)),
        name="moe_combine",
    )(rows_flat, ys, x, meta, norm_f.reshape(1, d))


def _moe_and_final_norm(x, n_p, norm_ffn, w_coarse, b_coarse, w_fine, b_fine, wg, wu, wd, norm_f):
    n_tok, d = x.shape
    pad = ROUTER_COLS - N_GROUPS - N_EXPERTS
    w_router = jnp.concatenate([w_coarse, w_fine, jnp.zeros((d, pad), F32)], axis=1)
    b_router = jnp.concatenate([b_coarse, b_fine, jnp.zeros((pad,), F32)]).reshape(1, ROUTER_COLS)
    h, meta, meta_t, counts = _router(x, norm_ffn, w_router, b_router)

    n_asg = n_tok * TOP_K_FINE
    n_blocks = -(-n_asg // MOE_BLOCK) + N_EXPERTS
    n_rows = n_blocks * MOE_BLOCK
    cnt = counts[0, ROUTER_OFF:ROUTER_OFF + N_EXPERTS].astype(I32)
    nblk = (cnt + MOE_BLOCK - 1) // MOE_BLOCK
    padded = nblk * MOE_BLOCK
    pend = jnp.cumsum(padded)
    pstart = pend - padded
    first = cnt - (nblk - 1) * MOE_BLOCK
    first_rows = jnp.where(cnt > 0, (first + MOE_ROWS - 1) // MOE_ROWS * MOE_ROWS, 0)
    rows_flat = _assignment_rows(pstart.astype(I32), first.astype(I32), meta_t).reshape(-1)
    n_used = (pend[-1] // MOE_BLOCK).astype(I32)
    blk = jnp.arange(n_blocks, dtype=I32)
    blk_row = jnp.minimum(blk, n_used - 1) * MOE_BLOCK
    blk_exp = jnp.sum((pend[None, :] <= blk_row[:, None]).astype(I32), axis=1)
    blk_exp = jnp.minimum(blk_exp, N_EXPERTS - 1)
    blk_parts = jnp.where(blk * MOE_BLOCK == pstart[blk_exp], first_rows[blk_exp] // MOE_ROWS, MOE_PARTS)
    blk_parts = jnp.where(blk < n_used, blk_parts, 0).astype(I32)
    cend = pstart + first_rows

    xs = _dispatch(rows_flat, jnp.concatenate([pstart, cend]).astype(I32), h, n_rows)
    ys = _experts(blk_exp, blk_parts, n_used.reshape(1), xs, wg, wu, wd)
    return _combine(rows_flat, ys, x, meta, norm_f, n_p)


def _mixer(x, s_ret, s_conv, pos0, norm_mix, w_in, conv_w, conv_b, conv_ln_w, conv_ln_b, w_o,
           total_rows, row0, prev, merge_w=None, side_rope=(), side_rest=(), side_conv=()):
    b, l, d = x.shape
    x2 = x.reshape(b * l, d)
    chunk = RET_CHUNK if l % RET_CHUNK == 0 else l
    h = _rmsnorm_cast(x2, norm_mix)
    qk, kk, cast_a = _inproj_rope(h, w_in, *_rope_inputs(l, pos0, chunk, min(ROPE_TM, b * l)), side=side_rope)
    v, _ = _inproj(h, w_in, [(COL_V, W_V)], BF16)
    u = _inproj_glu(h, w_in)
    rest, cast_b = _inproj(h, w_in, [(COL_G, W_G), (COL_GATES, W_REST - W_G)], F32, side_rest)
    if s_ret is None:
        og, s_ret_new, _ = _retention_prompt(qk, kk, v, rest, b, l)
        sw, s_conv_new, cast_c = _conv_branch(u, s_conv, conv_w, conv_b, conv_ln_w, conv_ln_b, b, l,
                                              tl=min(128, l), nb=1, side=side_conv)
    else:
        og, s_ret_new = _retention_sample(qk, kk, v, rest, s_ret, b, l)
        sw, s_conv_new, cast_c = _conv_branch(u, s_conv, conv_w, conv_b, conv_ln_w, conv_ln_b, b, l,
                                              tl=l, nb=min(SUBLANES, b), side=side_conv)
    if merge_w is None:
        merge_w = cast_a[:2]
    merged = _merge(og, sw, *merge_w, rest)
    x1 = _outproj_residual(merged, w_o, x2, total_rows, row0, prev)
    return x1, s_ret_new, s_conv_new, cast_a + cast_b + cast_c


def kernel(x_prompt, x_sample, state_ret, state_conv, norm_mix, w_in, w_ret_o, conv_w, conv_b, conv_ln_w,
           conv_ln_b, w_conv_o, w_o, norm_ffn, w_coarse, b_coarse, w_fine, b_fine, w_gate, w_up, w_down, norm_f):
    depth = w_in.shape[0]
    assert depth == 1, "single-layer trunk"
    bp, lp, d = x_prompt.shape
    bs, ls, _ = x_sample.shape
    n_p = bp * lp
    n_s = bs * ls
    mix_w = (norm_mix[0], w_in[0], conv_w, conv_b, conv_ln_w, conv_ln_b, w_o[0])
    n_exp, _, d_exp = w_gate.shape[1:]
    flat = lambda w: w.reshape(n_exp * w.shape[2], w.shape[3])
    x1, ret_p, conv_p, (wr, wc, wg, wu, wd) = _mixer(
        x_prompt, None, None, 0, *mix_w, n_p + n_s, 0, None,
        side_rope=(w_ret_o[0], w_conv_o[0]), side_rest=(flat(w_gate),), side_conv=(flat(w_up), flat(w_down)))
    conv_hist = jnp.transpose(state_conv, (0, 2, 1, 3))
    x1, ret_s, conv_s, _ = _mixer(x_sample, state_ret[0], conv_hist, PAST_LEN, *mix_w, n_p + n_s, n_p, x1,
                                  merge_w=(wr, wc))
    yp, ys = _moe_and_final_norm(x1, n_p, norm_ffn[0], w_coarse[0], b_coarse[0], w_fine[0], b_fine[0],
                                 wg.reshape(n_exp, d, d_exp), wu.reshape(n_exp, d, d_exp),
                                 wd.reshape(n_exp, d_exp, d), norm_f)
    return (yp.reshape(bp, lp, d), ys.reshape(bs, ls, d), ret_p[None], conv_p, ret_s[None], conv_s)
```

```python
import functools

import numpy as np
import jax
import jax.numpy as jnp
from jax import lax
from jax.experimental import pallas as pl
from jax.experimental.pallas import tpu as pltpu

F32 = jnp.float32
BF16 = jnp.bfloat16
I32 = jnp.int32

D_MODEL = 2048
PAST_LEN = 16384
RET_HEADS = 8
RET_DK = D_MODEL // RET_HEADS
RET_DV = 2 * D_MODEL // RET_HEADS
RET_HALF = RET_DK // 2
RET_CHUNK = 256
ROPE_BASE = 10000.0
CONV_WIDTH = 31
CONV_HIST = CONV_WIDTH - 1
N_GROUPS = 4
EXPERTS_PER_GROUP = 8
N_EXPERTS = N_GROUPS * EXPERTS_PER_GROUP
TOP_K_FINE = 2
D_EXPERT = D_MODEL // 2
EPS = 1e-6

W_QK = 2 * RET_HEADS * RET_DK
W_V = RET_HEADS * RET_DV
W_G = RET_HEADS * RET_DV
COL_QK = 0
COL_V = COL_QK + W_QK
COL_G = COL_V + W_V
COL_CA = COL_G + W_G
COL_CB = COL_CA + D_MODEL
COL_GATES = COL_CB + D_MODEL
REST_G = 0
REST_GRET = REST_G + W_G
REST_GCONV = REST_GRET + D_MODEL
W_REST = REST_GCONV + D_MODEL

LANES = 128
SUBLANES = 8
VMEM_LIMIT = 56 * 1024 * 1024
MOE_ROWS = 128
MOE_PARTS = 4
MOE_BLOCK = MOE_PARTS * MOE_ROWS
ROUTER_COLS = LANES
ROUTER_OFF = N_GROUPS
NEG = -3.0e38
ISSUE_UNROLL = 8


def _cparams(sem):
    return pltpu.CompilerParams(dimension_semantics=sem, vmem_limit_bytes=VMEM_LIMIT)


def _rmsnorm_kernel(x_ref, w_ref, o_ref):
    x = x_ref[...]
    ms = jnp.mean(x * x, axis=-1, keepdims=True)
    o_ref[...] = (x * lax.rsqrt(ms + EPS) * w_ref[...]).astype(o_ref.dtype)


def _rmsnorm_cast(x, w, tm=512):
    m, d = x.shape
    tm = min(tm, m)
    return pl.pallas_call(
        _rmsnorm_kernel,
        grid=(m // tm,),
        in_specs=[pl.BlockSpec((tm, d), lambda i: (i, 0)), pl.BlockSpec((1, d), lambda i: (0, 0))],
        out_specs=pl.BlockSpec((tm, d), lambda i: (i, 0)),
        out_shape=jax.ShapeDtypeStruct((m, d), BF16),
        compiler_params=_cparams(("arbitrary",)),
        name="rmsnorm_cast",
    )(x, w.reshape(1, d))


def _side_steps(total_steps):
    return 1 << (total_steps.bit_length() - 1)


def _side_specs(arrays, n_steps, linear_step):
    def spec(a):
        rows = a.shape[0] // n_steps
        assert rows * n_steps == a.shape[0] and rows % 16 == 0, (a.shape, n_steps)
        return pl.BlockSpec((rows, a.shape[1]), lambda *g: (jnp.minimum(linear_step(*g), n_steps - 1), 0))
    return [spec(a) for a in arrays]


def _side_cast(step, n_steps, srcs, dsts):
    def cast():
        for src, dst in zip(srcs, dsts):
            dst[...] = src[...].astype(dst.dtype)

    if n_steps is None:
        cast()
    else:
        pl.when(step < n_steps)(cast)


def _side_guard(n_steps, total_steps):
    return None if n_steps == total_steps else n_steps


def _inproj_kernel(*refs, n_side, side_steps):
    a_ref, w_ref = refs[:2]
    side_in = refs[2:2 + n_side]
    o_ref = refs[2 + n_side]
    side_out = refs[3 + n_side:3 + 2 * n_side]
    wb_ref = refs[-1]

    @pl.when(pl.program_id(1) == 0)
    def _():
        wb_ref[...] = w_ref[...].astype(BF16)

    o_ref[...] = jnp.dot(a_ref[...], wb_ref[...], preferred_element_type=F32).astype(o_ref.dtype)
    if n_side:
        _side_cast(pl.program_id(0) * pl.num_programs(1) + pl.program_id(1), side_steps, side_in, side_out)


def _inproj(a, w, col_spans, out_dtype, side=(), tm=1024, tn=1024):
    m, k = a.shape
    tm = min(tm, m)
    assert all(c0 % tn == 0 and n % tn == 0 for c0, n in col_spans)
    ncols = sum(n for _, n in col_spans)
    grid = (ncols // tn, m // tm)

    def w_tile(j):
        tile, first = 0, 0
        for c0, n in col_spans:
            tile = jnp.where(j >= first, c0 // tn + j - first, tile)
            first += n // tn
        return tile

    side_steps = _side_steps(grid[0] * grid[1])
    side_specs = _side_specs(side, side_steps, lambda j, i: j * grid[1] + i)
    outs = pl.pallas_call(
        functools.partial(_inproj_kernel, n_side=len(side),
                          side_steps=_side_guard(side_steps, grid[0] * grid[1])),
        grid=grid,
        in_specs=[pl.BlockSpec((tm, k), lambda j, i: (i, 0)),
                  pl.BlockSpec((k, tn), lambda j, i: (0, w_tile(j)))] + side_specs,
        out_specs=[pl.BlockSpec((tm, tn), lambda j, i: (i, j))] + side_specs,
        out_shape=[jax.ShapeDtypeStruct((m, ncols), out_dtype)]
                  + [jax.ShapeDtypeStruct(s.shape, BF16) for s in side],
        scratch_shapes=[pltpu.VMEM((k, tn), BF16)],
        compiler_params=_cparams(("arbitrary", "arbitrary")),
        name="inproj",
    )(a, w, *side)
    return outs[0], tuple(outs[1:])


def _inproj_glu_kernel(a_ref, wa_ref, wb_ref, o_ref, wab_ref, wbb_ref):
    @pl.when(pl.program_id(1) == 0)
    def _():
        wab_ref[...] = wa_ref[...].astype(BF16)
        wbb_ref[...] = wb_ref[...].astype(BF16)

    a = a_ref[...]
    ca = jnp.dot(a, wab_ref[...], preferred_element_type=F32)
    cb = jnp.dot(a, wbb_ref[...], preferred_element_type=F32)
    o_ref[...] = ca * jax.nn.sigmoid(cb)


def _inproj_glu(a, w, tm=1024, tn=512):
    m, k = a.shape
    tm = min(tm, m)
    n = D_MODEL
    assert COL_CA % tn == 0 and COL_CB % tn == 0 and n % tn == 0
    return pl.pallas_call(
        _inproj_glu_kernel,
        grid=(n // tn, m // tm),
        in_specs=[
            pl.BlockSpec((tm, k), lambda j, i: (i, 0)),
            pl.BlockSpec((k, tn), lambda j, i: (0, COL_CA // tn + j)),
            pl.BlockSpec((k, tn), lambda j, i: (0, COL_CB // tn + j)),
        ],
        out_specs=pl.BlockSpec((tm, tn), lambda j, i: (i, j)),
        out_shape=jax.ShapeDtypeStruct((m, n), F32),
        scratch_shapes=[pltpu.VMEM((k, tn), BF16), pltpu.VMEM((k, tn), BF16)],
        compiler_params=_cparams(("arbitrary", "arbitrary")),
        name="inproj_glu",
    )(a, w, w)


ROPE_ROWS = RET_CHUNK


def _inproj_rope_kernel(*refs, nq_tiles, n_side, side_steps):
    a_ref, w_ref, cos_ref, sin_ref, kd_ref = refs[:5]
    side_in = refs[5:5 + n_side]
    qk_ref, kk_ref = refs[5 + n_side:7 + n_side]
    side_out = refs[7 + n_side:7 + 2 * n_side]
    wb_ref, acc_ref = refs[-2:]
    j = pl.program_id(0)
    if n_side:
        _side_cast(j * pl.num_programs(1) + pl.program_id(1), side_steps, side_in, side_out)

    @pl.when(pl.program_id(1) == 0)
    def _():
        wb_ref[...] = w_ref[...].astype(BF16)

    acc_ref[...] = jnp.dot(a_ref[...], wb_ref[...], preferred_element_type=F32)
    tm, tn = acc_ref.shape
    scale = jnp.where(j >= nq_tiles, RET_DK ** -0.5, 1.0).astype(F32)
    for r in range(tm // ROPE_ROWS):
        rs = slice(r * ROPE_ROWS, (r + 1) * ROPE_ROWS)
        cos = cos_ref[rs, :]
        sin = sin_ref[rs, :]
        parts = []
        for hh in range(tn // RET_DK):
            x1 = acc_ref[rs, hh * RET_DK:hh * RET_DK + RET_HALF]
            x2 = acc_ref[rs, hh * RET_DK + RET_HALF:(hh + 1) * RET_DK]
            parts += [x1 * cos - x2 * sin, x1 * sin + x2 * cos]
        rot = jnp.concatenate(parts, axis=1) * scale
        qk_ref[rs, :] = rot.astype(qk_ref.dtype)

        @pl.when(j >= nq_tiles)
        def _():
            kk_ref[rs, :] = (rot * kd_ref[...]).astype(kk_ref.dtype)


ROPE_TM = 1024


def _inproj_rope(a, w, cos_rows, sin_rows, kd_rows, side=(), tn=1024):
    m, k = a.shape
    tm = min(ROPE_TM, m)
    wq = W_QK // 2
    assert tm % ROPE_ROWS == 0 and wq % tn == 0 and tn % RET_DK == 0 and cos_rows.shape[0] % tm == 0
    nq = wq // tn
    period = cos_rows.shape[0] // tm
    grid = (2 * nq, m // tm)
    side_steps = _side_steps(grid[0] * grid[1])
    side_specs = _side_specs(side, side_steps, lambda j, i: j * grid[1] + i)
    outs = pl.pallas_call(
        functools.partial(_inproj_rope_kernel, nq_tiles=nq, n_side=len(side),
                          side_steps=_side_guard(side_steps, grid[0] * grid[1])),
        grid=grid,
        in_specs=[
            pl.BlockSpec((tm, k), lambda j, i: (i, 0)),
            pl.BlockSpec((k, tn), lambda j, i: (0, COL_QK // tn + j)),
            pl.BlockSpec((tm, RET_HALF), lambda j, i: (i % period, 0)),
            pl.BlockSpec((tm, RET_HALF), lambda j, i: (i % period, 0)),
            pl.BlockSpec((ROPE_ROWS, tn), lambda j, i: (0, jnp.maximum(j - nq, 0))),
        ] + side_specs,
        out_specs=[
            pl.BlockSpec((tm, tn), lambda j, i: (i, j)),
            pl.BlockSpec((tm, tn), lambda j, i: (jnp.where(j >= nq, i, 0), jnp.maximum(j - nq, 0))),
        ] + side_specs,
        out_shape=[jax.ShapeDtypeStruct((m, W_QK), BF16), jax.ShapeDtypeStruct((m, wq), BF16)]
                  + [jax.ShapeDtypeStruct(s.shape, BF16) for s in side],
        scratch_shapes=[pltpu.VMEM((k, tn), BF16), pltpu.VMEM((tm, tn), F32)],
        compiler_params=_cparams(("arbitrary", "arbitrary")),
        name="inproj_rope",
    )(a, w, cos_rows, sin_rows, kd_rows, *side)
    return outs[0], outs[1], tuple(outs[2:])


def _retention_tables(c):
    h = RET_HEADS
    f32 = np.float32
    log_g = np.log1p(-np.exp2(f32(-5.0) - np.arange(h, dtype=f32)))
    idx = np.arange(c, dtype=f32)
    diff = idx[:, None] - idx[None, :]
    decay = np.where(diff[None] >= 0.0, np.exp(np.maximum(diff, f32(0.0))[None] * log_g[:, None, None]), f32(0.0))
    q_dec = np.exp((idx[:, None] + f32(1.0)) * log_g[None, :])
    k_dec = np.exp((f32(c) - f32(1.0) - idx)[:, None] * log_g[None, :])
    c_dec = np.exp(f32(c) * log_g)
    return decay.astype(f32), q_dec.astype(f32), k_dec.astype(f32), c_dec.astype(f32)


def _rope_tables(pos):
    f32 = np.float32
    inv = f32(ROPE_BASE) ** (-np.arange(RET_HALF, dtype=f32) / f32(RET_HALF))
    ang = pos.astype(f32)[:, None] * inv[None, :].astype(f32)
    return np.cos(ang).astype(f32), np.sin(ang).astype(f32)


def _lane_bcast(t):
    return np.ascontiguousarray(np.broadcast_to(t.T[:, :, None], (t.shape[1], t.shape[0], LANES)))


def _rope_inputs(l, pos0, c, tm):
    rows = max(l, tm)
    assert rows % l == 0 and rows % tm == 0 and ROPE_ROWS % c == 0
    cos, sin = _rope_tables(pos0 + np.arange(l))
    _, _, k_dec, _ = _retention_tables(c)
    kd = np.repeat(k_dec, RET_DK, axis=1)
    return np.tile(cos, (rows // l, 1)), np.tile(sin, (rows // l, 1)), np.tile(kd, (ROPE_ROWS // c, 1))


def _group_norm_gate(o, g):
    ms = jnp.mean(o * o, axis=-1, keepdims=True)
    return jax.nn.silu(g) * (o * lax.rsqrt(ms + EPS))


def _ret_prompt_kernel(*refs, n_side, side_steps):
    cdec_ref, q_ref, k_ref, kk_ref, v_ref, g_ref, dec_ref, qd_ref = refs[:8]
    side_in = refs[8:8 + n_side]
    o_ref, s_ref = refs[8 + n_side:10 + n_side]
    side_out = refs[10 + n_side:10 + 2 * n_side]
    if n_side:
        _side_cast(pl.program_id(0) * pl.num_programs(1) + pl.program_id(1), side_steps, side_in, side_out)

    @pl.when(pl.program_id(1) == 0)
    def _():
        s_ref[...] = jnp.zeros_like(s_ref)

    for h in range(RET_HEADS):
        qb = q_ref[:, h * RET_DK:(h + 1) * RET_DK]
        kb = k_ref[:, h * RET_DK:(h + 1) * RET_DK]
        kkb = kk_ref[:, h * RET_DK:(h + 1) * RET_DK]
        vb = v_ref[:, h * RET_DV:(h + 1) * RET_DV]
        att = lax.dot_general(qb, kb, (((1,), (1,)), ((), ())), preferred_element_type=F32) * dec_ref[h]
        s = s_ref[0, h]
        o = (jnp.dot(att.astype(BF16), vb, preferred_element_type=F32)
             + qd_ref[h][:, :1] * jnp.dot(qb, s.astype(BF16), preferred_element_type=F32))
        s_ref[0, h] = s * cdec_ref[h] + lax.dot_general(kkb, vb, (((0,), (0,)), ((), ())),
                                                         preferred_element_type=F32)
        og = _group_norm_gate(o, g_ref[:, h * RET_DV:(h + 1) * RET_DV])
        o_ref[:, h * RET_DV:(h + 1) * RET_DV] = og.astype(o_ref.dtype)


def _retention_prompt(qk, kk, v, rest, b, l, side=()):
    c = RET_CHUNK if l % RET_CHUNK == 0 else l
    nc = l // c
    decay, q_dec, _, c_dec = _retention_tables(c)
    wq = RET_HEADS * RET_DK
    wv = RET_HEADS * RET_DV
    full3 = lambda bi, ci: (0, 0, 0)
    side_steps = _side_steps(b * nc)
    side_specs = _side_specs(side, side_steps, lambda bi, ci: bi * nc + ci)
    outs = pl.pallas_call(
        functools.partial(_ret_prompt_kernel, n_side=len(side), side_steps=_side_guard(side_steps, b * nc)),
        grid=(b, nc),
        in_specs=[
            pl.BlockSpec(memory_space=pltpu.SMEM),
            pl.BlockSpec((c, wq), lambda bi, ci: (bi * nc + ci, 0)),
            pl.BlockSpec((c, wq), lambda bi, ci: (bi * nc + ci, 1)),
            pl.BlockSpec((c, wq), lambda bi, ci: (bi * nc + ci, 0)),
            pl.BlockSpec((c, wv), lambda bi, ci: (bi * nc + ci, 0)),
            pl.BlockSpec((c, wv), lambda bi, ci: (bi * nc + ci, REST_G // wv)),
            pl.BlockSpec((RET_HEADS, c, c), full3),
            pl.BlockSpec((RET_HEADS, c, LANES), full3),
        ] + side_specs,
        out_specs=[
            pl.BlockSpec((c, wv), lambda bi, ci: (bi * nc + ci, 0)),
            pl.BlockSpec((1, RET_HEADS, RET_DK, RET_DV), lambda bi, ci: (bi, 0, 0, 0)),
        ] + side_specs,
        out_shape=[
            jax.ShapeDtypeStruct((b * l, wv), BF16),
            jax.ShapeDtypeStruct((b, RET_HEADS, RET_DK, RET_DV), F32),
        ] + [jax.ShapeDtypeStruct(s.shape, BF16) for s in side],
        compiler_params=_cparams(("arbitrary", "arbitrary")),
        name="retention_prompt",
    )(c_dec, qk, qk, kk, v, rest, decay, _lane_bcast(q_dec), *side)
    return outs[0], outs[1], tuple(outs[2:])


SAMPLE_PAD = 16


def _pad_rows(x, rows):
    return jnp.concatenate([x, jnp.zeros((rows - x.shape[0], x.shape[1]), x.dtype)], axis=0)


def _ret_sample_kernel(cdec_ref, q_ref, k_ref, kk_ref, v_ref, g_ref, dec_ref, qd_ref, s0_ref, o_ref, s_ref,
                       *, nb, l):
    for h in range(RET_HEADS):
        qf = q_ref[:, h * RET_DK:(h + 1) * RET_DK].astype(F32)
        kf = k_ref[:, h * RET_DK:(h + 1) * RET_DK].astype(F32)
        kkf = kk_ref[:, h * RET_DK:(h + 1) * RET_DK].astype(F32)
        vf = v_ref[:, h * RET_DV:(h + 1) * RET_DV].astype(F32)
        for bt in range(nb):
            rs = slice(bt * l, (bt + 1) * l)
            qb = _pad_rows(qf[rs], SAMPLE_PAD).astype(BF16)
            kb = _pad_rows(kf[rs], SAMPLE_PAD).astype(BF16)
            kkb = _pad_rows(kkf[rs], SAMPLE_PAD).astype(BF16)
            vb = _pad_rows(vf[rs], SAMPLE_PAD).astype(BF16)
            att = lax.dot_general(qb, kb, (((1,), (1,)), ((), ())), preferred_element_type=F32) * dec_ref[h]
            s = s0_ref[bt, h]
            o = (jnp.dot(att.astype(BF16), vb, preferred_element_type=F32)
                 + qd_ref[h][:, :1] * jnp.dot(qb, s.astype(BF16), preferred_element_type=F32))
            s_ref[bt, h] = s * cdec_ref[h] + lax.dot_general(kkb, vb, (((0,), (0,)), ((), ())),
                                                              preferred_element_type=F32)
            og = _group_norm_gate(o[:l], g_ref[rs, h * RET_DV:(h + 1) * RET_DV])
            o_ref[rs, h * RET_DV:(h + 1) * RET_DV] = og.astype(o_ref.dtype)


def _retention_sample(qk, kk, v, rest, s0, b, l, nb=2):
    assert l % RET_CHUNK != 0 and l <= SAMPLE_PAD and b % nb == 0, (b, l)
    decay, q_dec, _, c_dec = _retention_tables(l)
    p = SAMPLE_PAD - l
    decay = np.pad(decay, ((0, 0), (0, p), (0, p)))
    q_dec = np.pad(q_dec, ((0, p), (0, 0)))
    wq = RET_HEADS * RET_DK
    wv = RET_HEADS * RET_DV
    rows = nb * l
    full3 = lambda bi: (0, 0, 0)
    state_spec = pl.BlockSpec((nb, RET_HEADS, RET_DK, RET_DV), lambda bi: (bi, 0, 0, 0))
    return pl.pallas_call(
        functools.partial(_ret_sample_kernel, nb=nb, l=l),
        grid=(b // nb,),
        in_specs=[
            pl.BlockSpec(memory_space=pltpu.SMEM),
            pl.BlockSpec((rows, wq), lambda bi: (bi, 0)),
            pl.BlockSpec((rows, wq), lambda bi: (bi, 1)),
            pl.BlockSpec((rows, wq), lambda bi: (bi, 0)),
            pl.BlockSpec((rows, wv), lambda bi: (bi, 0)),
            pl.BlockSpec((rows, wv), lambda bi: (bi, REST_G // wv)),
            pl.BlockSpec((RET_HEADS, SAMPLE_PAD, SAMPLE_PAD), full3),
            pl.BlockSpec((RET_HEADS, SAMPLE_PAD, LANES), full3),
            state_spec,
        ],
        out_specs=[pl.BlockSpec((rows, wv), lambda bi: (bi, 0)), state_spec],
        out_shape=[
            jax.ShapeDtypeStruct((b * l, wv), BF16),
            jax.ShapeDtypeStruct((b, RET_HEADS, RET_DK, RET_DV), F32),
        ],
        compiler_params=_cparams(("arbitrary",)),
        name="retention_sample",
    )(c_dec, qk, qk, kk, v, rest, decay, _lane_bcast(q_dec), s0)


CONV_PAD = 32
CONV_COLS = LANES


def _conv_strip(block, wts, nblk, store):
    lead = CONV_PAD - CONV_HIST
    taps = [((lead + w) // SUBLANES, (lead + w) % SUBLANES) for w in range(CONV_WIDTH)]
    rowid = lax.broadcasted_iota(I32, (SUBLANES, CONV_COLS), 0)

    def partial_sums(m, shifts):
        ys = {}
        for w, (a, s) in enumerate(taps):
            if s in shifts:
                term = block(m + a) * wts[w]
                ys[s] = term if s not in ys else ys[s] + term
        return ys

    shifted = range(1, SUBLANES)
    prev = partial_sums(0, shifted)
    for m in range(nblk):
        nxt = partial_sums(m + 1, shifted)
        acc = partial_sums(m, (0,))[0]
        for s in shifted:
            acc = acc + pltpu.roll(jnp.where(rowid >= s, prev[s], nxt[s]), SUBLANES - s, axis=0)
        store(m, acc)
        prev = nxt


def _conv_kernel(*refs, tl, nb, nl, has_state, n_side, side_steps):
    u_ref = refs[0]
    buf_ref = refs[1] if has_state else None
    n_in = 2 if has_state else 1
    cw_ref, bias_ref, lnw_ref, lnb_ref = refs[n_in:n_in + 4]
    side_in = refs[n_in + 4:n_in + 4 + n_side]
    sw_ref, nbuf_ref = refs[n_in + 4 + n_side:n_in + 6 + n_side]
    side_out = refs[n_in + 6 + n_side:n_in + 6 + 2 * n_side]
    upad_ref, c_ref = refs[-2:]
    if n_side:
        _side_cast(pl.program_id(0) * nl + pl.program_id(1), side_steps, side_in, side_out)
    li = pl.program_id(1)
    lead = CONV_PAD - CONV_HIST
    d = u_ref.shape[1]
    for bt in range(nb):
        @pl.when(li == 0)
        def _():
            if has_state:
                upad_ref[bt, 0:lead, :] = jnp.zeros((lead, d), F32)
                upad_ref[bt, lead:CONV_PAD, :] = buf_ref[0, :, bt, :]
            else:
                upad_ref[bt, 0:CONV_PAD, :] = jnp.zeros((CONV_PAD, d), F32)

        r0 = bt * tl
        upad_ref[bt, CONV_PAD:CONV_PAD + tl, :] = u_ref[r0:r0 + tl, :]
        for cc in range(d // CONV_COLS):
            cs = slice(cc * CONV_COLS, (cc + 1) * CONV_COLS)
            wts = [jnp.broadcast_to(cw_ref[0, w:w + 1, cs], (SUBLANES, CONV_COLS)) for w in range(CONV_WIDTH)]
            blocks = {}

            def block(j, cs=cs, blocks=blocks):
                if j not in blocks:
                    blocks[j] = upad_ref[bt, j * SUBLANES:(j + 1) * SUBLANES, cs]
                return blocks[j]

            def store(m, acc, cs=cs):
                c_ref[m * SUBLANES:(m + 1) * SUBLANES, cs] = acc

            _conv_strip(block, wts, tl // SUBLANES, store)
        cf = c_ref[...] + bias_ref[...]
        mu = jnp.mean(cf, axis=-1, keepdims=True)
        var = jnp.mean(jnp.square(cf - mu), axis=-1, keepdims=True)
        cn = (cf - mu) * lax.rsqrt(var + EPS) * lnw_ref[...] + lnb_ref[...]
        sw_ref[r0:r0 + tl, :] = jax.nn.silu(cn).astype(sw_ref.dtype)

        @pl.when(li == nl - 1)
        def _():
            nbuf_ref[0, bt] = upad_ref[bt, tl + lead:tl + CONV_PAD, :]

        if nl > 1:
            upad_ref[bt, 0:CONV_PAD, :] = upad_ref[bt, tl:tl + CONV_PAD, :]


def _conv_branch(u, buf_t, conv_w, conv_b, ln_w, ln_b, b, l, tl, nb, side=()):
    d = D_MODEL
    nl = l // tl
    assert nb == 1 or nl == 1
    rows = nb * tl
    const2 = lambda bi, li: (0, 0)
    grid = (b // nb, nl)
    side_steps = _side_steps(grid[0] * grid[1])
    side_specs = _side_specs(side, side_steps, lambda bi, li: bi * nl + li)
    state_spec = pl.BlockSpec((1, nb, CONV_HIST, d), lambda bi, li: (0, bi, 0, 0))
    has_state = buf_t is not None
    state_in = [pl.BlockSpec((1, CONV_HIST, nb, d), lambda bi, li: (0, 0, bi, 0))] if has_state else []
    outs = pl.pallas_call(
        functools.partial(_conv_kernel, tl=tl, nb=nb, nl=nl, has_state=has_state, n_side=len(side),
                          side_steps=_side_guard(side_steps, grid[0] * grid[1])),
        grid=grid,
        in_specs=[
            pl.BlockSpec((rows, d), lambda bi, li: (bi * nl + li, 0)),
        ] + state_in + [
            pl.BlockSpec((1, CONV_WIDTH, d), lambda bi, li: (0, 0, 0)),
            pl.BlockSpec((1, d), const2),
            pl.BlockSpec((1, d), const2),
            pl.BlockSpec((1, d), const2),
        ] + side_specs,
        out_specs=[pl.BlockSpec((rows, d), lambda bi, li: (bi * nl + li, 0)), state_spec] + side_specs,
        out_shape=[
            jax.ShapeDtypeStruct((b * l, d), BF16),
            jax.ShapeDtypeStruct((1, b, CONV_HIST, d), F32),
        ] + [jax.ShapeDtypeStruct(s.shape, BF16) for s in side],
        scratch_shapes=[pltpu.VMEM((nb, CONV_PAD + tl, d), F32), pltpu.VMEM((tl, d), F32)],
        compiler_params=_cparams(("arbitrary", "arbitrary")),
        name="conv_branch",
    )(u, *([buf_t] if has_state else []), conv_w, conv_b, ln_w, ln_b, *side)
    return outs[0], outs[1], tuple(outs[2:])


def _merge_kernel(og_ref, sw_ref, wr_ref, wc_ref, gr_ref, gc_ref, o_ref):
    ret_out = jnp.dot(og_ref[...], wr_ref[...], preferred_element_type=F32)
    conv_out = jnp.dot(sw_ref[...], wc_ref[...], preferred_element_type=F32)
    merged = jax.nn.sigmoid(gr_ref[...]) * ret_out + jax.nn.sigmoid(gc_ref[...]) * conv_out
    o_ref[...] = merged.astype(o_ref.dtype)


def _merge(og, sw, w_ret_o, w_conv_o, rest, tm=512, tn=1024):
    m = og.shape[0]
    n = w_ret_o.shape[1]
    tm = min(tm, m)
    kr = og.shape[1]
    kc = sw.shape[1]
    return pl.pallas_call(
        _merge_kernel,
        grid=(n // tn, m // tm),
        in_specs=[
            pl.BlockSpec((tm, kr), lambda j, i: (i, 0)),
            pl.BlockSpec((tm, kc), lambda j, i: (i, 0)),
            pl.BlockSpec((kr, tn), lambda j, i: (0, j)),
            pl.BlockSpec((kc, tn), lambda j, i: (0, j)),
            pl.BlockSpec((tm, tn), lambda j, i: (i, REST_GRET // tn + j)),
            pl.BlockSpec((tm, tn), lambda j, i: (i, REST_GCONV // tn + j)),
        ],
        out_specs=pl.BlockSpec((tm, tn), lambda j, i: (i, j)),
        out_shape=jax.ShapeDtypeStruct((m, n), BF16),
        compiler_params=_cparams(("arbitrary", "arbitrary")),
        name="merge",
    )(og, sw, w_ret_o, w_conv_o, rest, rest)


def _outproj_kernel(a_ref, w_ref, x_ref, *rest):
    o_ref, wb_ref = rest[-2], rest[-1]

    @pl.when(pl.program_id(1) == 0)
    def _():
        wb_ref[...] = w_ref[...].astype(BF16)

    o_ref[...] = x_ref[...] + jnp.dot(a_ref[...], wb_ref[...], preferred_element_type=F32)


def _outproj_residual(a, w, x, total_rows, row0, prev=None, tm=1024, tn=1024):
    m, k = a.shape
    n = w.shape[1]
    tm = min(tm, m)
    assert row0 % tm == 0
    off = row0 // tm
    in_specs = [
        pl.BlockSpec((tm, k), lambda j, i: (i, 0)),
        pl.BlockSpec((k, tn), lambda j, i: (0, j)),
        pl.BlockSpec((tm, tn), lambda j, i: (i, j)),
    ]
    args = [a, w, x]
    aliases = {}
    if prev is not None:
        in_specs.append(pl.BlockSpec(memory_space=pl.ANY))
        args.append(prev)
        aliases = {3: 0}
    return pl.pallas_call(
        _outproj_kernel,
        grid=(n // tn, m // tm),
        in_specs=in_specs,
        out_specs=pl.BlockSpec((tm, tn), lambda j, i: (i + off, j)),
        out_shape=jax.ShapeDtypeStruct((total_rows, n), F32),
        scratch_shapes=[pltpu.VMEM((k, tn), BF16)],
        input_output_aliases=aliases,
        compiler_params=_cparams(("arbitrary", "arbitrary")),
        name="outproj_residual",
    )(*args)


def _router_kernel(x_ref, nw_ref, w_ref, b_ref, h_ref, meta_ref, meta_t_ref, cnt_ref, base_ref):
    tm = x_ref.shape[0]

    @pl.when(pl.program_id(0) == 0)
    def _():
        base_ref[...] = jnp.zeros_like(base_ref)

    x = x_ref[...]
    ms = jnp.mean(x * x, axis=-1, keepdims=True)
    h = x * lax.rsqrt(ms + EPS) * nw_ref[...]
    h_ref[...] = h
    w = w_ref[...]
    h_hi = h.astype(BF16)
    h_lo = (h - h_hi.astype(F32)).astype(BF16)
    w_hi = w.astype(BF16)
    w_lo = (w - w_hi.astype(F32)).astype(BF16)
    logits = (jnp.dot(h_hi, w_lo, preferred_element_type=F32) + jnp.dot(h_lo, w_hi, preferred_element_type=F32)
              + jnp.dot(h_hi, w_hi, preferred_element_type=F32)) + b_ref[...]
    lane = lax.broadcasted_iota(I32, (tm, ROUTER_COLS), 1)

    def first_argmax(vals):
        top = jnp.max(vals, axis=1, keepdims=True)
        return top, jnp.min(jnp.where(vals == top, lane, ROUTER_COLS), axis=1, keepdims=True)

    cmask = lane < N_GROUPS
    lc = jnp.where(cmask, logits, NEG)
    mc, grp = first_argmax(lc)
    p_grp = 1.0 / jnp.sum(jnp.where(cmask, jnp.exp(lc - mc), 0.0), axis=1, keepdims=True)
    lo = ROUTER_OFF + grp * EXPERTS_PER_GROUP
    lf = jnp.where((lane >= lo) & (lane < lo + EXPERTS_PER_GROUP), logits, NEG)
    v1, i1 = first_argmax(lf)
    v2, i2 = first_argmax(jnp.where(lane == i1, NEG, lf))
    e2 = jnp.exp(v2 - v1)
    den = 1.0 + e2
    g1 = p_grp * (1.0 / den)
    g2 = p_grp * (e2 / den)

    oh1 = lane == i1
    oh2 = lane == i2
    both = jnp.where(oh1 | oh2, 1.0, 0.0)
    row = lax.broadcasted_iota(I32, (tm, tm), 0)
    col = lax.broadcasted_iota(I32, (tm, tm), 1)
    earlier = jnp.where(col < row, 1.0, 0.0).astype(BF16)
    before = jnp.dot(earlier, both.astype(BF16), preferred_element_type=F32) + base_ref[...]
    r1 = jnp.sum(jnp.where(oh1, before, 0.0), axis=1, keepdims=True)
    r2 = jnp.sum(jnp.where(oh2, before, 0.0), axis=1, keepdims=True)
    base_ref[...] = base_ref[...] + jnp.sum(both, axis=0, keepdims=True)
    cnt_ref[...] = base_ref[...]

    fields = [(i1 - ROUTER_OFF).astype(F32), (i2 - ROUTER_OFF).astype(F32), g1, g2, r1, r2]
    meta = jnp.zeros((tm, ROUTER_COLS), F32)
    for k, f in enumerate(fields):
        meta = jnp.where(lane == k, f, meta)
    meta_ref[...] = meta
    meta_t_ref[...] = meta.T[:SUBLANES]


def _router(x, norm_w, w_router, b_router, tm=512):
    n, d = x.shape
    tm = min(tm, n)
    const2 = lambda i: (0, 0)
    return pl.pallas_call(
        _router_kernel,
        grid=(n // tm,),
        in_specs=[
            pl.BlockSpec((tm, d), lambda i: (i, 0)),
            pl.BlockSpec((1, d), const2),
            pl.BlockSpec((d, ROUTER_COLS), const2),
            pl.BlockSpec((1, ROUTER_COLS), const2),
        ],
        out_specs=[
            pl.BlockSpec((tm, d), lambda i: (i, 0)),
            pl.BlockSpec((tm, ROUTER_COLS), lambda i: (i, 0)),
            pl.BlockSpec((SUBLANES, tm), lambda i: (0, i)),
            pl.BlockSpec((1, ROUTER_COLS), const2),
        ],
        out_shape=[
            jax.ShapeDtypeStruct((n, d), F32),
            jax.ShapeDtypeStruct((n, ROUTER_COLS), F32),
            jax.ShapeDtypeStruct((SUBLANES, n), F32),
            jax.ShapeDtypeStruct((1, ROUTER_COLS), F32),
        ],
        scratch_shapes=[pltpu.VMEM((1, ROUTER_COLS), F32)],
        compiler_params=_cparams(("arbitrary",)),
        name="router",
    )(x, norm_w.reshape(1, d), w_router, b_router)


def _rows_kernel(pstart_ref, first_ref, meta_ref, rows_ref):
    expert = meta_ref[0:TOP_K_FINE, :].astype(I32)
    rank = meta_ref[4:4 + TOP_K_FINE, :].astype(I32)
    rows = rank
    for e in range(N_EXPERTS):
        first = first_ref[e]
        rows = rows + jnp.where(expert == e, pstart_ref[e] + jnp.where(rank >= first, MOE_BLOCK - first, 0), 0)
    rows_ref[...] = rows


def _assignment_rows(pstart, first, meta_t):
    n = meta_t.shape[1]
    return pl.pallas_call(
        _rows_kernel,
        grid_spec=pltpu.PrefetchScalarGridSpec(
            num_scalar_prefetch=2,
            grid=(1,),
            in_specs=[pl.BlockSpec((SUBLANES, n), lambda i, ps, fs: (0, 0))],
            out_specs=pl.BlockSpec((TOP_K_FINE, n), lambda i, ps, fs: (0, 0)),
        ),
        out_shape=jax.ShapeDtypeStruct((TOP_K_FINE, n), I32),
        compiler_params=_cparams(("arbitrary",)),
        name="moe_rows",
    )(pstart, first, meta_t)


def _dispatch_kernel(rows_ref, span_ref, h_ref, xs_hbm, zero_ref, sem, zsem):
    td = h_ref.shape[0]
    i = pl.program_id(0)

    @pl.when(i == 0)
    def _():
        zero_ref[...] = jnp.zeros_like(zero_ref)

        def fill_copy(e):
            last = pl.multiple_of(span_ref[N_EXPERTS + e] - MOE_ROWS, MOE_ROWS)
            return pltpu.make_async_copy(zero_ref, xs_hbm.at[pl.ds(last, MOE_ROWS)], zsem)

        def nonempty(e):
            return span_ref[N_EXPERTS + e] > span_ref[e]

        def start(e, carry):
            @pl.when(nonempty(e))
            def _():
                fill_copy(e).start()
            return carry

        def wait(e, carry):
            @pl.when(nonempty(e))
            def _():
                fill_copy(e).wait()
            return carry

        lax.fori_loop(0, N_EXPERTS, start, 0)
        lax.fori_loop(0, N_EXPERTS, wait, 0)

    n_tok = pl.num_programs(0) * td

    def tokens(g, carry):
        t0 = pl.multiple_of(g * SUBLANES, SUBLANES)
        for u in range(SUBLANES):
            for k in range(TOP_K_FINE):
                r = rows_ref[k * n_tok + i * td + t0 + u]
                pltpu.make_async_copy(h_ref.at[pl.ds(t0 + u, 1)], xs_hbm.at[pl.ds(r, 1)], sem).start()
        return carry

    lax.fori_loop(0, td // SUBLANES, tokens, 0)
    for _ in range(TOP_K_FINE):
        pltpu.make_async_copy(h_ref, xs_hbm.at[pl.ds(0, td)], sem).wait()


def _dispatch(rows_flat, span, h, n_rows, td=1024):
    n_tok, d = h.shape
    td = min(td, n_tok)
    return pl.pallas_call(
        _dispatch_kernel,
        grid_spec=pltpu.PrefetchScalarGridSpec(
            num_scalar_prefetch=2,
            grid=(n_tok // td,),
            in_specs=[pl.BlockSpec((td, d), lambda i, rows, span: (i, 0))],
            out_specs=pl.BlockSpec(memory_space=pl.ANY),
            scratch_shapes=[pltpu.VMEM((MOE_ROWS, d), F32), pltpu.SemaphoreType.DMA(()),
                            pltpu.SemaphoreType.DMA(())],
        ),
        out_shape=jax.ShapeDtypeStruct((n_rows, d), F32),
        compiler_params=_cparams(("arbitrary",)),
        name="moe_dispatch",
    )(rows_flat, span, h)


def _expert_kernel(be_ref, nh_ref, nu_ref, x_ref, wg_ref, wu_ref, wd_ref, o_ref):
    del be_ref
    nh = nh_ref[pl.program_id(0)]

    def ffn(rows):
        xb = x_ref[0:rows, :].astype(BF16)
        a = jnp.dot(xb, wg_ref[0], preferred_element_type=F32)
        b = jnp.dot(xb, wu_ref[0], preferred_element_type=F32)
        hb = (jax.nn.silu(a) * b).astype(BF16)
        o_ref[0:rows, :] = jnp.dot(hb, wd_ref[0], preferred_element_type=F32)

    for parts in range(1, MOE_PARTS + 1):
        @pl.when(nh == parts)
        def _():
            ffn(parts * MOE_ROWS)
            if parts < MOE_PARTS:
                o_ref[parts * MOE_ROWS:, :] = jnp.zeros((MOE_BLOCK - parts * MOE_ROWS, o_ref.shape[1]), F32)


def _experts(blk_exp, blk_parts, n_used, xs, wg, wu, wd):
    n_rows, d = xs.shape
    n_blocks = n_rows // MOE_BLOCK
    de = wg.shape[2]
    row_map = lambda i, be, nh, nu: (jnp.minimum(i, jnp.maximum(nu[0] - 1, 0)), 0)
    w_map = lambda i, be, nh, nu: (be[i], 0, 0)
    return pl.pallas_call(
        _expert_kernel,
        grid_spec=pltpu.PrefetchScalarGridSpec(
            num_scalar_prefetch=3,
            grid=(n_blocks,),
            in_specs=[
                pl.BlockSpec((MOE_BLOCK, d), row_map),
                pl.BlockSpec((1, d, de), w_map),
                pl.BlockSpec((1, d, de), w_map),
                pl.BlockSpec((1, de, d), w_map),
            ],
            out_specs=pl.BlockSpec((MOE_BLOCK, d), row_map),
        ),
        out_shape=jax.ShapeDtypeStruct((n_rows, d), F32),
        compiler_params=_cparams(("arbitrary",)),
        name="moe_experts",
    )(blk_exp, blk_parts, n_used, xs, wg, wu, wd)


def _combine_kernel(rows_ref, ys_hbm, x_ref, meta_ref, nf_ref, op_ref, os_ref, buf_ref, sem, *, p_tiles):
    tm = x_ref.shape[0]
    i = pl.program_id(0)
    nt = pl.num_programs(0)
    n_tok = nt * tm
    slot = i % 2

    def gather(tile, dst_slot):
        def tokens(g, carry):
            t0 = pl.multiple_of(g * SUBLANES, SUBLANES)
            for u in range(SUBLANES):
                for k in range(TOP_K_FINE):
                    r = rows_ref[k * n_tok + tile * tm + t0 + u]
                    pltpu.make_async_copy(ys_hbm.at[pl.ds(r, 1)], buf_ref.at[dst_slot, k, pl.ds(t0 + u, 1)],
                                          sem.at[dst_slot]).start()
            return carry

        lax.fori_loop(0, tm // SUBLANES, tokens, 0)

    @pl.when(i == 0)
    def _():
        gather(0, 0)

    @pl.when(i + 1 < nt)
    def _():
        gather(i + 1, 1 - slot)

    for k in range(TOP_K_FINE):
        pltpu.make_async_copy(ys_hbm.at[pl.ds(0, tm)], buf_ref.at[slot, k], sem.at[slot]).wait()
    meta = meta_ref[...]
    y = x_ref[...] + (buf_ref[slot, 0] * meta[:, 2:3] + buf_ref[slot, 1] * meta[:, 3:4])
    ms = jnp.mean(y * y, axis=-1, keepdims=True)
    out = y * lax.rsqrt(ms + EPS) * nf_ref[...]

    @pl.when(i < p_tiles)
    def _():
        op_ref[...] = out

    @pl.when(i >= p_tiles)
    def _():
        os_ref[...] = out


def _combine(rows_flat, ys, x, meta, norm_f, n_p, tm=512):
    n, d = x.shape
    n_s = n - n_p
    tm = min(tm, n_p, n_s)
    assert n_p % tm == 0 and n_s % tm == 0
    p_tiles = n_p // tm
    return pl.pallas_call(
        functools.partial(_combine_kernel, p_tiles=p_tiles),
        grid_spec=pltpu.PrefetchScalarGridSpec(
            num_scalar_prefetch=1,
            grid=(n // tm,),
            in_specs=[
                pl.BlockSpec(memory_space=pl.ANY),
                pl.BlockSpec((tm, d), lambda i, rows: (i, 0)),
                pl.BlockSpec((tm, ROUTER_COLS), lambda i, rows: (i, 0)),
                pl.BlockSpec((1, d), lambda i, rows: (0, 0)),
            ],
            out_specs=[
                pl.BlockSpec((tm, d), lambda i, rows: (jnp.minimum(i, p_tiles - 1), 0)),
                pl.BlockSpec((tm, d), lambda i, rows: (jnp.maximum(i - p_tiles, 0), 0)),
            ],
            scratch_shapes=[pltpu.VMEM((2, TOP_K_FINE, tm, d), F32), pltpu.SemaphoreType.DMA((2,))],
        ),
        out_shape=[jax.ShapeDtypeStruct((n_p, d), F32), jax.ShapeDtypeStruct((n_s, d), F32)],
        compiler_params=_cparams(("arbitrary",)),
        name="moe_combine",
    )(rows_flat, ys, x, meta, norm_f.reshape(1, d))


def _moe_and_final_norm(x, n_p, norm_ffn, w_coarse, b_coarse, w_fine, b_fine, wg, wu, wd, norm_f):
    n_tok, d = x.shape
    pad = ROUTER_COLS - N_GROUPS - N_EXPERTS
    w_router = jnp.concatenate([w_coarse, w_fine, jnp.zeros((d, pad), F32)], axis=1)
    b_router = jnp.concatenate([b_coarse, b_fine, jnp.zeros((pad,), F32)]).reshape(1, ROUTER_COLS)
    h, meta, meta_t, counts = _router(x, norm_ffn, w_router, b_router)

    n_asg = n_tok * TOP_K_FINE
    n_blocks = -(-n_asg // MOE_BLOCK) + N_EXPERTS
    n_rows = n_blocks * MOE_BLOCK
    cnt = counts[0, ROUTER_OFF:ROUTER_OFF + N_EXPERTS].astype(I32)
    nblk = (cnt + MOE_BLOCK - 1) // MOE_BLOCK
    padded = nblk * MOE_BLOCK
    pend = jnp.cumsum(padded)
    pstart = pend - padded
    first = cnt - (nblk - 1) * MOE_BLOCK
    first_rows = jnp.where(cnt > 0, (first + MOE_ROWS - 1) // MOE_ROWS * MOE_ROWS, 0)
    rows_flat = _assignment_rows(pstart.astype(I32), first.astype(I32), meta_t).reshape(-1)
    n_used = (pend[-1] // MOE_BLOCK).astype(I32)
    blk = jnp.arange(n_blocks, dtype=I32)
    blk_row = jnp.minimum(blk, n_used - 1) * MOE_BLOCK
    blk_exp = jnp.sum((pend[None, :] <= blk_row[:, None]).astype(I32), axis=1)
    blk_exp = jnp.minimum(blk_exp, N_EXPERTS - 1)
    blk_parts = jnp.where(blk * MOE_BLOCK == pstart[blk_exp], first_rows[blk_exp] // MOE_ROWS, MOE_PARTS)
    blk_parts = jnp.where(blk < n_used, blk_parts, 0).astype(I32)
    cend = pstart + first_rows

    xs = _dispatch(rows_flat, jnp.concatenate([pstart, cend]).astype(I32), h, n_rows)
    ys = _experts(blk_exp, blk_parts, n_used.reshape(1), xs, wg, wu, wd)
    return _combine(rows_flat, ys, x, meta, norm_f, n_p)


def _mixer(x, s_ret, s_conv, pos0, norm_mix, w_in, conv_w, conv_b, conv_ln_w, conv_ln_b, w_o,
           total_rows, row0, prev, merge_w=None, side_rope=(), side_rest=(), side_conv=()):
    b, l, d = x.shape
    x2 = x.reshape(b * l, d)
    chunk = RET_CHUNK if l % RET_CHUNK == 0 else l
    h = _rmsnorm_cast(x2, norm_mix)
    qk, kk, cast_a = _inproj_rope(h, w_in, *_rope_inputs(l, pos0, chunk, min(ROPE_TM, b * l)), side=side_rope)
    v, _ = _inproj(h, w_in, [(COL_V, W_V)], BF16, tm=512, tn=2048)
    u = _inproj_glu(h, w_in)
    rest, cast_b = _inproj(h, w_in, [(COL_G, W_G), (COL_GATES, W_REST - W_G)], F32, side_rest)
    if s_ret is None:
        og, s_ret_new, _ = _retention_prompt(qk, kk, v, rest, b, l)
        sw, s_conv_new, cast_c = _conv_branch(u, s_conv, conv_w, conv_b, conv_ln_w, conv_ln_b, b, l,
                                              tl=min(128, l), nb=1, side=side_conv)
    else:
        og, s_ret_new = _retention_sample(qk, kk, v, rest, s_ret, b, l)
        sw, s_conv_new, cast_c = _conv_branch(u, s_conv, conv_w, conv_b, conv_ln_w, conv_ln_b, b, l,
                                              tl=l, nb=min(SUBLANES, b), side=side_conv)
    if merge_w is None:
        merge_w = cast_a[:2]
    merged = _merge(og, sw, *merge_w, rest)
    x1 = _outproj_residual(merged, w_o, x2, total_rows, row0, prev)
    return x1, s_ret_new, s_conv_new, cast_a + cast_b + cast_c


def kernel(x_prompt, x_sample, state_ret, state_conv, norm_mix, w_in, w_ret_o, conv_w, conv_b, conv_ln_w,
           conv_ln_b, w_conv_o, w_o, norm_ffn, w_coarse, b_coarse, w_fine, b_fine, w_gate, w_up, w_down, norm_f):
    depth = w_in.shape[0]
    assert depth == 1, "single-layer trunk"
    bp, lp, d = x_prompt.shape
    bs, ls, _ = x_sample.shape
    n_p = bp * lp
    n_s = bs * ls
    mix_w = (norm_mix[0], w_in[0], conv_w, conv_b, conv_ln_w, conv_ln_b, w_o[0])
    n_exp, _, d_exp = w_gate.shape[1:]
    flat = lambda w: w.reshape(n_exp * w.shape[2], w.shape[3])
    x1, ret_p, conv_p, (wr, wc, wg, wu, wd) = _mixer(
        x_prompt, None, None, 0, *mix_w, n_p + n_s, 0, None,
        side_rope=(w_ret_o[0], w_conv_o[0]), side_rest=(flat(w_gate),), side_conv=(flat(w_up), flat(w_down)))
    conv_hist = jnp.transpose(state_conv, (0, 2, 1, 3))
    x1, ret_s, conv_s, _ = _mixer(x_sample, state_ret[0], conv_hist, PAST_LEN, *mix_w, n_p + n_s, n_p, x1,
                                  merge_w=(wr, wc))
    yp, ys = _moe_and_final_norm(x1, n_p, norm_ffn[0], w_coarse[0], b_coarse[0], w_fine[0], b_fine[0],
                                 wg.reshape(n_exp, d, d_exp), wu.reshape(n_exp, d, d_exp),
                                 wd.reshape(n_exp, d_exp, d), norm_f)
    return (yp.reshape(bp, lp, d), ys.reshape(bs, ls, d), ret_p[None], conv_p, ret_s[None], conv_s)
```

```python
import functools

import numpy as np
import jax
import jax.numpy as jnp
from jax import lax
from jax.experimental import pallas as pl
from jax.experimental.pallas import tpu as pltpu

F32 = jnp.float32
BF16 = jnp.bfloat16
I32 = jnp.int32

D_MODEL = 2048
PAST_LEN = 16384
RET_HEADS = 8
RET_DK = D_MODEL // RET_HEADS
RET_DV = 2 * D_MODEL // RET_HEADS
RET_HALF = RET_DK // 2
RET_CHUNK = 256
ROPE_BASE = 10000.0
CONV_WIDTH = 31
CONV_HIST = CONV_WIDTH - 1
N_GROUPS = 4
EXPERTS_PER_GROUP = 8
N_EXPERTS = N_GROUPS * EXPERTS_PER_GROUP
TOP_K_FINE = 2
D_EXPERT = D_MODEL // 2
EPS = 1e-6

W_QK = 2 * RET_HEADS * RET_DK
W_V = RET_HEADS * RET_DV
W_G = RET_HEADS * RET_DV
COL_QK = 0
COL_V = COL_QK + W_QK
COL_G = COL_V + W_V
COL_CA = COL_G + W_G
COL_CB = COL_CA + D_MODEL
COL_GATES = COL_CB + D_MODEL
REST_G = 0
REST_GRET = REST_G + W_G
REST_GCONV = REST_GRET + D_MODEL
W_REST = REST_GCONV + D_MODEL

LANES = 128
SUBLANES = 8
VMEM_LIMIT = 58 * 1024 * 1024
MOE_ROWS = 128
MOE_PARTS = 4
MOE_BLOCK = MOE_PARTS * MOE_ROWS
ROUTER_COLS = LANES
ROUTER_OFF = N_GROUPS
NEG = -3.0e38
ISSUE_UNROLL = 8


def _cparams(sem):
    return pltpu.CompilerParams(dimension_semantics=sem, vmem_limit_bytes=VMEM_LIMIT)


def _rmsnorm_kernel(x_ref, w_ref, o_ref):
    x = x_ref[...]
    ms = jnp.mean(x * x, axis=-1, keepdims=True)
    o_ref[...] = (x * lax.rsqrt(ms + EPS) * w_ref[...]).astype(o_ref.dtype)


def _rmsnorm_cast(x, w, tm=512):
    m, d = x.shape
    tm = min(tm, m)
    return pl.pallas_call(
        _rmsnorm_kernel,
        grid=(m // tm,),
        in_specs=[pl.BlockSpec((tm, d), lambda i: (i, 0)), pl.BlockSpec((1, d), lambda i: (0, 0))],
        out_specs=pl.BlockSpec((tm, d), lambda i: (i, 0)),
        out_shape=jax.ShapeDtypeStruct((m, d), BF16),
        compiler_params=_cparams(("arbitrary",)),
        name="rmsnorm_cast",
    )(x, w.reshape(1, d))


def _side_steps(total_steps):
    return 1 << (total_steps.bit_length() - 1)


def _side_specs(arrays, n_steps, linear_step):
    def spec(a):
        rows = a.shape[0] // n_steps
        assert rows * n_steps == a.shape[0] and rows % 16 == 0, (a.shape, n_steps)
        return pl.BlockSpec((rows, a.shape[1]), lambda *g: (jnp.minimum(linear_step(*g), n_steps - 1), 0))
    return [spec(a) for a in arrays]


def _side_cast(step, n_steps, srcs, dsts):
    def cast():
        for src, dst in zip(srcs, dsts):
            dst[...] = src[...].astype(dst.dtype)

    if n_steps is None:
        cast()
    else:
        pl.when(step < n_steps)(cast)


def _side_guard(n_steps, total_steps):
    return None if n_steps == total_steps else n_steps


def _inproj_kernel(*refs, n_side, side_steps, emit_w):
    a_ref, w_ref = refs[:2]
    side_in = refs[2:2 + n_side]
    o_ref = refs[2 + n_side]
    n_out = 2 if emit_w else 1
    side_out = refs[2 + n_side + n_out:2 + n_out + 2 * n_side]
    wb_ref = refs[-1]

    @pl.when(pl.program_id(1) == 0)
    def _():
        wb_ref[...] = w_ref[...].astype(BF16)
        if emit_w:
            refs[3 + n_side][...] = wb_ref[...]

    o_ref[...] = jnp.dot(a_ref[...], wb_ref[...], preferred_element_type=F32).astype(o_ref.dtype)
    if n_side:
        _side_cast(pl.program_id(0) * pl.num_programs(1) + pl.program_id(1), side_steps, side_in, side_out)


def _inproj(a, w, col_spans, out_dtype, side=(), emit_w=False, tm=1024, tn=1024):
    m, k = a.shape
    tm = min(tm, m)
    assert all(c0 % tn == 0 and n % tn == 0 for c0, n in col_spans)
    ncols = sum(n for _, n in col_spans)
    grid = (ncols // tn, m // tm)
    w_local = w.shape[1] == ncols

    def w_tile(j):
        if w_local:
            return j
        tile, first = 0, 0
        for c0, n in col_spans:
            tile = jnp.where(j >= first, c0 // tn + j - first, tile)
            first += n // tn
        return tile

    side_steps = _side_steps(grid[0] * grid[1])
    side_specs = _side_specs(side, side_steps, lambda j, i: j * grid[1] + i)
    w_out_spec = [pl.BlockSpec((k, tn), lambda j, i: (0, j))] if emit_w else []
    w_out_shape = [jax.ShapeDtypeStruct((k, ncols), BF16)] if emit_w else []
    outs = pl.pallas_call(
        functools.partial(_inproj_kernel, n_side=len(side), emit_w=emit_w,
                          side_steps=_side_guard(side_steps, grid[0] * grid[1])),
        grid=grid,
        in_specs=[pl.BlockSpec((tm, k), lambda j, i: (i, 0)),
                  pl.BlockSpec((k, tn), lambda j, i: (0, w_tile(j)))] + side_specs,
        out_specs=[pl.BlockSpec((tm, tn), lambda j, i: (i, j))] + w_out_spec + side_specs,
        out_shape=[jax.ShapeDtypeStruct((m, ncols), out_dtype)] + w_out_shape
                  + [jax.ShapeDtypeStruct(s.shape, BF16) for s in side],
        scratch_shapes=[pltpu.VMEM((k, tn), BF16)],
        compiler_params=_cparams(("arbitrary", "arbitrary")),
        name="inproj",
    )(a, w, *side)
    n_out = 2 if emit_w else 1
    return outs[0], (outs[1] if emit_w else None), tuple(outs[n_out:])


def _inproj_glu_kernel(*refs, emit_w):
    a_ref, wa_ref, wb_ref, o_ref = refs[:4]
    wab_ref, wbb_ref = refs[-2:]

    @pl.when(pl.program_id(1) == 0)
    def _():
        wab_ref[...] = wa_ref[...].astype(BF16)
        wbb_ref[...] = wb_ref[...].astype(BF16)
        if emit_w:
            refs[4][...] = wab_ref[...]
            refs[5][...] = wbb_ref[...]

    a = a_ref[...]
    ca = jnp.dot(a, wab_ref[...], preferred_element_type=F32)
    cb = jnp.dot(a, wbb_ref[...], preferred_element_type=F32)
    o_ref[...] = ca * jax.nn.sigmoid(cb)


def _inproj_glu(a, w_ca, w_cb, emit_w=False, tm=1024, tn=512):
    m, k = a.shape
    tm = min(tm, m)
    n = D_MODEL
    assert COL_CA % tn == 0 and COL_CB % tn == 0 and n % tn == 0
    off_a = 0 if w_ca.shape[1] == n else COL_CA // tn
    off_b = 0 if w_cb.shape[1] == n else COL_CB // tn
    w_spec = pl.BlockSpec((k, tn), lambda j, i: (0, j))
    outs = pl.pallas_call(
        functools.partial(_inproj_glu_kernel, emit_w=emit_w),
        grid=(n // tn, m // tm),
        in_specs=[
            pl.BlockSpec((tm, k), lambda j, i: (i, 0)),
            pl.BlockSpec((k, tn), lambda j, i: (0, off_a + j)),
            pl.BlockSpec((k, tn), lambda j, i: (0, off_b + j)),
        ],
        out_specs=[pl.BlockSpec((tm, tn), lambda j, i: (i, j))] + ([w_spec, w_spec] if emit_w else []),
        out_shape=[jax.ShapeDtypeStruct((m, n), F32)]
                  + ([jax.ShapeDtypeStruct((k, n), BF16)] * 2 if emit_w else []),
        scratch_shapes=[pltpu.VMEM((k, tn), BF16), pltpu.VMEM((k, tn), BF16)],
        compiler_params=_cparams(("arbitrary", "arbitrary")),
        name="inproj_glu",
    )(a, w_ca, w_cb)
    return outs[0], ((outs[1], outs[2]) if emit_w else None)


ROPE_ROWS = RET_CHUNK


def _inproj_rope_kernel(*refs, nq_tiles, n_side, side_steps, emit_w):
    a_ref, w_ref, cos_ref, sin_ref, kd_ref = refs[:5]
    side_in = refs[5:5 + n_side]
    qk_ref, kk_ref = refs[5 + n_side:7 + n_side]
    n_out = 3 if emit_w else 2
    side_out = refs[5 + n_side + n_out:5 + n_out + 2 * n_side]
    wb_ref, acc_ref = refs[-2:]
    j = pl.program_id(0)
    if n_side:
        _side_cast(j * pl.num_programs(1) + pl.program_id(1), side_steps, side_in, side_out)

    @pl.when(pl.program_id(1) == 0)
    def _():
        wb_ref[...] = w_ref[...].astype(BF16)
        if emit_w:
            refs[7 + n_side][...] = wb_ref[...]

    acc_ref[...] = jnp.dot(a_ref[...], wb_ref[...], preferred_element_type=F32)
    tm, tn = acc_ref.shape
    scale = jnp.where(j >= nq_tiles, RET_DK ** -0.5, 1.0).astype(F32)
    for r in range(tm // ROPE_ROWS):
        rs = slice(r * ROPE_ROWS, (r + 1) * ROPE_ROWS)
        cos = cos_ref[rs, :]
        sin = sin_ref[rs, :]
        parts = []
        for hh in range(tn // RET_DK):
            x1 = acc_ref[rs, hh * RET_DK:hh * RET_DK + RET_HALF]
            x2 = acc_ref[rs, hh * RET_DK + RET_HALF:(hh + 1) * RET_DK]
            parts += [x1 * cos - x2 * sin, x1 * sin + x2 * cos]
        rot = jnp.concatenate(parts, axis=1) * scale
        qk_ref[rs, :] = rot.astype(qk_ref.dtype)

        @pl.when(j >= nq_tiles)
        def _():
            kk_ref[rs, :] = (rot * kd_ref[...]).astype(kk_ref.dtype)


ROPE_TM = 1024


def _inproj_rope(a, w, cos_rows, sin_rows, kd_rows, side=(), emit_w=False, tn=1024):
    m, k = a.shape
    tm = min(ROPE_TM, m)
    wq = W_QK // 2
    assert tm % ROPE_ROWS == 0 and wq % tn == 0 and tn % RET_DK == 0 and cos_rows.shape[0] % tm == 0
    nq = wq // tn
    period = cos_rows.shape[0] // tm
    grid = (2 * nq, m // tm)
    side_steps = _side_steps(grid[0] * grid[1])
    side_specs = _side_specs(side, side_steps, lambda j, i: j * grid[1] + i)
    w_out_spec = [pl.BlockSpec((k, tn), lambda j, i: (0, j))] if emit_w else []
    w_out_shape = [jax.ShapeDtypeStruct((k, W_QK), BF16)] if emit_w else []
    outs = pl.pallas_call(
        functools.partial(_inproj_rope_kernel, nq_tiles=nq, n_side=len(side), emit_w=emit_w,
                          side_steps=_side_guard(side_steps, grid[0] * grid[1])),
        grid=grid,
        in_specs=[
            pl.BlockSpec((tm, k), lambda j, i: (i, 0)),
            pl.BlockSpec((k, tn), lambda j, i: (0, COL_QK // tn + j)),
            pl.BlockSpec((tm, RET_HALF), lambda j, i: (i % period, 0)),
            pl.BlockSpec((tm, RET_HALF), lambda j, i: (i % period, 0)),
            pl.BlockSpec((ROPE_ROWS, tn), lambda j, i: (0, jnp.maximum(j - nq, 0))),
        ] + side_specs,
        out_specs=[
            pl.BlockSpec((tm, tn), lambda j, i: (i, j)),
            pl.BlockSpec((tm, tn), lambda j, i: (jnp.where(j >= nq, i, 0), jnp.maximum(j - nq, 0))),
        ] + w_out_spec + side_specs,
        out_shape=[jax.ShapeDtypeStruct((m, W_QK), BF16), jax.ShapeDtypeStruct((m, wq), BF16)] + w_out_shape
                  + [jax.ShapeDtypeStruct(s.shape, BF16) for s in side],
        scratch_shapes=[pltpu.VMEM((k, tn), BF16), pltpu.VMEM((tm, tn), F32)],
        compiler_params=_cparams(("arbitrary", "arbitrary")),
        name="inproj_rope",
    )(a, w, cos_rows, sin_rows, kd_rows, *side)
    n_out = 3 if emit_w else 2
    return outs[0], outs[1], (outs[2] if emit_w else None), tuple(outs[n_out:])


def _retention_tables(c):
    h = RET_HEADS
    f32 = np.float32
    log_g = np.log1p(-np.exp2(f32(-5.0) - np.arange(h, dtype=f32)))
    idx = np.arange(c, dtype=f32)
    diff = idx[:, None] - idx[None, :]
    decay = np.where(diff[None] >= 0.0, np.exp(np.maximum(diff, f32(0.0))[None] * log_g[:, None, None]), f32(0.0))
    q_dec = np.exp((idx[:, None] + f32(1.0)) * log_g[None, :])
    k_dec = np.exp((f32(c) - f32(1.0) - idx)[:, None] * log_g[None, :])
    c_dec = np.exp(f32(c) * log_g)
    return decay.astype(f32), q_dec.astype(f32), k_dec.astype(f32), c_dec.astype(f32)


def _rope_tables(pos):
    f32 = np.float32
    inv = f32(ROPE_BASE) ** (-np.arange(RET_HALF, dtype=f32) / f32(RET_HALF))
    ang = pos.astype(f32)[:, None] * inv[None, :].astype(f32)
    return np.cos(ang).astype(f32), np.sin(ang).astype(f32)


def _lane_bcast(t):
    return np.ascontiguousarray(np.broadcast_to(t.T[:, :, None], (t.shape[1], t.shape[0], LANES)))


def _rope_inputs(l, pos0, c, tm):
    rows = max(l, tm)
    assert rows % l == 0 and rows % tm == 0 and ROPE_ROWS % c == 0
    cos, sin = _rope_tables(pos0 + np.arange(l))
    _, _, k_dec, _ = _retention_tables(c)
    kd = np.repeat(k_dec, RET_DK, axis=1)
    return np.tile(cos, (rows // l, 1)), np.tile(sin, (rows // l, 1)), np.tile(kd, (ROPE_ROWS // c, 1))


def _group_norm_gate(o, g):
    ms = jnp.mean(o * o, axis=-1, keepdims=True)
    return jax.nn.silu(g) * (o * lax.rsqrt(ms + EPS))


def _ret_prompt_kernel(*refs, n_side, side_steps):
    cdec_ref, q_ref, k_ref, kk_ref, v_ref, g_ref, dec_ref, qd_ref = refs[:8]
    side_in = refs[8:8 + n_side]
    o_ref, s_ref = refs[8 + n_side:10 + n_side]
    side_out = refs[10 + n_side:10 + 2 * n_side]
    if n_side:
        _side_cast(pl.program_id(0) * pl.num_programs(1) + pl.program_id(1), side_steps, side_in, side_out)

    @pl.when(pl.program_id(1) == 0)
    def _():
        s_ref[...] = jnp.zeros_like(s_ref)

    for h in range(RET_HEADS):
        qb = q_ref[:, h * RET_DK:(h + 1) * RET_DK]
        kb = k_ref[:, h * RET_DK:(h + 1) * RET_DK]
        kkb = kk_ref[:, h * RET_DK:(h + 1) * RET_DK]
        vb = v_ref[:, h * RET_DV:(h + 1) * RET_DV]
        att = lax.dot_general(qb, kb, (((1,), (1,)), ((), ())), preferred_element_type=F32) * dec_ref[h]
        s = s_ref[0, h]
        o = (jnp.dot(att.astype(BF16), vb, preferred_element_type=F32)
             + qd_ref[h][:, :1] * jnp.dot(qb, s.astype(BF16), preferred_element_type=F32))
        s_ref[0, h] = s * cdec_ref[h] + lax.dot_general(kkb, vb, (((0,), (0,)), ((), ())),
                                                         preferred_element_type=F32)
        og = _group_norm_gate(o, g_ref[:, h * RET_DV:(h + 1) * RET_DV])
        o_ref[:, h * RET_DV:(h + 1) * RET_DV] = og.astype(o_ref.dtype)


def _retention_prompt(qk, kk, v, rest, b, l, side=()):
    c = RET_CHUNK if l % RET_CHUNK == 0 else l
    nc = l // c
    decay, q_dec, _, c_dec = _retention_tables(c)
    wq = RET_HEADS * RET_DK
    wv = RET_HEADS * RET_DV
    full3 = lambda bi, ci: (0, 0, 0)
    side_steps = _side_steps(b * nc)
    side_specs = _side_specs(side, side_steps, lambda bi, ci: bi * nc + ci)
    outs = pl.pallas_call(
        functools.partial(_ret_prompt_kernel, n_side=len(side), side_steps=_side_guard(side_steps, b * nc)),
        grid=(b, nc),
        in_specs=[
            pl.BlockSpec(memory_space=pltpu.SMEM),
            pl.BlockSpec((c, wq), lambda bi, ci: (bi * nc + ci, 0)),
            pl.BlockSpec((c, wq), lambda bi, ci: (bi * nc + ci, 1)),
            pl.BlockSpec((c, wq), lambda bi, ci: (bi * nc + ci, 0)),
            pl.BlockSpec((c, wv), lambda bi, ci: (bi * nc + ci, 0)),
            pl.BlockSpec((c, wv), lambda bi, ci: (bi * nc + ci, REST_G // wv)),
            pl.BlockSpec((RET_HEADS, c, c), full3),
            pl.BlockSpec((RET_HEADS, c, LANES), full3),
        ] + side_specs,
        out_specs=[
            pl.BlockSpec((c, wv), lambda bi, ci: (bi * nc + ci, 0)),
            pl.BlockSpec((1, RET_HEADS, RET_DK, RET_DV), lambda bi, ci: (bi, 0, 0, 0)),
        ] + side_specs,
        out_shape=[
            jax.ShapeDtypeStruct((b * l, wv), BF16),
            jax.ShapeDtypeStruct((b, RET_HEADS, RET_DK, RET_DV), F32),
        ] + [jax.ShapeDtypeStruct(s.shape, BF16) for s in side],
        compiler_params=_cparams(("arbitrary", "arbitrary")),
        name="retention_prompt",
    )(c_dec, qk, qk, kk, v, rest, decay, _lane_bcast(q_dec), *side)
    return outs[0], outs[1], tuple(outs[2:])


SAMPLE_PAD = 16


def _pad_rows(x, rows):
    return jnp.concatenate([x, jnp.zeros((rows - x.shape[0], x.shape[1]), x.dtype)], axis=0)


def _ret_sample_kernel(cdec_ref, q_ref, k_ref, kk_ref, v_ref, g_ref, dec_ref, qd_ref, s0_ref, o_ref, s_ref,
                       *, nb, l):
    for h in range(RET_HEADS):
        qf = q_ref[:, h * RET_DK:(h + 1) * RET_DK].astype(F32)
        kf = k_ref[:, h * RET_DK:(h + 1) * RET_DK].astype(F32)
        kkf = kk_ref[:, h * RET_DK:(h + 1) * RET_DK].astype(F32)
        vf = v_ref[:, h * RET_DV:(h + 1) * RET_DV].astype(F32)
        for bt in range(nb):
            rs = slice(bt * l, (bt + 1) * l)
            qb = _pad_rows(qf[rs], SAMPLE_PAD).astype(BF16)
            kb = _pad_rows(kf[rs], SAMPLE_PAD).astype(BF16)
            kkb = _pad_rows(kkf[rs], SAMPLE_PAD).astype(BF16)
            vb = _pad_rows(vf[rs], SAMPLE_PAD).astype(BF16)
            att = lax.dot_general(qb, kb, (((1,), (1,)), ((), ())), preferred_element_type=F32) * dec_ref[h]
            s = s0_ref[bt, h]
            o = (jnp.dot(att.astype(BF16), vb, preferred_element_type=F32)
                 + qd_ref[h][:, :1] * jnp.dot(qb, s.astype(BF16), preferred_element_type=F32))
            s_ref[bt, h] = s * cdec_ref[h] + lax.dot_general(kkb, vb, (((0,), (0,)), ((), ())),
                                                              preferred_element_type=F32)
            og = _group_norm_gate(o[:l], g_ref[rs, h * RET_DV:(h + 1) * RET_DV])
            o_ref[rs, h * RET_DV:(h + 1) * RET_DV] = og.astype(o_ref.dtype)


def _retention_sample(qk, kk, v, rest, s0, b, l, nb=2):
    assert l % RET_CHUNK != 0 and l <= SAMPLE_PAD and b % nb == 0, (b, l)
    decay, q_dec, _, c_dec = _retention_tables(l)
    p = SAMPLE_PAD - l
    decay = np.pad(decay, ((0, 0), (0, p), (0, p)))
    q_dec = np.pad(q_dec, ((0, p), (0, 0)))
    wq = RET_HEADS * RET_DK
    wv = RET_HEADS * RET_DV
    rows = nb * l
    full3 = lambda bi: (0, 0, 0)
    state_spec = pl.BlockSpec((nb, RET_HEADS, RET_DK, RET_DV), lambda bi: (bi, 0, 0, 0))
    return pl.pallas_call(
        functools.partial(_ret_sample_kernel, nb=nb, l=l),
        grid=(b // nb,),
        in_specs=[
            pl.BlockSpec(memory_space=pltpu.SMEM),
            pl.BlockSpec((rows, wq), lambda bi: (bi, 0)),
            pl.BlockSpec((rows, wq), lambda bi: (bi, 1)),
            pl.BlockSpec((rows, wq), lambda bi: (bi, 0)),
            pl.BlockSpec((rows, wv), lambda bi: (bi, 0)),
            pl.BlockSpec((rows, wv), lambda bi: (bi, REST_G // wv)),
            pl.BlockSpec((RET_HEADS, SAMPLE_PAD, SAMPLE_PAD), full3),
            pl.BlockSpec((RET_HEADS, SAMPLE_PAD, LANES), full3),
            state_spec,
        ],
        out_specs=[pl.BlockSpec((rows, wv), lambda bi: (bi, 0)), state_spec],
        out_shape=[
            jax.ShapeDtypeStruct((b * l, wv), BF16),
            jax.ShapeDtypeStruct((b, RET_HEADS, RET_DK, RET_DV), F32),
        ],
        compiler_params=_cparams(("arbitrary",)),
        name="retention_sample",
    )(c_dec, qk, qk, kk, v, rest, decay, _lane_bcast(q_dec), s0)


CONV_PAD = 32
CONV_COLS = LANES


def _conv_strip(block, wts, nblk, store):
    lead = CONV_PAD - CONV_HIST
    taps = [((lead + w) // SUBLANES, (lead + w) % SUBLANES) for w in range(CONV_WIDTH)]
    rowid = lax.broadcasted_iota(I32, (SUBLANES, CONV_COLS), 0)

    def partial_sums(m, shifts):
        ys = {}
        for w, (a, s) in enumerate(taps):
            if s in shifts:
                term = block(m + a) * wts[w]
                ys[s] = term if s not in ys else ys[s] + term
        return ys

    shifted = range(1, SUBLANES)
    prev = partial_sums(0, shifted)
    for m in range(nblk):
        nxt = partial_sums(m + 1, shifted)
        acc = partial_sums(m, (0,))[0]
        for s in shifted:
            acc = acc + pltpu.roll(jnp.where(rowid >= s, prev[s], nxt[s]), SUBLANES - s, axis=0)
        store(m, acc)
        prev = nxt


def _conv_kernel(*refs, tl, nb, nl, has_state, n_side, side_steps):
    u_ref = refs[0]
    buf_ref = refs[1] if has_state else None
    n_in = 2 if has_state else 1
    cw_ref, bias_ref, lnw_ref, lnb_ref = refs[n_in:n_in + 4]
    side_in = refs[n_in + 4:n_in + 4 + n_side]
    sw_ref, nbuf_ref = refs[n_in + 4 + n_side:n_in + 6 + n_side]
    side_out = refs[n_in + 6 + n_side:n_in + 6 + 2 * n_side]
    upad_ref, c_ref = refs[-2:]
    if n_side:
        _side_cast(pl.program_id(0) * nl + pl.program_id(1), side_steps, side_in, side_out)
    li = pl.program_id(1)
    lead = CONV_PAD - CONV_HIST
    d = u_ref.shape[1]
    for bt in range(nb):
        @pl.when(li == 0)
        def _():
            if has_state:
                upad_ref[bt, 0:lead, :] = jnp.zeros((lead, d), F32)
                upad_ref[bt, lead:CONV_PAD, :] = buf_ref[0, :, bt, :]
            else:
                upad_ref[bt, 0:CONV_PAD, :] = jnp.zeros((CONV_PAD, d), F32)

        r0 = bt * tl
        upad_ref[bt, CONV_PAD:CONV_PAD + tl, :] = u_ref[r0:r0 + tl, :]
        for cc in range(d // CONV_COLS):
            cs = slice(cc * CONV_COLS, (cc + 1) * CONV_COLS)
            wts = [jnp.broadcast_to(cw_ref[0, w:w + 1, cs], (SUBLANES, CONV_COLS)) for w in range(CONV_WIDTH)]
            blocks = {}

            def block(j, cs=cs, blocks=blocks):
                if j not in blocks:
                    blocks[j] = upad_ref[bt, j * SUBLANES:(j + 1) * SUBLANES, cs]
                return blocks[j]

            def store(m, acc, cs=cs):
                c_ref[m * SUBLANES:(m + 1) * SUBLANES, cs] = acc

            _conv_strip(block, wts, tl // SUBLANES, store)
        cf = c_ref[...] + bias_ref[...]
        mu = jnp.mean(cf, axis=-1, keepdims=True)
        var = jnp.mean(jnp.square(cf - mu), axis=-1, keepdims=True)
        cn = (cf - mu) * lax.rsqrt(var + EPS) * lnw_ref[...] + lnb_ref[...]
        sw_ref[r0:r0 + tl, :] = jax.nn.silu(cn).astype(sw_ref.dtype)

        @pl.when(li == nl - 1)
        def _():
            nbuf_ref[0, bt] = upad_ref[bt, tl + lead:tl + CONV_PAD, :]

        if nl > 1:
            upad_ref[bt, 0:CONV_PAD, :] = upad_ref[bt, tl:tl + CONV_PAD, :]


def _conv_branch(u, buf_t, conv_w, conv_b, ln_w, ln_b, b, l, tl, nb, side=()):
    d = D_MODEL
    nl = l // tl
    assert nb == 1 or nl == 1
    rows = nb * tl
    const2 = lambda bi, li: (0, 0)
    grid = (b // nb, nl)
    side_steps = _side_steps(grid[0] * grid[1])
    side_specs = _side_specs(side, side_steps, lambda bi, li: bi * nl + li)
    state_spec = pl.BlockSpec((1, nb, CONV_HIST, d), lambda bi, li: (0, bi, 0, 0))
    has_state = buf_t is not None
    state_in = [pl.BlockSpec((1, CONV_HIST, nb, d), lambda bi, li: (0, 0, bi, 0))] if has_state else []
    outs = pl.pallas_call(
        functools.partial(_conv_kernel, tl=tl, nb=nb, nl=nl, has_state=has_state, n_side=len(side),
                          side_steps=_side_guard(side_steps, grid[0] * grid[1])),
        grid=grid,
        in_specs=[
            pl.BlockSpec((rows, d), lambda bi, li: (bi * nl + li, 0)),
        ] + state_in + [
            pl.BlockSpec((1, CONV_WIDTH, d), lambda bi, li: (0, 0, 0)),
            pl.BlockSpec((1, d), const2),
            pl.BlockSpec((1, d), const2),
            pl.BlockSpec((1, d), const2),
        ] + side_specs,
        out_specs=[pl.BlockSpec((rows, d), lambda bi, li: (bi * nl + li, 0)), state_spec] + side_specs,
        out_shape=[
            jax.ShapeDtypeStruct((b * l, d), BF16),
            jax.ShapeDtypeStruct((1, b, CONV_HIST, d), F32),
        ] + [jax.ShapeDtypeStruct(s.shape, BF16) for s in side],
        scratch_shapes=[pltpu.VMEM((nb, CONV_PAD + tl, d), F32), pltpu.VMEM((tl, d), F32)],
        compiler_params=_cparams(("arbitrary", "arbitrary")),
        name="conv_branch",
    )(u, *([buf_t] if has_state else []), conv_w, conv_b, ln_w, ln_b, *side)
    return outs[0], outs[1], tuple(outs[2:])


def _merge_kernel(og_ref, sw_ref, wr_ref, wc_ref, gr_ref, gc_ref, o_ref):
    ret_out = jnp.dot(og_ref[...], wr_ref[...], preferred_element_type=F32)
    conv_out = jnp.dot(sw_ref[...], wc_ref[...], preferred_element_type=F32)
    merged = jax.nn.sigmoid(gr_ref[...]) * ret_out + jax.nn.sigmoid(gc_ref[...]) * conv_out
    o_ref[...] = merged.astype(o_ref.dtype)


def _merge(og, sw, w_ret_o, w_conv_o, rest, tm=512, tn=1024):
    m = og.shape[0]
    n = w_ret_o.shape[1]
    tm = min(tm, m)
    kr = og.shape[1]
    kc = sw.shape[1]
    return pl.pallas_call(
        _merge_kernel,
        grid=(n // tn, m // tm),
        in_specs=[
            pl.BlockSpec((tm, kr), lambda j, i: (i, 0)),
            pl.BlockSpec((tm, kc), lambda j, i: (i, 0)),
            pl.BlockSpec((kr, tn), lambda j, i: (0, j)),
            pl.BlockSpec((kc, tn), lambda j, i: (0, j)),
            pl.BlockSpec((tm, tn), lambda j, i: (i, REST_GRET // tn + j)),
            pl.BlockSpec((tm, tn), lambda j, i: (i, REST_GCONV // tn + j)),
        ],
        out_specs=pl.BlockSpec((tm, tn), lambda j, i: (i, j)),
        out_shape=jax.ShapeDtypeStruct((m, n), BF16),
        compiler_params=_cparams(("arbitrary", "arbitrary")),
        name="merge",
    )(og, sw, w_ret_o, w_conv_o, rest, rest)


def _outproj_kernel(a_ref, w_ref, x_ref, *rest):
    o_ref, wb_ref = rest[-2], rest[-1]

    @pl.when(pl.program_id(1) == 0)
    def _():
        wb_ref[...] = w_ref[...].astype(BF16)

    o_ref[...] = x_ref[...] + jnp.dot(a_ref[...], wb_ref[...], preferred_element_type=F32)


def _outproj_residual(a, w, x, total_rows, row0, prev=None, tm=1024, tn=1024):
    m, k = a.shape
    n = w.shape[1]
    tm = min(tm, m)
    assert row0 % tm == 0
    off = row0 // tm
    in_specs = [
        pl.BlockSpec((tm, k), lambda j, i: (i, 0)),
        pl.BlockSpec((k, tn), lambda j, i: (0, j)),
        pl.BlockSpec((tm, tn), lambda j, i: (i, j)),
    ]
    args = [a, w, x]
    aliases = {}
    if prev is not None:
        in_specs.append(pl.BlockSpec(memory_space=pl.ANY))
        args.append(prev)
        aliases = {3: 0}
    return pl.pallas_call(
        _outproj_kernel,
        grid=(n // tn, m // tm),
        in_specs=in_specs,
        out_specs=pl.BlockSpec((tm, tn), lambda j, i: (i + off, j)),
        out_shape=jax.ShapeDtypeStruct((total_rows, n), F32),
        scratch_shapes=[pltpu.VMEM((k, tn), BF16)],
        input_output_aliases=aliases,
        compiler_params=_cparams(("arbitrary", "arbitrary")),
        name="outproj_residual",
    )(*args)


def _router_kernel(x_ref, nw_ref, w_ref, b_ref, h_ref, meta_ref, meta_t_ref, cnt_ref, base_ref):
    tm = x_ref.shape[0]

    @pl.when(pl.program_id(0) == 0)
    def _():
        base_ref[...] = jnp.zeros_like(base_ref)

    x = x_ref[...]
    ms = jnp.mean(x * x, axis=-1, keepdims=True)
    h = x * lax.rsqrt(ms + EPS) * nw_ref[...]
    h_ref[...] = h
    w = w_ref[...]
    h_hi = h.astype(BF16)
    h_lo = (h - h_hi.astype(F32)).astype(BF16)
    w_hi = w.astype(BF16)
    w_lo = (w - w_hi.astype(F32)).astype(BF16)
    logits = (jnp.dot(h_hi, w_lo, preferred_element_type=F32) + jnp.dot(h_lo, w_hi, preferred_element_type=F32)
              + jnp.dot(h_hi, w_hi, preferred_element_type=F32)) + b_ref[...]
    lane = lax.broadcasted_iota(I32, (tm, ROUTER_COLS), 1)

    def first_argmax(vals):
        top = jnp.max(vals, axis=1, keepdims=True)
        return top, jnp.min(jnp.where(vals == top, lane, ROUTER_COLS), axis=1, keepdims=True)

    cmask = lane < N_GROUPS
    lc = jnp.where(cmask, logits, NEG)
    mc, grp = first_argmax(lc)
    p_grp = 1.0 / jnp.sum(jnp.where(cmask, jnp.exp(lc - mc), 0.0), axis=1, keepdims=True)
    lo = ROUTER_OFF + grp * EXPERTS_PER_GROUP
    lf = jnp.where((lane >= lo) & (lane < lo + EXPERTS_PER_GROUP), logits, NEG)
    v1, i1 = first_argmax(lf)
    v2, i2 = first_argmax(jnp.where(lane == i1, NEG, lf))
    e2 = jnp.exp(v2 - v1)
    den = 1.0 + e2
    g1 = p_grp * (1.0 / den)
    g2 = p_grp * (e2 / den)

    oh1 = lane == i1
    oh2 = lane == i2
    both = jnp.where(oh1 | oh2, 1.0, 0.0)
    row = lax.broadcasted_iota(I32, (tm, tm), 0)
    col = lax.broadcasted_iota(I32, (tm, tm), 1)
    earlier = jnp.where(col < row, 1.0, 0.0).astype(BF16)
    before = jnp.dot(earlier, both.astype(BF16), preferred_element_type=F32) + base_ref[...]
    r1 = jnp.sum(jnp.where(oh1, before, 0.0), axis=1, keepdims=True)
    r2 = jnp.sum(jnp.where(oh2, before, 0.0), axis=1, keepdims=True)
    base_ref[...] = base_ref[...] + jnp.sum(both, axis=0, keepdims=True)
    cnt_ref[...] = base_ref[...]

    fields = [(i1 - ROUTER_OFF).astype(F32), (i2 - ROUTER_OFF).astype(F32), g1, g2, r1, r2]
    meta = jnp.zeros((tm, ROUTER_COLS), F32)
    for k, f in enumerate(fields):
        meta = jnp.where(lane == k, f, meta)
    meta_ref[...] = meta
    meta_t_ref[...] = meta.T[:SUBLANES]


def _router(x, norm_w, w_router, b_router, tm=512):
    n, d = x.shape
    tm = min(tm, n)
    const2 = lambda i: (0, 0)
    return pl.pallas_call(
        _router_kernel,
        grid=(n // tm,),
        in_specs=[
            pl.BlockSpec((tm, d), lambda i: (i, 0)),
            pl.BlockSpec((1, d), const2),
            pl.BlockSpec((d, ROUTER_COLS), const2),
            pl.BlockSpec((1, ROUTER_COLS), const2),
        ],
        out_specs=[
            pl.BlockSpec((tm, d), lambda i: (i, 0)),
            pl.BlockSpec((tm, ROUTER_COLS), lambda i: (i, 0)),
            pl.BlockSpec((SUBLANES, tm), lambda i: (0, i)),
            pl.BlockSpec((1, ROUTER_COLS), const2),
        ],
        out_shape=[
            jax.ShapeDtypeStruct((n, d), F32),
            jax.ShapeDtypeStruct((n, ROUTER_COLS), F32),
            jax.ShapeDtypeStruct((SUBLANES, n), F32),
            jax.ShapeDtypeStruct((1, ROUTER_COLS), F32),
        ],
        scratch_shapes=[pltpu.VMEM((1, ROUTER_COLS), F32)],
        compiler_params=_cparams(("arbitrary",)),
        name="router",
    )(x, norm_w.reshape(1, d), w_router, b_router)


def _rows_kernel(pstart_ref, first_ref, meta_ref, rows_ref):
    expert = meta_ref[0:TOP_K_FINE, :].astype(I32)
    rank = meta_ref[4:4 + TOP_K_FINE, :].astype(I32)
    rows = rank
    for e in range(N_EXPERTS):
        first = first_ref[e]
        rows = rows + jnp.where(expert == e, pstart_ref[e] + jnp.where(rank >= first, MOE_BLOCK - first, 0), 0)
    rows_ref[...] = rows


def _assignment_rows(pstart, first, meta_t):
    n = meta_t.shape[1]
    return pl.pallas_call(
        _rows_kernel,
        grid_spec=pltpu.PrefetchScalarGridSpec(
            num_scalar_prefetch=2,
            grid=(1,),
            in_specs=[pl.BlockSpec((SUBLANES, n), lambda i, ps, fs: (0, 0))],
            out_specs=pl.BlockSpec((TOP_K_FINE, n), lambda i, ps, fs: (0, 0)),
        ),
        out_shape=jax.ShapeDtypeStruct((TOP_K_FINE, n), I32),
        compiler_params=_cparams(("arbitrary",)),
        name="moe_rows",
    )(pstart, first, meta_t)


def _dispatch_kernel(rows_ref, span_ref, h_ref, xs_hbm, zero_ref, sem, zsem):
    td = h_ref.shape[0]
    i = pl.program_id(0)

    @pl.when(i == 0)
    def _():
        zero_ref[...] = jnp.zeros_like(zero_ref)

        def fill_copy(e):
            last = pl.multiple_of(span_ref[N_EXPERTS + e] - MOE_ROWS, MOE_ROWS)
            return pltpu.make_async_copy(zero_ref, xs_hbm.at[pl.ds(last, MOE_ROWS)], zsem)

        def nonempty(e):
            return span_ref[N_EXPERTS + e] > span_ref[e]

        def start(e, carry):
            @pl.when(nonempty(e))
            def _():
                fill_copy(e).start()
            return carry

        def wait(e, carry):
            @pl.when(nonempty(e))
            def _():
                fill_copy(e).wait()
            return carry

        lax.fori_loop(0, N_EXPERTS, start, 0)
        lax.fori_loop(0, N_EXPERTS, wait, 0)

    n_tok = pl.num_programs(0) * td

    def tokens(g, carry):
        t0 = pl.multiple_of(g * SUBLANES, SUBLANES)
        for u in range(SUBLANES):
            for k in range(TOP_K_FINE):
                r = rows_ref[k * n_tok + i * td + t0 + u]
                pltpu.make_async_copy(h_ref.at[pl.ds(t0 + u, 1)], xs_hbm.at[pl.ds(r, 1)], sem).start()
        return carry

    lax.fori_loop(0, td // SUBLANES, tokens, 0)
    for _ in range(TOP_K_FINE):
        pltpu.make_async_copy(h_ref, xs_hbm.at[pl.ds(0, td)], sem).wait()


def _dispatch(rows_flat, span, h, n_rows, td=1024):
    n_tok, d = h.shape
    td = min(td, n_tok)
    return pl.pallas_call(
        _dispatch_kernel,
        grid_spec=pltpu.PrefetchScalarGridSpec(
            num_scalar_prefetch=2,
            grid=(n_tok // td,),
            in_specs=[pl.BlockSpec((td, d), lambda i, rows, span: (i, 0))],
            out_specs=pl.BlockSpec(memory_space=pl.ANY),
            scratch_shapes=[pltpu.VMEM((MOE_ROWS, d), F32), pltpu.SemaphoreType.DMA(()),
                            pltpu.SemaphoreType.DMA(())],
        ),
        out_shape=jax.ShapeDtypeStruct((n_rows, d), F32),
        compiler_params=_cparams(("arbitrary",)),
        name="moe_dispatch",
    )(rows_flat, span, h)


def _expert_kernel(be_ref, nh_ref, nu_ref, x_ref, wg_ref, wu_ref, wd_ref, o_ref):
    del be_ref
    nh = nh_ref[pl.program_id(0)]

    def ffn(rows):
        xb = x_ref[0:rows, :].astype(BF16)
        a = jnp.dot(xb, wg_ref[0], preferred_element_type=F32)
        b = jnp.dot(xb, wu_ref[0], preferred_element_type=F32)
        hb = (jax.nn.silu(a) * b).astype(BF16)
        o_ref[0:rows, :] = jnp.dot(hb, wd_ref[0], preferred_element_type=F32)

    for parts in range(1, MOE_PARTS + 1):
        @pl.when(nh == parts)
        def _():
            ffn(parts * MOE_ROWS)
            if parts < MOE_PARTS:
                o_ref[parts * MOE_ROWS:, :] = jnp.zeros((MOE_BLOCK - parts * MOE_ROWS, o_ref.shape[1]), F32)


def _experts(blk_exp, blk_parts, n_used, xs, wg, wu, wd):
    n_rows, d = xs.shape
    n_blocks = n_rows // MOE_BLOCK
    de = wg.shape[2]
    row_map = lambda i, be, nh, nu: (jnp.minimum(i, jnp.maximum(nu[0] - 1, 0)), 0)
    w_map = lambda i, be, nh, nu: (be[i], 0, 0)
    return pl.pallas_call(
        _expert_kernel,
        grid_spec=pltpu.PrefetchScalarGridSpec(
            num_scalar_prefetch=3,
            grid=(n_blocks,),
            in_specs=[
                pl.BlockSpec((MOE_BLOCK, d), row_map),
                pl.BlockSpec((1, d, de), w_map),
                pl.BlockSpec((1, d, de), w_map),
                pl.BlockSpec((1, de, d), w_map),
            ],
            out_specs=pl.BlockSpec((MOE_BLOCK, d), row_map),
        ),
        out_shape=jax.ShapeDtypeStruct((n_rows, d), F32),
        compiler_params=_cparams(("arbitrary",)),
        name="moe_experts",
    )(blk_exp, blk_parts, n_used, xs, wg, wu, wd)


def _combine_kernel(rows_ref, ys_hbm, x_ref, meta_ref, nf_ref, op_ref, os_ref, buf_ref, sem, *, p_tiles):
    tm = x_ref.shape[0]
    i = pl.program_id(0)
    nt = pl.num_programs(0)
    n_tok = nt * tm
    slot = i % 2

    def gather(tile, dst_slot):
        def tokens(g, carry):
            t0 = pl.multiple_of(g * SUBLANES, SUBLANES)
            for u in range(SUBLANES):
                for k in range(TOP_K_FINE):
                    r = rows_ref[k * n_tok + tile * tm + t0 + u]
                    pltpu.make_async_copy(ys_hbm.at[pl.ds(r, 1)], buf_ref.at[dst_slot, k, pl.ds(t0 + u, 1)],
                                          sem.at[dst_slot]).start()
            return carry

        lax.fori_loop(0, tm // SUBLANES, tokens, 0)

    @pl.when(i == 0)
    def _():
        gather(0, 0)

    @pl.when(i + 1 < nt)
    def _():
        gather(i + 1, 1 - slot)

    for k in range(TOP_K_FINE):
        pltpu.make_async_copy(ys_hbm.at[pl.ds(0, tm)], buf_ref.at[slot, k], sem.at[slot]).wait()
    meta = meta_ref[...]
    y = x_ref[...] + (buf_ref[slot, 0] * meta[:, 2:3] + buf_ref[slot, 1] * meta[:, 3:4])
    ms = jnp.mean(y * y, axis=-1, keepdims=True)
    out = y * lax.rsqrt(ms + EPS) * nf_ref[...]

    @pl.when(i < p_tiles)
    def _():
        op_ref[...] = out

    @pl.when(i >= p_tiles)
    def _():
        os_ref[...] = out


def _combine(rows_flat, ys, x, meta, norm_f, n_p, tm=512):
    n, d = x.shape
    n_s = n - n_p
    tm = min(tm, n_p, n_s)
    assert n_p % tm == 0 and n_s % tm == 0
    p_tiles = n_p // tm
    return pl.pallas_call(
        functools.partial(_combine_kernel, p_tiles=p_tiles),
        grid_spec=pltpu.PrefetchScalarGridSpec(
            num_scalar_prefetch=1,
            grid=(n // tm,),
            in_specs=[
                pl.BlockSpec(memory_space=pl.ANY),
                pl.BlockSpec((tm, d), lambda i, rows: (i, 0)),
                pl.BlockSpec((tm, ROUTER_COLS), lambda i, rows: (i, 0)),
                pl.BlockSpec((1, d), lambda i, rows: (0, 0)),
            ],
            out_specs=[
                pl.BlockSpec((tm, d), lambda i, rows: (jnp.minimum(i, p_tiles - 1), 0)),
                pl.BlockSpec((tm, d), lambda i, rows: (jnp.maximum(i - p_tiles, 0), 0)),
            ],
            scratch_shapes=[pltpu.VMEM((2, TOP_K_FINE, tm, d), F32), pltpu.SemaphoreType.DMA((2,))],
        ),
        out_shape=[jax.ShapeDtypeStruct((n_p, d), F32), jax.ShapeDtypeStruct((n_s, d), F32)],
        compiler_params=_cparams(("arbitrary",)),
        name="moe_combine",
    )(rows_flat, ys, x, meta, norm_f.reshape(1, d))


def _moe_and_final_norm(x, n_p, norm_ffn, w_coarse, b_coarse, w_fine, b_fine, wg, wu, wd, norm_f):
    n_tok, d = x.shape
    pad = ROUTER_COLS - N_GROUPS - N_EXPERTS
    w_router = jnp.concatenate([w_coarse, w_fine, jnp.zeros((d, pad), F32)], axis=1)
    b_router = jnp.concatenate([b_coarse, b_fine, jnp.zeros((pad,), F32)]).reshape(1, ROUTER_COLS)
    h, meta, meta_t, counts = _router(x, norm_ffn, w_router, b_router)

    n_asg = n_tok * TOP_K_FINE
    n_blocks = -(-n_asg // MOE_BLOCK) + N_EXPERTS
    n_rows = n_blocks * MOE_BLOCK
    cnt = counts[0, ROUTER_OFF:ROUTER_OFF + N_EXPERTS].astype(I32)
    nblk = (cnt + MOE_BLOCK - 1) // MOE_BLOCK
    padded = nblk * MOE_BLOCK
    pend = jnp.cumsum(padded)
    pstart = pend - padded
    first = cnt - (nblk - 1) * MOE_BLOCK
    first_rows = jnp.where(cnt > 0, (first + MOE_ROWS - 1) // MOE_ROWS * MOE_ROWS, 0)
    rows_flat = _assignment_rows(pstart.astype(I32), first.astype(I32), meta_t).reshape(-1)
    n_used = (pend[-1] // MOE_BLOCK).astype(I32)
    blk = jnp.arange(n_blocks, dtype=I32)
    blk_row = jnp.minimum(blk, n_used - 1) * MOE_BLOCK
    blk_exp = jnp.sum((pend[None, :] <= blk_row[:, None]).astype(I32), axis=1)
    blk_exp = jnp.minimum(blk_exp, N_EXPERTS - 1)
    blk_parts = jnp.where(blk * MOE_BLOCK == pstart[blk_exp], first_rows[blk_exp] // MOE_ROWS, MOE_PARTS)
    blk_parts = jnp.where(blk < n_used, blk_parts, 0).astype(I32)
    cend = pstart + first_rows

    xs = _dispatch(rows_flat, jnp.concatenate([pstart, cend]).astype(I32), h, n_rows)
    ys = _experts(blk_exp, blk_parts, n_used.reshape(1), xs, wg, wu, wd)
    return _combine(rows_flat, ys, x, meta, norm_f, n_p)


def _mixer(x, s_ret, s_conv, pos0, norm_mix, w_in, conv_w, conv_b, conv_ln_w, conv_ln_b, w_o,
           total_rows, row0, prev, merge_w=None, w_in_bf16=None, side_v=(), side_rest=(), side_conv=()):
    b, l, d = x.shape
    x2 = x.reshape(b * l, d)
    chunk = RET_CHUNK if l % RET_CHUNK == 0 else l
    h = _rmsnorm_cast(x2, norm_mix)
    emit = w_in_bf16 is None
    w_qk, w_v, w_ca, w_cb, w_rest = (w_in,) * 5 if emit else w_in_bf16
    qk, kk, c_qk, _ = _inproj_rope(h, w_qk, *_rope_inputs(l, pos0, chunk, min(ROPE_TM, b * l)), emit_w=emit)
    v, c_v, cast_a = _inproj(h, w_v, [(COL_V, W_V)], BF16, side_v, emit_w=emit)
    u, c_glu = _inproj_glu(h, w_ca, w_cb, emit_w=emit)
    rest, _, cast_b = _inproj(h, w_rest, [(COL_G, W_G), (COL_GATES, W_REST - W_G)], F32, side_rest)
    if emit:
        w_in_bf16 = (c_qk, c_v, *c_glu, w_in)
    if s_ret is None:
        og, s_ret_new, _ = _retention_prompt(qk, kk, v, rest, b, l)
        sw, s_conv_new, cast_c = _conv_branch(u, s_conv, conv_w, conv_b, conv_ln_w, conv_ln_b, b, l,
                                              tl=min(128, l), nb=1, side=side_conv)
    else:
        og, s_ret_new = _retention_sample(qk, kk, v, rest, s_ret, b, l)
        sw, s_conv_new, cast_c = _conv_branch(u, s_conv, conv_w, conv_b, conv_ln_w, conv_ln_b, b, l,
                                              tl=l, nb=min(SUBLANES, b), side=side_conv)
    if merge_w is None:
        merge_w = cast_a[:2]
    merged = _merge(og, sw, *merge_w, rest)
    x1 = _outproj_residual(merged, w_o, x2, total_rows, row0, prev)
    return x1, s_ret_new, s_conv_new, cast_a + cast_b + cast_c, w_in_bf16


def kernel(x_prompt, x_sample, state_ret, state_conv, norm_mix, w_in, w_ret_o, conv_w, conv_b, conv_ln_w,
           conv_ln_b, w_conv_o, w_o, norm_ffn, w_coarse, b_coarse, w_fine, b_fine, w_gate, w_up, w_down, norm_f):
    depth = w_in.shape[0]
    assert depth == 1, "single-layer trunk"
    bp, lp, d = x_prompt.shape
    bs, ls, _ = x_sample.shape
    n_p = bp * lp
    n_s = bs * ls
    mix_w = (norm_mix[0], w_in[0], conv_w, conv_b, conv_ln_w, conv_ln_b, w_o[0])
    n_exp, _, d_exp = w_gate.shape[1:]
    flat = lambda w: w.reshape(n_exp * w.shape[2], w.shape[3])
    x1, ret_p, conv_p, (wr, wc, wg, wu, wd), w_in_bf16 = _mixer(
        x_prompt, None, None, 0, *mix_w, n_p + n_s, 0, None,
        side_v=(w_ret_o[0], w_conv_o[0]), side_rest=(flat(w_gate),), side_conv=(flat(w_up), flat(w_down)))
    conv_hist = jnp.transpose(state_conv, (0, 2, 1, 3))
    x1, ret_s, conv_s, _, _ = _mixer(x_sample, state_ret[0], conv_hist, PAST_LEN, *mix_w, n_p + n_s, n_p, x1,
                                     merge_w=(wr, wc), w_in_bf16=w_in_bf16)
    yp, ys = _moe_and_final_norm(x1, n_p, norm_ffn[0], w_coarse[0], b_coarse[0], w_fine[0], b_fine[0],
                                 wg.reshape(n_exp, d, d_exp), wu.reshape(n_exp, d, d_exp),
                                 wd.reshape(n_exp, d_exp, d), norm_f)
    return (yp.reshape(bp, lp, d), ys.reshape(bs, ls, d), ret_p[None], conv_p, ret_s[None], conv_s)
```

```python
import functools

import numpy as np
import jax
import jax.numpy as jnp
from jax import lax
from jax.experimental import pallas as pl
from jax.experimental.pallas import tpu as pltpu

F32 = jnp.float32
BF16 = jnp.bfloat16
I32 = jnp.int32

D_MODEL = 2048
PAST_LEN = 16384
RET_HEADS = 8
RET_DK = D_MODEL // RET_HEADS
RET_DV = 2 * D_MODEL // RET_HEADS
RET_HALF = RET_DK // 2
RET_CHUNK = 256
ROPE_BASE = 10000.0
CONV_WIDTH = 31
CONV_HIST = CONV_WIDTH - 1
N_GROUPS = 4
EXPERTS_PER_GROUP = 8
N_EXPERTS = N_GROUPS * EXPERTS_PER_GROUP
TOP_K_FINE = 2
D_EXPERT = D_MODEL // 2
EPS = 1e-6

W_QK = 2 * RET_HEADS * RET_DK
W_V = RET_HEADS * RET_DV
W_G = RET_HEADS * RET_DV
COL_QK = 0
COL_V = COL_QK + W_QK
COL_G = COL_V + W_V
COL_CA = COL_G + W_G
COL_CB = COL_CA + D_MODEL
COL_GATES = COL_CB + D_MODEL
REST_G = 0
REST_GRET = REST_G + W_G
REST_GCONV = REST_GRET + D_MODEL
W_REST = REST_GCONV + D_MODEL

LANES = 128
SUBLANES = 8
VMEM_LIMIT = 56 * 1024 * 1024
MOE_ROWS = 128
MOE_PARTS = 4
MOE_BLOCK = MOE_PARTS * MOE_ROWS
ROUTER_COLS = LANES
ROUTER_OFF = N_GROUPS
NEG = -3.0e38
ISSUE_UNROLL = 8


def _cparams(sem):
    return pltpu.CompilerParams(dimension_semantics=sem, vmem_limit_bytes=VMEM_LIMIT)


def _rmsnorm_kernel(x_ref, w_ref, o_ref):
    x = x_ref[...]
    ms = jnp.mean(x * x, axis=-1, keepdims=True)
    o_ref[...] = (x * lax.rsqrt(ms + EPS) * w_ref[...]).astype(o_ref.dtype)


def _rmsnorm_cast(x, w, tm=512):
    m, d = x.shape
    tm = min(tm, m)
    return pl.pallas_call(
        _rmsnorm_kernel,
        grid=(m // tm,),
        in_specs=[pl.BlockSpec((tm, d), lambda i: (i, 0)), pl.BlockSpec((1, d), lambda i: (0, 0))],
        out_specs=pl.BlockSpec((tm, d), lambda i: (i, 0)),
        out_shape=jax.ShapeDtypeStruct((m, d), BF16),
        compiler_params=_cparams(("arbitrary",)),
        name="rmsnorm_cast",
    )(x, w.reshape(1, d))


def _side_steps(total_steps):
    return 1 << (total_steps.bit_length() - 1)


def _side_specs(arrays, n_steps, linear_step):
    def spec(a):
        rows = a.shape[0] // n_steps
        assert rows * n_steps == a.shape[0] and rows % 16 == 0, (a.shape, n_steps)
        return pl.BlockSpec((rows, a.shape[1]), lambda *g: (jnp.minimum(linear_step(*g), n_steps - 1), 0))
    return [spec(a) for a in arrays]


def _side_cast(step, n_steps, srcs, dsts):
    def cast():
        for src, dst in zip(srcs, dsts):
            dst[...] = src[...].astype(dst.dtype)

    if n_steps is None:
        cast()
    else:
        pl.when(step < n_steps)(cast)


def _side_guard(n_steps, total_steps):
    return None if n_steps == total_steps else n_steps


def _inproj_kernel(*refs, n_side, side_steps):
    a_ref, w_ref = refs[:2]
    side_in = refs[2:2 + n_side]
    o_ref = refs[2 + n_side]
    side_out = refs[3 + n_side:3 + 2 * n_side]
    wb_ref = refs[-1]

    @pl.when(pl.program_id(1) == 0)
    def _():
        wb_ref[...] = w_ref[...].astype(BF16)

    o_ref[...] = jnp.dot(a_ref[...], wb_ref[...], preferred_element_type=F32).astype(o_ref.dtype)
    if n_side:
        _side_cast(pl.program_id(0) * pl.num_programs(1) + pl.program_id(1), side_steps, side_in, side_out)


def _inproj(a, w, col_spans, out_dtype, side=(), tm=1024, tn=1024):
    m, k = a.shape
    tm = min(tm, m)
    assert all(c0 % tn == 0 and n % tn == 0 for c0, n in col_spans)
    ncols = sum(n for _, n in col_spans)
    grid = (ncols // tn, m // tm)

    def w_tile(j):
        tile, first = 0, 0
        for c0, n in col_spans:
            tile = jnp.where(j >= first, c0 // tn + j - first, tile)
            first += n // tn
        return tile

    side_steps = _side_steps(grid[0] * grid[1])
    side_specs = _side_specs(side, side_steps, lambda j, i: j * grid[1] + i)
    outs = pl.pallas_call(
        functools.partial(_inproj_kernel, n_side=len(side),
                          side_steps=_side_guard(side_steps, grid[0] * grid[1])),
        grid=grid,
        in_specs=[pl.BlockSpec((tm, k), lambda j, i: (i, 0)),
                  pl.BlockSpec((k, tn), lambda j, i: (0, w_tile(j)))] + side_specs,
        out_specs=[pl.BlockSpec((tm, tn), lambda j, i: (i, j))] + side_specs,
        out_shape=[jax.ShapeDtypeStruct((m, ncols), out_dtype)]
                  + [jax.ShapeDtypeStruct(s.shape, BF16) for s in side],
        scratch_shapes=[pltpu.VMEM((k, tn), BF16)],
        compiler_params=_cparams(("arbitrary", "arbitrary")),
        name="inproj",
    )(a, w, *side)
    return outs[0], tuple(outs[1:])


def _inproj_glu_kernel(a_ref, wa_ref, wb_ref, o_ref, wab_ref, wbb_ref):
    @pl.when(pl.program_id(1) == 0)
    def _():
        wab_ref[...] = wa_ref[...].astype(BF16)
        wbb_ref[...] = wb_ref[...].astype(BF16)

    a = a_ref[...]
    ca = jnp.dot(a, wab_ref[...], preferred_element_type=F32)
    cb = jnp.dot(a, wbb_ref[...], preferred_element_type=F32)
    o_ref[...] = ca * jax.nn.sigmoid(cb)


def _inproj_glu(a, w, tm=1024, tn=512):
    m, k = a.shape
    tm = min(tm, m)
    n = D_MODEL
    assert COL_CA % tn == 0 and COL_CB % tn == 0 and n % tn == 0
    return pl.pallas_call(
        _inproj_glu_kernel,
        grid=(n // tn, m // tm),
        in_specs=[
            pl.BlockSpec((tm, k), lambda j, i: (i, 0)),
            pl.BlockSpec((k, tn), lambda j, i: (0, COL_CA // tn + j)),
            pl.BlockSpec((k, tn), lambda j, i: (0, COL_CB // tn + j)),
        ],
        out_specs=pl.BlockSpec((tm, tn), lambda j, i: (i, j)),
        out_shape=jax.ShapeDtypeStruct((m, n), F32),
        scratch_shapes=[pltpu.VMEM((k, tn), BF16), pltpu.VMEM((k, tn), BF16)],
        compiler_params=_cparams(("arbitrary", "arbitrary")),
        name="inproj_glu",
    )(a, w, w)


ROPE_ROWS = RET_CHUNK


def _inproj_rope_kernel(*refs, nq_tiles, n_side, side_steps):
    a_ref, w_ref, cos_ref, sin_ref, kd_ref = refs[:5]
    side_in = refs[5:5 + n_side]
    qk_ref, kk_ref = refs[5 + n_side:7 + n_side]
    side_out = refs[7 + n_side:7 + 2 * n_side]
    wb_ref, acc_ref = refs[-2:]
    j = pl.program_id(0)
    if n_side:
        _side_cast(j * pl.num_programs(1) + pl.program_id(1), side_steps, side_in, side_out)

    @pl.when(pl.program_id(1) == 0)
    def _():
        wb_ref[...] = w_ref[...].astype(BF16)

    acc_ref[...] = jnp.dot(a_ref[...], wb_ref[...], preferred_element_type=F32)
    tm, tn = acc_ref.shape
    scale = jnp.where(j >= nq_tiles, RET_DK ** -0.5, 1.0).astype(F32)
    for r in range(tm // ROPE_ROWS):
        rs = slice(r * ROPE_ROWS, (r + 1) * ROPE_ROWS)
        cos = cos_ref[rs, :]
        sin = sin_ref[rs, :]
        parts = []
        for hh in range(tn // RET_DK):
            x1 = acc_ref[rs, hh * RET_DK:hh * RET_DK + RET_HALF]
            x2 = acc_ref[rs, hh * RET_DK + RET_HALF:(hh + 1) * RET_DK]
            parts += [x1 * cos - x2 * sin, x1 * sin + x2 * cos]
        rot = jnp.concatenate(parts, axis=1) * scale
        qk_ref[rs, :] = rot.astype(qk_ref.dtype)

        @pl.when(j >= nq_tiles)
        def _():
            kk_ref[rs, :] = (rot * kd_ref[...]).astype(kk_ref.dtype)


ROPE_TM = 1024


def _inproj_rope(a, w, cos_rows, sin_rows, kd_rows, side=(), tn=1024):
    m, k = a.shape
    tm = min(ROPE_TM, m)
    wq = W_QK // 2
    assert tm % ROPE_ROWS == 0 and wq % tn == 0 and tn % RET_DK == 0 and cos_rows.shape[0] % tm == 0
    nq = wq // tn
    period = cos_rows.shape[0] // tm
    grid = (2 * nq, m // tm)
    side_steps = _side_steps(grid[0] * grid[1])
    side_specs = _side_specs(side, side_steps, lambda j, i: j * grid[1] + i)
    outs = pl.pallas_call(
        functools.partial(_inproj_rope_kernel, nq_tiles=nq, n_side=len(side),
                          side_steps=_side_guard(side_steps, grid[0] * grid[1])),
        grid=grid,
        in_specs=[
            pl.BlockSpec((tm, k), lambda j, i: (i, 0)),
            pl.BlockSpec((k, tn), lambda j, i: (0, COL_QK // tn + j)),
            pl.BlockSpec((tm, RET_HALF), lambda j, i: (i % period, 0)),
            pl.BlockSpec((tm, RET_HALF), lambda j, i: (i % period, 0)),
            pl.BlockSpec((ROPE_ROWS, tn), lambda j, i: (0, jnp.maximum(j - nq, 0))),
        ] + side_specs,
        out_specs=[
            pl.BlockSpec((tm, tn), lambda j, i: (i, j)),
            pl.BlockSpec((tm, tn), lambda j, i: (jnp.where(j >= nq, i, 0), jnp.maximum(j - nq, 0))),
        ] + side_specs,
        out_shape=[jax.ShapeDtypeStruct((m, W_QK), BF16), jax.ShapeDtypeStruct((m, wq), BF16)]
                  + [jax.ShapeDtypeStruct(s.shape, BF16) for s in side],
        scratch_shapes=[pltpu.VMEM((k, tn), BF16), pltpu.VMEM((tm, tn), F32)],
        compiler_params=_cparams(("arbitrary", "arbitrary")),
        name="inproj_rope",
    )(a, w, cos_rows, sin_rows, kd_rows, *side)
    return outs[0], outs[1], tuple(outs[2:])


def _retention_tables(c):
    h = RET_HEADS
    f32 = np.float32
    log_g = np.log1p(-np.exp2(f32(-5.0) - np.arange(h, dtype=f32)))
    idx = np.arange(c, dtype=f32)
    diff = idx[:, None] - idx[None, :]
    decay = np.where(diff[None] >= 0.0, np.exp(np.maximum(diff, f32(0.0))[None] * log_g[:, None, None]), f32(0.0))
    q_dec = np.exp((idx[:, None] + f32(1.0)) * log_g[None, :])
    k_dec = np.exp((f32(c) - f32(1.0) - idx)[:, None] * log_g[None, :])
    c_dec = np.exp(f32(c) * log_g)
    return decay.astype(f32), q_dec.astype(f32), k_dec.astype(f32), c_dec.astype(f32)


def _rope_tables(pos):
    f32 = np.float32
    inv = f32(ROPE_BASE) ** (-np.arange(RET_HALF, dtype=f32) / f32(RET_HALF))
    ang = pos.astype(f32)[:, None] * inv[None, :].astype(f32)
    return np.cos(ang).astype(f32), np.sin(ang).astype(f32)


def _lane_bcast(t):
    return np.ascontiguousarray(np.broadcast_to(t.T[:, :, None], (t.shape[1], t.shape[0], LANES)))


def _rope_inputs(l, pos0, c, tm):
    rows = max(l, tm)
    assert rows % l == 0 and rows % tm == 0 and ROPE_ROWS % c == 0
    cos, sin = _rope_tables(pos0 + np.arange(l))
    _, _, k_dec, _ = _retention_tables(c)
    kd = np.repeat(k_dec, RET_DK, axis=1)
    return np.tile(cos, (rows // l, 1)), np.tile(sin, (rows // l, 1)), np.tile(kd, (ROPE_ROWS // c, 1))


def _group_norm_gate(o, g):
    ms = jnp.mean(o * o, axis=-1, keepdims=True)
    return jax.nn.silu(g) * (o * lax.rsqrt(ms + EPS))


def _ret_prompt_kernel(*refs, n_side, side_steps):
    cdec_ref, q_ref, k_ref, kk_ref, v_ref, g_ref, dec_ref, qd_ref = refs[:8]
    side_in = refs[8:8 + n_side]
    o_ref, s_ref = refs[8 + n_side:10 + n_side]
    side_out = refs[10 + n_side:10 + 2 * n_side]
    if n_side:
        _side_cast(pl.program_id(0) * pl.num_programs(1) + pl.program_id(1), side_steps, side_in, side_out)

    @pl.when(pl.program_id(1) == 0)
    def _():
        s_ref[...] = jnp.zeros_like(s_ref)

    for h in range(RET_HEADS):
        qb = q_ref[:, h * RET_DK:(h + 1) * RET_DK]
        kb = k_ref[:, h * RET_DK:(h + 1) * RET_DK]
        kkb = kk_ref[:, h * RET_DK:(h + 1) * RET_DK]
        vb = v_ref[:, h * RET_DV:(h + 1) * RET_DV]
        att = lax.dot_general(qb, kb, (((1,), (1,)), ((), ())), preferred_element_type=F32) * dec_ref[h]
        s = s_ref[0, h]
        o = (jnp.dot(att.astype(BF16), vb, preferred_element_type=F32)
             + qd_ref[h][:, :1] * jnp.dot(qb, s.astype(BF16), preferred_element_type=F32))
        s_ref[0, h] = s * cdec_ref[h] + lax.dot_general(kkb, vb, (((0,), (0,)), ((), ())),
                                                         preferred_element_type=F32)
        og = _group_norm_gate(o, g_ref[:, h * RET_DV:(h + 1) * RET_DV])
        o_ref[:, h * RET_DV:(h + 1) * RET_DV] = og.astype(o_ref.dtype)


def _retention_prompt(qk, kk, v, rest, b, l, side=()):
    c = RET_CHUNK if l % RET_CHUNK == 0 else l
    nc = l // c
    decay, q_dec, _, c_dec = _retention_tables(c)
    wq = RET_HEADS * RET_DK
    wv = RET_HEADS * RET_DV
    full3 = lambda bi, ci: (0, 0, 0)
    side_steps = _side_steps(b * nc)
    side_specs = _side_specs(side, side_steps, lambda bi, ci: bi * nc + ci)
    outs = pl.pallas_call(
        functools.partial(_ret_prompt_kernel, n_side=len(side), side_steps=_side_guard(side_steps, b * nc)),
        grid=(b, nc),
        in_specs=[
            pl.BlockSpec(memory_space=pltpu.SMEM),
            pl.BlockSpec((c, wq), lambda bi, ci: (bi * nc + ci, 0)),
            pl.BlockSpec((c, wq), lambda bi, ci: (bi * nc + ci, 1)),
            pl.BlockSpec((c, wq), lambda bi, ci: (bi * nc + ci, 0)),
            pl.BlockSpec((c, wv), lambda bi, ci: (bi * nc + ci, 0)),
            pl.BlockSpec((c, wv), lambda bi, ci: (bi * nc + ci, REST_G // wv)),
            pl.BlockSpec((RET_HEADS, c, c), full3),
            pl.BlockSpec((RET_HEADS, c, LANES), full3),
        ] + side_specs,
        out_specs=[
            pl.BlockSpec((c, wv), lambda bi, ci: (bi * nc + ci, 0)),
            pl.BlockSpec((1, RET_HEADS, RET_DK, RET_DV), lambda bi, ci: (bi, 0, 0, 0)),
        ] + side_specs,
        out_shape=[
            jax.ShapeDtypeStruct((b * l, wv), BF16),
            jax.ShapeDtypeStruct((b, RET_HEADS, RET_DK, RET_DV), F32),
        ] + [jax.ShapeDtypeStruct(s.shape, BF16) for s in side],
        compiler_params=_cparams(("arbitrary", "arbitrary")),
        name="retention_prompt",
    )(c_dec, qk, qk, kk, v, rest, decay, _lane_bcast(q_dec), *side)
    return outs[0], outs[1], tuple(outs[2:])


SAMPLE_PAD = 16


def _pad_rows(x, rows):
    return jnp.concatenate([x, jnp.zeros((rows - x.shape[0], x.shape[1]), x.dtype)], axis=0)


def _ret_sample_kernel(cdec_ref, q_ref, k_ref, kk_ref, v_ref, g_ref, dec_ref, qd_ref, s0_ref, o_ref, s_ref,
                       *, nb, l):
    for h in range(RET_HEADS):
        qf = q_ref[:, h * RET_DK:(h + 1) * RET_DK].astype(F32)
        kf = k_ref[:, h * RET_DK:(h + 1) * RET_DK].astype(F32)
        kkf = kk_ref[:, h * RET_DK:(h + 1) * RET_DK].astype(F32)
        vf = v_ref[:, h * RET_DV:(h + 1) * RET_DV].astype(F32)
        for bt in range(nb):
            rs = slice(bt * l, (bt + 1) * l)
            qb = _pad_rows(qf[rs], SAMPLE_PAD).astype(BF16)
            kb = _pad_rows(kf[rs], SAMPLE_PAD).astype(BF16)
            kkb = _pad_rows(kkf[rs], SAMPLE_PAD).astype(BF16)
            vb = _pad_rows(vf[rs], SAMPLE_PAD).astype(BF16)
            att = lax.dot_general(qb, kb, (((1,), (1,)), ((), ())), preferred_element_type=F32) * dec_ref[h]
            s = s0_ref[bt, h]
            o = (jnp.dot(att.astype(BF16), vb, preferred_element_type=F32)
                 + qd_ref[h][:, :1] * jnp.dot(qb, s.astype(BF16), preferred_element_type=F32))
            s_ref[bt, h] = s * cdec_ref[h] + lax.dot_general(kkb, vb, (((0,), (0,)), ((), ())),
                                                              preferred_element_type=F32)
            og = _group_norm_gate(o[:l], g_ref[rs, h * RET_DV:(h + 1) * RET_DV])
            o_ref[rs, h * RET_DV:(h + 1) * RET_DV] = og.astype(o_ref.dtype)


def _retention_sample(qk, kk, v, rest, s0, b, l, nb=2):
    assert l % RET_CHUNK != 0 and l <= SAMPLE_PAD and b % nb == 0, (b, l)
    decay, q_dec, _, c_dec = _retention_tables(l)
    p = SAMPLE_PAD - l
    decay = np.pad(decay, ((0, 0), (0, p), (0, p)))
    q_dec = np.pad(q_dec, ((0, p), (0, 0)))
    wq = RET_HEADS * RET_DK
    wv = RET_HEADS * RET_DV
    rows = nb * l
    full3 = lambda bi: (0, 0, 0)
    state_spec = pl.BlockSpec((nb, RET_HEADS, RET_DK, RET_DV), lambda bi: (bi, 0, 0, 0))
    return pl.pallas_call(
        functools.partial(_ret_sample_kernel, nb=nb, l=l),
        grid=(b // nb,),
        in_specs=[
            pl.BlockSpec(memory_space=pltpu.SMEM),
            pl.BlockSpec((rows, wq), lambda bi: (bi, 0)),
            pl.BlockSpec((rows, wq), lambda bi: (bi, 1)),
            pl.BlockSpec((rows, wq), lambda bi: (bi, 0)),
            pl.BlockSpec((rows, wv), lambda bi: (bi, 0)),
            pl.BlockSpec((rows, wv), lambda bi: (bi, REST_G // wv)),
            pl.BlockSpec((RET_HEADS, SAMPLE_PAD, SAMPLE_PAD), full3),
            pl.BlockSpec((RET_HEADS, SAMPLE_PAD, LANES), full3),
            state_spec,
        ],
        out_specs=[pl.BlockSpec((rows, wv), lambda bi: (bi, 0)), state_spec],
        out_shape=[
            jax.ShapeDtypeStruct((b * l, wv), BF16),
            jax.ShapeDtypeStruct((b, RET_HEADS, RET_DK, RET_DV), F32),
        ],
        compiler_params=_cparams(("arbitrary",)),
        name="retention_sample",
    )(c_dec, qk, qk, kk, v, rest, decay, _lane_bcast(q_dec), s0)


CONV_PAD = 32
CONV_COLS = LANES


def _conv_strip(block, wts, nblk, store):
    lead = CONV_PAD - CONV_HIST
    taps = [((lead + w) // SUBLANES, (lead + w) % SUBLANES) for w in range(CONV_WIDTH)]
    rowid = lax.broadcasted_iota(I32, (SUBLANES, CONV_COLS), 0)

    def partial_sums(m, shifts):
        ys = {}
        for w, (a, s) in enumerate(taps):
            if s in shifts:
                term = block(m + a) * wts[w]
                ys[s] = term if s not in ys else ys[s] + term
        return ys

    shifted = range(1, SUBLANES)
    prev = partial_sums(0, shifted)
    for m in range(nblk):
        nxt = partial_sums(m + 1, shifted)
        acc = partial_sums(m, (0,))[0]
        for s in shifted:
            acc = acc + pltpu.roll(jnp.where(rowid >= s, prev[s], nxt[s]), SUBLANES - s, axis=0)
        store(m, acc)
        prev = nxt


def _conv_kernel(*refs, tl, nb, nl, has_state, n_side, side_steps):
    u_ref = refs[0]
    buf_ref = refs[1] if has_state else None
    n_in = 2 if has_state else 1
    cw_ref, bias_ref, lnw_ref, lnb_ref = refs[n_in:n_in + 4]
    side_in = refs[n_in + 4:n_in + 4 + n_side]
    sw_ref, nbuf_ref = refs[n_in + 4 + n_side:n_in + 6 + n_side]
    side_out = refs[n_in + 6 + n_side:n_in + 6 + 2 * n_side]
    upad_ref, c_ref = refs[-2:]
    if n_side:
        _side_cast(pl.program_id(0) * nl + pl.program_id(1), side_steps, side_in, side_out)
    li = pl.program_id(1)
    lead = CONV_PAD - CONV_HIST
    d = u_ref.shape[1]
    for bt in range(nb):
        @pl.when(li == 0)
        def _():
            if has_state:
                upad_ref[bt, 0:lead, :] = jnp.zeros((lead, d), F32)
                upad_ref[bt, lead:CONV_PAD, :] = buf_ref[0, :, bt, :]
            else:
                upad_ref[bt, 0:CONV_PAD, :] = jnp.zeros((CONV_PAD, d), F32)

        r0 = bt * tl
        upad_ref[bt, CONV_PAD:CONV_PAD + tl, :] = u_ref[r0:r0 + tl, :]
        for cc in range(d // CONV_COLS):
            cs = slice(cc * CONV_COLS, (cc + 1) * CONV_COLS)
            wts = [jnp.broadcast_to(cw_ref[0, w:w + 1, cs], (SUBLANES, CONV_COLS)) for w in range(CONV_WIDTH)]
            blocks = {}

            def block(j, cs=cs, blocks=blocks):
                if j not in blocks:
                    blocks[j] = upad_ref[bt, j * SUBLANES:(j + 1) * SUBLANES, cs]
                return blocks[j]

            def store(m, acc, cs=cs):
                c_ref[m * SUBLANES:(m + 1) * SUBLANES, cs] = acc

            _conv_strip(block, wts, tl // SUBLANES, store)
        cf = c_ref[...] + bias_ref[...]
        mu = jnp.mean(cf, axis=-1, keepdims=True)
        var = jnp.mean(jnp.square(cf - mu), axis=-1, keepdims=True)
        cn = (cf - mu) * lax.rsqrt(var + EPS) * lnw_ref[...] + lnb_ref[...]
        sw_ref[r0:r0 + tl, :] = jax.nn.silu(cn).astype(sw_ref.dtype)

        @pl.when(li == nl - 1)
        def _():
            nbuf_ref[0, bt] = upad_ref[bt, tl + lead:tl + CONV_PAD, :]

        if nl > 1:
            upad_ref[bt, 0:CONV_PAD, :] = upad_ref[bt, tl:tl + CONV_PAD, :]


def _conv_branch(u, buf_t, conv_w, conv_b, ln_w, ln_b, b, l, tl, nb, side=()):
    d = D_MODEL
    nl = l // tl
    assert nb == 1 or nl == 1
    rows = nb * tl
    const2 = lambda bi, li: (0, 0)
    grid = (b // nb, nl)
    side_steps = _side_steps(grid[0] * grid[1])
    side_specs = _side_specs(side, side_steps, lambda bi, li: bi * nl + li)
    state_spec = pl.BlockSpec((1, nb, CONV_HIST, d), lambda bi, li: (0, bi, 0, 0))
    has_state = buf_t is not None
    state_in = [pl.BlockSpec((1, CONV_HIST, nb, d), lambda bi, li: (0, 0, bi, 0))] if has_state else []
    outs = pl.pallas_call(
        functools.partial(_conv_kernel, tl=tl, nb=nb, nl=nl, has_state=has_state, n_side=len(side),
                          side_steps=_side_guard(side_steps, grid[0] * grid[1])),
        grid=grid,
        in_specs=[
            pl.BlockSpec((rows, d), lambda bi, li: (bi * nl + li, 0)),
        ] + state_in + [
            pl.BlockSpec((1, CONV_WIDTH, d), lambda bi, li: (0, 0, 0)),
            pl.BlockSpec((1, d), const2),
            pl.BlockSpec((1, d), const2),
            pl.BlockSpec((1, d), const2),
        ] + side_specs,
        out_specs=[pl.BlockSpec((rows, d), lambda bi, li: (bi * nl + li, 0)), state_spec] + side_specs,
        out_shape=[
            jax.ShapeDtypeStruct((b * l, d), BF16),
            jax.ShapeDtypeStruct((1, b, CONV_HIST, d), F32),
        ] + [jax.ShapeDtypeStruct(s.shape, BF16) for s in side],
        scratch_shapes=[pltpu.VMEM((nb, CONV_PAD + tl, d), F32), pltpu.VMEM((tl, d), F32)],
        compiler_params=_cparams(("arbitrary", "arbitrary")),
        name="conv_branch",
    )(u, *([buf_t] if has_state else []), conv_w, conv_b, ln_w, ln_b, *side)
    return outs[0], outs[1], tuple(outs[2:])


def _merge_kernel(og_ref, sw_ref, wr_ref, wc_ref, gr_ref, gc_ref, o_ref):
    ret_out = jnp.dot(og_ref[...], wr_ref[...], preferred_element_type=F32)
    conv_out = jnp.dot(sw_ref[...], wc_ref[...], preferred_element_type=F32)
    merged = jax.nn.sigmoid(gr_ref[...]) * ret_out + jax.nn.sigmoid(gc_ref[...]) * conv_out
    o_ref[...] = merged.astype(o_ref.dtype)


def _merge(og, sw, w_ret_o, w_conv_o, rest, tm=512, tn=1024):
    m = og.shape[0]
    n = w_ret_o.shape[1]
    tm = min(tm, m)
    kr = og.shape[1]
    kc = sw.shape[1]
    return pl.pallas_call(
        _merge_kernel,
        grid=(n // tn, m // tm),
        in_specs=[
            pl.BlockSpec((tm, kr), lambda j, i: (i, 0)),
            pl.BlockSpec((tm, kc), lambda j, i: (i, 0)),
            pl.BlockSpec((kr, tn), lambda j, i: (0, j)),
            pl.BlockSpec((kc, tn), lambda j, i: (0, j)),
            pl.BlockSpec((tm, tn), lambda j, i: (i, REST_GRET // tn + j)),
            pl.BlockSpec((tm, tn), lambda j, i: (i, REST_GCONV // tn + j)),
        ],
        out_specs=pl.BlockSpec((tm, tn), lambda j, i: (i, j)),
        out_shape=jax.ShapeDtypeStruct((m, n), BF16),
        compiler_params=_cparams(("arbitrary", "arbitrary")),
        name="merge",
    )(og, sw, w_ret_o, w_conv_o, rest, rest)


def _outproj_kernel(a_ref, w_ref, x_ref, *rest):
    o_ref, wb_ref = rest[-2], rest[-1]

    @pl.when(pl.program_id(1) == 0)
    def _():
        wb_ref[...] = w_ref[...].astype(BF16)

    o_ref[...] = x_ref[...] + jnp.dot(a_ref[...], wb_ref[...], preferred_element_type=F32)


def _outproj_residual(a, w, x, total_rows, row0, prev=None, tm=1024, tn=1024):
    m, k = a.shape
    n = w.shape[1]
    tm = min(tm, m)
    assert row0 % tm == 0
    off = row0 // tm
    in_specs = [
        pl.BlockSpec((tm, k), lambda j, i: (i, 0)),
        pl.BlockSpec((k, tn), lambda j, i: (0, j)),
        pl.BlockSpec((tm, tn), lambda j, i: (i, j)),
    ]
    args = [a, w, x]
    aliases = {}
    if prev is not None:
        in_specs.append(pl.BlockSpec(memory_space=pl.ANY))
        args.append(prev)
        aliases = {3: 0}
    return pl.pallas_call(
        _outproj_kernel,
        grid=(n // tn, m // tm),
        in_specs=in_specs,
        out_specs=pl.BlockSpec((tm, tn), lambda j, i: (i + off, j)),
        out_shape=jax.ShapeDtypeStruct((total_rows, n), F32),
        scratch_shapes=[pltpu.VMEM((k, tn), BF16)],
        input_output_aliases=aliases,
        compiler_params=_cparams(("arbitrary", "arbitrary")),
        name="outproj_residual",
    )(*args)


def _router_kernel(x_ref, nw_ref, w_ref, b_ref, h_ref, meta_ref, meta_t_ref, cnt_ref, base_ref):
    tm = x_ref.shape[0]

    @pl.when(pl.program_id(0) == 0)
    def _():
        base_ref[...] = jnp.zeros_like(base_ref)

    x = x_ref[...]
    ms = jnp.mean(x * x, axis=-1, keepdims=True)
    h = x * lax.rsqrt(ms + EPS) * nw_ref[...]
    h_ref[...] = h
    w = w_ref[...]
    h_hi = h.astype(BF16)
    h_lo = (h - h_hi.astype(F32)).astype(BF16)
    w_hi = w.astype(BF16)
    w_lo = (w - w_hi.astype(F32)).astype(BF16)
    logits = (jnp.dot(h_hi, w_lo, preferred_element_type=F32) + jnp.dot(h_lo, w_hi, preferred_element_type=F32)
              + jnp.dot(h_hi, w_hi, preferred_element_type=F32)) + b_ref[...]
    lane = lax.broadcasted_iota(I32, (tm, ROUTER_COLS), 1)

    def first_argmax(vals):
        top = jnp.max(vals, axis=1, keepdims=True)
        return top, jnp.min(jnp.where(vals == top, lane, ROUTER_COLS), axis=1, keepdims=True)

    cmask = lane < N_GROUPS
    lc = jnp.where(cmask, logits, NEG)
    mc, grp = first_argmax(lc)
    p_grp = 1.0 / jnp.sum(jnp.where(cmask, jnp.exp(lc - mc), 0.0), axis=1, keepdims=True)
    lo = ROUTER_OFF + grp * EXPERTS_PER_GROUP
    lf = jnp.where((lane >= lo) & (lane < lo + EXPERTS_PER_GROUP), logits, NEG)
    v1, i1 = first_argmax(lf)
    v2, i2 = first_argmax(jnp.where(lane == i1, NEG, lf))
    e2 = jnp.exp(v2 - v1)
    den = 1.0 + e2
    g1 = p_grp * (1.0 / den)
    g2 = p_grp * (e2 / den)

    oh1 = lane == i1
    oh2 = lane == i2
    both = jnp.where(oh1 | oh2, 1.0, 0.0)
    row = lax.broadcasted_iota(I32, (tm, tm), 0)
    col = lax.broadcasted_iota(I32, (tm, tm), 1)
    earlier = jnp.where(col < row, 1.0, 0.0).astype(BF16)
    before = jnp.dot(earlier, both.astype(BF16), preferred_element_type=F32) + base_ref[...]
    r1 = jnp.sum(jnp.where(oh1, before, 0.0), axis=1, keepdims=True)
    r2 = jnp.sum(jnp.where(oh2, before, 0.0), axis=1, keepdims=True)
    base_ref[...] = base_ref[...] + jnp.sum(both, axis=0, keepdims=True)
    cnt_ref[...] = base_ref[...]

    fields = [(i1 - ROUTER_OFF).astype(F32), (i2 - ROUTER_OFF).astype(F32), g1, g2, r1, r2]
    meta = jnp.zeros((tm, ROUTER_COLS), F32)
    for k, f in enumerate(fields):
        meta = jnp.where(lane == k, f, meta)
    meta_ref[...] = meta
    meta_t_ref[...] = meta.T[:SUBLANES]


def _router(x, norm_w, w_router, b_router, tm=512):
    n, d = x.shape
    tm = min(tm, n)
    const2 = lambda i: (0, 0)
    return pl.pallas_call(
        _router_kernel,
        grid=(n // tm,),
        in_specs=[
            pl.BlockSpec((tm, d), lambda i: (i, 0)),
            pl.BlockSpec((1, d), const2),
            pl.BlockSpec((d, ROUTER_COLS), const2),
            pl.BlockSpec((1, ROUTER_COLS), const2),
        ],
        out_specs=[
            pl.BlockSpec((tm, d), lambda i: (i, 0)),
            pl.BlockSpec((tm, ROUTER_COLS), lambda i: (i, 0)),
            pl.BlockSpec((SUBLANES, tm), lambda i: (0, i)),
            pl.BlockSpec((1, ROUTER_COLS), const2),
        ],
        out_shape=[
            jax.ShapeDtypeStruct((n, d), F32),
            jax.ShapeDtypeStruct((n, ROUTER_COLS), F32),
            jax.ShapeDtypeStruct((SUBLANES, n), F32),
            jax.ShapeDtypeStruct((1, ROUTER_COLS), F32),
        ],
        scratch_shapes=[pltpu.VMEM((1, ROUTER_COLS), F32)],
        compiler_params=_cparams(("arbitrary",)),
        name="router",
    )(x, norm_w.reshape(1, d), w_router, b_router)


def _rows_kernel(pstart_ref, first_ref, meta_ref, rows_ref):
    expert = meta_ref[0:TOP_K_FINE, :].astype(I32)
    rank = meta_ref[4:4 + TOP_K_FINE, :].astype(I32)
    rows = rank
    for e in range(N_EXPERTS):
        first = first_ref[e]
        rows = rows + jnp.where(expert == e, pstart_ref[e] + jnp.where(rank >= first, MOE_BLOCK - first, 0), 0)
    rows_ref[...] = rows


def _assignment_rows(pstart, first, meta_t):
    n = meta_t.shape[1]
    return pl.pallas_call(
        _rows_kernel,
        grid_spec=pltpu.PrefetchScalarGridSpec(
            num_scalar_prefetch=2,
            grid=(1,),
            in_specs=[pl.BlockSpec((SUBLANES, n), lambda i, ps, fs: (0, 0))],
            out_specs=pl.BlockSpec((TOP_K_FINE, n), lambda i, ps, fs: (0, 0)),
        ),
        out_shape=jax.ShapeDtypeStruct((TOP_K_FINE, n), I32),
        compiler_params=_cparams(("arbitrary",)),
        name="moe_rows",
    )(pstart, first, meta_t)


def _dispatch_kernel(rows_ref, span_ref, h_ref, xs_hbm, zero_ref, sem, zsem):
    td = h_ref.shape[0]
    i = pl.program_id(0)

    @pl.when(i == 0)
    def _():
        zero_ref[...] = jnp.zeros_like(zero_ref)

        def fill_copy(e):
            last = pl.multiple_of(span_ref[N_EXPERTS + e] - MOE_ROWS, MOE_ROWS)
            return pltpu.make_async_copy(zero_ref, xs_hbm.at[pl.ds(last, MOE_ROWS)], zsem)

        def nonempty(e):
            return span_ref[N_EXPERTS + e] > span_ref[e]

        def start(e, carry):
            @pl.when(nonempty(e))
            def _():
                fill_copy(e).start()
            return carry

        def wait(e, carry):
            @pl.when(nonempty(e))
            def _():
                fill_copy(e).wait()
            return carry

        lax.fori_loop(0, N_EXPERTS, start, 0)
        lax.fori_loop(0, N_EXPERTS, wait, 0)

    n_tok = pl.num_programs(0) * td

    def tokens(g, carry):
        t0 = pl.multiple_of(g * SUBLANES, SUBLANES)
        for u in range(SUBLANES):
            for k in range(TOP_K_FINE):
                r = rows_ref[k * n_tok + i * td + t0 + u]
                pltpu.make_async_copy(h_ref.at[pl.ds(t0 + u, 1)], xs_hbm.at[pl.ds(r, 1)], sem).start(priority=k)
        return carry

    lax.fori_loop(0, td // SUBLANES, tokens, 0)
    for _ in range(TOP_K_FINE):
        pltpu.make_async_copy(h_ref, xs_hbm.at[pl.ds(0, td)], sem).wait()


def _dispatch(rows_flat, span, h, n_rows, td=1024):
    n_tok, d = h.shape
    td = min(td, n_tok)
    return pl.pallas_call(
        _dispatch_kernel,
        grid_spec=pltpu.PrefetchScalarGridSpec(
            num_scalar_prefetch=2,
            grid=(n_tok // td,),
            in_specs=[pl.BlockSpec((td, d), lambda i, rows, span: (i, 0))],
            out_specs=pl.BlockSpec(memory_space=pl.ANY),
            scratch_shapes=[pltpu.VMEM((MOE_ROWS, d), F32), pltpu.SemaphoreType.DMA(()),
                            pltpu.SemaphoreType.DMA(())],
        ),
        out_shape=jax.ShapeDtypeStruct((n_rows, d), F32),
        compiler_params=_cparams(("arbitrary",)),
        name="moe_dispatch",
    )(rows_flat, span, h)


def _expert_kernel(be_ref, nh_ref, nu_ref, x_ref, wg_ref, wu_ref, wd_ref, o_ref):
    del be_ref
    nh = nh_ref[pl.program_id(0)]

    def ffn(rows):
        xb = x_ref[0:rows, :].astype(BF16)
        a = jnp.dot(xb, wg_ref[0], preferred_element_type=F32)
        b = jnp.dot(xb, wu_ref[0], preferred_element_type=F32)
        hb = (jax.nn.silu(a) * b).astype(BF16)
        o_ref[0:rows, :] = jnp.dot(hb, wd_ref[0], preferred_element_type=F32)

    for parts in range(1, MOE_PARTS + 1):
        @pl.when(nh == parts)
        def _():
            ffn(parts * MOE_ROWS)
            if parts < MOE_PARTS:
                o_ref[parts * MOE_ROWS:, :] = jnp.zeros((MOE_BLOCK - parts * MOE_ROWS, o_ref.shape[1]), F32)


def _experts(blk_exp, blk_parts, n_used, xs, wg, wu, wd):
    n_rows, d = xs.shape
    n_blocks = n_rows // MOE_BLOCK
    de = wg.shape[2]
    row_map = lambda i, be, nh, nu: (jnp.minimum(i, jnp.maximum(nu[0] - 1, 0)), 0)
    w_map = lambda i, be, nh, nu: (be[i], 0, 0)
    return pl.pallas_call(
        _expert_kernel,
        grid_spec=pltpu.PrefetchScalarGridSpec(
            num_scalar_prefetch=3,
            grid=(n_blocks,),
            in_specs=[
                pl.BlockSpec((MOE_BLOCK, d), row_map),
                pl.BlockSpec((1, d, de), w_map),
                pl.BlockSpec((1, d, de), w_map),
                pl.BlockSpec((1, de, d), w_map),
            ],
            out_specs=pl.BlockSpec((MOE_BLOCK, d), row_map),
        ),
        out_shape=jax.ShapeDtypeStruct((n_rows, d), F32),
        compiler_params=_cparams(("arbitrary",)),
        name="moe_experts",
    )(blk_exp, blk_parts, n_used, xs, wg, wu, wd)


def _combine_kernel(rows_ref, ys_hbm, x_ref, meta_ref, nf_ref, op_ref, os_ref, buf_ref, sem, *, p_tiles):
    tm = x_ref.shape[0]
    i = pl.program_id(0)
    nt = pl.num_programs(0)
    n_tok = nt * tm
    slot = i % 2

    def gather(tile, dst_slot):
        def tokens(g, carry):
            t0 = pl.multiple_of(g * SUBLANES, SUBLANES)
            for u in range(SUBLANES):
                for k in range(TOP_K_FINE):
                    r = rows_ref[k * n_tok + tile * tm + t0 + u]
                    pltpu.make_async_copy(ys_hbm.at[pl.ds(r, 1)], buf_ref.at[dst_slot, k, pl.ds(t0 + u, 1)],
                                          sem.at[dst_slot]).start(priority=k)
            return carry

        lax.fori_loop(0, tm // SUBLANES, tokens, 0)

    @pl.when(i == 0)
    def _():
        gather(0, 0)

    @pl.when(i + 1 < nt)
    def _():
        gather(i + 1, 1 - slot)

    for k in range(TOP_K_FINE):
        pltpu.make_async_copy(ys_hbm.at[pl.ds(0, tm)], buf_ref.at[slot, k], sem.at[slot]).wait()
    meta = meta_ref[...]
    y = x_ref[...] + (buf_ref[slot, 0] * meta[:, 2:3] + buf_ref[slot, 1] * meta[:, 3:4])
    ms = jnp.mean(y * y, axis=-1, keepdims=True)
    out = y * lax.rsqrt(ms + EPS) * nf_ref[...]

    @pl.when(i < p_tiles)
    def _():
        op_ref[...] = out

    @pl.when(i >= p_tiles)
    def _():
        os_ref[...] = out


def _combine(rows_flat, ys, x, meta, norm_f, n_p, tm=512):
    n, d = x.shape
    n_s = n - n_p
    tm = min(tm, n_p, n_s)
    assert n_p % tm == 0 and n_s % tm == 0
    p_tiles = n_p // tm
    return pl.pallas_call(
        functools.partial(_combine_kernel, p_tiles=p_tiles),
        grid_spec=pltpu.PrefetchScalarGridSpec(
            num_scalar_prefetch=1,
            grid=(n // tm,),
            in_specs=[
                pl.BlockSpec(memory_space=pl.ANY),
                pl.BlockSpec((tm, d), lambda i, rows: (i, 0)),
                pl.BlockSpec((tm, ROUTER_COLS), lambda i, rows: (i, 0)),
                pl.BlockSpec((1, d), lambda i, rows: (0, 0)),
            ],
            out_specs=[
                pl.BlockSpec((tm, d), lambda i, rows: (jnp.minimum(i, p_tiles - 1), 0)),
                pl.BlockSpec((tm, d), lambda i, rows: (jnp.maximum(i - p_tiles, 0), 0)),
            ],
            scratch_shapes=[pltpu.VMEM((2, TOP_K_FINE, tm, d), F32), pltpu.SemaphoreType.DMA((2,))],
        ),
        out_shape=[jax.ShapeDtypeStruct((n_p, d), F32), jax.ShapeDtypeStruct((n_s, d), F32)],
        compiler_params=_cparams(("arbitrary",)),
        name="moe_combine",
    )(rows_flat, ys, x, meta, norm_f.reshape(1, d))


def _moe_and_final_norm(x, n_p, norm_ffn, w_coarse, b_coarse, w_fine, b_fine, wg, wu, wd, norm_f):
    n_tok, d = x.shape
    pad = ROUTER_COLS - N_GROUPS - N_EXPERTS
    w_router = jnp.concatenate([w_coarse, w_fine, jnp.zeros((d, pad), F32)], axis=1)
    b_router = jnp.concatenate([b_coarse, b_fine, jnp.zeros((pad,), F32)]).reshape(1, ROUTER_COLS)
    h, meta, meta_t, counts = _router(x, norm_ffn, w_router, b_router)

    n_asg = n_tok * TOP_K_FINE
    n_blocks = -(-n_asg // MOE_BLOCK) + N_EXPERTS
    n_rows = n_blocks * MOE_BLOCK
    cnt = counts[0, ROUTER_OFF:ROUTER_OFF + N_EXPERTS].astype(I32)
    nblk = (cnt + MOE_BLOCK - 1) // MOE_BLOCK
    padded = nblk * MOE_BLOCK
    pend = jnp.cumsum(padded)
    pstart = pend - padded
    first = cnt - (nblk - 1) * MOE_BLOCK
    first_rows = jnp.where(cnt > 0, (first + MOE_ROWS - 1) // MOE_ROWS * MOE_ROWS, 0)
    rows_flat = _assignment_rows(pstart.astype(I32), first.astype(I32), meta_t).reshape(-1)
    n_used = (pend[-1] // MOE_BLOCK).astype(I32)
    blk = jnp.arange(n_blocks, dtype=I32)
    blk_row = jnp.minimum(blk, n_used - 1) * MOE_BLOCK
    blk_exp = jnp.sum((pend[None, :] <= blk_row[:, None]).astype(I32), axis=1)
    blk_exp = jnp.minimum(blk_exp, N_EXPERTS - 1)
    blk_parts = jnp.where(blk * MOE_BLOCK == pstart[blk_exp], first_rows[blk_exp] // MOE_ROWS, MOE_PARTS)
    blk_parts = jnp.where(blk < n_used, blk_parts, 0).astype(I32)
    cend = pstart + first_rows

    xs = _dispatch(rows_flat, jnp.concatenate([pstart, cend]).astype(I32), h, n_rows)
    ys = _experts(blk_exp, blk_parts, n_used.reshape(1), xs, wg, wu, wd)
    return _combine(rows_flat, ys, x, meta, norm_f, n_p)


def _mixer(x, s_ret, s_conv, pos0, norm_mix, w_in, conv_w, conv_b, conv_ln_w, conv_ln_b, w_o,
           total_rows, row0, prev, merge_w=None, side_rope=(), side_rest=(), side_conv=()):
    b, l, d = x.shape
    x2 = x.reshape(b * l, d)
    chunk = RET_CHUNK if l % RET_CHUNK == 0 else l
    h = _rmsnorm_cast(x2, norm_mix)
    qk, kk, cast_a = _inproj_rope(h, w_in, *_rope_inputs(l, pos0, chunk, min(ROPE_TM, b * l)), side=side_rope)
    v, _ = _inproj(h, w_in, [(COL_V, W_V)], BF16)
    u = _inproj_glu(h, w_in)
    rest, cast_b = _inproj(h, w_in, [(COL_G, W_G), (COL_GATES, W_REST - W_G)], F32, side_rest)
    if s_ret is None:
        og, s_ret_new, _ = _retention_prompt(qk, kk, v, rest, b, l)
        sw, s_conv_new, cast_c = _conv_branch(u, s_conv, conv_w, conv_b, conv_ln_w, conv_ln_b, b, l,
                                              tl=min(128, l), nb=1, side=side_conv)
    else:
        og, s_ret_new = _retention_sample(qk, kk, v, rest, s_ret, b, l)
        sw, s_conv_new, cast_c = _conv_branch(u, s_conv, conv_w, conv_b, conv_ln_w, conv_ln_b, b, l,
                                              tl=l, nb=min(SUBLANES, b), side=side_conv)
    if merge_w is None:
        merge_w = cast_a[:2]
    merged = _merge(og, sw, *merge_w, rest)
    x1 = _outproj_residual(merged, w_o, x2, total_rows, row0, prev)
    return x1, s_ret_new, s_conv_new, cast_a + cast_b + cast_c


def kernel(x_prompt, x_sample, state_ret, state_conv, norm_mix, w_in, w_ret_o, conv_w, conv_b, conv_ln_w,
           conv_ln_b, w_conv_o, w_o, norm_ffn, w_coarse, b_coarse, w_fine, b_fine, w_gate, w_up, w_down, norm_f):
    depth = w_in.shape[0]
    assert depth == 1, "single-layer trunk"
    bp, lp, d = x_prompt.shape
    bs, ls, _ = x_sample.shape
    n_p = bp * lp
    n_s = bs * ls
    mix_w = (norm_mix[0], w_in[0], conv_w, conv_b, conv_ln_w, conv_ln_b, w_o[0])
    n_exp, _, d_exp = w_gate.shape[1:]
    flat = lambda w: w.reshape(n_exp * w.shape[2], w.shape[3])
    x1, ret_p, conv_p, (wr, wc, wg, wu, wd) = _mixer(
        x_prompt, None, None, 0, *mix_w, n_p + n_s, 0, None,
        side_rope=(w_ret_o[0], w_conv_o[0]), side_rest=(flat(w_gate),), side_conv=(flat(w_up), flat(w_down)))
    conv_hist = jnp.transpose(state_conv, (0, 2, 1, 3))
    x1, ret_s, conv_s, _ = _mixer(x_sample, state_ret[0], conv_hist, PAST_LEN, *mix_w, n_p + n_s, n_p, x1,
                                  merge_w=(wr, wc))
    yp, ys = _moe_and_final_norm(x1, n_p, norm_ffn[0], w_coarse[0], b_coarse[0], w_fine[0], b_fine[0],
                                 wg.reshape(n_exp, d, d_exp), wu.reshape(n_exp, d, d_exp),
                                 wd.reshape(n_exp, d_exp, d), norm_f)
    return (yp.reshape(bp, lp, d), ys.reshape(bs, ls, d), ret_p[None], conv_p, ret_s[None], conv_s)
```
